```python
import math
import jax, jax.numpy as jnp
from jax import lax
import numpy as np

D_MODEL = 1024
BATCH = 32
SEQ = 256
DEPTH = 2
DEC_BATCH = 8
DEC_SEQ = 1024
PAST_LEN = 512

GRID_W = 64
D_CONV = D_MODEL // 4
D_HYENA = D_MODEL // 4
D_ATTN = D_MODEL - D_CONV - D_HYENA
DIFF_DK = 64
N_HEADS = D_ATTN // (2 * DIFF_DK)
DIFF_DV = 2 * DIFF_DK
IN_COLS = 2 * D_CONV + 3 * D_HYENA + 3 * D_ATTN
CONV_WIDTH = 31
SHORT_WIDTH = 3
HYENA_ORDER = 2
HYENA_EMB = 17
HYENA_BANDS = (HYENA_EMB - 1) // 2
HYENA_HIDDEN = 64
HYENA_MIN_DECAY = math.log(1e-2) / 1.5
HYENA_MAX_DECAY = math.log(1e-2) / 0.3
N_EXPERTS = 64
TOP_K = 8
N_GROUPS = 8
TOPK_GROUPS = 4
GROUP_SCORE_TOP = 2
D_EXPERT = 256
D_SHARED = 256
ROUTED_SCALE = 2.5
EXPERT_BLOCK = 128
Q_BLOCK = 128
ROPE_BASE = 10000.0
EPS = 1e-6
N_MOD = 6

kernel_name = 'diff_hymba_flow_step'


def rms_norm(x, g):
    x32 = x.astype(jnp.float32)
    y = x32 * lax.rsqrt(jnp.mean(x32 * x32, axis=-1, keepdims=True) + EPS)
    return y.astype(x.dtype) * g


def layer_norm(x, g, b):
    x32 = x.astype(jnp.float32)
    mu = jnp.mean(x32, axis=-1, keepdims=True)
    var = jnp.mean(jnp.square(x32 - mu), axis=-1, keepdims=True)
    return ((x32 - mu) * lax.rsqrt(var + EPS)).astype(x.dtype) * g + b


def dwconv(x, w, b):
    pad = (w.shape[0] - 1) // 2
    y = lax.conv_general_dilated(x, w[:, None, :], window_strides=(1,), padding=[(pad, pad)],
                                 dimension_numbers=('NWC', 'WIO', 'NWC'),
                                 feature_group_count=x.shape[-1])
    return y + b


def conformer_group(u, p):
    a = u[..., :D_CONV] * jax.nn.sigmoid(u[..., D_CONV:])
    a = dwconv(a, p['conv_w'], p['conv_b'])
    a = layer_norm(a, p['conv_ln_g'], p['conv_ln_b'])
    return jax.nn.silu(a)


def hyena_filter_spectrum(L, p):
    f32 = jnp.float32
    t = jnp.linspace(0.0, 1.0, L, dtype=f32)
    bands = jnp.linspace(1e-4, HYENA_BANDS - 1, HYENA_BANDS, dtype=f32)
    ang = (2.0 * math.pi / L) * jnp.arange(L, dtype=f32)[:, None] * bands[None, :]
    feats = jnp.concatenate([t[:, None], jnp.cos(ang), -jnp.sin(ang)], axis=-1)
    fr = p['hy_freq'].astype(f32)
    hid = jnp.sin(fr * (feats @ p['hy_w1'].astype(f32) + p['hy_b1'].astype(f32)))
    hid = jnp.sin(fr * (hid @ p['hy_w2'].astype(f32) + p['hy_b2'].astype(f32)))
    filt = (hid @ p['hy_w3'].astype(f32)).reshape(L, 2, HYENA_ORDER, D_HYENA)
    deltas = jnp.abs(jnp.linspace(HYENA_MIN_DECAY, HYENA_MAX_DECAY, D_HYENA, dtype=f32))
    window = jnp.exp(-t[:, None] * deltas[None, :])
    filt = filt * window[:, None, None, :]
    fwd, bwd = filt[:, 0], filt[:, 1]
    k = jnp.concatenate([fwd, jnp.zeros((1, HYENA_ORDER, D_HYENA), f32), bwd[:0:-1]], axis=0)
    return jnp.fft.rfft(k, axis=0)


def hyena_group(u, p):
    L = u.shape[1]
    u = dwconv(u, p['hy_short_w'], p['hy_short_b'])
    v, x1, x2 = jnp.split(u, 3, axis=-1)
    kf = hyena_filter_spectrum(L, p)
    z = v.astype(jnp.float32)
    for o, gate in enumerate((x1, x2)):
        spec = jnp.fft.rfft(z, n=2 * L, axis=1)
        conv = jnp.fft.irfft(spec * kf[None, :, o, :], n=2 * L, axis=1)[:, :L]
        z = gate.astype(jnp.float32) * (conv + p['hy_bias'][o].astype(jnp.float32) * z)
    return z.astype(u.dtype)


def axial_angles(L):
    rows = L // GRID_W
    row = jnp.repeat(jnp.arange(rows), GRID_W).astype(jnp.float32)
    col = jnp.tile(jnp.arange(GRID_W), rows).astype(jnp.float32)
    n = DIFF_DK // 4
    inv = ROPE_BASE ** (-jnp.arange(n, dtype=jnp.float32) / n)
    return row[:, None] * inv[None, :], col[:, None] * inv[None, :]


def rope_half(x, ang):
    x1, x2 = jnp.split(x, 2, axis=-1)
    cos, sin = jnp.cos(ang), jnp.sin(ang)
    return jnp.concatenate([x1 * cos - x2 * sin, x1 * sin + x2 * cos], axis=-1)


def apply_axial_rope(x, ang):
    ang_r, ang_c = ang
    x32 = x.astype(jnp.float32)
    half = DIFF_DK // 2
    out = jnp.concatenate([rope_half(x32[..., :half], ang_r[:, None, :]),
                           rope_half(x32[..., half:], ang_c[:, None, :])], axis=-1)
    return out.astype(x.dtype)


def diff_attention(q, k, v, lam):
    B, H, Lq = q.shape[:3]
    nb = Lq // Q_BLOCK
    scale = DIFF_DK ** -0.5
    k32 = k.astype(jnp.float32)
    v32 = v.astype(jnp.float32)
    qb = jnp.moveaxis(q.astype(jnp.float32).reshape(B, H, nb, Q_BLOCK, 2, DIFF_DK), 2, 0)

    def block(qi):
        s = jnp.einsum('bhqcd,bhkcd->cbhqk', qi, k32) * scale
        pr = jax.nn.softmax(s, axis=-1)
        return jnp.einsum('bhqk,bhkd->bhqd', pr[0] - lam * pr[1], v32)

    o = lax.map(block, qb)
    return jnp.moveaxis(o, 0, 2).reshape(B, H, Lq, DIFF_DV)


def mixer(h, p, l, ctx_kv, ang):
    B, L, _ = h.shape
    u = h @ p['w_in']
    u_a = u[..., :2 * D_CONV]
    u_b = u[..., 2 * D_CONV:2 * D_CONV + 3 * D_HYENA]
    u_c = u[..., 2 * D_CONV + 3 * D_HYENA:]
    y_a = conformer_group(u_a, p)
    y_b = hyena_group(u_b, p)
    q, k, v = jnp.split(u_c, 3, axis=-1)
    q = q.reshape(B, L, N_HEADS, 2, DIFF_DK).transpose(0, 2, 1, 3, 4)
    k = k.reshape(B, L, N_HEADS, 2, DIFF_DK).transpose(0, 2, 1, 3, 4)
    v = v.reshape(B, L, N_HEADS, DIFF_DV).transpose(0, 2, 1, 3)
    if ang is not None:
        q = apply_axial_rope(q, ang)
        k = apply_axial_rope(k, ang)
    if ctx_kv is None:
        k_all, v_all = k, v
    else:
        k_all = jnp.concatenate([ctx_kv[0].astype(k.dtype), k], axis=2)
        v_all = jnp.concatenate([ctx_kv[1].astype(v.dtype), v], axis=2)
    lam_init = 0.8 - 0.6 * math.exp(-0.3 * l)
    f32 = jnp.float32
    lam = (jnp.exp(jnp.sum(p['lam_q1'].astype(f32) * p['lam_k1'].astype(f32)))
           - jnp.exp(jnp.sum(p['lam_q2'].astype(f32) * p['lam_k2'].astype(f32))) + lam_init)
    o = diff_attention(q, k_all, v_all, lam).astype(h.dtype)
    o = rms_norm(o, p['subln_g']) * (1.0 - lam_init)
    o = o.transpose(0, 2, 1, 3).reshape(B, L, D_ATTN)
    y = jnp.concatenate([y_a, y_b, o], axis=-1) @ p['w_out']
    return y, k, v


def swiglu(x, wg, wu, wd):
    return (jax.nn.silu(x @ wg) * (x @ wu)) @ wd


def moe(h, p):
    T, D = h.shape
    f32 = jnp.float32
    s = jax.nn.sigmoid(h.astype(f32) @ p['router_w'].astype(f32))
    sel = s + p['router_b'].astype(f32)
    per_group = N_EXPERTS // N_GROUPS
    gscore = lax.top_k(sel.reshape(T, N_GROUPS, per_group), GROUP_SCORE_TOP)[0].sum(-1)
    _, gidx = lax.top_k(gscore, TOPK_GROUPS)
    gmask = jnp.any(gidx[:, :, None] == jnp.arange(N_GROUPS)[None, None, :], axis=1)
    masked = jnp.where(jnp.repeat(gmask, per_group, axis=1), sel, -jnp.inf)
    _, eidx = lax.top_k(masked, TOP_K)
    w = jnp.take_along_axis(s, eidx, axis=1)
    w = w / jnp.sum(w, axis=-1, keepdims=True) * ROUTED_SCALE
    N = T * TOP_K
    flat_e = eidx.reshape(N)
    flat_tok = jnp.repeat(jnp.arange(T), TOP_K)
    flat_w = w.reshape(N)
    order = jnp.argsort(flat_e)
    se, stok, sw = flat_e[order], flat_tok[order], flat_w[order]
    counts = jnp.bincount(flat_e, length=N_EXPERTS)
    starts = jnp.cumsum(counts) - counts
    pcounts = ((counts + EXPERT_BLOCK - 1) // EXPERT_BLOCK) * EXPERT_BLOCK
    pend = jnp.cumsum(pcounts)
    pstart = pend - pcounts
    dest = pstart[se] + (jnp.arange(N) - starts[se])
    n_blocks = -(-N // EXPERT_BLOCK) + N_EXPERTS
    P = n_blocks * EXPERT_BLOCK
    x_buf = jnp.zeros((P, D), h.dtype).at[dest].set(h[stok])
    w_buf = jnp.zeros((P,), f32).at[dest].set(sw)
    tok_buf = jnp.full((P,), T, jnp.int32).at[dest].set(stok.astype(jnp.int32))
    block_e = jnp.minimum(jnp.searchsorted(pend, jnp.arange(n_blocks) * EXPERT_BLOCK, side='right'),
                          N_EXPERTS - 1)
    wg, wu, wd = p['exp_w_gate'], p['exp_w_up'], p['exp_w_down']

    def expert_block(args):
        xb, e = args
        return swiglu(xb, wg[e], wu[e], wd[e])

    y_buf = lax.map(expert_block, (x_buf.reshape(n_blocks, EXPERT_BLOCK, D), block_e)).reshape(P, D)
    routed = jax.ops.segment_sum(y_buf.astype(f32) * w_buf[:, None], tok_buf, num_segments=T + 1)[:T]
    shared = swiglu(h, p['sh_w_gate'], p['sh_w_up'], p['sh_w_down'])
    return (shared.astype(f32) + routed).astype(h.dtype)


def trunk_layer(x, cond, p, l, ctx_kv, ang):
    mod = cond @ p['w_mod'] + p['b_mod']
    sh1, sc1, g1, sh2, sc2, g2 = jnp.split(mod[:, None, :], N_MOD, axis=-1)
    h = rms_norm(x, p['g_pre_mix']) * (1.0 + sc1) + sh1
    y, k, v = mixer(h, p, l, ctx_kv, ang)
    x = x + g1 * rms_norm(y, p['g_post_mix'])
    h = rms_norm(x, p['g_pre_ffn']) * (1.0 + sc2) + sh2
    B, L, D = x.shape
    y = moe(h.reshape(B * L, D), p).reshape(B, L, D)
    x = x + g2 * rms_norm(y, p['g_post_ffn'])
    return x, k, v


def setup_inputs(seed: int = 0) -> dict:
    key = jax.random.key(seed)
    ks = iter(jax.random.split(key, 48))

    def nrm(shape, scale=1.0):
        return jax.random.normal(next(ks), shape, jnp.float32) * scale

    def gain(shape):
        return 1.0 + nrm(shape, 0.05)

    D = D_MODEL
    return {
        'x_prompt': nrm((BATCH, SEQ, D)),
        'x_sample': nrm((DEC_BATCH, DEC_SEQ, D)),
        'cache_k': nrm((DEC_BATCH, DEPTH, N_HEADS, PAST_LEN, 2 * DIFF_DK)),
        'cache_v': nrm((DEC_BATCH, DEPTH, N_HEADS, PAST_LEN, DIFF_DV)),
        'c': nrm((DEC_BATCH, D)),
        'c_ctx': nrm((D,)),
        'w_mod': nrm((DEPTH, D, N_MOD * D), 0.5 * D ** -0.5),
        'b_mod': nrm((DEPTH, N_MOD * D), 0.02),
        'g_pre_mix': gain((DEPTH, D)),
        'g_post_mix': gain((DEPTH, D)),
        'g_pre_ffn': gain((DEPTH, D)),
        'g_post_ffn': gain((DEPTH, D)),
        'w_in': nrm((DEPTH, D, IN_COLS), D ** -0.5),
        'w_out': nrm((DEPTH, D_CONV + D_HYENA + D_ATTN, D), D ** -0.5),
        'conv_w': nrm((DEPTH, CONV_WIDTH, D_CONV), CONV_WIDTH ** -0.5),
        'conv_b': nrm((DEPTH, D_CONV), 0.02),
        'conv_ln_g': gain((DEPTH, D_CONV)),
        'conv_ln_b': nrm((DEPTH, D_CONV), 0.02),
        'hy_short_w': nrm((DEPTH, SHORT_WIDTH, 3 * D_HYENA), SHORT_WIDTH ** -0.5),
        'hy_short_b': nrm((DEPTH, 3 * D_HYENA), 0.02),
        'hy_w1': nrm((DEPTH, HYENA_EMB, HYENA_HIDDEN), HYENA_EMB ** -0.5),
        'hy_b1': nrm((DEPTH, HYENA_HIDDEN), 0.02),
        'hy_w2': nrm((DEPTH, HYENA_HIDDEN, HYENA_HIDDEN), HYENA_HIDDEN ** -0.5),
        'hy_b2': nrm((DEPTH, HYENA_HIDDEN), 0.02),
        'hy_freq': gain((DEPTH, HYENA_HIDDEN)),
        'hy_w3': nrm((DEPTH, HYENA_HIDDEN, 2 * HYENA_ORDER * D_HYENA), 0.05 * HYENA_HIDDEN ** -0.5),
        'hy_bias': nrm((DEPTH, HYENA_ORDER, D_HYENA), 1.0),
        'lam_q1': nrm((DEPTH, DIFF_DK), 0.1),
        'lam_k1': nrm((DEPTH, DIFF_DK), 0.1),
        'lam_q2': nrm((DEPTH, DIFF_DK), 0.1),
        'lam_k2': nrm((DEPTH, DIFF_DK), 0.1),
        'subln_g': gain((DEPTH, DIFF_DV)),
        'router_w': nrm((DEPTH, D, N_EXPERTS), D ** -0.5),
        'router_b': nrm((DEPTH, N_EXPERTS), 0.01),
        'exp_w_gate': nrm((DEPTH, N_EXPERTS, D, D_EXPERT), D ** -0.5),
        'exp_w_up': nrm((DEPTH, N_EXPERTS, D, D_EXPERT), D ** -0.5),
        'exp_w_down': nrm((DEPTH, N_EXPERTS, D_EXPERT, D), D_EXPERT ** -0.5),
        'sh_w_gate': nrm((DEPTH, D, D_SHARED), D ** -0.5),
        'sh_w_up': nrm((DEPTH, D, D_SHARED), D ** -0.5),
        'sh_w_down': nrm((DEPTH, D_SHARED, D), D_SHARED ** -0.5),
    }


def reference(x_prompt, x_sample, cache_k, cache_v, c, c_ctx, w_mod, b_mod, g_pre_mix, g_post_mix,
              g_pre_ffn, g_post_ffn, w_in, w_out, conv_w, conv_b, conv_ln_g, conv_ln_b, hy_short_w,
              hy_short_b, hy_w1, hy_b1, hy_w2, hy_b2, hy_freq, hy_w3, hy_bias, lam_q1, lam_k1, lam_q2,
              lam_k2, subln_g, router_w, router_b, exp_w_gate, exp_w_up, exp_w_down, sh_w_gate,
              sh_w_up, sh_w_down):
    def params(l):
        return {
            'w_mod': w_mod[l], 'b_mod': b_mod[l],
            'g_pre_mix': g_pre_mix[l], 'g_post_mix': g_post_mix[l],
            'g_pre_ffn': g_pre_ffn[l], 'g_post_ffn': g_post_ffn[l],
            'w_in': w_in[l], 'w_out': w_out[l],
            'conv_w': conv_w[l], 'conv_b': conv_b[l], 'conv_ln_g': conv_ln_g[l], 'conv_ln_b': conv_ln_b[l],
            'hy_short_w': hy_short_w[l], 'hy_short_b': hy_short_b[l],
            'hy_w1': hy_w1[l], 'hy_b1': hy_b1[l], 'hy_w2': hy_w2[l], 'hy_b2': hy_b2[l],
            'hy_freq': hy_freq[l], 'hy_w3': hy_w3[l], 'hy_bias': hy_bias[l],
            'lam_q1': lam_q1[l], 'lam_k1': lam_k1[l], 'lam_q2': lam_q2[l], 'lam_k2': lam_k2[l],
            'subln_g': subln_g[l],
            'router_w': router_w[l], 'router_b': router_b[l],
            'exp_w_gate': exp_w_gate[l], 'exp_w_up': exp_w_up[l], 'exp_w_down': exp_w_down[l],
            'sh_w_gate': sh_w_gate[l], 'sh_w_up': sh_w_up[l], 'sh_w_down': sh_w_down[l],
        }

    cond_ctx = jax.nn.silu(c_ctx)[None, :]
    x = x_prompt
    ctx_k, ctx_v = [], []
    for l in range(DEPTH):
        x, k, v = trunk_layer(x, cond_ctx, params(l), l, None, None)
        b, nh, L = k.shape[:3]
        ctx_k.append(k.reshape(b, nh, L, 2 * DIFF_DK))
        ctx_v.append(v)
    y_prompt = x
    new_cache_k = jnp.stack(ctx_k, axis=1)
    new_cache_v = jnp.stack(ctx_v, axis=1)

    cond_lat = jax.nn.silu(c)
    ang = axial_angles(x_sample.shape[1])
    x = x_sample
    for l in range(DEPTH):
        ck = cache_k[:, l]
        ck = ck.reshape(ck.shape[0], N_HEADS, ck.shape[2], 2, DIFF_DK)
        x, _, _ = trunk_layer(x, cond_lat, params(l), l, (ck, cache_v[:, l]), ang)
    y_sample = x
    return (y_prompt, y_sample, new_cache_k, new_cache_v)
```

```python
import functools
import math

import numpy as np
import jax
import jax.numpy as jnp
from jax import lax
from jax.experimental import pallas as pl
from jax.experimental.pallas import tpu as pltpu

F32 = jnp.float32
BF16 = jnp.bfloat16

D_MODEL = 1024
DEPTH = 2
GRID_W = 64
D_CONV = 256
D_HYENA = 256
D_ATTN = 512
DIFF_DK = 64
N_HEADS = 4
DIFF_DV = 128
HEAD_W = 2 * DIFF_DK
IN_COLS = 2 * D_CONV + 3 * D_HYENA + 3 * D_ATTN
COL_HY = 2 * D_CONV
COL_Q = COL_HY + 3 * D_HYENA
COL_K = COL_Q + D_ATTN
COL_V = COL_K + D_ATTN
CONV_WIDTH = 31
CONV_PAD = 16
SUBLANES = 8
HYENA_EMB = 17
HYENA_BANDS = 8
HYENA_MIN_DECAY = math.log(1e-2) / 1.5
HYENA_MAX_DECAY = math.log(1e-2) / 0.3
N_EXPERTS = 64
TOP_K = 8
N_GROUPS = 8
TOPK_GROUPS = 4
GROUP_SCORE_TOP = 2
D_EXPERT = 256
ROUTED_SCALE = 2.5
ROPE_BASE = 10000.0
EPS = 1e-6
N_MOD = 6
COND_ROWS = 16

TOKEN_BLOCK = 256
EXPERT_BLOCK = 256
VMEM_LIMIT = 48 * 1024 * 1024


def _cparams(*sem):
    return pltpu.CompilerParams(dimension_semantics=sem, vmem_limit_bytes=VMEM_LIMIT)


def _rms(x, g):
    return x * lax.rsqrt(jnp.mean(x * x, axis=-1, keepdims=True) + EPS) * g


def _silu(x):
    return x * jax.nn.sigmoid(x)


def _dot(a, b):
    return jnp.dot(a, b, preferred_element_type=F32)


def _mod_kernel(c_ref, w_ref, b_ref, o_ref):
    s = _silu(c_ref[...])
    o_ref[...] = _dot(s.astype(BF16), w_ref[...].astype(BF16)) + b_ref[...]


def _modulation(cond, w_mod, b_mod):
    cols = N_MOD * D_MODEL
    cb = 1536
    return pl.pallas_call(
        _mod_kernel,
        out_shape=jax.ShapeDtypeStruct((DEPTH, COND_ROWS, cols), F32),
        grid=(DEPTH, cols // cb),
        in_specs=[
            pl.BlockSpec((COND_ROWS, D_MODEL), lambda l, j: (0, 0)),
            pl.BlockSpec((None, D_MODEL, cb), lambda l, j: (l, 0, j)),
            pl.BlockSpec((None, 1, cb), lambda l, j: (l, 0, j)),
        ],
        out_specs=pl.BlockSpec((None, COND_ROWS, cb), lambda l, j: (l, 0, j)),
        compiler_params=_cparams("arbitrary", "arbitrary"),
        name="modulation",
    )(cond, w_mod, b_mod.reshape(DEPTH, 1, cols))


def _rope_slab(x, cos, sin, lo_mask):
    swapped = jnp.where(lo_mask, pltpu.roll(x, HEAD_W - 16, axis=1), pltpu.roll(x, 16, axis=1))
    return x * cos + swapped * sin


def _premix_kernel(*refs, rope, cache):
    x_ref, mod_ref, g_ref, w_ref = refs[:4]
    pos = 4
    if rope:
        cos_ref, sin_ref = refs[pos:pos + 2]
        pos += 2
    if cache == "alias":
        pos += 2
    uc_ref, uh_ref, q_ref, k_ref, v_ref = refs[pos:pos + 5]
    pos += 5
    if cache:
        ck_ref, cv_ref = refs[pos:pos + 2]

    x = x_ref[...]
    mod = mod_ref[...]
    h = (_rms(x, g_ref[...]) * (1.0 + mod[1:2]) + mod[0:1]).astype(BF16)
    uc_ref[...] = _dot(h, w_ref[:, 0:COL_HY]).astype(BF16)
    uh_ref[...] = _dot(h, w_ref[:, COL_HY:COL_Q]).astype(BF16)
    uq = _dot(h, w_ref[:, COL_Q:COL_K]) * (DIFF_DK ** -0.5)
    uk = _dot(h, w_ref[:, COL_K:COL_V])
    uv = _dot(h, w_ref[:, COL_V:IN_COLS])
    if rope:
        cos = cos_ref[...]
        sin = sin_ref[...]
        lane = lax.broadcasted_iota(jnp.int32, cos.shape, 1)
        lo_mask = (lane % 32) < 16
        for hh in range(N_HEADS):
            sl = slice(hh * HEAD_W, (hh + 1) * HEAD_W)
            q_ref[:, sl] = _rope_slab(uq[:, sl], cos, sin, lo_mask).astype(BF16)
            k_ref[:, sl] = _rope_slab(uk[:, sl], cos, sin, lo_mask).astype(BF16)
    else:
        q_ref[...] = uq.astype(BF16)
        k_ref[...] = uk.astype(BF16)
    v_ref[...] = uv.astype(BF16)
    if cache:
        for hh in range(N_HEADS):
            sl = slice(hh * HEAD_W, (hh + 1) * HEAD_W)
            ck_ref[hh] = uk[:, sl]
            cv_ref[hh] = uv[:, sl]


def _premix(x, mod_l, g, w_in_bf, *, seq_len, cond_row0, per_seq_cond, layer,
            rope_tables=None, caches=None, want_cache=False):
    T = x.shape[0]
    tm = TOKEN_BLOCK
    nb = seq_len // tm
    n_seq = T // seq_len
    if per_seq_cond:
        mod_map = lambda i: (cond_row0 + i // nb, 0, 0)
    else:
        mod_map = lambda i: (cond_row0, 0, 0)
    in_specs = [
        pl.BlockSpec((tm, D_MODEL), lambda i: (i, 0)),
        pl.BlockSpec((None, N_MOD, D_MODEL), mod_map),
        pl.BlockSpec((1, D_MODEL), lambda i: (0, 0)),
        pl.BlockSpec((D_MODEL, IN_COLS), lambda i: (0, 0)),
    ]
    args = [x, mod_l, g.reshape(1, D_MODEL), w_in_bf]
    rope = rope_tables is not None
    if rope:
        in_specs += [pl.BlockSpec((tm, HEAD_W), lambda i: (i % nb, 0))] * 2
        args += list(rope_tables)
    out_shape = [
        jax.ShapeDtypeStruct((T, COL_HY), BF16),
        jax.ShapeDtypeStruct((T, 3 * D_HYENA), BF16),
        jax.ShapeDtypeStruct((T, D_ATTN), BF16),
        jax.ShapeDtypeStruct((T, D_ATTN), BF16),
        jax.ShapeDtypeStruct((T, D_ATTN), BF16),
    ]
    out_specs = [
        pl.BlockSpec((tm, COL_HY), lambda i: (i, 0)),
        pl.BlockSpec((tm, 3 * D_HYENA), lambda i: (i, 0)),
        pl.BlockSpec((tm, D_ATTN), lambda i: (i, 0)),
        pl.BlockSpec((tm, D_ATTN), lambda i: (i, 0)),
        pl.BlockSpec((tm, D_ATTN), lambda i: (i, 0)),
    ]
    cache = None
    aliases = {}
    if want_cache:
        cache = "new"
        cshape = (n_seq, DEPTH, N_HEADS, seq_len, HEAD_W)
        out_shape += [jax.ShapeDtypeStruct(cshape, F32)] * 2
        out_specs += [pl.BlockSpec((None, None, N_HEADS, tm, HEAD_W),
                                   lambda i: (i // nb, layer, 0, i % nb, 0))] * 2
        if caches is not None:
            cache = "alias"
            aliases = {len(args): 5, len(args) + 1: 6}
            in_specs += [pl.BlockSpec(memory_space=pl.ANY)] * 2
            args += list(caches)
    return pl.pallas_call(
        functools.partial(_premix_kernel, rope=rope, cache=cache),
        out_shape=out_shape,
        grid=(T // tm,),
        in_specs=in_specs,
        out_specs=out_specs,
        input_output_aliases=aliases,
        compiler_params=_cparams("arbitrary"),
        name="premix",
    )(*args)


def _conformer_kernel(u_ref, w_ref, cb_ref, g_ref, b_ref, o_ref, pad_ref, sh_ref, *, seq_len, rows):
    L = seq_len
    u = u_ref[...].astype(F32)
    a = u[:, :D_CONV] * jax.nn.sigmoid(u[:, D_CONV:])
    zeros = jnp.zeros((CONV_PAD, D_CONV), F32)
    pad_ref[0:CONV_PAD, :] = zeros
    pad_ref[CONV_PAD + L:2 * CONV_PAD + L, :] = zeros
    pad_ref[CONV_PAD:CONV_PAD + L, :] = a
    span = L + 2 * CONV_PAD - SUBLANES
    for r in range(SUBLANES):
        sh_ref[r, 0:span, :] = pad_ref[r:r + span, :]
    first = CONV_PAD - (CONV_WIDTH - 1) // 2

    def chunk(c, carry):
        base = pl.multiple_of(c * rows, rows)
        acc = jnp.zeros((rows, D_CONV), F32)
        for t in range(CONV_WIDTH):
            off = first + t
            tap = sh_ref[off % SUBLANES, pl.ds(base + (off // SUBLANES) * SUBLANES, rows), :]
            acc = acc + tap * w_ref[t:t + 1, :]
        acc = acc + cb_ref[...]
        mu = jnp.mean(acc, axis=-1, keepdims=True)
        d = acc - mu
        var = jnp.mean(d * d, axis=-1, keepdims=True)
        y = d * lax.rsqrt(var + EPS) * g_ref[...] + b_ref[...]
        o_ref[pl.ds(base, rows), :] = _silu(y).astype(BF16)
        return carry

    lax.fori_loop(0, L // rows, chunk, 0)


def _conformer(uc, conv_w, conv_b, ln_g, ln_b, *, seq_len):
    T = uc.shape[0]
    n_seq = T // seq_len
    vec = lambda a: a.reshape(1, D_CONV)
    return pl.pallas_call(
        functools.partial(_conformer_kernel, seq_len=seq_len, rows=64),
        out_shape=jax.ShapeDtypeStruct((T, D_CONV), BF16),
        grid=(n_seq,),
        in_specs=[
            pl.BlockSpec((seq_len, 2 * D_CONV), lambda i: (i, 0)),
            pl.BlockSpec((CONV_WIDTH, D_CONV), lambda i: (0, 0)),
            pl.BlockSpec((1, D_CONV), lambda i: (0, 0)),
            pl.BlockSpec((1, D_CONV), lambda i: (0, 0)),
            pl.BlockSpec((1, D_CONV), lambda i: (0, 0)),
        ],
        out_specs=pl.BlockSpec((seq_len, D_CONV), lambda i: (i, 0)),
        scratch_shapes=[pltpu.VMEM((seq_len + 2 * CONV_PAD, D_CONV), F32),
                        pltpu.VMEM((SUBLANES, seq_len + 2 * CONV_PAD, D_CONV), F32)],
        compiler_params=_cparams("arbitrary"),
        name="conformer",
    )(uc, conv_w, vec(conv_b), vec(ln_g), vec(ln_b))


def _dft_matrices(L):
    k = np.arange(L, dtype=np.int64)[:, None]
    m = np.arange(L, dtype=np.int64)[None, :]
    ang = 2.0 * np.pi * (((2 * k + 1) * m) % (4 * L)).astype(np.float64) / (4 * L)
    fwd = np.concatenate([np.cos(ang), np.sin(ang)], axis=0).astype(np.float32)
    return fwd, np.ascontiguousarray(fwd.T)


def _hyena_filters(L, hy_w1, hy_b1, hy_w2, hy_b2, hy_freq, hy_w3):
    hp = lax.Precision.HIGHEST
    t = jnp.linspace(0.0, 1.0, L, dtype=F32)
    bands = jnp.linspace(1e-4, HYENA_BANDS - 1, HYENA_BANDS, dtype=F32)
    ang = (2.0 * math.pi / L) * jnp.arange(L, dtype=F32)[:, None] * bands[None, :]
    feats = jnp.concatenate([t[:, None], jnp.cos(ang), -jnp.sin(ang)], axis=-1)
    hid = jnp.sin(hy_freq * (jnp.dot(feats, hy_w1, precision=hp) + hy_b1))
    hid = jnp.sin(hy_freq * (jnp.dot(hid, hy_w2, precision=hp) + hy_b2))
    filt = jnp.dot(hid, hy_w3, precision=hp).reshape(L, 2, 2, D_HYENA)
    deltas = jnp.abs(jnp.linspace(HYENA_MIN_DECAY, HYENA_MAX_DECAY, D_HYENA, dtype=F32))
    window = jnp.exp(-t[:, None] * deltas[None, :])
    return filt * window[:, None, None, :]


def _hyena_spectrum(L, fmat, *filter_params):
    hp = lax.Precision.HIGHEST
    filt = _hyena_filters(L, *filter_params)
    fwd, bwd = filt[:, 0], filt[:, 1]
    bwd = bwd.at[0].set(0.0)
    cos_m, sin_m = fmat[:L], fmat[L:]
    out = []
    for o in range(2):
        kre = jnp.dot(cos_m, fwd[:, o] + bwd[:, o], precision=hp)
        kim = -jnp.dot(sin_m, fwd[:, o] - bwd[:, o], precision=hp)
        out.append(jnp.concatenate([kre, kim], axis=0) * (1.0 / L))
    return jnp.stack(out, axis=0)


def _hyena_kernel(u_ref, sw_ref, sb_ref, hb_ref, f_ref, ft_ref, ks_ref, o_ref, pad_ref,
                  *, seq_len, n_seq):
    L = seq_len
    C = D_HYENA
    zrow = jnp.zeros((8, 3 * C), F32)
    for b in range(n_seq):
        u = u_ref[b].astype(F32)
        pad_ref[0:8, :] = zrow
        pad_ref[8 + L:16 + L, :] = zrow
        pad_ref[8:8 + L, :] = u
        s = (pad_ref[7:7 + L, :] * sw_ref[0:1, :] + u * sw_ref[1:2, :]
             + pad_ref[9:9 + L, :] * sw_ref[2:3, :] + sb_ref[...])
        z = s[:, 0:C]
        for o in range(2):
            gate = s[:, (o + 1) * C:(o + 2) * C]
            spec = _dot(f_ref[...], z.astype(BF16))
            zr, za = spec[:L], spec[L:]
            kre, kim = ks_ref[o, 0:L, :], ks_ref[o, L:2 * L, :]
            y = jnp.concatenate([zr * kre + za * kim, za * kre - zr * kim], axis=0)
            conv = _dot(ft_ref[...], y.astype(BF16))
            z = gate * (conv + hb_ref[o:o + 1, :] * z)
        o_ref[b] = z.astype(BF16)


def _hyena(uh, short_w, short_b, hy_bias, fmat_bf, fmat_t_bf, kspec, *, seq_len, seqs_per_step):
    T = uh.shape[0]
    n_seq = T // seq_len
    L = seq_len
    nb = seqs_per_step
    u3 = uh.reshape(n_seq, L, 3 * D_HYENA)
    out = pl.pallas_call(
        functools.partial(_hyena_kernel, seq_len=L, n_seq=nb),
        out_shape=jax.ShapeDtypeStruct((n_seq, L, D_HYENA), BF16),
        grid=(n_seq // nb,),
        in_specs=[
            pl.BlockSpec((nb, L, 3 * D_HYENA), lambda i: (i, 0, 0)),
            pl.BlockSpec((3, 3 * D_HYENA), lambda i: (0, 0)),
            pl.BlockSpec((1, 3 * D_HYENA), lambda i: (0, 0)),
            pl.BlockSpec((2, D_HYENA), lambda i: (0, 0)),
            pl.BlockSpec((2 * L, L), lambda i: (0, 0)),
            pl.BlockSpec((L, 2 * L), lambda i: (0, 0)),
            pl.BlockSpec((2, 2 * L, D_HYENA), lambda i: (0, 0, 0)),
        ],
        out_specs=pl.BlockSpec((nb, L, D_HYENA), lambda i: (i, 0, 0)),
        scratch_shapes=[pltpu.VMEM((L + 16, 3 * D_HYENA), F32)],
        compiler_params=_cparams("arbitrary"),
        name="hyena",
    )(u3, short_w, short_b.reshape(1, 3 * D_HYENA), hy_bias, fmat_bf, fmat_t_bf, kspec)
    return out.reshape(T, D_HYENA)


def _attn_kernel(*refs, lam_init, has_ctx):
    q_ref, k_ref, v_ref = refs[:3]
    pos = 3
    if has_ctx:
        ck_ref, cv_ref = refs[pos:pos + 2]
        pos += 2
    lam_ref, g_ref, o_ref = refs[pos:pos + 3]

    lv = lam_ref[...]
    lam = (jnp.exp(jnp.sum(lv[0:1] * lv[1:2], axis=-1, keepdims=True))
           - jnp.exp(jnp.sum(lv[2:3] * lv[3:4], axis=-1, keepdims=True)) + lam_init)

    q = q_ref[...]
    k = k_ref[...]
    v = v_ref[...]
    if has_ctx:
        ck = ck_ref[...].astype(BF16)
        cv = cv_ref[...].astype(BF16)
    lane = lax.broadcasted_iota(jnp.int32, q.shape, 1)
    zero = jnp.zeros_like(q)
    dn = (((1,), (1,)), ((), ()))
    outs = []
    for c in range(2):
        qc = jnp.where((lane >= c * DIFF_DK) & (lane < (c + 1) * DIFF_DK), q, zero)
        s = lax.dot_general(qc, k, dn, preferred_element_type=F32)
        m = jnp.max(s, axis=-1, keepdims=True)
        if has_ctx:
            sc = lax.dot_general(qc, ck, dn, preferred_element_type=F32)
            m = jnp.maximum(m, jnp.max(sc, axis=-1, keepdims=True))
        p = jnp.exp(s - m)
        den = jnp.sum(p, axis=-1, keepdims=True)
        acc = _dot(p.astype(BF16), v)
        if has_ctx:
            pc = jnp.exp(sc - m)
            den = den + jnp.sum(pc, axis=-1, keepdims=True)
            acc = acc + _dot(pc.astype(BF16), cv)
        outs.append(acc / den)
    o = outs[0] - lam * outs[1]
    o_ref[...] = (_rms(o, g_ref[...]) * (1.0 - lam_init)).astype(BF16)


def _attention(q, k, v, lam_vecs, subln_g, *, seq_len, layer, ctx=None):
    T = q.shape[0]
    L = seq_len
    n_seq = T // L
    tq = TOKEN_BLOCK
    nq = L // tq
    lam_init = 0.8 - 0.6 * math.exp(-0.3 * layer)
    q3, k3, v3 = (a.reshape(n_seq, L, D_ATTN) for a in (q, k, v))
    in_specs = [
        pl.BlockSpec((None, tq, HEAD_W), lambda b, h, i: (b, i, h)),
        pl.BlockSpec((None, L, HEAD_W), lambda b, h, i: (b, 0, h)),
        pl.BlockSpec((None, L, HEAD_W), lambda b, h, i: (b, 0, h)),
    ]
    args = [q3, k3, v3]
    if ctx is not None:
        past = ctx[0].shape[3]
        in_specs += [pl.BlockSpec((None, None, None, past, HEAD_W),
                                  lambda b, h, i: (b, layer, h, 0, 0))] * 2
        args += list(ctx)
    in_specs += [
        pl.BlockSpec((4, DIFF_DK), lambda b, h, i: (0, 0)),
        pl.BlockSpec((1, DIFF_DV), lambda b, h, i: (0, 0)),
    ]
    args += [lam_vecs, subln_g.reshape(1, DIFF_DV)]
    out = pl.pallas_call(
        functools.partial(_attn_kernel, lam_init=lam_init, has_ctx=ctx is not None),
        out_shape=jax.ShapeDtypeStruct((n_seq, L, D_ATTN), BF16),
        grid=(n_seq, N_HEADS, nq),
        in_specs=in_specs,
        out_specs=pl.BlockSpec((None, tq, HEAD_W), lambda b, h, i: (b, i, h)),
        compiler_params=_cparams("arbitrary", "arbitrary", "arbitrary"),
        name="diff_attention",
    )(*args)
    return out.reshape(T, D_ATTN)


def _split_bf16(a):
    hi = a.astype(BF16)
    lo = (a - hi.astype(F32)).astype(BF16)
    return hi, lo


def _postmix_kernel(ya_ref, yb_ref, o_ref, x_ref, mod_ref, gpost_ref, gpre_ref, w_ref, rw_ref,
                    x1_ref, h2_ref, lg_ref):
    y = (_dot(ya_ref[...], w_ref[0:D_CONV, :])
         + _dot(yb_ref[...], w_ref[D_CONV:D_CONV + D_HYENA, :])
         + _dot(o_ref[...], w_ref[D_CONV + D_HYENA:, :]))
    mod = mod_ref[...]
    x1 = x_ref[...] + mod[2:3] * _rms(y, gpost_ref[...])
    x1_ref[...] = x1
    h2 = _rms(x1, gpre_ref[...]) * (1.0 + mod[4:5]) + mod[3:4]
    h2_ref[...] = h2.astype(BF16)
    h_hi, h_lo = _split_bf16(h2)
    r_hi, r_lo = _split_bf16(rw_ref[...])
    lg_ref[...] = _dot(h_hi, r_hi) + (_dot(h_hi, r_lo) + _dot(h_lo, r_hi))


def _postmix(ya, yb, o, x, mod_l, g_post, g_pre, w_out_bf, router_w, *, seq_len, cond_row0,
             per_seq_cond):
    T = x.shape[0]
    tm = TOKEN_BLOCK
    nb = seq_len // tm
    if per_seq_cond:
        mod_map = lambda i: (cond_row0 + i // nb, 0, 0)
    else:
        mod_map = lambda i: (cond_row0, 0, 0)
    row = lambda w: pl.BlockSpec((tm, w), lambda i: (i, 0))
    full = lambda a: pl.BlockSpec(a.shape, lambda i: (0,) * a.ndim)
    gp, ge = g_post.reshape(1, D_MODEL), g_pre.reshape(1, D_MODEL)
    return pl.pallas_call(
        _postmix_kernel,
        out_shape=[
            jax.ShapeDtypeStruct((T, D_MODEL), F32),
            jax.ShapeDtypeStruct((T, D_MODEL), BF16),
            jax.ShapeDtypeStruct((T, N_EXPERTS), F32),
        ],
        grid=(T // tm,),
        in_specs=[row(D_CONV), row(D_HYENA), row(D_ATTN), row(D_MODEL),
                  pl.BlockSpec((None, N_MOD, D_MODEL), mod_map),
                  full(gp), full(ge), full(w_out_bf), full(router_w)],
        out_specs=[row(D_MODEL), row(D_MODEL), row(N_EXPERTS)],
        compiler_params=_cparams("arbitrary"),
        name="postmix",
    )(ya, yb, o, x, mod_l, gp, ge, w_out_bf, router_w)


def _expert_kernel(be_ref, nu_ref, x_ref, wg_ref, wu_ref, wd_ref, y_ref, wg_bf, wu_bf, wd_bf):
    i = pl.program_id(0)
    prev = be_ref[jnp.maximum(i - 1, 0)]
    fresh = (i == 0) | (be_ref[i] != prev)

    @pl.when(fresh)
    def _():
        wg_bf[...] = wg_ref[...].astype(BF16)
        wu_bf[...] = wu_ref[...].astype(BF16)
        wd_bf[...] = wd_ref[...].astype(BF16)

    @pl.when(i < nu_ref[0])
    def _():
        x = x_ref[...]
        hid = _silu(_dot(x, wg_bf[...])) * _dot(x, wu_bf[...])
        y_ref[...] = _dot(hid.astype(BF16), wd_bf[...]).astype(BF16)

    @pl.when(i >= nu_ref[0])
    def _():
        y_ref[...] = jnp.zeros_like(y_ref)


def _experts(x_buf, block_e, n_used, wg, wu, wd):
    P = x_buf.shape[0]
    n_blocks = P // EXPERT_BLOCK
    grid_spec = pltpu.PrefetchScalarGridSpec(
        num_scalar_prefetch=2,
        grid=(n_blocks,),
        in_specs=[
            pl.BlockSpec((EXPERT_BLOCK, D_MODEL), lambda i, be, nu: (i, 0)),
            pl.BlockSpec((None, D_MODEL, D_EXPERT), lambda i, be, nu: (be[i], 0, 0)),
            pl.BlockSpec((None, D_MODEL, D_EXPERT), lambda i, be, nu: (be[i], 0, 0)),
            pl.BlockSpec((None, D_EXPERT, D_MODEL), lambda i, be, nu: (be[i], 0, 0)),
        ],
        out_specs=pl.BlockSpec((EXPERT_BLOCK, D_MODEL), lambda i, be, nu: (i, 0)),
        scratch_shapes=[
            pltpu.VMEM((D_MODEL, D_EXPERT), BF16),
            pltpu.VMEM((D_MODEL, D_EXPERT), BF16),
            pltpu.VMEM((D_EXPERT, D_MODEL), BF16),
        ],
    )
    return pl.pallas_call(
        _expert_kernel,
        out_shape=jax.ShapeDtypeStruct((P, D_MODEL), BF16),
        grid_spec=grid_spec,
        compiler_params=_cparams("arbitrary"),
        name="experts",
    )(block_e, n_used, x_buf, wg, wu, wd)


def _route(logits, router_b):
    T = logits.shape[0]
    s = jax.nn.sigmoid(logits)
    sel = s + router_b
    per_group = N_EXPERTS // N_GROUPS
    gscore = lax.top_k(sel.reshape(T, N_GROUPS, per_group), GROUP_SCORE_TOP)[0].sum(-1)
    _, gidx = lax.top_k(gscore, TOPK_GROUPS)
    gmask = jnp.any(gidx[:, :, None] == jnp.arange(N_GROUPS)[None, None, :], axis=1)
    masked = jnp.where(jnp.repeat(gmask, per_group, axis=1), sel, -jnp.inf)
    _, eidx = lax.top_k(masked, TOP_K)
    w = jnp.take_along_axis(s, eidx, axis=1)
    w = w / jnp.sum(w, axis=-1, keepdims=True) * ROUTED_SCALE

    N = T * TOP_K
    flat_e = eidx.reshape(N)
    flat_tok = jnp.repeat(jnp.arange(T, dtype=jnp.int32), TOP_K)
    order = jnp.argsort(flat_e)
    se, stok = flat_e[order], flat_tok[order]
    counts = jnp.bincount(flat_e, length=N_EXPERTS)
    starts = jnp.cumsum(counts) - counts
    pcounts = ((counts + EXPERT_BLOCK - 1) // EXPERT_BLOCK) * EXPERT_BLOCK
    pend = jnp.cumsum(pcounts)
    pstart = pend - pcounts
    dest = (pstart[se] + (jnp.arange(N) - starts[se])).astype(jnp.int32)
    n_blocks = N // EXPERT_BLOCK + N_EXPERTS
    P = n_blocks * EXPERT_BLOCK
    src_tok = jnp.zeros((P,), jnp.int32).at[dest].set(stok)
    pos = jnp.zeros((N,), jnp.int32).at[order].set(dest).reshape(T, TOP_K)
    block_e = jnp.minimum(
        jnp.searchsorted(pend, jnp.arange(n_blocks) * EXPERT_BLOCK, side='right'),
        N_EXPERTS - 1).astype(jnp.int32)
    n_used = (pend[-1] // EXPERT_BLOCK).astype(jnp.int32).reshape(1)
    return w, pos, src_tok, block_e, n_used


def _ffn_out_kernel(h_ref, r_ref, x_ref, mod_ref, g_ref, wg_ref, wu_ref, wd_ref, o_ref):
    h = h_ref[...]
    hid = _silu(_dot(h, wg_ref[...])) * _dot(h, wu_ref[...])
    y = _dot(hid.astype(BF16), wd_ref[...]) + r_ref[...]
    mod = mod_ref[...]
    o_ref[...] = x_ref[...] + mod[5:6] * _rms(y, g_ref[...])


def _ffn_out(h2, routed, x1, mod_l, g_post, swg, swu, swd, *, seq_len, cond_row0, per_seq_cond):
    T = x1.shape[0]
    tm = TOKEN_BLOCK
    nb = seq_len // tm
    if per_seq_cond:
        mod_map = lambda i: (cond_row0 + i // nb, 0, 0)
    else:
        mod_map = lambda i: (cond_row0, 0, 0)
    row = lambda w: pl.BlockSpec((tm, w), lambda i: (i, 0))
    full = lambda a: pl.BlockSpec(a.shape, lambda i: (0,) * a.ndim)
    g = g_post.reshape(1, D_MODEL)
    return pl.pallas_call(
        _ffn_out_kernel,
        out_shape=jax.ShapeDtypeStruct((T, D_MODEL), F32),
        grid=(T // tm,),
        in_specs=[row(D_MODEL), row(D_MODEL), row(D_MODEL),
                  pl.BlockSpec((None, N_MOD, D_MODEL), mod_map),
                  full(g), full(swg), full(swu), full(swd)],
        out_specs=row(D_MODEL),
        compiler_params=_cparams("arbitrary"),
        name="ffn_out",
    )(h2, routed, x1, mod_l, g, swg, swu, swd)


def _rope_tables(L):
    rows = L // GRID_W
    row = jnp.repeat(jnp.arange(rows), GRID_W).astype(F32)
    col = jnp.tile(jnp.arange(GRID_W), rows).astype(F32)
    n = DIFF_DK // 4
    inv = ROPE_BASE ** (-jnp.arange(n, dtype=F32) / n)
    ar, ac = row[:, None] * inv[None, :], col[:, None] * inv[None, :]
    cos64 = jnp.concatenate([jnp.cos(ar), jnp.cos(ar), jnp.cos(ac), jnp.cos(ac)], axis=-1)
    sin64 = jnp.concatenate([-jnp.sin(ar), jnp.sin(ar), -jnp.sin(ac), jnp.sin(ac)], axis=-1)
    return jnp.tile(cos64, (1, 2)), jnp.tile(sin64, (1, 2))


def kernel(x_prompt, x_sample, cache_k, cache_v, c, c_ctx, w_mod, b_mod, g_pre_mix, g_post_mix, g_pre_ffn, g_post_ffn, w_in, w_out, conv_w, conv_b, conv_ln_g, conv_ln_b, hy_short_w, hy_short_b, hy_w1, hy_b1, hy_w2, hy_b2, hy_freq, hy_w3, hy_bias, lam_q1, lam_k1, lam_q2, lam_k2, subln_g, router_w, router_b, exp_w_gate, exp_w_up, exp_w_down, sh_w_gate, sh_w_up, sh_w_down):
    n_p, l_p, _ = x_prompt.shape
    n_s, l_s, _ = x_sample.shape
    t_p, t_s = n_p * l_p, n_s * l_s

    cond = jnp.zeros((COND_ROWS, D_MODEL), F32).at[0].set(c_ctx).at[1:1 + n_s].set(c)
    mod = _modulation(cond, w_mod, b_mod).reshape(DEPTH, COND_ROWS, N_MOD, D_MODEL)
    rope = _rope_tables(l_s)

    streams = {
        "p": dict(x=x_prompt.reshape(t_p, D_MODEL), L=l_p, row0=0, per_seq=False, hy_nb=8),
        "s": dict(x=x_sample.reshape(t_s, D_MODEL), L=l_s, row0=1, per_seq=True, hy_nb=1),
    }
    dft = {}
    for st in streams.values():
        fwd, fwd_t = _dft_matrices(st["L"])
        st["fmat32"] = jnp.asarray(fwd)
        st["fmat"] = jnp.asarray(fwd).astype(BF16)
        st["fmat_t"] = jnp.asarray(fwd_t).astype(BF16)

    caches = None
    for l in range(DEPTH):
        w_in_bf = w_in[l].astype(BF16)
        w_out_bf = w_out[l].astype(BF16)
        swg, swu, swd = (a[l].astype(BF16) for a in (sh_w_gate, sh_w_up, sh_w_down))
        lam_vecs = jnp.stack([lam_q1[l], lam_k1[l], lam_q2[l], lam_k2[l]], axis=0)
        filt_params = (hy_w1[l], hy_b1[l], hy_w2[l], hy_b2[l], hy_freq[l], hy_w3[l])
        mid = {}
        for name, st in streams.items():
            L = st["L"]
            kw = dict(seq_len=L, cond_row0=st["row0"], per_seq_cond=st["per_seq"])
            if name == "p":
                uc, uh, q, k, v, ck, cv = _premix(st["x"], mod[l], g_pre_mix[l], w_in_bf, layer=l,
                                                  caches=caches, want_cache=True, **kw)
                caches = (ck, cv)
                ctx = None
            else:
                uc, uh, q, k, v = _premix(st["x"], mod[l], g_pre_mix[l], w_in_bf, layer=l,
                                          rope_tables=rope, **kw)
                ctx = (cache_k, cache_v)
            ya = _conformer(uc, conv_w[l], conv_b[l], conv_ln_g[l], conv_ln_b[l], seq_len=L)
            kspec = _hyena_spectrum(L, st["fmat32"], *filt_params)
            yb = _hyena(uh, hy_short_w[l], hy_short_b[l], hy_bias[l], st["fmat"], st["fmat_t"],
                        kspec, seq_len=L, seqs_per_step=st["hy_nb"])
            o = _attention(q, k, v, lam_vecs, subln_g[l], seq_len=L, layer=l, ctx=ctx)
            mid[name] = _postmix(ya, yb, o, st["x"], mod[l], g_post_mix[l], g_pre_ffn[l],
                                 w_out_bf, router_w[l], **kw)

        h2 = jnp.concatenate([mid["p"][1], mid["s"][1]], axis=0)
        logits = jnp.concatenate([mid["p"][2], mid["s"][2]], axis=0)
        w, pos, src_tok, block_e, n_used = _route(logits, router_b[l])
        x_buf = jnp.take(h2, src_tok, axis=0)
        y_buf = _experts(x_buf, block_e, n_used, exp_w_gate[l], exp_w_up[l], exp_w_down[l])
        routed = jnp.sum(jnp.take(y_buf, pos, axis=0).astype(F32) * w[:, :, None], axis=1)
        offs = 0
        for name, st in streams.items():
            t = st["x"].shape[0]
            kw = dict(seq_len=st["L"], cond_row0=st["row0"], per_seq_cond=st["per_seq"])
            st["x"] = _ffn_out(mid[name][1], routed[offs:offs + t], mid[name][0], mod[l],
                               g_post_ffn[l], swg, swu, swd, **kw)
            offs += t

    y_prompt = streams["p"]["x"].reshape(n_p, l_p, D_MODEL)
    y_sample = streams["s"]["x"].reshape(n_s, l_s, D_MODEL)
    return (y_prompt, y_sample, caches[0], caches[1])
```

```python
import functools
import math

import numpy as np
import jax
import jax.numpy as jnp
from jax import lax
from jax.experimental import pallas as pl
from jax.experimental.pallas import tpu as pltpu

F32 = jnp.float32
BF16 = jnp.bfloat16

D_MODEL = 1024
DEPTH = 2
GRID_W = 64
D_CONV = 256
D_HYENA = 256
D_ATTN = 512
DIFF_DK = 64
N_HEADS = 4
DIFF_DV = 128
HEAD_W = 2 * DIFF_DK
IN_COLS = 2 * D_CONV + 3 * D_HYENA + 3 * D_ATTN
COL_HY = 2 * D_CONV
COL_Q = COL_HY + 3 * D_HYENA
COL_K = COL_Q + D_ATTN
COL_V = COL_K + D_ATTN
CONV_WIDTH = 31
CONV_PAD = 16
SUBLANES = 8
HYENA_EMB = 17
HYENA_BANDS = 8
HYENA_MIN_DECAY = math.log(1e-2) / 1.5
HYENA_MAX_DECAY = math.log(1e-2) / 0.3
N_EXPERTS = 64
TOP_K = 8
N_GROUPS = 8
TOPK_GROUPS = 4
GROUP_SCORE_TOP = 2
D_EXPERT = 256
ROUTED_SCALE = 2.5
ROPE_BASE = 10000.0
EPS = 1e-6
N_MOD = 6
COND_ROWS = 16

HALF = D_MODEL // 2
LANES = 128
ROUTE_BLOCK = 512
TOKEN_BLOCK = 256
EXPERT_BLOCK = 256
VMEM_LIMIT = 48 * 1024 * 1024


def _cparams(*sem):
    return pltpu.CompilerParams(dimension_semantics=sem, vmem_limit_bytes=VMEM_LIMIT)


def _rms(x, g):
    return x * lax.rsqrt(jnp.mean(x * x, axis=-1, keepdims=True) + EPS) * g


def _silu(x):
    return x * jax.nn.sigmoid(x)


def _dot(a, b):
    return jnp.dot(a, b, preferred_element_type=F32)


def _mod_kernel(c_ref, w_ref, b_ref, o_ref):
    s = _silu(c_ref[...])
    o_ref[...] = _dot(s.astype(BF16), w_ref[...].astype(BF16)) + b_ref[...]


def _modulation(cond, w_mod, b_mod):
    cols = N_MOD * D_MODEL
    cb = 1536
    return pl.pallas_call(
        _mod_kernel,
        out_shape=jax.ShapeDtypeStruct((DEPTH, COND_ROWS, cols), F32),
        grid=(DEPTH, cols // cb),
        in_specs=[
            pl.BlockSpec((COND_ROWS, D_MODEL), lambda l, j: (0, 0)),
            pl.BlockSpec((None, D_MODEL, cb), lambda l, j: (l, 0, j)),
            pl.BlockSpec((None, 1, cb), lambda l, j: (l, 0, j)),
        ],
        out_specs=pl.BlockSpec((None, COND_ROWS, cb), lambda l, j: (l, 0, j)),
        compiler_params=_cparams("arbitrary", "arbitrary"),
        name="modulation",
    )(cond, w_mod, b_mod.reshape(DEPTH, 1, cols))


def _rope_slab(x, cos, sin, lo_mask):
    swapped = jnp.where(lo_mask, pltpu.roll(x, HEAD_W - 16, axis=1), pltpu.roll(x, 16, axis=1))
    return x * cos + swapped * sin


def _premix_kernel(*refs, rope, cache, layer):
    x_ref, mod_ref, g_ref, w_ref = refs[:4]
    pos = 4
    if rope:
        cos_ref, sin_ref = refs[pos:pos + 2]
        pos += 2
    if cache == "alias":
        pos += 2
    uc_ref, uh_ref, q_ref, k_ref, v_ref = refs[pos:pos + 5]
    pos += 5
    if cache:
        ck_ref, cv_ref = refs[pos:pos + 2]

    x = x_ref[...]
    mod = mod_ref[...]
    h = (_rms(x, g_ref[...]) * (1.0 + mod[1:2]) + mod[0:1]).astype(BF16)
    uc_ref[...] = _dot(h, w_ref[:, 0:COL_HY]).astype(BF16)
    uh_ref[...] = _dot(h, w_ref[:, COL_HY:COL_Q]).astype(BF16)
    uq = _dot(h, w_ref[:, COL_Q:COL_K]) * (DIFF_DK ** -0.5)
    uk = _dot(h, w_ref[:, COL_K:COL_V])
    uv = _dot(h, w_ref[:, COL_V:IN_COLS])
    if rope:
        cos = cos_ref[...]
        sin = sin_ref[...]
        lane = lax.broadcasted_iota(jnp.int32, cos.shape, 1)
        lo_mask = (lane % 32) < 16
        for hh in range(N_HEADS):
            sl = slice(hh * HEAD_W, (hh + 1) * HEAD_W)
            q_ref[:, sl] = _rope_slab(uq[:, sl], cos, sin, lo_mask).astype(BF16)
            k_ref[:, sl] = _rope_slab(uk[:, sl], cos, sin, lo_mask).astype(BF16)
    else:
        q_ref[...] = uq.astype(BF16)
        k_ref[...] = uk.astype(BF16)
    v_ref[...] = uv.astype(BF16)
    if cache == "alias":
        for hh in range(N_HEADS):
            sl = slice(hh * HEAD_W, (hh + 1) * HEAD_W)
            ck_ref[hh] = uk[:, sl]
            cv_ref[hh] = uv[:, sl]
    elif cache == "new":
        for d in range(DEPTH):
            for hh in range(N_HEADS):
                sl = slice(hh * HEAD_W, (hh + 1) * HEAD_W)
                ck_ref[d, hh] = uk[:, sl] if d == layer else jnp.zeros_like(uk[:, sl])
                cv_ref[d, hh] = uv[:, sl] if d == layer else jnp.zeros_like(uv[:, sl])


def _premix(x, mod_l, g, w_in_bf, *, seq_len, cond_row0, per_seq_cond, layer, n_tokens, tok0,
            rope_tables=None, caches=None, want_cache=False):
    T = n_tokens
    tm = TOKEN_BLOCK
    nb = seq_len // tm
    n_seq = T // seq_len
    b0 = tok0 // tm
    if per_seq_cond:
        mod_map = lambda i: (cond_row0 + i // nb, 0, 0)
    else:
        mod_map = lambda i: (cond_row0, 0, 0)
    in_specs = [
        pl.BlockSpec((tm, D_MODEL), lambda i: (i + b0, 0)),
        pl.BlockSpec((None, N_MOD, D_MODEL), mod_map),
        pl.BlockSpec((1, D_MODEL), lambda i: (0, 0)),
        pl.BlockSpec((D_MODEL, IN_COLS), lambda i: (0, 0)),
    ]
    args = [x, mod_l, g.reshape(1, D_MODEL), w_in_bf]
    rope = rope_tables is not None
    if rope:
        in_specs += [pl.BlockSpec((tm, HEAD_W), lambda i: (i % nb, 0))] * 2
        args += list(rope_tables)
    out_shape = [
        jax.ShapeDtypeStruct((T, COL_HY), BF16),
        jax.ShapeDtypeStruct((T, 3 * D_HYENA), BF16),
        jax.ShapeDtypeStruct((T, D_ATTN), BF16),
        jax.ShapeDtypeStruct((T, D_ATTN), BF16),
        jax.ShapeDtypeStruct((T, D_ATTN), BF16),
    ]
    out_specs = [
        pl.BlockSpec((tm, COL_HY), lambda i: (i, 0)),
        pl.BlockSpec((tm, 3 * D_HYENA), lambda i: (i, 0)),
        pl.BlockSpec((tm, D_ATTN), lambda i: (i, 0)),
        pl.BlockSpec((tm, D_ATTN), lambda i: (i, 0)),
        pl.BlockSpec((tm, D_ATTN), lambda i: (i, 0)),
    ]
    cache = None
    aliases = {}
    if want_cache:
        cache = "new"
        cshape = (n_seq, DEPTH, N_HEADS, seq_len, HEAD_W)
        out_shape += [jax.ShapeDtypeStruct(cshape, F32)] * 2
        if caches is None:
            out_specs += [pl.BlockSpec((None, DEPTH, N_HEADS, tm, HEAD_W),
                                       lambda i: (i // nb, 0, 0, i % nb, 0))] * 2
        else:
            cache = "alias"
            out_specs += [pl.BlockSpec((None, None, N_HEADS, tm, HEAD_W),
                                       lambda i: (i // nb, layer, 0, i % nb, 0))] * 2
            aliases = {len(args): 5, len(args) + 1: 6}
            in_specs += [pl.BlockSpec(memory_space=pl.ANY)] * 2
            args += list(caches)
    return pl.pallas_call(
        functools.partial(_premix_kernel, rope=rope, cache=cache, layer=layer),
        out_shape=out_shape,
        grid=(T // tm,),
        in_specs=in_specs,
        out_specs=out_specs,
        input_output_aliases=aliases,
        compiler_params=_cparams("arbitrary"),
        name="premix",
    )(*args)


def _conformer_kernel(u_ref, w_ref, cb_ref, g_ref, b_ref, o_ref, pad_ref, sh_ref, *, seq_len, rows):
    L = seq_len
    u = u_ref[...].astype(F32)
    a = u[:, :D_CONV] * jax.nn.sigmoid(u[:, D_CONV:])
    zeros = jnp.zeros((CONV_PAD, D_CONV), F32)
    pad_ref[0:CONV_PAD, :] = zeros
    pad_ref[CONV_PAD + L:2 * CONV_PAD + L, :] = zeros
    pad_ref[CONV_PAD:CONV_PAD + L, :] = a
    span = L + 2 * CONV_PAD - SUBLANES
    for r in range(SUBLANES):
        sh_ref[r, 0:span, :] = pad_ref[r:r + span, :]
    first = CONV_PAD - (CONV_WIDTH - 1) // 2

    def chunk(c, carry):
        base = pl.multiple_of(c * rows, rows)
        acc = jnp.zeros((rows, D_CONV), F32)
        for t in range(CONV_WIDTH):
            off = first + t
            tap = sh_ref[off % SUBLANES, pl.ds(base + (off // SUBLANES) * SUBLANES, rows), :]
            acc = acc + tap * w_ref[t:t + 1, :]
        acc = acc + cb_ref[...]
        mu = jnp.mean(acc, axis=-1, keepdims=True)
        d = acc - mu
        var = jnp.mean(d * d, axis=-1, keepdims=True)
        y = d * lax.rsqrt(var + EPS) * g_ref[...] + b_ref[...]
        o_ref[pl.ds(base, rows), :] = _silu(y).astype(BF16)
        return carry

    lax.fori_loop(0, L // rows, chunk, 0)


def _conformer(uc, conv_w, conv_b, ln_g, ln_b, *, seq_len):
    T = uc.shape[0]
    n_seq = T // seq_len
    vec = lambda a: a.reshape(1, D_CONV)
    return pl.pallas_call(
        functools.partial(_conformer_kernel, seq_len=seq_len, rows=64),
        out_shape=jax.ShapeDtypeStruct((T, D_CONV), BF16),
        grid=(n_seq,),
        in_specs=[
            pl.BlockSpec((seq_len, 2 * D_CONV), lambda i: (i, 0)),
            pl.BlockSpec((CONV_WIDTH, D_CONV), lambda i: (0, 0)),
            pl.BlockSpec((1, D_CONV), lambda i: (0, 0)),
            pl.BlockSpec((1, D_CONV), lambda i: (0, 0)),
            pl.BlockSpec((1, D_CONV), lambda i: (0, 0)),
        ],
        out_specs=pl.BlockSpec((seq_len, D_CONV), lambda i: (i, 0)),
        scratch_shapes=[pltpu.VMEM((seq_len + 2 * CONV_PAD, D_CONV), F32),
                        pltpu.VMEM((SUBLANES, seq_len + 2 * CONV_PAD, D_CONV), F32)],
        compiler_params=_cparams("arbitrary"),
        name="conformer",
    )(uc, conv_w, vec(conv_b), vec(ln_g), vec(ln_b))


def _dft_matrices(L):
    k = np.arange(L, dtype=np.int64)[:, None]
    m = np.arange(L, dtype=np.int64)[None, :]
    ang = 2.0 * np.pi * (((2 * k + 1) * m) % (4 * L)).astype(np.float64) / (4 * L)
    fwd = np.concatenate([np.cos(ang), np.sin(ang)], axis=0).astype(np.float32)
    return fwd, np.ascontiguousarray(fwd.T)


def _hyena_filters(L, hy_w1, hy_b1, hy_w2, hy_b2, hy_freq, hy_w3):
    hp = lax.Precision.HIGHEST
    t = jnp.linspace(0.0, 1.0, L, dtype=F32)
    bands = jnp.linspace(1e-4, HYENA_BANDS - 1, HYENA_BANDS, dtype=F32)
    ang = (2.0 * math.pi / L) * jnp.arange(L, dtype=F32)[:, None] * bands[None, :]
    feats = jnp.concatenate([t[:, None], jnp.cos(ang), -jnp.sin(ang)], axis=-1)
    hid = jnp.sin(hy_freq * (jnp.dot(feats, hy_w1, precision=hp) + hy_b1))
    hid = jnp.sin(hy_freq * (jnp.dot(hid, hy_w2, precision=hp) + hy_b2))
    filt = jnp.dot(hid, hy_w3, precision=hp).reshape(L, 2, 2, D_HYENA)
    deltas = jnp.abs(jnp.linspace(HYENA_MIN_DECAY, HYENA_MAX_DECAY, D_HYENA, dtype=F32))
    window = jnp.exp(-t[:, None] * deltas[None, :])
    return filt * window[:, None, None, :]


def _hyena_spectrum(L, fmat, *filter_params):
    hp = lax.Precision.HIGHEST
    filt = _hyena_filters(L, *filter_params)
    fwd, bwd = filt[:, 0], filt[:, 1]
    bwd = bwd.at[0].set(0.0)
    cos_m, sin_m = fmat[:L], fmat[L:]
    out = []
    for o in range(2):
        kre = jnp.dot(cos_m, fwd[:, o] + bwd[:, o], precision=hp)
        kim = -jnp.dot(sin_m, fwd[:, o] - bwd[:, o], precision=hp)
        out.append(jnp.concatenate([kre, kim], axis=0) * (1.0 / L))
    return jnp.stack(out, axis=0)


def _hyena_kernel(u_ref, sw_ref, sb_ref, hb_ref, f_ref, ft_ref, ks_ref, o_ref, pad_ref,
                  *, seq_len, n_seq):
    L = seq_len
    C = D_HYENA
    zrow = jnp.zeros((8, 3 * C), F32)
    for b in range(n_seq):
        u = u_ref[b].astype(F32)
        pad_ref[0:8, :] = zrow
        pad_ref[8 + L:16 + L, :] = zrow
        pad_ref[8:8 + L, :] = u
        s = (pad_ref[7:7 + L, :] * sw_ref[0:1, :] + u * sw_ref[1:2, :]
             + pad_ref[9:9 + L, :] * sw_ref[2:3, :] + sb_ref[...])
        z = s[:, 0:C]
        for o in range(2):
            gate = s[:, (o + 1) * C:(o + 2) * C]
            spec = _dot(f_ref[...], z.astype(BF16))
            zr, za = spec[:L], spec[L:]
            kre, kim = ks_ref[o, 0:L, :], ks_ref[o, L:2 * L, :]
            y = jnp.concatenate([zr * kre + za * kim, za * kre - zr * kim], axis=0)
            conv = _dot(ft_ref[...], y.astype(BF16))
            z = gate * (conv + hb_ref[o:o + 1, :] * z)
        o_ref[b] = z.astype(BF16)


def _hyena(uh, short_w, short_b, hy_bias, fmat_bf, fmat_t_bf, kspec, *, seq_len, seqs_per_step):
    T = uh.shape[0]
    n_seq = T // seq_len
    L = seq_len
    nb = seqs_per_step
    u3 = uh.reshape(n_seq, L, 3 * D_HYENA)
    out = pl.pallas_call(
        functools.partial(_hyena_kernel, seq_len=L, n_seq=nb),
        out_shape=jax.ShapeDtypeStruct((n_seq, L, D_HYENA), BF16),
        grid=(n_seq // nb,),
        in_specs=[
            pl.BlockSpec((nb, L, 3 * D_HYENA), lambda i: (i, 0, 0)),
            pl.BlockSpec((3, 3 * D_HYENA), lambda i: (0, 0)),
            pl.BlockSpec((1, 3 * D_HYENA), lambda i: (0, 0)),
            pl.BlockSpec((2, D_HYENA), lambda i: (0, 0)),
            pl.BlockSpec((2 * L, L), lambda i: (0, 0)),
            pl.BlockSpec((L, 2 * L), lambda i: (0, 0)),
            pl.BlockSpec((2, 2 * L, D_HYENA), lambda i: (0, 0, 0)),
        ],
        out_specs=pl.BlockSpec((nb, L, D_HYENA), lambda i: (i, 0, 0)),
        scratch_shapes=[pltpu.VMEM((L + 16, 3 * D_HYENA), F32)],
        compiler_params=_cparams("arbitrary"),
        name="hyena",
    )(u3, short_w, short_b.reshape(1, 3 * D_HYENA), hy_bias, fmat_bf, fmat_t_bf, kspec)
    return out.reshape(T, D_HYENA)


def _attn_kernel(*refs, lam_init, has_ctx):
    q_ref, k_ref, v_ref = refs[:3]
    pos = 3
    if has_ctx:
        ck_ref, cv_ref = refs[pos:pos + 2]
        pos += 2
    lam_ref, g_ref, o_ref = refs[pos:pos + 3]

    lv = lam_ref[...]
    lam = (jnp.exp(jnp.sum(lv[0:1] * lv[1:2], axis=-1, keepdims=True))
           - jnp.exp(jnp.sum(lv[2:3] * lv[3:4], axis=-1, keepdims=True)) + lam_init)

    q = q_ref[...]
    k = k_ref[...]
    v = v_ref[...]
    if has_ctx:
        ck = ck_ref[...].astype(BF16)
        cv = cv_ref[...].astype(BF16)
    lane = lax.broadcasted_iota(jnp.int32, q.shape, 1)
    zero = jnp.zeros_like(q)
    dn = (((1,), (1,)), ((), ()))
    outs = []
    for c in range(2):
        qc = jnp.where((lane >= c * DIFF_DK) & (lane < (c + 1) * DIFF_DK), q, zero)
        s = lax.dot_general(qc, k, dn, preferred_element_type=F32)
        m = jnp.max(s, axis=-1, keepdims=True)
        if has_ctx:
            sc = lax.dot_general(qc, ck, dn, preferred_element_type=F32)
            m = jnp.maximum(m, jnp.max(sc, axis=-1, keepdims=True))
        p = jnp.exp(s - m)
        den = jnp.sum(p, axis=-1, keepdims=True)
        acc = _dot(p.astype(BF16), v)
        if has_ctx:
            pc = jnp.exp(sc - m)
            den = den + jnp.sum(pc, axis=-1, keepdims=True)
            acc = acc + _dot(pc.astype(BF16), cv)
        outs.append(acc / den)
    o = outs[0] - lam * outs[1]
    o_ref[...] = (_rms(o, g_ref[...]) * (1.0 - lam_init)).astype(BF16)


def _attention(q, k, v, lam_vecs, subln_g, *, seq_len, layer, ctx=None):
    T = q.shape[0]
    L = seq_len
    n_seq = T // L
    tq = TOKEN_BLOCK
    nq = L // tq
    lam_init = 0.8 - 0.6 * math.exp(-0.3 * layer)
    q3, k3, v3 = (a.reshape(n_seq, L, D_ATTN) for a in (q, k, v))
    in_specs = [
        pl.BlockSpec((None, tq, HEAD_W), lambda b, h, i: (b, i, h)),
        pl.BlockSpec((None, L, HEAD_W), lambda b, h, i: (b, 0, h)),
        pl.BlockSpec((None, L, HEAD_W), lambda b, h, i: (b, 0, h)),
    ]
    args = [q3, k3, v3]
    if ctx is not None:
        past = ctx[0].shape[3]
        in_specs += [pl.BlockSpec((None, None, None, past, HEAD_W),
                                  lambda b, h, i: (b, layer, h, 0, 0))] * 2
        args += list(ctx)
    in_specs += [
        pl.BlockSpec((4, DIFF_DK), lambda b, h, i: (0, 0)),
        pl.BlockSpec((1, DIFF_DV), lambda b, h, i: (0, 0)),
    ]
    args += [lam_vecs, subln_g.reshape(1, DIFF_DV)]
    out = pl.pallas_call(
        functools.partial(_attn_kernel, lam_init=lam_init, has_ctx=ctx is not None),
        out_shape=jax.ShapeDtypeStruct((n_seq, L, D_ATTN), BF16),
        grid=(n_seq, N_HEADS, nq),
        in_specs=in_specs,
        out_specs=pl.BlockSpec((None, tq, HEAD_W), lambda b, h, i: (b, i, h)),
        compiler_params=_cparams("arbitrary", "arbitrary", "arbitrary"),
        name="diff_attention",
    )(*args)
    return out.reshape(T, D_ATTN)


def _split_bf16(a):
    hi = a.astype(BF16)
    lo = (a - hi.astype(F32)).astype(BF16)
    return hi, lo


def _pack_pair(a, b):
    ua = lax.bitcast_convert_type(a.astype(BF16).astype(F32), jnp.uint32)
    ub = lax.bitcast_convert_type(b.astype(BF16).astype(F32), jnp.uint32)
    return (ua >> 16) | ub


def _unpack_pair(w):
    a = lax.bitcast_convert_type(w << 16, F32)
    b = lax.bitcast_convert_type(w & jnp.uint32(0xFFFF0000), F32)
    return a, b


def _postmix_kernel(*refs, stream_blocks):
    ns = len(stream_blocks)
    x_ref, mod_ref, gpost_ref, gpre_ref, w_ref, rw_ref, x1_ref, h2_ref, lg_ref = refs[3 * ns:]
    i = pl.program_id(0)

    def body(ya_ref, yb_ref, o_ref):
        y = (_dot(ya_ref[...], w_ref[0:D_CONV, :])
             + _dot(yb_ref[...], w_ref[D_CONV:D_CONV + D_HYENA, :])
             + _dot(o_ref[...], w_ref[D_CONV + D_HYENA:, :]))
        mod = mod_ref[...]
        x1 = x_ref[...] + mod[2:3] * _rms(y, gpost_ref[...])
        x1_ref[...] = x1
        h2 = _rms(x1, gpre_ref[...]) * (1.0 + mod[4:5]) + mod[3:4]
        h2_ref[...] = _pack_pair(h2[:, :HALF], h2[:, HALF:])
        h_hi, h_lo = _split_bf16(h2)
        r_hi, r_lo = _split_bf16(rw_ref[...])
        dn = (((1,), (1,)), ((), ()))
        mm = lambda a, b: lax.dot_general(a, b, dn, preferred_element_type=F32)
        lg_ref[...] = mm(r_hi, h_hi) + (mm(r_lo, h_hi) + mm(r_hi, h_lo))

    lo = 0
    for j, nbk in enumerate(stream_blocks):
        pl.when((i >= lo) & (i < lo + nbk))(functools.partial(body, *refs[3 * j:3 * j + 3]))
        lo += nbk


def _postmix(parts, x_all, mod_l, g_post, g_pre, w_out_bf, router_wt, *, mod_map):
    t_all = x_all.shape[0]
    tm = TOKEN_BLOCK
    full = lambda a: pl.BlockSpec(a.shape, lambda i: (0,) * a.ndim)
    gp, ge = g_post.reshape(1, D_MODEL), g_pre.reshape(1, D_MODEL)
    stream_blocks = tuple(p[0].shape[0] // tm for p in parts)
    in_specs, args = [], []
    lo = 0
    for part, nbk in zip(parts, stream_blocks):
        local = lambda i, lo=lo, nbk=nbk: (jnp.clip(i - lo, 0, nbk - 1), 0)
        in_specs += [pl.BlockSpec((tm, D_CONV), local), pl.BlockSpec((tm, D_HYENA), local),
                     pl.BlockSpec((tm, D_ATTN), local)]
        args += list(part)
        lo += nbk
    in_specs += [pl.BlockSpec((tm, D_MODEL), lambda i: (i, 0)),
                 pl.BlockSpec((None, N_MOD, D_MODEL), mod_map),
                 full(gp), full(ge), full(w_out_bf), full(router_wt)]
    args += [x_all, mod_l, gp, ge, w_out_bf, router_wt]
    return pl.pallas_call(
        functools.partial(_postmix_kernel, stream_blocks=stream_blocks),
        out_shape=[
            jax.ShapeDtypeStruct((t_all, D_MODEL), F32),
            jax.ShapeDtypeStruct((t_all, HALF), jnp.uint32),
            jax.ShapeDtypeStruct((N_EXPERTS, t_all), F32),
        ],
        grid=(t_all // tm,),
        in_specs=in_specs,
        out_specs=[pl.BlockSpec((tm, D_MODEL), lambda i: (i, 0)),
                   pl.BlockSpec((tm, HALF), lambda i: (i, 0)),
                   pl.BlockSpec((N_EXPERTS, tm), lambda i: (0, i))],
        compiler_params=_cparams("arbitrary"),
        name="postmix",
    )(*args)


def _first_max(vals, idx, sentinel):
    m = jnp.max(vals, axis=0, keepdims=True)
    i = jnp.min(jnp.where(vals == m, idx, sentinel), axis=0, keepdims=True)
    return m, i


def _route_topk_kernel(lg_ref, rb_ref, eidx_ref, w_ref, cnt_ref):
    step = pl.program_id(0)
    per = N_EXPERTS // N_GROUPS
    n = lg_ref.shape[1]
    s = jax.nn.sigmoid(lg_ref[...])
    sel = s + rb_ref[...]
    sub = lax.broadcasted_iota(jnp.int32, (per, n), 0)
    neg = jnp.float32(-jnp.inf)

    gs = []
    for g in range(N_GROUPS):
        xg = sel[g * per:(g + 1) * per]
        total = jnp.zeros((1, n), F32)
        for _ in range(GROUP_SCORE_TOP):
            m, i = _first_max(xg, sub, per)
            total = total + m
            xg = jnp.where(sub == i, neg, xg)
        gs.append(total)
    gscore = jnp.concatenate(gs, axis=0)
    gkeep = jnp.zeros((N_GROUPS, n), F32)
    for _ in range(TOPK_GROUPS):
        _, gi = _first_max(gscore, sub, N_GROUPS)
        hit = sub == gi
        gkeep = jnp.where(hit, 1.0, gkeep)
        gscore = jnp.where(hit, neg, gscore)

    eio = [sub + g * per for g in range(N_GROUPS)]
    sg = [s[g * per:(g + 1) * per] for g in range(N_GROUPS)]
    mk = [jnp.where(gkeep[g:g + 1] > 0.5, sel[g * per:(g + 1) * per], neg) for g in range(N_GROUPS)]
    oh = [jnp.zeros((per, n), F32) for _ in range(N_GROUPS)]
    idx_rows, s_rows = [], []
    for _ in range(TOP_K):
        best = functools.reduce(jnp.maximum, mk)
        m = jnp.max(best, axis=0, keepdims=True)
        cand = functools.reduce(jnp.minimum,
                                [jnp.where(mk[g] == m, eio[g], N_EXPERTS) for g in range(N_GROUPS)])
        idx = jnp.min(cand, axis=0, keepdims=True)
        acc = jnp.zeros((per, n), F32)
        for g in range(N_GROUPS):
            hit = eio[g] == idx
            acc = acc + jnp.where(hit, sg[g], 0.0)
            mk[g] = jnp.where(hit, neg, mk[g])
            oh[g] = jnp.where(hit, 1.0, oh[g])
        idx_rows.append(idx)
        s_rows.append(jnp.sum(acc, axis=0, keepdims=True))
    eidx_ref[...] = jnp.concatenate(idx_rows, axis=0)
    sk = jnp.concatenate(s_rows, axis=0)
    w_ref[...] = sk / jnp.sum(sk, axis=0, keepdims=True) * ROUTED_SCALE

    @pl.when(step == 0)
    def _():
        cnt_ref[...] = jnp.zeros_like(cnt_ref)

    tot = jnp.concatenate([jnp.sum(o, axis=1, keepdims=True) for o in oh], axis=0)
    cnt_ref[...] = cnt_ref[...] + jnp.broadcast_to(tot, cnt_ref.shape).astype(jnp.int32)


def _route_topk(logits_t, router_b):
    t_all = logits_t.shape[1]
    rb = ROUTE_BLOCK
    return pl.pallas_call(
        _route_topk_kernel,
        out_shape=[
            jax.ShapeDtypeStruct((TOP_K, t_all), jnp.int32),
            jax.ShapeDtypeStruct((TOP_K, t_all), F32),
            jax.ShapeDtypeStruct((N_EXPERTS, LANES), jnp.int32),
        ],
        grid=(t_all // rb,),
        in_specs=[pl.BlockSpec((N_EXPERTS, rb), lambda i: (0, i)),
                  pl.BlockSpec((N_EXPERTS, 1), lambda i: (0, 0))],
        out_specs=[pl.BlockSpec((TOP_K, rb), lambda i: (0, i)),
                   pl.BlockSpec((TOP_K, rb), lambda i: (0, i)),
                   pl.BlockSpec((N_EXPERTS, LANES), lambda i: (0, 0))],
        compiler_params=_cparams("arbitrary"),
        name="route_topk",
    )(logits_t, router_b.reshape(N_EXPERTS, 1))


def _route_dest_kernel(eidx_ref, ps_ref, dest_ref, carry_ref):
    step = pl.program_id(0)
    per = N_EXPERTS // N_GROUPS
    n = eidx_ref.shape[1]

    @pl.when(step == 0)
    def _():
        carry_ref[...] = jnp.zeros_like(carry_ref)

    eidx = eidx_ref[...]
    sub = lax.broadcasted_iota(jnp.int32, (per, n), 0)
    rows = []
    for g in range(N_GROUPS):
        eio = sub + g * per
        o = jnp.zeros((per, n), F32)
        for k in range(TOP_K):
            o = jnp.where(eio == eidx[k:k + 1], 1.0, o)
        rows.append(o)
    onehot = jnp.concatenate(rows, axis=0)
    r = lax.broadcasted_iota(jnp.int32, (n, n), 0)
    c = lax.broadcasted_iota(jnp.int32, (n, n), 1)
    before = jnp.where(r < c, 1.0, 0.0).astype(BF16)
    rank = _dot(onehot.astype(BF16), before)
    base = (rank + carry_ref[:, 0:1]).astype(jnp.int32) + ps_ref[...]
    out = []
    for k in range(TOP_K):
        acc = jnp.zeros((per, n), jnp.int32)
        for g in range(N_GROUPS):
            acc = acc + jnp.where(sub + g * per == eidx[k:k + 1], base[g * per:(g + 1) * per], 0)
        out.append(jnp.sum(acc, axis=0, keepdims=True))
    dest_ref[...] = jnp.concatenate(out, axis=0)
    tot = jnp.sum(onehot, axis=1, keepdims=True)
    carry_ref[...] = carry_ref[...] + jnp.broadcast_to(tot, carry_ref.shape)


def _route_dest(eidx_t, pstart):
    t_all = eidx_t.shape[1]
    rb = ROUTE_BLOCK
    return pl.pallas_call(
        _route_dest_kernel,
        out_shape=jax.ShapeDtypeStruct((TOP_K, t_all), jnp.int32),
        grid=(t_all // rb,),
        in_specs=[pl.BlockSpec((TOP_K, rb), lambda i: (0, i)),
                  pl.BlockSpec((N_EXPERTS, 1), lambda i: (0, 0))],
        out_specs=pl.BlockSpec((TOP_K, rb), lambda i: (0, i)),
        scratch_shapes=[pltpu.VMEM((N_EXPERTS, LANES), F32)],
        compiler_params=_cparams("arbitrary"),
        name="route_dest",
    )(eidx_t, pstart.reshape(N_EXPERTS, 1))


def _row_copy(src_hbm, src_row, dst_hbm, dst_row, sem):
    return pltpu.make_async_copy(src_hbm.at[pl.ds(src_row, 1)], dst_hbm.at[pl.ds(dst_row, 1)], sem)


def _wait_rows(hbm, sem, n_rows):
    pltpu.make_async_copy(hbm.at[pl.ds(0, n_rows)], hbm.at[pl.ds(0, n_rows)], sem).wait()


def _zero_padding(cnt_ref, ps_ref, nu_ref, zero_ref, x_hbm, zsem, n_blocks, *, wait):
    def copy(row, size):
        cp = pltpu.make_async_copy(zero_ref.at[pl.ds(0, size)], x_hbm.at[pl.ds(row, size)], zsem)
        if wait:
            cp.wait()
        else:
            cp.start()

    def per_expert(e, carry):
        cnt = cnt_ref[e]
        pad = (EXPERT_BLOCK - cnt % EXPERT_BLOCK) % EXPERT_BLOCK
        row = ps_ref[e] + cnt
        head = pad % SUBLANES
        for j in range(SUBLANES - 1):
            pl.when(j < head)(functools.partial(copy, row + j, 1))
        row = row + head
        size = EXPERT_BLOCK // 2
        while size >= SUBLANES:
            take = (pad & size) != 0
            pl.when(take)(functools.partial(copy, pl.multiple_of(row, SUBLANES), size))
            row = row + jnp.where(take, size, 0)
            size //= 2
        return carry

    lax.fori_loop(0, N_EXPERTS, per_expert, 0)

    def per_block(j, carry):
        copy(pl.multiple_of(j * EXPERT_BLOCK, EXPERT_BLOCK), EXPERT_BLOCK)
        return carry

    lax.fori_loop(nu_ref[0], n_blocks, per_block, 0)


def _dispatch_kernel(dest_ref, cnt_ref, ps_ref, nu_ref, h_hbm, x_hbm, zero_ref, sem, zsem):
    step = pl.program_id(0)
    nsteps = pl.num_programs(0)
    n = dest_ref.shape[1]
    zero_args = (cnt_ref, ps_ref, nu_ref, zero_ref, x_hbm, zsem, x_hbm.shape[0] // EXPERT_BLOCK)

    @pl.when(step == 0)
    def _():
        zero_ref[...] = jnp.zeros_like(zero_ref)
        _zero_padding(*zero_args, wait=False)

    def per_token(t, carry):
        for k in range(TOP_K):
            _row_copy(h_hbm, step * n + t, x_hbm, dest_ref[k, t], sem).start()
        return carry

    lax.fori_loop(0, n, per_token, 0)

    @pl.when(step > 0)
    def _():
        _wait_rows(x_hbm, sem, n * TOP_K)

    @pl.when(step == nsteps - 1)
    def _():
        _wait_rows(x_hbm, sem, n * TOP_K)
        _zero_padding(*zero_args, wait=True)


def _dispatch(dest_t, counts, pstart, n_used, h2p, n_blocks):
    t_all = h2p.shape[0]
    sb = TOKEN_BLOCK
    smem = lambda: pl.BlockSpec(memory_space=pltpu.SMEM)
    return pl.pallas_call(
        _dispatch_kernel,
        out_shape=jax.ShapeDtypeStruct((n_blocks * EXPERT_BLOCK, HALF), jnp.uint32),
        grid=(t_all // sb,),
        in_specs=[pl.BlockSpec((TOP_K, sb), lambda i: (0, i), memory_space=pltpu.SMEM),
                  smem(), smem(), smem(),
                  pl.BlockSpec(memory_space=pl.ANY)],
        out_specs=pl.BlockSpec(memory_space=pl.ANY),
        scratch_shapes=[pltpu.VMEM((EXPERT_BLOCK, HALF), jnp.uint32),
                        pltpu.SemaphoreType.DMA, pltpu.SemaphoreType.DMA],
        compiler_params=_cparams("arbitrary"),
        name="dispatch",
    )(dest_t, counts, pstart, n_used, h2p)


def _expert_kernel(be_ref, nu_ref, x_ref, wg_ref, wu_ref, wd_ref, y_ref, wg_bf, wu_bf, wd_bf):
    i = pl.program_id(0)
    prev = be_ref[jnp.maximum(i - 1, 0)]
    fresh = (i == 0) | (be_ref[i] != prev)

    @pl.when(fresh)
    def _():
        wg_bf[...] = wg_ref[...].astype(BF16)
        wu_bf[...] = wu_ref[...].astype(BF16)
        wd_bf[...] = wd_ref[...].astype(BF16)

    @pl.when(i < nu_ref[0])
    def _():
        y = _swiglu_packed(x_ref[...], wg_bf, wu_bf, wd_bf)
        y_ref[...] = _pack_pair(y[:, :HALF], y[:, HALF:])

    @pl.when(i >= nu_ref[0])
    def _():
        y_ref[...] = jnp.zeros_like(y_ref)


def _swiglu_packed(xp, wg_ref, wu_ref, wd_ref):
    xa, xb = _unpack_pair(xp)
    xa, xb = xa.astype(BF16), xb.astype(BF16)
    gate = _dot(xa, wg_ref[0:HALF, :]) + _dot(xb, wg_ref[HALF:, :])
    up = _dot(xa, wu_ref[0:HALF, :]) + _dot(xb, wu_ref[HALF:, :])
    return _dot((_silu(gate) * up).astype(BF16), wd_ref[...])


def _experts(x_buf, block_e, n_used, wg, wu, wd):
    n_blocks = x_buf.shape[0] // EXPERT_BLOCK
    last = lambda i, nu: jnp.minimum(i, nu[0] - 1)
    grid_spec = pltpu.PrefetchScalarGridSpec(
        num_scalar_prefetch=2,
        grid=(n_blocks,),
        in_specs=[
            pl.BlockSpec((EXPERT_BLOCK, HALF), lambda i, be, nu: (last(i, nu), 0)),
            pl.BlockSpec((None, D_MODEL, D_EXPERT), lambda i, be, nu: (be[i], 0, 0)),
            pl.BlockSpec((None, D_MODEL, D_EXPERT), lambda i, be, nu: (be[i], 0, 0)),
            pl.BlockSpec((None, D_EXPERT, D_MODEL), lambda i, be, nu: (be[i], 0, 0)),
        ],
        out_specs=pl.BlockSpec((EXPERT_BLOCK, HALF), lambda i, be, nu: (i, 0)),
        scratch_shapes=[
            pltpu.VMEM((D_MODEL, D_EXPERT), BF16),
            pltpu.VMEM((D_MODEL, D_EXPERT), BF16),
            pltpu.VMEM((D_EXPERT, D_MODEL), BF16),
        ],
    )
    return pl.pallas_call(
        _expert_kernel,
        out_shape=jax.ShapeDtypeStruct(x_buf.shape, jnp.uint32),
        grid_spec=grid_spec,
        compiler_params=_cparams("arbitrary"),
        name="experts",
    )(block_e, n_used, x_buf, wg, wu, wd)


def _block_plan(counts, n_blocks):
    nblk = (counts + EXPERT_BLOCK - 1) // EXPERT_BLOCK
    pend = jnp.cumsum(nblk)
    pstart = ((pend - nblk) * EXPERT_BLOCK).astype(jnp.int32)
    block_e = jnp.minimum(jnp.searchsorted(pend, jnp.arange(n_blocks), side='right'),
                          N_EXPERTS - 1).astype(jnp.int32)
    return pstart, block_e, pend[-1:].astype(jnp.int32)


def _ffn_out_kernel(dcur_ref, dnxt_ref, h_ref, w_ref, x_ref, mod_ref, g_ref, wg_ref, wu_ref, wd_ref,
                    y_hbm, o_ref, rows_ref, sem):
    step = pl.program_id(0)
    nsteps = pl.num_programs(0)
    n = h_ref.shape[0]

    def gather(d_ref, slot):
        def per_token(t, carry):
            for k in range(TOP_K):
                pltpu.make_async_copy(y_hbm.at[pl.ds(d_ref[k, t], 1)],
                                      rows_ref.at[slot, k, pl.ds(t, 1)], sem.at[slot]).start()
            return carry
        lax.fori_loop(0, n, per_token, 0)

    slot = step % 2

    @pl.when(step == 0)
    def _():
        gather(dcur_ref, 0)

    @pl.when(step + 1 < nsteps)
    def _():
        gather(dnxt_ref, 1 - slot)

    ys = _swiglu_packed(h_ref[...], wg_ref, wu_ref, wd_ref)
    pltpu.make_async_copy(rows_ref.at[slot], rows_ref.at[slot], sem.at[slot]).wait()
    w = w_ref[...]
    ya, yb = ys[:, :HALF], ys[:, HALF:]
    for k in range(TOP_K):
        ra, rb = _unpack_pair(rows_ref[slot, k])
        ya = ya + w[:, k:k + 1] * ra
        yb = yb + w[:, k:k + 1] * rb
    ms = (jnp.sum(ya * ya, axis=-1, keepdims=True)
          + jnp.sum(yb * yb, axis=-1, keepdims=True)) * (1.0 / D_MODEL)
    inv = lax.rsqrt(ms + EPS)
    mod = mod_ref[...]
    g = g_ref[...]
    o_ref[:, 0:HALF] = x_ref[:, 0:HALF] + mod[5:6, 0:HALF] * (ya * inv * g[:, 0:HALF])
    o_ref[:, HALF:] = x_ref[:, HALF:] + mod[5:6, HALF:] * (yb * inv * g[:, HALF:])


def _ffn_out(dest_t, h2p, w_rows, x1, mod_l, g_post, swg, swu, swd, y_buf, *, mod_map):
    t_all = x1.shape[0]
    tm = TOKEN_BLOCK
    nsteps = t_all // tm
    row = lambda w: pl.BlockSpec((tm, w), lambda i: (i, 0))
    full = lambda a: pl.BlockSpec(a.shape, lambda i: (0,) * a.ndim)
    g = g_post.reshape(1, D_MODEL)
    return pl.pallas_call(
        _ffn_out_kernel,
        out_shape=jax.ShapeDtypeStruct((t_all, D_MODEL), F32),
        grid=(nsteps,),
        in_specs=[pl.BlockSpec((TOP_K, tm), lambda i: (0, i), memory_space=pltpu.SMEM),
                  pl.BlockSpec((TOP_K, tm), lambda i: (0, jnp.minimum(i + 1, nsteps - 1)),
                               memory_space=pltpu.SMEM),
                  row(HALF), row(TOP_K), row(D_MODEL),
                  pl.BlockSpec((None, N_MOD, D_MODEL), mod_map),
                  full(g), full(swg), full(swu), full(swd),
                  pl.BlockSpec(memory_space=pl.ANY)],
        out_specs=row(D_MODEL),
        scratch_shapes=[pltpu.VMEM((2, TOP_K, tm, HALF), jnp.uint32),
                        pltpu.SemaphoreType.DMA((2,))],
        compiler_params=_cparams("arbitrary"),
        name="ffn_out",
    )(dest_t, dest_t, h2p, w_rows, x1, mod_l, g, swg, swu, swd, y_buf)


def _rope_tables(L):
    rows = L // GRID_W
    row = jnp.repeat(jnp.arange(rows), GRID_W).astype(F32)
    col = jnp.tile(jnp.arange(GRID_W), rows).astype(F32)
    n = DIFF_DK // 4
    inv = ROPE_BASE ** (-jnp.arange(n, dtype=F32) / n)
    ar, ac = row[:, None] * inv[None, :], col[:, None] * inv[None, :]
    cos64 = jnp.concatenate([jnp.cos(ar), jnp.cos(ar), jnp.cos(ac), jnp.cos(ac)], axis=-1)
    sin64 = jnp.concatenate([-jnp.sin(ar), jnp.sin(ar), -jnp.sin(ac), jnp.sin(ac)], axis=-1)
    return jnp.tile(cos64, (1, 2)), jnp.tile(sin64, (1, 2))


def kernel(x_prompt, x_sample, cache_k, cache_v, c, c_ctx, w_mod, b_mod, g_pre_mix, g_post_mix, g_pre_ffn, g_post_ffn, w_in, w_out, conv_w, conv_b, conv_ln_g, conv_ln_b, hy_short_w, hy_short_b, hy_w1, hy_b1, hy_w2, hy_b2, hy_freq, hy_w3, hy_bias, lam_q1, lam_k1, lam_q2, lam_k2, subln_g, router_w, router_b, exp_w_gate, exp_w_up, exp_w_down, sh_w_gate, sh_w_up, sh_w_down):
    n_p, l_p, _ = x_prompt.shape
    n_s, l_s, _ = x_sample.shape
    t_p, t_s = n_p * l_p, n_s * l_s

    cond = jnp.zeros((COND_ROWS, D_MODEL), F32).at[0].set(c_ctx).at[1:1 + n_s].set(c)
    mod = _modulation(cond, w_mod, b_mod).reshape(DEPTH, COND_ROWS, N_MOD, D_MODEL)
    rope = _rope_tables(l_s)

    t_all = t_p + t_s
    streams = {
        "p": dict(T=t_p, tok0=0, L=l_p, row0=0, per_seq=False, hy_nb=8),
        "s": dict(T=t_s, tok0=t_p, L=l_s, row0=1, per_seq=True, hy_nb=1),
    }
    for st in streams.values():
        fwd, fwd_t = _dft_matrices(st["L"])
        st["fmat32"] = jnp.asarray(fwd)
        st["fmat"] = jnp.asarray(fwd).astype(BF16)
        st["fmat_t"] = jnp.asarray(fwd_t).astype(BF16)
    x_all = jnp.concatenate([x_prompt.reshape(t_p, D_MODEL), x_sample.reshape(t_s, D_MODEL)], axis=0)
    nb_p, nb_s = t_p // TOKEN_BLOCK, l_s // TOKEN_BLOCK
    merged_mod_map = lambda i: (jnp.where(i < nb_p, 0, 1 + (i - nb_p) // nb_s), 0, 0)
    n_blocks = t_all * TOP_K // EXPERT_BLOCK + N_EXPERTS

    caches = None
    for l in range(DEPTH):
        w_in_bf = w_in[l].astype(BF16)
        w_out_bf = w_out[l].astype(BF16)
        swg, swu, swd = (a[l].astype(BF16) for a in (sh_w_gate, sh_w_up, sh_w_down))
        lam_vecs = jnp.stack([lam_q1[l], lam_k1[l], lam_q2[l], lam_k2[l]], axis=0)
        filt_params = (hy_w1[l], hy_b1[l], hy_w2[l], hy_b2[l], hy_freq[l], hy_w3[l])
        parts = []
        for name, st in streams.items():
            L = st["L"]
            kw = dict(seq_len=L, cond_row0=st["row0"], per_seq_cond=st["per_seq"])
            if name == "p":
                uc, uh, q, k, v, ck, cv = _premix(x_all, mod[l], g_pre_mix[l], w_in_bf, layer=l,
                                                  n_tokens=st["T"], tok0=st["tok0"],
                                                  caches=caches, want_cache=True, **kw)
                caches = (ck, cv)
                ctx = None
            else:
                uc, uh, q, k, v = _premix(x_all, mod[l], g_pre_mix[l], w_in_bf, layer=l,
                                          n_tokens=st["T"], tok0=st["tok0"],
                                          rope_tables=rope, **kw)
                ctx = (cache_k, cache_v)
            ya = _conformer(uc, conv_w[l], conv_b[l], conv_ln_g[l], conv_ln_b[l], seq_len=L)
            kspec = _hyena_spectrum(L, st["fmat32"], *filt_params)
            yb = _hyena(uh, hy_short_w[l], hy_short_b[l], hy_bias[l], st["fmat"], st["fmat_t"],
                        kspec, seq_len=L, seqs_per_step=st["hy_nb"])
            o = _attention(q, k, v, lam_vecs, subln_g[l], seq_len=L, layer=l, ctx=ctx)
            parts.append((ya, yb, o))
        x1, h2p, logits_t = _postmix(parts, x_all, mod[l], g_post_mix[l], g_pre_ffn[l], w_out_bf,
                                     router_w[l].T, mod_map=merged_mod_map)

        eidx_t, w_t, cnt = _route_topk(logits_t, router_b[l])
        counts = cnt[:, 0]
        pstart, block_e, n_used = _block_plan(counts, n_blocks)
        dest_t = _route_dest(eidx_t, pstart)
        x_buf = _dispatch(dest_t, counts, pstart, n_used, h2p, n_blocks)
        y_buf = _experts(x_buf, block_e, n_used, exp_w_gate[l], exp_w_up[l], exp_w_down[l])
        x_all = _ffn_out(dest_t, h2p, w_t.T, x1, mod[l], g_post_ffn[l], swg, swu, swd, y_buf,
                         mod_map=merged_mod_map)

    y_prompt = x_all[:t_p].reshape(n_p, l_p, D_MODEL)
    y_sample = x_all[t_p:].reshape(n_s, l_s, D_MODEL)
    return (y_prompt, y_sample, caches[0], caches[1])
```

```python
import functools
import math

import numpy as np
import jax
import jax.numpy as jnp
from jax import lax
from jax.experimental import pallas as pl
from jax.experimental.pallas import tpu as pltpu

F32 = jnp.float32
BF16 = jnp.bfloat16

D_MODEL = 1024
DEPTH = 2
GRID_W = 64
D_CONV = 256
D_HYENA = 256
D_ATTN = 512
DIFF_DK = 64
N_HEADS = 4
DIFF_DV = 128
HEAD_W = 2 * DIFF_DK
IN_COLS = 2 * D_CONV + 3 * D_HYENA + 3 * D_ATTN
COL_HY = 2 * D_CONV
COL_Q = COL_HY + 3 * D_HYENA
COL_K = COL_Q + D_ATTN
COL_V = COL_K + D_ATTN
CONV_WIDTH = 31
CONV_PAD = 16
SUBLANES = 8
HYENA_EMB = 17
FEAT_PAD = 32
HYENA_BANDS = 8
HYENA_MIN_DECAY = math.log(1e-2) / 1.5
HYENA_MAX_DECAY = math.log(1e-2) / 0.3
N_EXPERTS = 64
TOP_K = 8
N_GROUPS = 8
TOPK_GROUPS = 4
GROUP_SCORE_TOP = 2
D_EXPERT = 256
ROUTED_SCALE = 2.5
ROPE_BASE = 10000.0
EPS = 1e-6
N_MOD = 6
COND_ROWS = 16

HALF = D_MODEL // 2
LANES = 128
ROUTE_BLOCK = 512
TOKEN_BLOCK = 256
EXPERT_BLOCK = 256
VMEM_LIMIT = 48 * 1024 * 1024


def _cparams(*sem):
    return pltpu.CompilerParams(dimension_semantics=sem, vmem_limit_bytes=VMEM_LIMIT)


def _rms(x, g):
    return x * lax.rsqrt(jnp.mean(x * x, axis=-1, keepdims=True) + EPS) * g


def _silu(x):
    return x * jax.nn.sigmoid(x)


def _dot(a, b):
    return jnp.dot(a, b, preferred_element_type=F32)


def _mod_kernel(c_ref, w_ref, b_ref, o_ref):
    s = _silu(c_ref[...])
    o_ref[...] = _dot(s.astype(BF16), w_ref[...].astype(BF16)) + b_ref[...]


def _modulation(cond, w_mod, b_mod):
    cols = N_MOD * D_MODEL
    cb = 1536
    return pl.pallas_call(
        _mod_kernel,
        out_shape=jax.ShapeDtypeStruct((DEPTH, COND_ROWS, cols), F32),
        grid=(DEPTH, cols // cb),
        in_specs=[
            pl.BlockSpec((COND_ROWS, D_MODEL), lambda l, j: (0, 0)),
            pl.BlockSpec((None, D_MODEL, cb), lambda l, j: (l, 0, j)),
            pl.BlockSpec((None, 1, cb), lambda l, j: (l, 0, j)),
        ],
        out_specs=pl.BlockSpec((None, COND_ROWS, cb), lambda l, j: (l, 0, j)),
        compiler_params=_cparams("arbitrary", "arbitrary"),
        name="modulation",
    )(cond, w_mod, b_mod.reshape(DEPTH, 1, cols))


def _rope_slab(x, cos, sin, lo_mask):
    swapped = jnp.where(lo_mask, pltpu.roll(x, HEAD_W - 16, axis=1), pltpu.roll(x, 16, axis=1))
    return x * cos + swapped * sin


def _premix_kernel(*refs, rope, cache, layer):
    x_ref, mod_ref, g_ref, w_ref = refs[:4]
    pos = 4
    if rope:
        cos_ref, sin_ref = refs[pos:pos + 2]
        pos += 2
    if cache == "alias":
        pos += 2
    uc_ref, uh_ref, q_ref, k_ref, v_ref = refs[pos:pos + 5]
    pos += 5
    if cache:
        ck_ref, cv_ref = refs[pos:pos + 2]

    x = x_ref[...]
    mod = mod_ref[...]
    h = (_rms(x, g_ref[...]) * (1.0 + mod[1:2]) + mod[0:1]).astype(BF16)
    uc_ref[...] = _dot(h, w_ref[:, 0:COL_HY]).astype(BF16)
    uh_ref[...] = _dot(h, w_ref[:, COL_HY:COL_Q]).astype(BF16)
    uq = _dot(h, w_ref[:, COL_Q:COL_K]) * (DIFF_DK ** -0.5)
    uk = _dot(h, w_ref[:, COL_K:COL_V])
    uv = _dot(h, w_ref[:, COL_V:IN_COLS])
    if rope:
        cos = cos_ref[...]
        sin = sin_ref[...]
        lane = lax.broadcasted_iota(jnp.int32, cos.shape, 1)
        lo_mask = (lane % 32) < 16
        for hh in range(N_HEADS):
            sl = slice(hh * HEAD_W, (hh + 1) * HEAD_W)
            q_ref[:, sl] = _rope_slab(uq[:, sl], cos, sin, lo_mask).astype(BF16)
            k_ref[:, sl] = _rope_slab(uk[:, sl], cos, sin, lo_mask).astype(BF16)
    else:
        q_ref[...] = uq.astype(BF16)
        k_ref[...] = uk.astype(BF16)
    v_ref[...] = uv.astype(BF16)
    if cache == "alias":
        for hh in range(N_HEADS):
            sl = slice(hh * HEAD_W, (hh + 1) * HEAD_W)
            ck_ref[hh] = uk[:, sl]
            cv_ref[hh] = uv[:, sl]
    elif cache == "new":
        for d in range(DEPTH):
            for hh in range(N_HEADS):
                sl = slice(hh * HEAD_W, (hh + 1) * HEAD_W)
                ck_ref[d, hh] = uk[:, sl] if d == layer else jnp.zeros_like(uk[:, sl])
                cv_ref[d, hh] = uv[:, sl] if d == layer else jnp.zeros_like(uv[:, sl])


def _premix(x, mod_l, g, w_in_bf, *, seq_len, cond_row0, per_seq_cond, layer, n_tokens, tok0,
            rope_tables=None, caches=None, want_cache=False):
    T = n_tokens
    tm = TOKEN_BLOCK
    nb = seq_len // tm
    n_seq = T // seq_len
    b0 = tok0 // tm
    if per_seq_cond:
        mod_map = lambda i: (cond_row0 + i // nb, 0, 0)
    else:
        mod_map = lambda i: (cond_row0, 0, 0)
    in_specs = [
        pl.BlockSpec((tm, D_MODEL), lambda i: (i + b0, 0)),
        pl.BlockSpec((None, N_MOD, D_MODEL), mod_map),
        pl.BlockSpec((1, D_MODEL), lambda i: (0, 0)),
        pl.BlockSpec((D_MODEL, IN_COLS), lambda i: (0, 0)),
    ]
    args = [x, mod_l, g.reshape(1, D_MODEL), w_in_bf]
    rope = rope_tables is not None
    if rope:
        in_specs += [pl.BlockSpec((tm, HEAD_W), lambda i: (i % nb, 0))] * 2
        args += list(rope_tables)
    out_shape = [
        jax.ShapeDtypeStruct((T, COL_HY), BF16),
        jax.ShapeDtypeStruct((T, 3 * D_HYENA), BF16),
        jax.ShapeDtypeStruct((T, D_ATTN), BF16),
        jax.ShapeDtypeStruct((T, D_ATTN), BF16),
        jax.ShapeDtypeStruct((T, D_ATTN), BF16),
    ]
    out_specs = [
        pl.BlockSpec((tm, COL_HY), lambda i: (i, 0)),
        pl.BlockSpec((tm, 3 * D_HYENA), lambda i: (i, 0)),
        pl.BlockSpec((tm, D_ATTN), lambda i: (i, 0)),
        pl.BlockSpec((tm, D_ATTN), lambda i: (i, 0)),
        pl.BlockSpec((tm, D_ATTN), lambda i: (i, 0)),
    ]
    cache = None
    aliases = {}
    if want_cache:
        cache = "new"
        cshape = (n_seq, DEPTH, N_HEADS, seq_len, HEAD_W)
        out_shape += [jax.ShapeDtypeStruct(cshape, F32)] * 2
        if caches is None:
            out_specs += [pl.BlockSpec((None, DEPTH, N_HEADS, tm, HEAD_W),
                                       lambda i: (i // nb, 0, 0, i % nb, 0))] * 2
        else:
            cache = "alias"
            out_specs += [pl.BlockSpec((None, None, N_HEADS, tm, HEAD_W),
                                       lambda i: (i // nb, layer, 0, i % nb, 0))] * 2
            aliases = {len(args): 5, len(args) + 1: 6}
            in_specs += [pl.BlockSpec(memory_space=pl.ANY)] * 2
            args += list(caches)
    return pl.pallas_call(
        functools.partial(_premix_kernel, rope=rope, cache=cache, layer=layer),
        out_shape=out_shape,
        grid=(T // tm,),
        in_specs=in_specs,
        out_specs=out_specs,
        input_output_aliases=aliases,
        compiler_params=_cparams("arbitrary"),
        name="premix",
    )(*args)


def _conformer_kernel(u_ref, w_ref, cb_ref, g_ref, b_ref, o_ref, pad_ref, sh_ref, *, seq_len, rows):
    L = seq_len
    u = u_ref[...].astype(F32)
    a = u[:, :D_CONV] * jax.nn.sigmoid(u[:, D_CONV:])
    zeros = jnp.zeros((CONV_PAD, D_CONV), F32)
    pad_ref[0:CONV_PAD, :] = zeros
    pad_ref[CONV_PAD + L:2 * CONV_PAD + L, :] = zeros
    pad_ref[CONV_PAD:CONV_PAD + L, :] = a
    span = L + 2 * CONV_PAD - SUBLANES
    for r in range(SUBLANES):
        sh_ref[r, 0:span, :] = pad_ref[r:r + span, :]
    first = CONV_PAD - (CONV_WIDTH - 1) // 2

    def chunk(c, carry):
        base = pl.multiple_of(c * rows, rows)
        acc = jnp.zeros((rows, D_CONV), F32)
        for t in range(CONV_WIDTH):
            off = first + t
            tap = sh_ref[off % SUBLANES, pl.ds(base + (off // SUBLANES) * SUBLANES, rows), :]
            acc = acc + tap * w_ref[t:t + 1, :]
        acc = acc + cb_ref[...]
        mu = jnp.mean(acc, axis=-1, keepdims=True)
        d = acc - mu
        var = jnp.mean(d * d, axis=-1, keepdims=True)
        y = d * lax.rsqrt(var + EPS) * g_ref[...] + b_ref[...]
        o_ref[pl.ds(base, rows), :] = _silu(y).astype(BF16)
        return carry

    lax.fori_loop(0, L // rows, chunk, 0)


def _conformer(uc, conv_w, conv_b, ln_g, ln_b, *, seq_len):
    T = uc.shape[0]
    n_seq = T // seq_len
    vec = lambda a: a.reshape(1, D_CONV)
    return pl.pallas_call(
        functools.partial(_conformer_kernel, seq_len=seq_len, rows=64),
        out_shape=jax.ShapeDtypeStruct((T, D_CONV), BF16),
        grid=(n_seq,),
        in_specs=[
            pl.BlockSpec((seq_len, 2 * D_CONV), lambda i: (i, 0)),
            pl.BlockSpec((CONV_WIDTH, D_CONV), lambda i: (0, 0)),
            pl.BlockSpec((1, D_CONV), lambda i: (0, 0)),
            pl.BlockSpec((1, D_CONV), lambda i: (0, 0)),
            pl.BlockSpec((1, D_CONV), lambda i: (0, 0)),
        ],
        out_specs=pl.BlockSpec((seq_len, D_CONV), lambda i: (i, 0)),
        scratch_shapes=[pltpu.VMEM((seq_len + 2 * CONV_PAD, D_CONV), F32),
                        pltpu.VMEM((SUBLANES, seq_len + 2 * CONV_PAD, D_CONV), F32)],
        compiler_params=_cparams("arbitrary"),
        name="conformer",
    )(uc, conv_w, vec(conv_b), vec(ln_g), vec(ln_b))


def _dft_matrices(L):
    k = np.arange(L, dtype=np.int64)[:, None]
    m = np.arange(L, dtype=np.int64)[None, :]
    ang = 2.0 * np.pi * (((2 * k + 1) * m) % (4 * L)).astype(np.float64) / (4 * L)
    fwd = np.concatenate([np.cos(ang), np.sin(ang)], axis=0).astype(np.float32)
    return fwd, np.ascontiguousarray(fwd.T)


def _filter_tables(L):
    t = np.linspace(0.0, 1.0, L)
    bands = np.linspace(1e-4, HYENA_BANDS - 1, HYENA_BANDS)
    ang = (2.0 * np.pi / L) * np.arange(L)[:, None] * bands[None, :]
    feats = np.zeros((L, FEAT_PAD), np.float32)
    feats[:, :HYENA_EMB] = np.concatenate([t[:, None], np.cos(ang), -np.sin(ang)], axis=-1)
    deltas = np.abs(np.linspace(HYENA_MIN_DECAY, HYENA_MAX_DECAY, D_HYENA))
    window = np.exp(-t[:, None] * deltas[None, :]).astype(np.float32)
    return feats, window


def _dot3(a, b):
    a_hi, a_lo = _split_bf16(a)
    b_hi, b_lo = _split_bf16(b)
    return _dot(a_hi, b_hi) + (_dot(a_hi, b_lo) + _dot(a_lo, b_hi))


def _spectrum_kernel(feat_ref, win_ref, w1_ref, b1_ref, w2_ref, b2_ref, fr_ref, w3_ref, f_ref,
                     o_ref, *, seq_len):
    L = seq_len
    C = D_HYENA
    fr = fr_ref[...]
    hid = jnp.sin(fr * (_dot3(feat_ref[...], w1_ref[...]) + b1_ref[...]))
    hid = jnp.sin(fr * (_dot3(hid, w2_ref[...]) + b2_ref[...]))
    filt = _dot3(hid, w3_ref[...])
    win = win_ref[...]
    first = lax.broadcasted_iota(jnp.int32, (L, C), 0) == 0
    for o in range(2):
        fwd = filt[:, o * C:(o + 1) * C] * win
        bwd = jnp.where(first, 0.0, filt[:, (2 + o) * C:(3 + o) * C] * win)
        o_ref[o, 0:L, :] = _dot3(f_ref[0:L, :], fwd + bwd) * (1.0 / L)
        o_ref[o, L:2 * L, :] = _dot3(f_ref[L:2 * L, :], bwd - fwd) * (1.0 / L)


def _hyena_spectrum(L, fmat, feats, window, hy_w1, hy_b1, hy_w2, hy_b2, hy_freq, hy_w3):
    w1 = jnp.zeros((FEAT_PAD, hy_w1.shape[1]), F32).at[:HYENA_EMB].set(hy_w1)
    vec = lambda a: a.reshape(1, -1)
    return pl.pallas_call(
        functools.partial(_spectrum_kernel, seq_len=L),
        out_shape=jax.ShapeDtypeStruct((2, 2 * L, D_HYENA), F32),
        compiler_params=pltpu.CompilerParams(vmem_limit_bytes=VMEM_LIMIT),
        name="hyena_spectrum",
    )(feats, window, w1, vec(hy_b1), hy_w2, vec(hy_b2), vec(hy_freq), hy_w3, fmat)


def _hyena_kernel(u_ref, sw_ref, sb_ref, hb_ref, f_ref, ft_ref, ks_ref, o_ref, pad_ref,
                  *, seq_len, n_seq):
    L = seq_len
    C = D_HYENA
    zrow = jnp.zeros((8, 3 * C), F32)
    for b in range(n_seq):
        u = u_ref[b].astype(F32)
        pad_ref[0:8, :] = zrow
        pad_ref[8 + L:16 + L, :] = zrow
        pad_ref[8:8 + L, :] = u
        s = (pad_ref[7:7 + L, :] * sw_ref[0:1, :] + u * sw_ref[1:2, :]
             + pad_ref[9:9 + L, :] * sw_ref[2:3, :] + sb_ref[...])
        z = s[:, 0:C]
        for o in range(2):
            gate = s[:, (o + 1) * C:(o + 2) * C]
            spec = _dot(f_ref[...], z.astype(BF16))
            zr, za = spec[:L], spec[L:]
            kre, kim = ks_ref[o, 0:L, :], ks_ref[o, L:2 * L, :]
            y = jnp.concatenate([zr * kre + za * kim, za * kre - zr * kim], axis=0)
            conv = _dot(ft_ref[...], y.astype(BF16))
            z = gate * (conv + hb_ref[o:o + 1, :] * z)
        o_ref[b] = z.astype(BF16)


def _hyena(uh, short_w, short_b, hy_bias, fmat_bf, fmat_t_bf, kspec, *, seq_len, seqs_per_step):
    T = uh.shape[0]
    n_seq = T // seq_len
    L = seq_len
    nb = seqs_per_step
    u3 = uh.reshape(n_seq, L, 3 * D_HYENA)
    out = pl.pallas_call(
        functools.partial(_hyena_kernel, seq_len=L, n_seq=nb),
        out_shape=jax.ShapeDtypeStruct((n_seq, L, D_HYENA), BF16),
        grid=(n_seq // nb,),
        in_specs=[
            pl.BlockSpec((nb, L, 3 * D_HYENA), lambda i: (i, 0, 0)),
            pl.BlockSpec((3, 3 * D_HYENA), lambda i: (0, 0)),
            pl.BlockSpec((1, 3 * D_HYENA), lambda i: (0, 0)),
            pl.BlockSpec((2, D_HYENA), lambda i: (0, 0)),
            pl.BlockSpec((2 * L, L), lambda i: (0, 0)),
            pl.BlockSpec((L, 2 * L), lambda i: (0, 0)),
            pl.BlockSpec((2, 2 * L, D_HYENA), lambda i: (0, 0, 0)),
        ],
        out_specs=pl.BlockSpec((nb, L, D_HYENA), lambda i: (i, 0, 0)),
        scratch_shapes=[pltpu.VMEM((L + 16, 3 * D_HYENA), F32)],
        compiler_params=_cparams("arbitrary"),
        name="hyena",
    )(u3, short_w, short_b.reshape(1, 3 * D_HYENA), hy_bias, fmat_bf, fmat_t_bf, kspec)
    return out.reshape(T, D_HYENA)


def _attn_kernel(*refs, lam_init, has_ctx):
    q_ref, k_ref, v_ref = refs[:3]
    pos = 3
    if has_ctx:
        ck_ref, cv_ref = refs[pos:pos + 2]
        pos += 2
    lam_ref, g_ref, o_ref = refs[pos:pos + 3]

    lv = lam_ref[...]
    lam = (jnp.exp(jnp.sum(lv[0:1] * lv[1:2], axis=-1, keepdims=True))
           - jnp.exp(jnp.sum(lv[2:3] * lv[3:4], axis=-1, keepdims=True)) + lam_init)

    lane = lax.broadcasted_iota(jnp.int32, (q_ref.shape[0], HEAD_W), 1)
    dn = (((1,), (1,)), ((), ()))
    for h in range(N_HEADS):
        sl = slice(h * HEAD_W, (h + 1) * HEAD_W)
        q = q_ref[:, sl]
        k = k_ref[:, sl]
        v = v_ref[:, sl]
        if has_ctx:
            ck = ck_ref[h].astype(BF16)
            cv = cv_ref[h].astype(BF16)
        zero = jnp.zeros_like(q)
        outs = []
        for c in range(2):
            qc = jnp.where((lane >= c * DIFF_DK) & (lane < (c + 1) * DIFF_DK), q, zero)
            s = lax.dot_general(qc, k, dn, preferred_element_type=F32)
            m = jnp.max(s, axis=-1, keepdims=True)
            if has_ctx:
                sc = lax.dot_general(qc, ck, dn, preferred_element_type=F32)
                m = jnp.maximum(m, jnp.max(sc, axis=-1, keepdims=True))
            p = jnp.exp(s - m)
            den = jnp.sum(p, axis=-1, keepdims=True)
            acc = _dot(p.astype(BF16), v)
            if has_ctx:
                pc = jnp.exp(sc - m)
                den = den + jnp.sum(pc, axis=-1, keepdims=True)
                acc = acc + _dot(pc.astype(BF16), cv)
            outs.append(acc / den)
        o = outs[0] - lam * outs[1]
        o_ref[:, sl] = (_rms(o, g_ref[...]) * (1.0 - lam_init)).astype(BF16)


def _attention(q, k, v, lam_vecs, subln_g, *, seq_len, layer, ctx=None):
    T = q.shape[0]
    L = seq_len
    n_seq = T // L
    tq = TOKEN_BLOCK
    nq = L // tq
    lam_init = 0.8 - 0.6 * math.exp(-0.3 * layer)
    q3, k3, v3 = (a.reshape(n_seq, L, D_ATTN) for a in (q, k, v))
    in_specs = [
        pl.BlockSpec((None, tq, D_ATTN), lambda b, i: (b, i, 0)),
        pl.BlockSpec((None, L, D_ATTN), lambda b, i: (b, 0, 0)),
        pl.BlockSpec((None, L, D_ATTN), lambda b, i: (b, 0, 0)),
    ]
    args = [q3, k3, v3]
    if ctx is not None:
        past = ctx[0].shape[3]
        in_specs += [pl.BlockSpec((None, None, N_HEADS, past, HEAD_W),
                                  lambda b, i: (b, layer, 0, 0, 0))] * 2
        args += list(ctx)
    in_specs += [
        pl.BlockSpec((4, DIFF_DK), lambda b, i: (0, 0)),
        pl.BlockSpec((1, DIFF_DV), lambda b, i: (0, 0)),
    ]
    args += [lam_vecs, subln_g.reshape(1, DIFF_DV)]
    out = pl.pallas_call(
        functools.partial(_attn_kernel, lam_init=lam_init, has_ctx=ctx is not None),
        out_shape=jax.ShapeDtypeStruct((n_seq, L, D_ATTN), BF16),
        grid=(n_seq, nq),
        in_specs=in_specs,
        out_specs=pl.BlockSpec((None, tq, D_ATTN), lambda b, i: (b, i, 0)),
        compiler_params=_cparams("arbitrary", "arbitrary"),
        name="diff_attention",
    )(*args)
    return out.reshape(T, D_ATTN)


def _split_bf16(a):
    hi = a.astype(BF16)
    lo = (a - hi.astype(F32)).astype(BF16)
    return hi, lo


def _pack_pair(a, b):
    ua = lax.bitcast_convert_type(a.astype(BF16).astype(F32), jnp.uint32)
    ub = lax.bitcast_convert_type(b.astype(BF16).astype(F32), jnp.uint32)
    return (ua >> 16) | ub


def _unpack_pair(w):
    a = lax.bitcast_convert_type(w << 16, F32)
    b = lax.bitcast_convert_type(w & jnp.uint32(0xFFFF0000), F32)
    return a, b


def _postmix_kernel(*refs, stream_blocks):
    ns = len(stream_blocks)
    mod_ref, gpost_ref, gpre_ref, w_ref, rw_ref, x1_ref, h2_ref, lg_ref = refs[4 * ns:]
    i = pl.program_id(0)

    def body(ya_ref, yb_ref, o_ref, x_ref):
        y = (_dot(ya_ref[...], w_ref[0:D_CONV, :])
             + _dot(yb_ref[...], w_ref[D_CONV:D_CONV + D_HYENA, :])
             + _dot(o_ref[...], w_ref[D_CONV + D_HYENA:, :]))
        mod = mod_ref[...]
        x1 = x_ref[...] + mod[2:3] * _rms(y, gpost_ref[...])
        x1_ref[...] = x1
        h2 = _rms(x1, gpre_ref[...]) * (1.0 + mod[4:5]) + mod[3:4]
        h2_ref[...] = _pack_pair(h2[:, :HALF], h2[:, HALF:])
        h_hi, h_lo = _split_bf16(h2)
        r_hi, r_lo = _split_bf16(rw_ref[...])
        dn = (((1,), (1,)), ((), ()))
        mm = lambda a, b: lax.dot_general(a, b, dn, preferred_element_type=F32)
        lg_ref[...] = mm(r_hi, h_hi) + (mm(r_lo, h_hi) + mm(r_hi, h_lo))

    lo = 0
    for j, nbk in enumerate(stream_blocks):
        pl.when((i >= lo) & (i < lo + nbk))(functools.partial(body, *refs[4 * j:4 * j + 4]))
        lo += nbk


def _postmix(parts, mod_l, g_post, g_pre, w_out_bf, router_wt, *, mod_map):
    t_all = sum(p[0].shape[0] for p in parts)
    tm = TOKEN_BLOCK
    full = lambda a: pl.BlockSpec(a.shape, lambda i: (0,) * a.ndim)
    gp, ge = g_post.reshape(1, D_MODEL), g_pre.reshape(1, D_MODEL)
    stream_blocks = tuple(p[0].shape[0] // tm for p in parts)
    in_specs, args = [], []
    lo = 0
    for part, nbk in zip(parts, stream_blocks):
        local = lambda i, lo=lo, nbk=nbk: (jnp.clip(i - lo, 0, nbk - 1), 0)
        in_specs += [pl.BlockSpec((tm, D_CONV), local), pl.BlockSpec((tm, D_HYENA), local),
                     pl.BlockSpec((tm, D_ATTN), local), pl.BlockSpec((tm, D_MODEL), local)]
        args += list(part)
        lo += nbk
    in_specs += [pl.BlockSpec((None, N_MOD, D_MODEL), mod_map),
                 full(gp), full(ge), full(w_out_bf), full(router_wt)]
    args += [mod_l, gp, ge, w_out_bf, router_wt]
    return pl.pallas_call(
        functools.partial(_postmix_kernel, stream_blocks=stream_blocks),
        out_shape=[
            jax.ShapeDtypeStruct((t_all, D_MODEL), F32),
            jax.ShapeDtypeStruct((t_all, HALF), jnp.uint32),
            jax.ShapeDtypeStruct((N_EXPERTS, t_all), F32),
        ],
        grid=(t_all // tm,),
        in_specs=in_specs,
        out_specs=[pl.BlockSpec((tm, D_MODEL), lambda i: (i, 0)),
                   pl.BlockSpec((tm, HALF), lambda i: (i, 0)),
                   pl.BlockSpec((N_EXPERTS, tm), lambda i: (0, i))],
        compiler_params=_cparams("arbitrary"),
        name="postmix",
    )(*args)


def _first_max(vals, idx, sentinel):
    m = jnp.max(vals, axis=0, keepdims=True)
    i = jnp.min(jnp.where(vals == m, idx, sentinel), axis=0, keepdims=True)
    return m, i


def _route_topk_kernel(lg_ref, rb_ref, eidx_ref, w_ref, cnt_ref):
    step = pl.program_id(0)
    per = N_EXPERTS // N_GROUPS
    n = lg_ref.shape[1]
    s = jax.nn.sigmoid(lg_ref[...])
    sel = s + rb_ref[...]
    sub = lax.broadcasted_iota(jnp.int32, (per, n), 0)
    neg = jnp.float32(-jnp.inf)

    gs = []
    for g in range(N_GROUPS):
        xg = sel[g * per:(g + 1) * per]
        total = jnp.zeros((1, n), F32)
        for _ in range(GROUP_SCORE_TOP):
            m, i = _first_max(xg, sub, per)
            total = total + m
            xg = jnp.where(sub == i, neg, xg)
        gs.append(total)
    gscore = jnp.concatenate(gs, axis=0)
    gkeep = jnp.zeros((N_GROUPS, n), F32)
    for _ in range(TOPK_GROUPS):
        _, gi = _first_max(gscore, sub, N_GROUPS)
        hit = sub == gi
        gkeep = jnp.where(hit, 1.0, gkeep)
        gscore = jnp.where(hit, neg, gscore)

    eio = [sub + g * per for g in range(N_GROUPS)]
    sg = [s[g * per:(g + 1) * per] for g in range(N_GROUPS)]
    mk = [jnp.where(gkeep[g:g + 1] > 0.5, sel[g * per:(g + 1) * per], neg) for g in range(N_GROUPS)]
    oh = [jnp.zeros((per, n), F32) for _ in range(N_GROUPS)]
    idx_rows, s_rows = [], []
    for _ in range(TOP_K):
        best = functools.reduce(jnp.maximum, mk)
        m = jnp.max(best, axis=0, keepdims=True)
        cand = functools.reduce(jnp.minimum,
                                [jnp.where(mk[g] == m, eio[g], N_EXPERTS) for g in range(N_GROUPS)])
        idx = jnp.min(cand, axis=0, keepdims=True)
        acc = jnp.zeros((per, n), F32)
        for g in range(N_GROUPS):
            hit = eio[g] == idx
            acc = acc + jnp.where(hit, sg[g], 0.0)
            mk[g] = jnp.where(hit, neg, mk[g])
            oh[g] = jnp.where(hit, 1.0, oh[g])
        idx_rows.append(idx)
        s_rows.append(jnp.sum(acc, axis=0, keepdims=True))
    eidx_ref[...] = jnp.concatenate(idx_rows, axis=0)
    sk = jnp.concatenate(s_rows, axis=0)
    w_ref[...] = sk / jnp.sum(sk, axis=0, keepdims=True) * ROUTED_SCALE

    @pl.when(step == 0)
    def _():
        cnt_ref[...] = jnp.zeros_like(cnt_ref)

    tot = jnp.concatenate([jnp.sum(o, axis=1, keepdims=True) for o in oh], axis=0)
    cnt_ref[...] = cnt_ref[...] + jnp.broadcast_to(tot, cnt_ref.shape).astype(jnp.int32)


def _route_topk(logits_t, router_b):
    t_all = logits_t.shape[1]
    rb = ROUTE_BLOCK
    return pl.pallas_call(
        _route_topk_kernel,
        out_shape=[
            jax.ShapeDtypeStruct((TOP_K, t_all), jnp.int32),
            jax.ShapeDtypeStruct((TOP_K, t_all), F32),
            jax.ShapeDtypeStruct((N_EXPERTS, LANES), jnp.int32),
        ],
        grid=(t_all // rb,),
        in_specs=[pl.BlockSpec((N_EXPERTS, rb), lambda i: (0, i)),
                  pl.BlockSpec((N_EXPERTS, 1), lambda i: (0, 0))],
        out_specs=[pl.BlockSpec((TOP_K, rb), lambda i: (0, i)),
                   pl.BlockSpec((TOP_K, rb), lambda i: (0, i)),
                   pl.BlockSpec((N_EXPERTS, LANES), lambda i: (0, 0))],
        compiler_params=_cparams("arbitrary"),
        name="route_topk",
    )(logits_t, router_b.reshape(N_EXPERTS, 1))


def _route_dest_kernel(eidx_ref, ps_ref, dest_ref, carry_ref):
    step = pl.program_id(0)
    per = N_EXPERTS // N_GROUPS
    n = eidx_ref.shape[1]

    @pl.when(step == 0)
    def _():
        carry_ref[...] = jnp.zeros_like(carry_ref)

    eidx = eidx_ref[...]
    sub = lax.broadcasted_iota(jnp.int32, (per, n), 0)
    rows = []
    for g in range(N_GROUPS):
        eio = sub + g * per
        o = jnp.zeros((per, n), F32)
        for k in range(TOP_K):
            o = jnp.where(eio == eidx[k:k + 1], 1.0, o)
        rows.append(o)
    onehot = jnp.concatenate(rows, axis=0)
    r = lax.broadcasted_iota(jnp.int32, (n, n), 0)
    c = lax.broadcasted_iota(jnp.int32, (n, n), 1)
    before = jnp.where(r < c, 1.0, 0.0).astype(BF16)
    rank = _dot(onehot.astype(BF16), before)
    base = (rank + carry_ref[:, 0:1]).astype(jnp.int32) + ps_ref[...]
    out = []
    for k in range(TOP_K):
        acc = jnp.zeros((per, n), jnp.int32)
        for g in range(N_GROUPS):
            acc = acc + jnp.where(sub + g * per == eidx[k:k + 1], base[g * per:(g + 1) * per], 0)
        out.append(jnp.sum(acc, axis=0, keepdims=True))
    dest_ref[...] = jnp.concatenate(out, axis=0)
    tot = jnp.sum(onehot, axis=1, keepdims=True)
    carry_ref[...] = carry_ref[...] + jnp.broadcast_to(tot, carry_ref.shape)


def _route_dest(eidx_t, pstart):
    t_all = eidx_t.shape[1]
    rb = ROUTE_BLOCK
    return pl.pallas_call(
        _route_dest_kernel,
        out_shape=jax.ShapeDtypeStruct((TOP_K, t_all), jnp.int32),
        grid=(t_all // rb,),
        in_specs=[pl.BlockSpec((TOP_K, rb), lambda i: (0, i)),
                  pl.BlockSpec((N_EXPERTS, 1), lambda i: (0, 0))],
        out_specs=pl.BlockSpec((TOP_K, rb), lambda i: (0, i)),
        scratch_shapes=[pltpu.VMEM((N_EXPERTS, LANES), F32)],
        compiler_params=_cparams("arbitrary"),
        name="route_dest",
    )(eidx_t, pstart.reshape(N_EXPERTS, 1))


def _row_copy(src_hbm, src_row, dst_hbm, dst_row, sem):
    return pltpu.make_async_copy(src_hbm.at[pl.ds(src_row, 1)], dst_hbm.at[pl.ds(dst_row, 1)], sem)


def _wait_rows(hbm, sem, n_rows):
    pltpu.make_async_copy(hbm.at[pl.ds(0, n_rows)], hbm.at[pl.ds(0, n_rows)], sem).wait()


def _zero_padding(cnt_ref, ps_ref, nu_ref, zero_ref, x_hbm, zsem, n_blocks, *, wait):
    def copy(row, size):
        cp = pltpu.make_async_copy(zero_ref.at[pl.ds(0, size)], x_hbm.at[pl.ds(row, size)], zsem)
        if wait:
            cp.wait()
        else:
            cp.start()

    def per_expert(e, carry):
        cnt = cnt_ref[e]
        pad = (EXPERT_BLOCK - cnt % EXPERT_BLOCK) % EXPERT_BLOCK
        row = ps_ref[e] + cnt
        head = pad % SUBLANES
        for j in range(SUBLANES - 1):
            pl.when(j < head)(functools.partial(copy, row + j, 1))
        row = row + head
        size = EXPERT_BLOCK // 2
        while size >= SUBLANES:
            take = (pad & size) != 0
            pl.when(take)(functools.partial(copy, pl.multiple_of(row, SUBLANES), size))
            row = row + jnp.where(take, size, 0)
            size //= 2
        return carry

    lax.fori_loop(0, N_EXPERTS, per_expert, 0)

    def per_block(j, carry):
        copy(pl.multiple_of(j * EXPERT_BLOCK, EXPERT_BLOCK), EXPERT_BLOCK)
        return carry

    lax.fori_loop(nu_ref[0], n_blocks, per_block, 0)


def _dispatch_kernel(dest_ref, cnt_ref, ps_ref, nu_ref, h_ref, x_hbm, zero_ref, sem, zsem):
    step = pl.program_id(0)
    nsteps = pl.num_programs(0)
    n = h_ref.shape[0]
    zero_args = (cnt_ref, ps_ref, nu_ref, zero_ref, x_hbm, zsem, x_hbm.shape[0] // EXPERT_BLOCK)

    @pl.when(step == 0)
    def _():
        zero_ref[...] = jnp.zeros_like(zero_ref)
        _zero_padding(*zero_args, wait=False)

    def per_group(g, carry):
        base = pl.multiple_of(g * SUBLANES, SUBLANES)
        for j in range(SUBLANES):
            for k in range(TOP_K):
                pltpu.make_async_copy(h_ref.at[pl.ds(base + j, 1)],
                                      x_hbm.at[pl.ds(dest_ref[k, base + j], 1)], sem).start()
        return carry

    lax.fori_loop(0, n // SUBLANES, per_group, 0)
    _wait_rows(x_hbm, sem, n * TOP_K)

    @pl.when(step == nsteps - 1)
    def _():
        _zero_padding(*zero_args, wait=True)


def _dispatch(dest_t, counts, pstart, n_used, h2p, n_blocks):
    t_all = h2p.shape[0]
    sb = TOKEN_BLOCK
    smem = lambda: pl.BlockSpec(memory_space=pltpu.SMEM)
    return pl.pallas_call(
        _dispatch_kernel,
        out_shape=jax.ShapeDtypeStruct((n_blocks * EXPERT_BLOCK, HALF), jnp.uint32),
        grid=(t_all // sb,),
        in_specs=[pl.BlockSpec((TOP_K, sb), lambda i: (0, i), memory_space=pltpu.SMEM),
                  smem(), smem(), smem(),
                  pl.BlockSpec((sb, HALF), lambda i: (i, 0))],
        out_specs=pl.BlockSpec(memory_space=pl.ANY),
        scratch_shapes=[pltpu.VMEM((EXPERT_BLOCK, HALF), jnp.uint32),
                        pltpu.SemaphoreType.DMA, pltpu.SemaphoreType.DMA],
        compiler_params=_cparams("arbitrary"),
        name="dispatch",
    )(dest_t, counts, pstart, n_used, h2p)


def _expert_kernel(nb_ref, ps_ref, nu_ref, wg_ref, wu_ref, wd_ref, x_hbm, y_hbm,
                   wg_bf, wu_bf, wd_bf, xbuf, ybuf, xsem, ysem):
    e = pl.program_id(0)
    nblk = nb_ref[e]
    row0 = ps_ref[e]
    wg_bf[...] = wg_ref[...].astype(BF16)
    wu_bf[...] = wu_ref[...].astype(BF16)
    wd_bf[...] = wd_ref[...].astype(BF16)

    def rows(j):
        return pl.ds(pl.multiple_of(row0 + j * EXPERT_BLOCK, EXPERT_BLOCK), EXPERT_BLOCK)

    def x_copy(j, slot):
        return pltpu.make_async_copy(x_hbm.at[rows(j)], xbuf.at[slot], xsem.at[slot])

    def y_copy(j, slot):
        return pltpu.make_async_copy(ybuf.at[slot], y_hbm.at[rows(j)], ysem.at[slot])

    @pl.when(nblk > 0)
    def _():
        x_copy(0, 0).start()

    def block(j, carry):
        slot = j % 2
        x_copy(j, slot).wait()

        @pl.when(j + 1 < nblk)
        def _():
            x_copy(j + 1, 1 - slot).start()

        @pl.when(j >= 2)
        def _():
            y_copy(j - 2, slot).wait()

        y = _swiglu_packed(xbuf[slot], wg_bf, wu_bf, wd_bf)
        ybuf[slot] = _pack_pair(y[:, :HALF], y[:, HALF:])
        y_copy(j, slot).start()
        return carry

    lax.fori_loop(0, nblk, block, 0)

    @pl.when(nblk >= 2)
    def _():
        y_copy(nblk - 2, nblk % 2).wait()

    @pl.when(nblk >= 1)
    def _():
        y_copy(nblk - 1, (nblk - 1) % 2).wait()

    @pl.when(e == pl.num_programs(0) - 1)
    def _():
        n_blocks = y_hbm.shape[0] // EXPERT_BLOCK
        ybuf[0] = jnp.zeros_like(ybuf[0])

        def tail(j, carry, wait):
            cp = pltpu.make_async_copy(
                ybuf.at[0], y_hbm.at[pl.ds(pl.multiple_of(j * EXPERT_BLOCK, EXPERT_BLOCK),
                                           EXPERT_BLOCK)], ysem.at[0])
            cp.wait() if wait else cp.start()
            return carry

        lax.fori_loop(nu_ref[0], n_blocks, functools.partial(tail, wait=False), 0)
        lax.fori_loop(nu_ref[0], n_blocks, functools.partial(tail, wait=True), 0)


def _swiglu_packed(xp, wg_ref, wu_ref, wd_ref):
    xa, xb = _unpack_pair(xp)
    xa, xb = xa.astype(BF16), xb.astype(BF16)
    gate = _dot(xa, wg_ref[0:HALF, :]) + _dot(xb, wg_ref[HALF:, :])
    up = _dot(xa, wu_ref[0:HALF, :]) + _dot(xb, wu_ref[HALF:, :])
    return _dot((_silu(gate) * up).astype(BF16), wd_ref[...])


def _experts(x_buf, nblk, pstart, n_used, wg, wu, wd):
    grid_spec = pltpu.PrefetchScalarGridSpec(
        num_scalar_prefetch=3,
        grid=(N_EXPERTS,),
        in_specs=[
            pl.BlockSpec((None, D_MODEL, D_EXPERT), lambda e, *_: (e, 0, 0)),
            pl.BlockSpec((None, D_MODEL, D_EXPERT), lambda e, *_: (e, 0, 0)),
            pl.BlockSpec((None, D_EXPERT, D_MODEL), lambda e, *_: (e, 0, 0)),
            pl.BlockSpec(memory_space=pl.ANY),
        ],
        out_specs=pl.BlockSpec(memory_space=pl.ANY),
        scratch_shapes=[
            pltpu.VMEM((D_MODEL, D_EXPERT), BF16),
            pltpu.VMEM((D_MODEL, D_EXPERT), BF16),
            pltpu.VMEM((D_EXPERT, D_MODEL), BF16),
            pltpu.VMEM((2, EXPERT_BLOCK, HALF), jnp.uint32),
            pltpu.VMEM((2, EXPERT_BLOCK, HALF), jnp.uint32),
            pltpu.SemaphoreType.DMA((2,)),
            pltpu.SemaphoreType.DMA((2,)),
        ],
    )
    return pl.pallas_call(
        _expert_kernel,
        out_shape=jax.ShapeDtypeStruct(x_buf.shape, jnp.uint32),
        grid_spec=grid_spec,
        compiler_params=_cparams("arbitrary"),
        name="experts",
    )(nblk, pstart, n_used, wg, wu, wd, x_buf)


def _block_plan(counts, n_blocks):
    nblk = (counts + EXPERT_BLOCK - 1) // EXPERT_BLOCK
    pend = jnp.cumsum(nblk)
    pstart = ((pend - nblk) * EXPERT_BLOCK).astype(jnp.int32)
    return pstart, nblk.astype(jnp.int32), pend[-1:].astype(jnp.int32)


def _ffn_out_kernel(*refs, stream_blocks):
    (dcur_ref, dnxt_ref, h_ref, w_ref, x_ref, mod_ref, g_ref, wg_ref, wu_ref, wd_ref,
     y_hbm) = refs[:11]
    ns = len(stream_blocks)
    out_refs = refs[11:11 + ns]
    rows_ref, sem = refs[11 + ns:]
    step = pl.program_id(0)
    nsteps = pl.num_programs(0)
    n = h_ref.shape[0]

    def gather(d_ref, slot):
        def per_group(g, carry):
            base = pl.multiple_of(g * SUBLANES, SUBLANES)
            for j in range(SUBLANES):
                for k in range(TOP_K):
                    pltpu.make_async_copy(y_hbm.at[pl.ds(d_ref[k, base + j], 1)],
                                          rows_ref.at[slot, k, pl.ds(base + j, 1)],
                                          sem.at[slot]).start()
            return carry
        lax.fori_loop(0, n // SUBLANES, per_group, 0)

    slot = step % 2

    @pl.when(step == 0)
    def _():
        gather(dcur_ref, 0)

    @pl.when(step + 1 < nsteps)
    def _():
        gather(dnxt_ref, 1 - slot)

    ys = _swiglu_packed(h_ref[...], wg_ref, wu_ref, wd_ref)
    pltpu.make_async_copy(rows_ref.at[slot], rows_ref.at[slot], sem.at[slot]).wait()
    w = w_ref[...]
    ya, yb = ys[:, :HALF], ys[:, HALF:]
    for k in range(TOP_K):
        ra, rb = _unpack_pair(rows_ref[slot, k])
        ya = ya + w[:, k:k + 1] * ra
        yb = yb + w[:, k:k + 1] * rb
    ms = (jnp.sum(ya * ya, axis=-1, keepdims=True)
          + jnp.sum(yb * yb, axis=-1, keepdims=True)) * (1.0 / D_MODEL)
    inv = lax.rsqrt(ms + EPS)
    mod = mod_ref[...]
    g = g_ref[...]
    out_a = x_ref[:, 0:HALF] + mod[5:6, 0:HALF] * (ya * inv * g[:, 0:HALF])
    out_b = x_ref[:, HALF:] + mod[5:6, HALF:] * (yb * inv * g[:, HALF:])

    def store(o_ref):
        o_ref[:, 0:HALF] = out_a
        o_ref[:, HALF:] = out_b

    lo = 0
    for o_ref, nbk in zip(out_refs, stream_blocks):
        pl.when((step >= lo) & (step < lo + nbk))(functools.partial(store, o_ref))
        lo += nbk


def _ffn_out(dest_t, h2p, w_rows, x1, mod_l, g_post, swg, swu, swd, y_buf, *, mod_map,
             stream_tokens):
    t_all = x1.shape[0]
    tm = TOKEN_BLOCK
    nsteps = t_all // tm
    row = lambda w: pl.BlockSpec((tm, w), lambda i: (i, 0))
    full = lambda a: pl.BlockSpec(a.shape, lambda i: (0,) * a.ndim)
    g = g_post.reshape(1, D_MODEL)
    stream_blocks = tuple(t // tm for t in stream_tokens)
    out_shape, out_specs = [], []
    lo = 0
    for t, nbk in zip(stream_tokens, stream_blocks):
        out_shape.append(jax.ShapeDtypeStruct((t, D_MODEL), F32))
        out_specs.append(pl.BlockSpec((tm, D_MODEL),
                                      lambda i, lo=lo, nbk=nbk: (jnp.clip(i - lo, 0, nbk - 1), 0)))
        lo += nbk
    return pl.pallas_call(
        functools.partial(_ffn_out_kernel, stream_blocks=stream_blocks),
        out_shape=out_shape,
        grid=(nsteps,),
        in_specs=[pl.BlockSpec((TOP_K, tm), lambda i: (0, i), memory_space=pltpu.SMEM),
                  pl.BlockSpec((TOP_K, tm), lambda i: (0, jnp.minimum(i + 1, nsteps - 1)),
                               memory_space=pltpu.SMEM),
                  row(HALF), row(TOP_K), row(D_MODEL),
                  pl.BlockSpec((None, N_MOD, D_MODEL), mod_map),
                  full(g), full(swg), full(swu), full(swd),
                  pl.BlockSpec(memory_space=pl.ANY)],
        out_specs=out_specs,
        scratch_shapes=[pltpu.VMEM((2, TOP_K, tm, HALF), jnp.uint32),
                        pltpu.SemaphoreType.DMA((2,))],
        compiler_params=_cparams("arbitrary"),
        name="ffn_out",
    )(dest_t, dest_t, h2p, w_rows, x1, mod_l, g, swg, swu, swd, y_buf)


def _rope_tables(L):
    rows = L // GRID_W
    row = jnp.repeat(jnp.arange(rows), GRID_W).astype(F32)
    col = jnp.tile(jnp.arange(GRID_W), rows).astype(F32)
    n = DIFF_DK // 4
    inv = ROPE_BASE ** (-jnp.arange(n, dtype=F32) / n)
    ar, ac = row[:, None] * inv[None, :], col[:, None] * inv[None, :]
    cos64 = jnp.concatenate([jnp.cos(ar), jnp.cos(ar), jnp.cos(ac), jnp.cos(ac)], axis=-1)
    sin64 = jnp.concatenate([-jnp.sin(ar), jnp.sin(ar), -jnp.sin(ac), jnp.sin(ac)], axis=-1)
    return jnp.tile(cos64, (1, 2)), jnp.tile(sin64, (1, 2))


def kernel(x_prompt, x_sample, cache_k, cache_v, c, c_ctx, w_mod, b_mod, g_pre_mix, g_post_mix, g_pre_ffn, g_post_ffn, w_in, w_out, conv_w, conv_b, conv_ln_g, conv_ln_b, hy_short_w, hy_short_b, hy_w1, hy_b1, hy_w2, hy_b2, hy_freq, hy_w3, hy_bias, lam_q1, lam_k1, lam_q2, lam_k2, subln_g, router_w, router_b, exp_w_gate, exp_w_up, exp_w_down, sh_w_gate, sh_w_up, sh_w_down):
    n_p, l_p, _ = x_prompt.shape
    n_s, l_s, _ = x_sample.shape
    t_p, t_s = n_p * l_p, n_s * l_s

    cond = jnp.zeros((COND_ROWS, D_MODEL), F32).at[0].set(c_ctx).at[1:1 + n_s].set(c)
    mod = _modulation(cond, w_mod, b_mod).reshape(DEPTH, COND_ROWS, N_MOD, D_MODEL)
    rope = _rope_tables(l_s)

    t_all = t_p + t_s
    streams = {
        "p": dict(x=x_prompt.reshape(t_p, D_MODEL), T=t_p, L=l_p, row0=0, per_seq=False, hy_nb=8),
        "s": dict(x=x_sample.reshape(t_s, D_MODEL), T=t_s, L=l_s, row0=1, per_seq=True, hy_nb=1),
    }
    for st in streams.values():
        fwd, fwd_t = _dft_matrices(st["L"])
        feats, window = _filter_tables(st["L"])
        st["fmat32"] = jnp.asarray(fwd)
        st["fmat"] = jnp.asarray(fwd).astype(BF16)
        st["fmat_t"] = jnp.asarray(fwd_t).astype(BF16)
        st["filter_tables"] = (jnp.asarray(feats), jnp.asarray(window))
    nb_p, nb_s = t_p // TOKEN_BLOCK, l_s // TOKEN_BLOCK
    merged_mod_map = lambda i: (jnp.where(i < nb_p, 0, 1 + (i - nb_p) // nb_s), 0, 0)
    n_blocks = t_all * TOP_K // EXPERT_BLOCK + N_EXPERTS

    caches = None
    for l in range(DEPTH):
        w_in_bf = w_in[l].astype(BF16)
        w_out_bf = w_out[l].astype(BF16)
        swg, swu, swd = (a[l].astype(BF16) for a in (sh_w_gate, sh_w_up, sh_w_down))
        lam_vecs = jnp.stack([lam_q1[l], lam_k1[l], lam_q2[l], lam_k2[l]], axis=0)
        filt_params = (hy_w1[l], hy_b1[l], hy_w2[l], hy_b2[l], hy_freq[l], hy_w3[l])
        parts = []
        for name, st in streams.items():
            L = st["L"]
            kw = dict(seq_len=L, cond_row0=st["row0"], per_seq_cond=st["per_seq"])
            if name == "p":
                uc, uh, q, k, v, ck, cv = _premix(st["x"], mod[l], g_pre_mix[l], w_in_bf, layer=l,
                                                  n_tokens=st["T"], tok0=0,
                                                  caches=caches, want_cache=True, **kw)
                caches = (ck, cv)
                ctx = None
            else:
                uc, uh, q, k, v = _premix(st["x"], mod[l], g_pre_mix[l], w_in_bf, layer=l,
                                          n_tokens=st["T"], tok0=0, rope_tables=rope, **kw)
                ctx = (cache_k, cache_v)
            ya = _conformer(uc, conv_w[l], conv_b[l], conv_ln_g[l], conv_ln_b[l], seq_len=L)
            kspec = _hyena_spectrum(L, st["fmat32"], *st["filter_tables"], *filt_params)
            yb = _hyena(uh, hy_short_w[l], hy_short_b[l], hy_bias[l], st["fmat"], st["fmat_t"],
                        kspec, seq_len=L, seqs_per_step=st["hy_nb"])
            o = _attention(q, k, v, lam_vecs, subln_g[l], seq_len=L, layer=l, ctx=ctx)
            parts.append((ya, yb, o, st["x"]))
        x1, h2p, logits_t = _postmix(parts, mod[l], g_post_mix[l], g_pre_ffn[l], w_out_bf,
                                     router_w[l].T, mod_map=merged_mod_map)

        eidx_t, w_t, cnt = _route_topk(logits_t, router_b[l])
        counts = cnt[:, 0]
        pstart, nblk, n_used = _block_plan(counts, n_blocks)
        dest_t = _route_dest(eidx_t, pstart)
        x_buf = _dispatch(dest_t, counts, pstart, n_used, h2p, n_blocks)
        y_buf = _experts(x_buf, nblk, pstart, n_used, exp_w_gate[l], exp_w_up[l], exp_w_down[l])
        outs = _ffn_out(dest_t, h2p, w_t.T, x1, mod[l], g_post_ffn[l], swg, swu, swd, y_buf,
                        mod_map=merged_mod_map, stream_tokens=(t_p, t_s))
        streams["p"]["x"], streams["s"]["x"] = outs

    y_prompt = streams["p"]["x"].reshape(n_p, l_p, D_MODEL)
    y_sample = streams["s"]["x"].reshape(n_s, l_s, D_MODEL)
    return (y_prompt, y_sample, caches[0], caches[1])
```

```python
import functools
import math

import numpy as np
import jax
import jax.numpy as jnp
from jax import lax
from jax.experimental import pallas as pl
from jax.experimental.pallas import tpu as pltpu

F32 = jnp.float32
BF16 = jnp.bfloat16

D_MODEL = 1024
DEPTH = 2
GRID_W = 64
D_CONV = 256
D_HYENA = 256
D_ATTN = 512
DIFF_DK = 64
N_HEADS = 4
DIFF_DV = 128
HEAD_W = 2 * DIFF_DK
IN_COLS = 2 * D_CONV + 3 * D_HYENA + 3 * D_ATTN
COL_HY = 2 * D_CONV
COL_Q = COL_HY + 3 * D_HYENA
COL_K = COL_Q + D_ATTN
COL_V = COL_K + D_ATTN
CONV_WIDTH = 31
CONV_PAD = 16
SUBLANES = 8
HYENA_EMB = 17
FEAT_PAD = 32
HYENA_BANDS = 8
HYENA_MIN_DECAY = math.log(1e-2) / 1.5
HYENA_MAX_DECAY = math.log(1e-2) / 0.3
N_EXPERTS = 64
TOP_K = 8
N_GROUPS = 8
TOPK_GROUPS = 4
GROUP_SCORE_TOP = 2
D_EXPERT = 256
ROUTED_SCALE = 2.5
ROPE_BASE = 10000.0
EPS = 1e-6
N_MOD = 6
COND_ROWS = 16

HALF = D_MODEL // 2
LANES = 128
ROUTE_BLOCK = 512
TOKEN_BLOCK = 256
EXPERT_BLOCK = 256
EXPERT_IN_SLOTS = 4
EXPERT_OUT_SLOTS = 3
VMEM_LIMIT = 48 * 1024 * 1024


def _cparams(*sem):
    return pltpu.CompilerParams(dimension_semantics=sem, vmem_limit_bytes=VMEM_LIMIT)


def _rms(x, g):
    return x * lax.rsqrt(jnp.mean(x * x, axis=-1, keepdims=True) + EPS) * g


def _silu(x):
    return x * jax.nn.sigmoid(x)


def _dot(a, b):
    return jnp.dot(a, b, preferred_element_type=F32)


def _mod_kernel(c_ref, w_ref, b_ref, o_ref):
    s = _silu(c_ref[...])
    o_ref[...] = _dot(s.astype(BF16), w_ref[...].astype(BF16)) + b_ref[...]


def _modulation(cond, w_mod, b_mod):
    cols = N_MOD * D_MODEL
    cb = 1536
    return pl.pallas_call(
        _mod_kernel,
        out_shape=jax.ShapeDtypeStruct((DEPTH, COND_ROWS, cols), F32),
        grid=(DEPTH, cols // cb),
        in_specs=[
            pl.BlockSpec((COND_ROWS, D_MODEL), lambda l, j: (0, 0)),
            pl.BlockSpec((None, D_MODEL, cb), lambda l, j: (l, 0, j)),
            pl.BlockSpec((None, 1, cb), lambda l, j: (l, 0, j)),
        ],
        out_specs=pl.BlockSpec((None, COND_ROWS, cb), lambda l, j: (l, 0, j)),
        compiler_params=_cparams("arbitrary", "arbitrary"),
        name="modulation",
    )(cond, w_mod, b_mod.reshape(DEPTH, 1, cols))


def _rope_slab(x, cos, sin, lo_mask):
    swapped = jnp.where(lo_mask, pltpu.roll(x, HEAD_W - 16, axis=1), pltpu.roll(x, 16, axis=1))
    return x * cos + swapped * sin


def _premix_kernel(*refs, rope, cache, layer):
    x_ref, mod_ref, g_ref, w_ref = refs[:4]
    pos = 4
    if rope:
        cos_ref, sin_ref = refs[pos:pos + 2]
        pos += 2
    if cache == "alias":
        pos += 2
    uc_ref, uh_ref, q_ref, k_ref, v_ref = refs[pos:pos + 5]
    pos += 5
    if cache:
        ck_ref, cv_ref = refs[pos:pos + 2]

    x = x_ref[...]
    mod = mod_ref[...]
    h = (_rms(x, g_ref[...]) * (1.0 + mod[1:2]) + mod[0:1]).astype(BF16)
    uc_ref[...] = _dot(h, w_ref[:, 0:COL_HY]).astype(BF16)
    uh_ref[...] = _dot(h, w_ref[:, COL_HY:COL_Q]).astype(BF16)
    uq = _dot(h, w_ref[:, COL_Q:COL_K]) * (DIFF_DK ** -0.5)
    uk = _dot(h, w_ref[:, COL_K:COL_V])
    uv = _dot(h, w_ref[:, COL_V:IN_COLS])
    if rope:
        cos = cos_ref[...]
        sin = sin_ref[...]
        lane = lax.broadcasted_iota(jnp.int32, cos.shape, 1)
        lo_mask = (lane % 32) < 16
        for hh in range(N_HEADS):
            sl = slice(hh * HEAD_W, (hh + 1) * HEAD_W)
            q_ref[:, sl] = _rope_slab(uq[:, sl], cos, sin, lo_mask).astype(BF16)
            k_ref[:, sl] = _rope_slab(uk[:, sl], cos, sin, lo_mask).astype(BF16)
    else:
        q_ref[...] = uq.astype(BF16)
        k_ref[...] = uk.astype(BF16)
    v_ref[...] = uv.astype(BF16)
    if cache == "alias":
        for hh in range(N_HEADS):
            sl = slice(hh * HEAD_W, (hh + 1) * HEAD_W)
            ck_ref[hh] = uk[:, sl]
            cv_ref[hh] = uv[:, sl]
    elif cache == "new":
        for d in range(DEPTH):
            for hh in range(N_HEADS):
                sl = slice(hh * HEAD_W, (hh + 1) * HEAD_W)
                ck_ref[d, hh] = uk[:, sl] if d == layer else jnp.zeros_like(uk[:, sl])
                cv_ref[d, hh] = uv[:, sl] if d == layer else jnp.zeros_like(uv[:, sl])


def _premix(x, mod_l, g, w_in_bf, *, seq_len, cond_row0, per_seq_cond, layer, n_tokens, tok0,
            rope_tables=None, caches=None, want_cache=False):
    T = n_tokens
    tm = TOKEN_BLOCK
    nb = seq_len // tm
    n_seq = T // seq_len
    b0 = tok0 // tm
    if per_seq_cond:
        mod_map = lambda i: (cond_row0 + i // nb, 0, 0)
    else:
        mod_map = lambda i: (cond_row0, 0, 0)
    in_specs = [
        pl.BlockSpec((tm, D_MODEL), lambda i: (i + b0, 0)),
        pl.BlockSpec((None, N_MOD, D_MODEL), mod_map),
        pl.BlockSpec((1, D_MODEL), lambda i: (0, 0)),
        pl.BlockSpec((D_MODEL, IN_COLS), lambda i: (0, 0)),
    ]
    args = [x, mod_l, g.reshape(1, D_MODEL), w_in_bf]
    rope = rope_tables is not None
    if rope:
        in_specs += [pl.BlockSpec((tm, HEAD_W), lambda i: (i % nb, 0))] * 2
        args += list(rope_tables)
    out_shape = [
        jax.ShapeDtypeStruct((T, COL_HY), BF16),
        jax.ShapeDtypeStruct((T, 3 * D_HYENA), BF16),
        jax.ShapeDtypeStruct((T, D_ATTN), BF16),
        jax.ShapeDtypeStruct((T, D_ATTN), BF16),
        jax.ShapeDtypeStruct((T, D_ATTN), BF16),
    ]
    out_specs = [
        pl.BlockSpec((tm, COL_HY), lambda i: (i, 0)),
        pl.BlockSpec((tm, 3 * D_HYENA), lambda i: (i, 0)),
        pl.BlockSpec((tm, D_ATTN), lambda i: (i, 0)),
        pl.BlockSpec((tm, D_ATTN), lambda i: (i, 0)),
        pl.BlockSpec((tm, D_ATTN), lambda i: (i, 0)),
    ]
    cache = None
    aliases = {}
    if want_cache:
        cache = "new"
        cshape = (n_seq, DEPTH, N_HEADS, seq_len, HEAD_W)
        out_shape += [jax.ShapeDtypeStruct(cshape, F32)] * 2
        if caches is None:
            out_specs += [pl.BlockSpec((None, DEPTH, N_HEADS, tm, HEAD_W),
                                       lambda i: (i // nb, 0, 0, i % nb, 0))] * 2
        else:
            cache = "alias"
            out_specs += [pl.BlockSpec((None, None, N_HEADS, tm, HEAD_W),
                                       lambda i: (i // nb, layer, 0, i % nb, 0))] * 2
            aliases = {len(args): 5, len(args) + 1: 6}
            in_specs += [pl.BlockSpec(memory_space=pl.ANY)] * 2
            args += list(caches)
    return pl.pallas_call(
        functools.partial(_premix_kernel, rope=rope, cache=cache, layer=layer),
        out_shape=out_shape,
        grid=(T // tm,),
        in_specs=in_specs,
        out_specs=out_specs,
        input_output_aliases=aliases,
        compiler_params=_cparams("arbitrary"),
        name="premix",
    )(*args)


def _conformer_kernel(u_ref, w_ref, cb_ref, g_ref, b_ref, o_ref, pad_ref, sh_ref, *, seq_len, rows):
    L = seq_len
    u = u_ref[...].astype(F32)
    a = u[:, :D_CONV] * jax.nn.sigmoid(u[:, D_CONV:])
    zeros = jnp.zeros((CONV_PAD, D_CONV), F32)
    pad_ref[0:CONV_PAD, :] = zeros
    pad_ref[CONV_PAD + L:2 * CONV_PAD + L, :] = zeros
    pad_ref[CONV_PAD:CONV_PAD + L, :] = a
    span = L + 2 * CONV_PAD - SUBLANES
    for r in range(SUBLANES):
        sh_ref[r, 0:span, :] = pad_ref[r:r + span, :]
    first = CONV_PAD - (CONV_WIDTH - 1) // 2

    def chunk(c, carry):
        base = pl.multiple_of(c * rows, rows)
        acc = jnp.zeros((rows, D_CONV), F32)
        for t in range(CONV_WIDTH):
            off = first + t
            tap = sh_ref[off % SUBLANES, pl.ds(base + (off // SUBLANES) * SUBLANES, rows), :]
            acc = acc + tap * w_ref[t:t + 1, :]
        acc = acc + cb_ref[...]
        mu = jnp.mean(acc, axis=-1, keepdims=True)
        d = acc - mu
        var = jnp.mean(d * d, axis=-1, keepdims=True)
        y = d * lax.rsqrt(var + EPS) * g_ref[...] + b_ref[...]
        o_ref[pl.ds(base, rows), :] = _silu(y).astype(BF16)
        return carry

    lax.fori_loop(0, L // rows, chunk, 0)


def _conformer(uc, conv_w, conv_b, ln_g, ln_b, *, seq_len):
    T = uc.shape[0]
    n_seq = T // seq_len
    vec = lambda a: a.reshape(1, D_CONV)
    return pl.pallas_call(
        functools.partial(_conformer_kernel, seq_len=seq_len, rows=64),
        out_shape=jax.ShapeDtypeStruct((T, D_CONV), BF16),
        grid=(n_seq,),
        in_specs=[
            pl.BlockSpec((seq_len, 2 * D_CONV), lambda i: (i, 0)),
            pl.BlockSpec((CONV_WIDTH, D_CONV), lambda i: (0, 0)),
            pl.BlockSpec((1, D_CONV), lambda i: (0, 0)),
            pl.BlockSpec((1, D_CONV), lambda i: (0, 0)),
            pl.BlockSpec((1, D_CONV), lambda i: (0, 0)),
        ],
        out_specs=pl.BlockSpec((seq_len, D_CONV), lambda i: (i, 0)),
        scratch_shapes=[pltpu.VMEM((seq_len + 2 * CONV_PAD, D_CONV), F32),
                        pltpu.VMEM((SUBLANES, seq_len + 2 * CONV_PAD, D_CONV), F32)],
        compiler_params=_cparams("arbitrary"),
        name="conformer",
    )(uc, conv_w, vec(conv_b), vec(ln_g), vec(ln_b))


def _dft_matrices(L):
    k = np.arange(L, dtype=np.int64)[:, None]
    m = np.arange(L, dtype=np.int64)[None, :]
    ang = 2.0 * np.pi * (((2 * k + 1) * m) % (4 * L)).astype(np.float64) / (4 * L)
    fwd = np.concatenate([np.cos(ang), np.sin(ang)], axis=0).astype(np.float32)
    return fwd, np.ascontiguousarray(fwd.T)


def _filter_tables(L):
    t = np.linspace(0.0, 1.0, L)
    bands = np.linspace(1e-4, HYENA_BANDS - 1, HYENA_BANDS)
    ang = (2.0 * np.pi / L) * np.arange(L)[:, None] * bands[None, :]
    feats = np.zeros((L, FEAT_PAD), np.float32)
    feats[:, :HYENA_EMB] = np.concatenate([t[:, None], np.cos(ang), -np.sin(ang)], axis=-1)
    deltas = np.abs(np.linspace(HYENA_MIN_DECAY, HYENA_MAX_DECAY, D_HYENA))
    window = np.exp(-t[:, None] * deltas[None, :]).astype(np.float32)
    return feats, window


def _dot3(a, b):
    a_hi, a_lo = _split_bf16(a)
    b_hi, b_lo = _split_bf16(b)
    return _dot(a_hi, b_hi) + (_dot(a_hi, b_lo) + _dot(a_lo, b_hi))


def _spectrum_kernel(feat_ref, win_ref, w1_ref, b1_ref, w2_ref, b2_ref, fr_ref, w3_ref, f_ref,
                     o_ref, *, seq_len):
    L = seq_len
    C = D_HYENA
    fr = fr_ref[...]
    hid = jnp.sin(fr * (_dot3(feat_ref[...], w1_ref[...]) + b1_ref[...]))
    hid = jnp.sin(fr * (_dot3(hid, w2_ref[...]) + b2_ref[...]))
    filt = _dot3(hid, w3_ref[...])
    win = win_ref[...]
    first = lax.broadcasted_iota(jnp.int32, (L, C), 0) == 0
    for o in range(2):
        fwd = filt[:, o * C:(o + 1) * C] * win
        bwd = jnp.where(first, 0.0, filt[:, (2 + o) * C:(3 + o) * C] * win)
        o_ref[o, 0:L, :] = _dot3(f_ref[0:L, :], fwd + bwd) * (1.0 / L)
        o_ref[o, L:2 * L, :] = _dot3(f_ref[L:2 * L, :], bwd - fwd) * (1.0 / L)


def _hyena_spectrum(L, fmat, feats, window, hy_w1, hy_b1, hy_w2, hy_b2, hy_freq, hy_w3):
    w1 = jnp.zeros((FEAT_PAD, hy_w1.shape[1]), F32).at[:HYENA_EMB].set(hy_w1)
    vec = lambda a: a.reshape(1, -1)
    return pl.pallas_call(
        functools.partial(_spectrum_kernel, seq_len=L),
        out_shape=jax.ShapeDtypeStruct((2, 2 * L, D_HYENA), F32),
        compiler_params=pltpu.CompilerParams(vmem_limit_bytes=VMEM_LIMIT),
        name="hyena_spectrum",
    )(feats, window, w1, vec(hy_b1), hy_w2, vec(hy_b2), vec(hy_freq), hy_w3, fmat)


def _hyena_kernel(u_ref, sw_ref, sb_ref, hb_ref, f_ref, ft_ref, ks_ref, o_ref, pad_ref,
                  *, seq_len, n_seq):
    L = seq_len
    C = D_HYENA
    zrow = jnp.zeros((8, 3 * C), F32)
    for b in range(n_seq):
        u = u_ref[b].astype(F32)
        pad_ref[0:8, :] = zrow
        pad_ref[8 + L:16 + L, :] = zrow
        pad_ref[8:8 + L, :] = u
        s = (pad_ref[7:7 + L, :] * sw_ref[0:1, :] + u * sw_ref[1:2, :]
             + pad_ref[9:9 + L, :] * sw_ref[2:3, :] + sb_ref[...])
        z = s[:, 0:C]
        for o in range(2):
            gate = s[:, (o + 1) * C:(o + 2) * C]
            spec = _dot(f_ref[...], z.astype(BF16))
            zr, za = spec[:L], spec[L:]
            kre, kim = ks_ref[o, 0:L, :], ks_ref[o, L:2 * L, :]
            y = jnp.concatenate([zr * kre + za * kim, za * kre - zr * kim], axis=0)
            conv = _dot(ft_ref[...], y.astype(BF16))
            z = gate * (conv + hb_ref[o:o + 1, :] * z)
        o_ref[b] = z.astype(BF16)


def _hyena(uh, short_w, short_b, hy_bias, fmat_bf, fmat_t_bf, kspec, *, seq_len, seqs_per_step):
    T = uh.shape[0]
    n_seq = T // seq_len
    L = seq_len
    nb = seqs_per_step
    u3 = uh.reshape(n_seq, L, 3 * D_HYENA)
    out = pl.pallas_call(
        functools.partial(_hyena_kernel, seq_len=L, n_seq=nb),
        out_shape=jax.ShapeDtypeStruct((n_seq, L, D_HYENA), BF16),
        grid=(n_seq // nb,),
        in_specs=[
            pl.BlockSpec((nb, L, 3 * D_HYENA), lambda i: (i, 0, 0)),
            pl.BlockSpec((3, 3 * D_HYENA), lambda i: (0, 0)),
            pl.BlockSpec((1, 3 * D_HYENA), lambda i: (0, 0)),
            pl.BlockSpec((2, D_HYENA), lambda i: (0, 0)),
            pl.BlockSpec((2 * L, L), lambda i: (0, 0)),
            pl.BlockSpec((L, 2 * L), lambda i: (0, 0)),
            pl.BlockSpec((2, 2 * L, D_HYENA), lambda i: (0, 0, 0)),
        ],
        out_specs=pl.BlockSpec((nb, L, D_HYENA), lambda i: (i, 0, 0)),
        scratch_shapes=[pltpu.VMEM((L + 16, 3 * D_HYENA), F32)],
        compiler_params=_cparams("arbitrary"),
        name="hyena",
    )(u3, short_w, short_b.reshape(1, 3 * D_HYENA), hy_bias, fmat_bf, fmat_t_bf, kspec)
    return out.reshape(T, D_HYENA)


def _attn_kernel(*refs, lam_init, has_ctx):
    q_ref, k_ref, v_ref = refs[:3]
    pos = 3
    if has_ctx:
        ck_ref, cv_ref = refs[pos:pos + 2]
        pos += 2
    lam_ref, g_ref, o_ref = refs[pos:pos + 3]

    lv = lam_ref[...]
    lam = (jnp.exp(jnp.sum(lv[0:1] * lv[1:2], axis=-1, keepdims=True))
           - jnp.exp(jnp.sum(lv[2:3] * lv[3:4], axis=-1, keepdims=True)) + lam_init)

    lane = lax.broadcasted_iota(jnp.int32, (q_ref.shape[0], HEAD_W), 1)
    dn = (((1,), (1,)), ((), ()))
    for h in range(N_HEADS):
        sl = slice(h * HEAD_W, (h + 1) * HEAD_W)
        q = q_ref[:, sl]
        k = k_ref[:, sl]
        v = v_ref[:, sl]
        if has_ctx:
            ck = ck_ref[h].astype(BF16)
            cv = cv_ref[h].astype(BF16)
        zero = jnp.zeros_like(q)
        outs = []
        for c in range(2):
            qc = jnp.where((lane >= c * DIFF_DK) & (lane < (c + 1) * DIFF_DK), q, zero)
            s = lax.dot_general(qc, k, dn, preferred_element_type=F32)
            m = jnp.max(s, axis=-1, keepdims=True)
            if has_ctx:
                sc = lax.dot_general(qc, ck, dn, preferred_element_type=F32)
                m = jnp.maximum(m, jnp.max(sc, axis=-1, keepdims=True))
            p = jnp.exp(s - m)
            den = jnp.sum(p, axis=-1, keepdims=True)
            acc = _dot(p.astype(BF16), v)
            if has_ctx:
                pc = jnp.exp(sc - m)
                den = den + jnp.sum(pc, axis=-1, keepdims=True)
                acc = acc + _dot(pc.astype(BF16), cv)
            outs.append(acc / den)
        o = outs[0] - lam * outs[1]
        o_ref[:, sl] = (_rms(o, g_ref[...]) * (1.0 - lam_init)).astype(BF16)


def _attention(q, k, v, lam_vecs, subln_g, *, seq_len, layer, ctx=None):
    T = q.shape[0]
    L = seq_len
    n_seq = T // L
    tq = TOKEN_BLOCK
    nq = L // tq
    lam_init = 0.8 - 0.6 * math.exp(-0.3 * layer)
    q3, k3, v3 = (a.reshape(n_seq, L, D_ATTN) for a in (q, k, v))
    in_specs = [
        pl.BlockSpec((None, tq, D_ATTN), lambda b, i: (b, i, 0)),
        pl.BlockSpec((None, L, D_ATTN), lambda b, i: (b, 0, 0)),
        pl.BlockSpec((None, L, D_ATTN), lambda b, i: (b, 0, 0)),
    ]
    args = [q3, k3, v3]
    if ctx is not None:
        past = ctx[0].shape[3]
        in_specs += [pl.BlockSpec((None, None, N_HEADS, past, HEAD_W),
                                  lambda b, i: (b, layer, 0, 0, 0))] * 2
        args += list(ctx)
    in_specs += [
        pl.BlockSpec((4, DIFF_DK), lambda b, i: (0, 0)),
        pl.BlockSpec((1, DIFF_DV), lambda b, i: (0, 0)),
    ]
    args += [lam_vecs, subln_g.reshape(1, DIFF_DV)]
    out = pl.pallas_call(
        functools.partial(_attn_kernel, lam_init=lam_init, has_ctx=ctx is not None),
        out_shape=jax.ShapeDtypeStruct((n_seq, L, D_ATTN), BF16),
        grid=(n_seq, nq),
        in_specs=in_specs,
        out_specs=pl.BlockSpec((None, tq, D_ATTN), lambda b, i: (b, i, 0)),
        compiler_params=_cparams("arbitrary", "arbitrary"),
        name="diff_attention",
    )(*args)
    return out.reshape(T, D_ATTN)


def _split_bf16(a):
    hi = a.astype(BF16)
    lo = (a - hi.astype(F32)).astype(BF16)
    return hi, lo


def _pack_pair(a, b):
    ua = lax.bitcast_convert_type(a.astype(BF16).astype(F32), jnp.uint32)
    ub = lax.bitcast_convert_type(b.astype(BF16).astype(F32), jnp.uint32)
    return (ua >> 16) | ub


def _unpack_pair(w):
    a = lax.bitcast_convert_type(w << 16, F32)
    b = lax.bitcast_convert_type(w & jnp.uint32(0xFFFF0000), F32)
    return a, b


def _postmix_kernel(*refs, stream_blocks):
    ns = len(stream_blocks)
    mod_ref, gpost_ref, gpre_ref, w_ref, rw_ref, x1_ref, h2_ref, lg_ref = refs[4 * ns:]
    i = pl.program_id(0)

    def body(ya_ref, yb_ref, o_ref, x_ref):
        y = (_dot(ya_ref[...], w_ref[0:D_CONV, :])
             + _dot(yb_ref[...], w_ref[D_CONV:D_CONV + D_HYENA, :])
             + _dot(o_ref[...], w_ref[D_CONV + D_HYENA:, :]))
        mod = mod_ref[...]
        x1 = x_ref[...] + mod[2:3] * _rms(y, gpost_ref[...])
        x1_ref[...] = x1
        h2 = _rms(x1, gpre_ref[...]) * (1.0 + mod[4:5]) + mod[3:4]
        h2_ref[...] = _pack_pair(h2[:, :HALF], h2[:, HALF:])
        h_hi, h_lo = _split_bf16(h2)
        r_hi, r_lo = _split_bf16(rw_ref[...])
        dn = (((1,), (1,)), ((), ()))
        mm = lambda a, b: lax.dot_general(a, b, dn, preferred_element_type=F32)
        lg_ref[...] = mm(r_hi, h_hi) + (mm(r_lo, h_hi) + mm(r_hi, h_lo))

    lo = 0
    for j, nbk in enumerate(stream_blocks):
        pl.when((i >= lo) & (i < lo + nbk))(functools.partial(body, *refs[4 * j:4 * j + 4]))
        lo += nbk


def _postmix(parts, mod_l, g_post, g_pre, w_out_bf, router_wt, *, mod_map):
    t_all = sum(p[0].shape[0] for p in parts)
    tm = TOKEN_BLOCK
    full = lambda a: pl.BlockSpec(a.shape, lambda i: (0,) * a.ndim)
    gp, ge = g_post.reshape(1, D_MODEL), g_pre.reshape(1, D_MODEL)
    stream_blocks = tuple(p[0].shape[0] // tm for p in parts)
    in_specs, args = [], []
    lo = 0
    for part, nbk in zip(parts, stream_blocks):
        local = lambda i, lo=lo, nbk=nbk: (jnp.clip(i - lo, 0, nbk - 1), 0)
        in_specs += [pl.BlockSpec((tm, D_CONV), local), pl.BlockSpec((tm, D_HYENA), local),
                     pl.BlockSpec((tm, D_ATTN), local), pl.BlockSpec((tm, D_MODEL), local)]
        args += list(part)
        lo += nbk
    in_specs += [pl.BlockSpec((None, N_MOD, D_MODEL), mod_map),
                 full(gp), full(ge), full(w_out_bf), full(router_wt)]
    args += [mod_l, gp, ge, w_out_bf, router_wt]
    return pl.pallas_call(
        functools.partial(_postmix_kernel, stream_blocks=stream_blocks),
        out_shape=[
            jax.ShapeDtypeStruct((t_all, D_MODEL), F32),
            jax.ShapeDtypeStruct((t_all, HALF), jnp.uint32),
            jax.ShapeDtypeStruct((N_EXPERTS, t_all), F32),
        ],
        grid=(t_all // tm,),
        in_specs=in_specs,
        out_specs=[pl.BlockSpec((tm, D_MODEL), lambda i: (i, 0)),
                   pl.BlockSpec((tm, HALF), lambda i: (i, 0)),
                   pl.BlockSpec((N_EXPERTS, tm), lambda i: (0, i))],
        compiler_params=_cparams("arbitrary"),
        name="postmix",
    )(*args)


def _first_max(vals, idx, sentinel):
    m = jnp.max(vals, axis=0, keepdims=True)
    i = jnp.min(jnp.where(vals == m, idx, sentinel), axis=0, keepdims=True)
    return m, i


def _route_topk_kernel(lg_ref, rb_ref, eidx_ref, w_ref, cnt_ref):
    step = pl.program_id(0)
    per = N_EXPERTS // N_GROUPS
    n = lg_ref.shape[1]
    s = jax.nn.sigmoid(lg_ref[...])
    sel = s + rb_ref[...]
    sub = lax.broadcasted_iota(jnp.int32, (per, n), 0)
    neg = jnp.float32(-jnp.inf)

    gs = []
    for g in range(N_GROUPS):
        xg = sel[g * per:(g + 1) * per]
        total = jnp.zeros((1, n), F32)
        for _ in range(GROUP_SCORE_TOP):
            m, i = _first_max(xg, sub, per)
            total = total + m
            xg = jnp.where(sub == i, neg, xg)
        gs.append(total)
    gscore = jnp.concatenate(gs, axis=0)
    gkeep = jnp.zeros((N_GROUPS, n), F32)
    for _ in range(TOPK_GROUPS):
        _, gi = _first_max(gscore, sub, N_GROUPS)
        hit = sub == gi
        gkeep = jnp.where(hit, 1.0, gkeep)
        gscore = jnp.where(hit, neg, gscore)

    eio = [sub + g * per for g in range(N_GROUPS)]
    sg = [s[g * per:(g + 1) * per] for g in range(N_GROUPS)]
    mk = [jnp.where(gkeep[g:g + 1] > 0.5, sel[g * per:(g + 1) * per], neg) for g in range(N_GROUPS)]
    oh = [jnp.zeros((per, n), F32) for _ in range(N_GROUPS)]
    idx_rows, s_rows = [], []
    for _ in range(TOP_K):
        best = functools.reduce(jnp.maximum, mk)
        m = jnp.max(best, axis=0, keepdims=True)
        cand = functools.reduce(jnp.minimum,
                                [jnp.where(mk[g] == m, eio[g], N_EXPERTS) for g in range(N_GROUPS)])
        idx = jnp.min(cand, axis=0, keepdims=True)
        acc = jnp.zeros((per, n), F32)
        for g in range(N_GROUPS):
            hit = eio[g] == idx
            acc = acc + jnp.where(hit, sg[g], 0.0)
            mk[g] = jnp.where(hit, neg, mk[g])
            oh[g] = jnp.where(hit, 1.0, oh[g])
        idx_rows.append(idx)
        s_rows.append(jnp.sum(acc, axis=0, keepdims=True))
    eidx_ref[...] = jnp.concatenate(idx_rows, axis=0)
    sk = jnp.concatenate(s_rows, axis=0)
    w_ref[...] = sk / jnp.sum(sk, axis=0, keepdims=True) * ROUTED_SCALE

    @pl.when(step == 0)
    def _():
        cnt_ref[...] = jnp.zeros_like(cnt_ref)

    tot = jnp.concatenate([jnp.sum(o, axis=1, keepdims=True) for o in oh], axis=0)
    cnt_ref[...] = cnt_ref[...] + jnp.broadcast_to(tot, cnt_ref.shape).astype(jnp.int32)


def _route_topk(logits_t, router_b):
    t_all = logits_t.shape[1]
    rb = ROUTE_BLOCK
    return pl.pallas_call(
        _route_topk_kernel,
        out_shape=[
            jax.ShapeDtypeStruct((TOP_K, t_all), jnp.int32),
            jax.ShapeDtypeStruct((TOP_K, t_all), F32),
            jax.ShapeDtypeStruct((N_EXPERTS, LANES), jnp.int32),
        ],
        grid=(t_all // rb,),
        in_specs=[pl.BlockSpec((N_EXPERTS, rb), lambda i: (0, i)),
                  pl.BlockSpec((N_EXPERTS, 1), lambda i: (0, 0))],
        out_specs=[pl.BlockSpec((TOP_K, rb), lambda i: (0, i)),
                   pl.BlockSpec((TOP_K, rb), lambda i: (0, i)),
                   pl.BlockSpec((N_EXPERTS, LANES), lambda i: (0, 0))],
        compiler_params=_cparams("arbitrary"),
        name="route_topk",
    )(logits_t, router_b.reshape(N_EXPERTS, 1))


def _route_dest_kernel(eidx_ref, ps_ref, dest_ref, carry_ref):
    step = pl.program_id(0)
    per = N_EXPERTS // N_GROUPS
    n = eidx_ref.shape[1]

    @pl.when(step == 0)
    def _():
        carry_ref[...] = jnp.zeros_like(carry_ref)

    eidx = eidx_ref[...]
    sub = lax.broadcasted_iota(jnp.int32, (per, n), 0)
    rows = []
    for g in range(N_GROUPS):
        eio = sub + g * per
        o = jnp.zeros((per, n), F32)
        for k in range(TOP_K):
            o = jnp.where(eio == eidx[k:k + 1], 1.0, o)
        rows.append(o)
    onehot = jnp.concatenate(rows, axis=0)
    r = lax.broadcasted_iota(jnp.int32, (n, n), 0)
    c = lax.broadcasted_iota(jnp.int32, (n, n), 1)
    before = jnp.where(r < c, 1.0, 0.0).astype(BF16)
    rank = _dot(onehot.astype(BF16), before)
    base = (rank + carry_ref[:, 0:1]).astype(jnp.int32) + ps_ref[...]
    out = []
    for k in range(TOP_K):
        acc = jnp.zeros((per, n), jnp.int32)
        for g in range(N_GROUPS):
            acc = acc + jnp.where(sub + g * per == eidx[k:k + 1], base[g * per:(g + 1) * per], 0)
        out.append(jnp.sum(acc, axis=0, keepdims=True))
    dest_ref[...] = jnp.concatenate(out, axis=0)
    tot = jnp.sum(onehot, axis=1, keepdims=True)
    carry_ref[...] = carry_ref[...] + jnp.broadcast_to(tot, carry_ref.shape)


def _route_dest(eidx_t, pstart):
    t_all = eidx_t.shape[1]
    rb = ROUTE_BLOCK
    return pl.pallas_call(
        _route_dest_kernel,
        out_shape=jax.ShapeDtypeStruct((TOP_K, t_all), jnp.int32),
        grid=(t_all // rb,),
        in_specs=[pl.BlockSpec((TOP_K, rb), lambda i: (0, i)),
                  pl.BlockSpec((N_EXPERTS, 1), lambda i: (0, 0))],
        out_specs=pl.BlockSpec((TOP_K, rb), lambda i: (0, i)),
        scratch_shapes=[pltpu.VMEM((N_EXPERTS, LANES), F32)],
        compiler_params=_cparams("arbitrary"),
        name="route_dest",
    )(eidx_t, pstart.reshape(N_EXPERTS, 1))


def _row_copy(src_hbm, src_row, dst_hbm, dst_row, sem):
    return pltpu.make_async_copy(src_hbm.at[pl.ds(src_row, 1)], dst_hbm.at[pl.ds(dst_row, 1)], sem)


def _wait_rows(hbm, sem, n_rows):
    pltpu.make_async_copy(hbm.at[pl.ds(0, n_rows)], hbm.at[pl.ds(0, n_rows)], sem).wait()


def _zero_padding(cnt_ref, ps_ref, nu_ref, zero_ref, x_hbm, zsem, n_blocks, *, wait):
    def copy(row, size):
        cp = pltpu.make_async_copy(zero_ref.at[pl.ds(0, size)], x_hbm.at[pl.ds(row, size)], zsem)
        if wait:
            cp.wait()
        else:
            cp.start()

    def per_expert(e, carry):
        cnt = cnt_ref[e]
        pad = (EXPERT_BLOCK - cnt % EXPERT_BLOCK) % EXPERT_BLOCK
        row = ps_ref[e] + cnt
        head = pad % SUBLANES
        for j in range(SUBLANES - 1):
            pl.when(j < head)(functools.partial(copy, row + j, 1))
        row = row + head
        size = EXPERT_BLOCK // 2
        while size >= SUBLANES:
            take = (pad & size) != 0
            pl.when(take)(functools.partial(copy, pl.multiple_of(row, SUBLANES), size))
            row = row + jnp.where(take, size, 0)
            size //= 2
        return carry

    lax.fori_loop(0, N_EXPERTS, per_expert, 0)

    def per_block(j, carry):
        copy(pl.multiple_of(j * EXPERT_BLOCK, EXPERT_BLOCK), EXPERT_BLOCK)
        return carry

    lax.fori_loop(nu_ref[0], n_blocks, per_block, 0)


def _dispatch_kernel(dest_ref, cnt_ref, ps_ref, nu_ref, h_ref, x_hbm, zero_ref, sem, zsem):
    step = pl.program_id(0)
    nsteps = pl.num_programs(0)
    n = h_ref.shape[0]
    zero_args = (cnt_ref, ps_ref, nu_ref, zero_ref, x_hbm, zsem, x_hbm.shape[0] // EXPERT_BLOCK)

    @pl.when(step == 0)
    def _():
        zero_ref[...] = jnp.zeros_like(zero_ref)
        _zero_padding(*zero_args, wait=False)

    def per_group(g, carry):
        base = pl.multiple_of(g * SUBLANES, SUBLANES)
        for j in range(SUBLANES):
            for k in range(TOP_K):
                d = dest_ref[base * TOP_K + (j * TOP_K + k)]
                pltpu.make_async_copy(h_ref.at[pl.ds(base + j, 1)],
                                      x_hbm.at[pl.ds(d, 1)], sem).start()
        return carry

    lax.fori_loop(0, n // SUBLANES, per_group, 0)
    _wait_rows(x_hbm, sem, n * TOP_K)

    @pl.when(step == nsteps - 1)
    def _():
        _zero_padding(*zero_args, wait=True)


def _dispatch(dest, counts, pstart, n_used, h2p, n_blocks):
    t_all = h2p.shape[0]
    sb = TOKEN_BLOCK
    smem = lambda: pl.BlockSpec(memory_space=pltpu.SMEM)
    return pl.pallas_call(
        _dispatch_kernel,
        out_shape=jax.ShapeDtypeStruct((n_blocks * EXPERT_BLOCK, HALF), jnp.uint32),
        grid=(t_all // sb,),
        in_specs=[pl.BlockSpec((sb * TOP_K,), lambda i: (i,), memory_space=pltpu.SMEM),
                  smem(), smem(), smem(),
                  pl.BlockSpec((sb, HALF), lambda i: (i, 0))],
        out_specs=pl.BlockSpec(memory_space=pl.ANY),
        scratch_shapes=[pltpu.VMEM((EXPERT_BLOCK, HALF), jnp.uint32),
                        pltpu.SemaphoreType.DMA, pltpu.SemaphoreType.DMA],
        compiler_params=_cparams("arbitrary"),
        name="dispatch",
    )(dest, counts, pstart, n_used, h2p)


def _expert_kernel(nb_ref, ps_ref, nu_ref, wg_ref, wu_ref, wd_ref, x_hbm, y_hbm,
                   wg_bf, wu_bf, wd_bf, xbuf, ybuf, xsem, ysem):
    e = pl.program_id(0)
    nblk = nb_ref[e]
    first = ps_ref[e] // EXPERT_BLOCK
    n_used = nu_ref[0]
    xs, ys = xbuf.shape[0], ybuf.shape[0]
    wg_bf[...] = wg_ref[...].astype(BF16)
    wu_bf[...] = wu_ref[...].astype(BF16)
    wd_bf[...] = wd_ref[...].astype(BF16)

    def rows(b):
        return pl.ds(pl.multiple_of(b * EXPERT_BLOCK, EXPERT_BLOCK), EXPERT_BLOCK)

    def x_copy(b):
        return pltpu.make_async_copy(x_hbm.at[rows(b)], xbuf.at[b % xs], xsem.at[b % xs])

    def y_copy(b):
        return pltpu.make_async_copy(ybuf.at[b % ys], y_hbm.at[rows(b)], ysem.at[b % ys])

    @pl.when(e == 0)
    def _():
        for b in range(xs - 1):
            pl.when(b < n_used)(lambda b=b: x_copy(b).start())

    def block(j, carry):
        b = first + j
        x_copy(b).wait()
        pl.when(b + xs - 1 < n_used)(lambda: x_copy(b + xs - 1).start())
        pl.when(b >= ys)(lambda: y_copy(b - ys).wait())
        y = _swiglu_packed(xbuf[b % xs], wg_bf, wu_bf, wd_bf)
        ybuf[b % ys] = _pack_pair(y[:, :HALF], y[:, HALF:])
        y_copy(b).start()
        return carry

    lax.fori_loop(0, nblk, block, 0)

    @pl.when(e == pl.num_programs(0) - 1)
    def _():
        for d in range(ys):
            b = n_used - ys + d
            pl.when(b >= 0)(lambda b=b: y_copy(b).wait())
        n_blocks = y_hbm.shape[0] // EXPERT_BLOCK
        ybuf[0] = jnp.zeros_like(ybuf[0])

        def tail(j, carry, wait):
            cp = pltpu.make_async_copy(
                ybuf.at[0], y_hbm.at[pl.ds(pl.multiple_of(j * EXPERT_BLOCK, EXPERT_BLOCK),
                                           EXPERT_BLOCK)], ysem.at[0])
            cp.wait() if wait else cp.start()
            return carry

        lax.fori_loop(nu_ref[0], n_blocks, functools.partial(tail, wait=False), 0)
        lax.fori_loop(nu_ref[0], n_blocks, functools.partial(tail, wait=True), 0)


def _swiglu_packed(xp, wg_ref, wu_ref, wd_ref):
    xa, xb = _unpack_pair(xp)
    xa, xb = xa.astype(BF16), xb.astype(BF16)
    gate = _dot(xa, wg_ref[0:HALF, :]) + _dot(xb, wg_ref[HALF:, :])
    up = _dot(xa, wu_ref[0:HALF, :]) + _dot(xb, wu_ref[HALF:, :])
    return _dot((_silu(gate) * up).astype(BF16), wd_ref[...])


def _experts(x_buf, nblk, pstart, n_used, wg, wu, wd, *, layer):
    grid_spec = pltpu.PrefetchScalarGridSpec(
        num_scalar_prefetch=3,
        grid=(N_EXPERTS,),
        in_specs=[
            pl.BlockSpec((None, None, D_MODEL, D_EXPERT), lambda e, *_: (layer, e, 0, 0)),
            pl.BlockSpec((None, None, D_MODEL, D_EXPERT), lambda e, *_: (layer, e, 0, 0)),
            pl.BlockSpec((None, None, D_EXPERT, D_MODEL), lambda e, *_: (layer, e, 0, 0)),
            pl.BlockSpec(memory_space=pl.ANY),
        ],
        out_specs=pl.BlockSpec(memory_space=pl.ANY),
        scratch_shapes=[
            pltpu.VMEM((D_MODEL, D_EXPERT), BF16),
            pltpu.VMEM((D_MODEL, D_EXPERT), BF16),
            pltpu.VMEM((D_EXPERT, D_MODEL), BF16),
            pltpu.VMEM((EXPERT_IN_SLOTS, EXPERT_BLOCK, HALF), jnp.uint32),
            pltpu.VMEM((EXPERT_OUT_SLOTS, EXPERT_BLOCK, HALF), jnp.uint32),
            pltpu.SemaphoreType.DMA((EXPERT_IN_SLOTS,)),
            pltpu.SemaphoreType.DMA((EXPERT_OUT_SLOTS,)),
        ],
    )
    return pl.pallas_call(
        _expert_kernel,
        out_shape=jax.ShapeDtypeStruct(x_buf.shape, jnp.uint32),
        grid_spec=grid_spec,
        compiler_params=_cparams("arbitrary"),
        name="experts",
    )(nblk, pstart, n_used, wg, wu, wd, x_buf)


def _block_plan(counts, n_blocks):
    nblk = (counts + EXPERT_BLOCK - 1) // EXPERT_BLOCK
    pend = jnp.cumsum(nblk)
    pstart = ((pend - nblk) * EXPERT_BLOCK).astype(jnp.int32)
    return pstart, nblk.astype(jnp.int32), pend[-1:].astype(jnp.int32)


def _ffn_out_kernel(*refs, stream_blocks):
    (dcur_ref, dnxt_ref, h_ref, w_ref, x_ref, mod_ref, g_ref, wg_ref, wu_ref, wd_ref,
     y_hbm) = refs[:11]
    ns = len(stream_blocks)
    out_refs = refs[11:11 + ns]
    rows_ref, sem = refs[11 + ns:]
    step = pl.program_id(0)
    nsteps = pl.num_programs(0)
    n = h_ref.shape[0]

    def gather(d_ref, slot):
        def per_group(g, carry):
            base = pl.multiple_of(g * SUBLANES, SUBLANES)
            for j in range(SUBLANES):
                for k in range(TOP_K):
                    d = d_ref[base * TOP_K + (j * TOP_K + k)]
                    pltpu.make_async_copy(y_hbm.at[pl.ds(d, 1)],
                                          rows_ref.at[slot, k, pl.ds(base + j, 1)],
                                          sem.at[slot]).start()
            return carry
        lax.fori_loop(0, n // SUBLANES, per_group, 0)

    slot = step % 2

    @pl.when(step == 0)
    def _():
        gather(dcur_ref, 0)

    for parity in range(2):
        pl.when((step + 1 < nsteps) & (slot == parity))(
            functools.partial(gather, dnxt_ref, 1 - parity))

    ys = _swiglu_packed(h_ref[...], wg_ref, wu_ref, wd_ref)
    pltpu.make_async_copy(rows_ref.at[slot], rows_ref.at[slot], sem.at[slot]).wait()
    w = w_ref[...]
    ya, yb = ys[:, :HALF], ys[:, HALF:]
    for k in range(TOP_K):
        ra, rb = _unpack_pair(rows_ref[slot, k])
        ya = ya + w[:, k:k + 1] * ra
        yb = yb + w[:, k:k + 1] * rb
    ms = (jnp.sum(ya * ya, axis=-1, keepdims=True)
          + jnp.sum(yb * yb, axis=-1, keepdims=True)) * (1.0 / D_MODEL)
    inv = lax.rsqrt(ms + EPS)
    mod = mod_ref[...]
    g = g_ref[...]
    out_a = x_ref[:, 0:HALF] + mod[5:6, 0:HALF] * (ya * inv * g[:, 0:HALF])
    out_b = x_ref[:, HALF:] + mod[5:6, HALF:] * (yb * inv * g[:, HALF:])

    def store(o_ref):
        o_ref[:, 0:HALF] = out_a
        o_ref[:, HALF:] = out_b

    lo = 0
    for o_ref, nbk in zip(out_refs, stream_blocks):
        pl.when((step >= lo) & (step < lo + nbk))(functools.partial(store, o_ref))
        lo += nbk


def _ffn_out(dest, h2p, w_rows, x1, mod_l, g_post, swg, swu, swd, y_buf, *, mod_map,
             stream_tokens):
    t_all = x1.shape[0]
    tm = TOKEN_BLOCK
    nsteps = t_all // tm
    row = lambda w: pl.BlockSpec((tm, w), lambda i: (i, 0))
    full = lambda a: pl.BlockSpec(a.shape, lambda i: (0,) * a.ndim)
    g = g_post.reshape(1, D_MODEL)
    stream_blocks = tuple(t // tm for t in stream_tokens)
    out_shape, out_specs = [], []
    lo = 0
    for t, nbk in zip(stream_tokens, stream_blocks):
        out_shape.append(jax.ShapeDtypeStruct((t, D_MODEL), F32))
        out_specs.append(pl.BlockSpec((tm, D_MODEL),
                                      lambda i, lo=lo, nbk=nbk: (jnp.clip(i - lo, 0, nbk - 1), 0)))
        lo += nbk
    return pl.pallas_call(
        functools.partial(_ffn_out_kernel, stream_blocks=stream_blocks),
        out_shape=out_shape,
        grid=(nsteps,),
        in_specs=[pl.BlockSpec((tm * TOP_K,), lambda i: (i,), memory_space=pltpu.SMEM),
                  pl.BlockSpec((tm * TOP_K,), lambda i: (jnp.minimum(i + 1, nsteps - 1),),
                               memory_space=pltpu.SMEM),
                  row(HALF), row(TOP_K), row(D_MODEL),
                  pl.BlockSpec((None, N_MOD, D_MODEL), mod_map),
                  full(g), full(swg), full(swu), full(swd),
                  pl.BlockSpec(memory_space=pl.ANY)],
        out_specs=out_specs,
        scratch_shapes=[pltpu.VMEM((2, TOP_K, tm, HALF), jnp.uint32),
                        pltpu.SemaphoreType.DMA((2,))],
        compiler_params=_cparams("arbitrary"),
        name="ffn_out",
    )(dest, dest, h2p, w_rows, x1, mod_l, g, swg, swu, swd, y_buf)


def _rope_tables(L):
    rows = L // GRID_W
    row = jnp.repeat(jnp.arange(rows), GRID_W).astype(F32)
    col = jnp.tile(jnp.arange(GRID_W), rows).astype(F32)
    n = DIFF_DK // 4
    inv = ROPE_BASE ** (-jnp.arange(n, dtype=F32) / n)
    ar, ac = row[:, None] * inv[None, :], col[:, None] * inv[None, :]
    cos64 = jnp.concatenate([jnp.cos(ar), jnp.cos(ar), jnp.cos(ac), jnp.cos(ac)], axis=-1)
    sin64 = jnp.concatenate([-jnp.sin(ar), jnp.sin(ar), -jnp.sin(ac), jnp.sin(ac)], axis=-1)
    return jnp.tile(cos64, (1, 2)), jnp.tile(sin64, (1, 2))


def kernel(x_prompt, x_sample, cache_k, cache_v, c, c_ctx, w_mod, b_mod, g_pre_mix, g_post_mix, g_pre_ffn, g_post_ffn, w_in, w_out, conv_w, conv_b, conv_ln_g, conv_ln_b, hy_short_w, hy_short_b, hy_w1, hy_b1, hy_w2, hy_b2, hy_freq, hy_w3, hy_bias, lam_q1, lam_k1, lam_q2, lam_k2, subln_g, router_w, router_b, exp_w_gate, exp_w_up, exp_w_down, sh_w_gate, sh_w_up, sh_w_down):
    n_p, l_p, _ = x_prompt.shape
    n_s, l_s, _ = x_sample.shape
    t_p, t_s = n_p * l_p, n_s * l_s

    cond = jnp.zeros((COND_ROWS, D_MODEL), F32).at[0].set(c_ctx).at[1:1 + n_s].set(c)
    mod = _modulation(cond, w_mod, b_mod).reshape(DEPTH, COND_ROWS, N_MOD, D_MODEL)
    rope = _rope_tables(l_s)

    t_all = t_p + t_s
    streams = {
        "p": dict(x=x_prompt.reshape(t_p, D_MODEL), T=t_p, L=l_p, row0=0, per_seq=False, hy_nb=8),
        "s": dict(x=x_sample.reshape(t_s, D_MODEL), T=t_s, L=l_s, row0=1, per_seq=True, hy_nb=1),
    }
    for st in streams.values():
        fwd, fwd_t = _dft_matrices(st["L"])
        feats, window = _filter_tables(st["L"])
        st["fmat32"] = jnp.asarray(fwd)
        st["fmat"] = jnp.asarray(fwd).astype(BF16)
        st["fmat_t"] = jnp.asarray(fwd_t).astype(BF16)
        st["filter_tables"] = (jnp.asarray(feats), jnp.asarray(window))
    nb_p, nb_s = t_p // TOKEN_BLOCK, l_s // TOKEN_BLOCK
    merged_mod_map = lambda i: (jnp.where(i < nb_p, 0, 1 + (i - nb_p) // nb_s), 0, 0)
    n_blocks = t_all * TOP_K // EXPERT_BLOCK + N_EXPERTS

    caches = None
    for l in range(DEPTH):
        w_in_bf = w_in[l].astype(BF16)
        w_out_bf = w_out[l].astype(BF16)
        swg, swu, swd = (a[l].astype(BF16) for a in (sh_w_gate, sh_w_up, sh_w_down))
        lam_vecs = jnp.stack([lam_q1[l], lam_k1[l], lam_q2[l], lam_k2[l]], axis=0)
        filt_params = (hy_w1[l], hy_b1[l], hy_w2[l], hy_b2[l], hy_freq[l], hy_w3[l])
        parts = []
        for name, st in streams.items():
            L = st["L"]
            kw = dict(seq_len=L, cond_row0=st["row0"], per_seq_cond=st["per_seq"])
            if name == "p":
                uc, uh, q, k, v, ck, cv = _premix(st["x"], mod[l], g_pre_mix[l], w_in_bf, layer=l,
                                                  n_tokens=st["T"], tok0=0,
                                                  caches=caches, want_cache=True, **kw)
                caches = (ck, cv)
                ctx = None
            else:
                uc, uh, q, k, v = _premix(st["x"], mod[l], g_pre_mix[l], w_in_bf, layer=l,
                                          n_tokens=st["T"], tok0=0, rope_tables=rope, **kw)
                ctx = (cache_k, cache_v)
            ya = _conformer(uc, conv_w[l], conv_b[l], conv_ln_g[l], conv_ln_b[l], seq_len=L)
            kspec = _hyena_spectrum(L, st["fmat32"], *st["filter_tables"], *filt_params)
            yb = _hyena(uh, hy_short_w[l], hy_short_b[l], hy_bias[l], st["fmat"], st["fmat_t"],
                        kspec, seq_len=L, seqs_per_step=st["hy_nb"])
            o = _attention(q, k, v, lam_vecs, subln_g[l], seq_len=L, layer=l, ctx=ctx)
            parts.append((ya, yb, o, st["x"]))
        x1, h2p, logits_t = _postmix(parts, mod[l], g_post_mix[l], g_pre_ffn[l], w_out_bf,
                                     router_w[l].T, mod_map=merged_mod_map)

        eidx_t, w_t, cnt = _route_topk(logits_t, router_b[l])
        counts = cnt[:, 0]
        pstart, nblk, n_used = _block_plan(counts, n_blocks)
        dest = _route_dest(eidx_t, pstart).T.reshape(-1)
        x_buf = _dispatch(dest, counts, pstart, n_used, h2p, n_blocks)
        y_buf = _experts(x_buf, nblk, pstart, n_used, exp_w_gate, exp_w_up, exp_w_down, layer=l)
        outs = _ffn_out(dest, h2p, w_t.T, x1, mod[l], g_post_ffn[l], swg, swu, swd, y_buf,
                        mod_map=merged_mod_map, stream_tokens=(t_p, t_s))
        streams["p"]["x"], streams["s"]["x"] = outs

    y_prompt = streams["p"]["x"].reshape(n_p, l_p, D_MODEL)
    y_sample = streams["s"]["x"].reshape(n_s, l_s, D_MODEL)
    return (y_prompt, y_sample, caches[0], caches[1])
```

```python
import functools
import math

import numpy as np
import jax
import jax.numpy as jnp
from jax import lax
from jax.experimental import pallas as pl
from jax.experimental.pallas import tpu as pltpu

F32 = jnp.float32
BF16 = jnp.bfloat16

D_MODEL = 1024
DEPTH = 2
GRID_W = 64
D_CONV = 256
D_HYENA = 256
D_ATTN = 512
DIFF_DK = 64
N_HEADS = 4
DIFF_DV = 128
HEAD_W = 2 * DIFF_DK
IN_COLS = 2 * D_CONV + 3 * D_HYENA + 3 * D_ATTN
COL_HY = 2 * D_CONV
COL_Q = COL_HY + 3 * D_HYENA
COL_K = COL_Q + D_ATTN
COL_V = COL_K + D_ATTN
CONV_WIDTH = 31
CONV_PAD = 16
SUBLANES = 8
HYENA_EMB = 17
FEAT_PAD = 32
HYENA_BANDS = 8
HYENA_MIN_DECAY = math.log(1e-2) / 1.5
HYENA_MAX_DECAY = math.log(1e-2) / 0.3
N_EXPERTS = 64
TOP_K = 8
N_GROUPS = 8
TOPK_GROUPS = 4
GROUP_SCORE_TOP = 2
D_EXPERT = 256
ROUTED_SCALE = 2.5
ROPE_BASE = 10000.0
EPS = 1e-6
N_MOD = 6
COND_ROWS = 16

HALF = D_MODEL // 2
LANES = 128
TOKEN_BLOCK = 256
SPARE_ROWS = N_EXPERTS * SUBLANES
LOCAL_ROWS = TOKEN_BLOCK * TOP_K + SPARE_ROWS
EXPERT_BLOCK = 256
EXPERT_IN_SLOTS = 4
EXPERT_OUT_SLOTS = 3
VMEM_LIMIT = 48 * 1024 * 1024


def _cparams(*sem):
    return pltpu.CompilerParams(dimension_semantics=sem, vmem_limit_bytes=VMEM_LIMIT)


def _rms(x, g):
    return x * lax.rsqrt(jnp.mean(x * x, axis=-1, keepdims=True) + EPS) * g


def _silu(x):
    return x * jax.nn.sigmoid(x)


def _dot(a, b):
    return jnp.dot(a, b, preferred_element_type=F32)


def _mod_kernel(c_ref, w_ref, b_ref, o_ref):
    s = _silu(c_ref[...])
    o_ref[...] = _dot(s.astype(BF16), w_ref[...].astype(BF16)) + b_ref[...]


def _modulation(cond, w_mod, b_mod):
    cols = N_MOD * D_MODEL
    cb = 1536
    return pl.pallas_call(
        _mod_kernel,
        out_shape=jax.ShapeDtypeStruct((DEPTH, COND_ROWS, cols), F32),
        grid=(DEPTH, cols // cb),
        in_specs=[
            pl.BlockSpec((COND_ROWS, D_MODEL), lambda l, j: (0, 0)),
            pl.BlockSpec((None, D_MODEL, cb), lambda l, j: (l, 0, j)),
            pl.BlockSpec((None, 1, cb), lambda l, j: (l, 0, j)),
        ],
        out_specs=pl.BlockSpec((None, COND_ROWS, cb), lambda l, j: (l, 0, j)),
        compiler_params=_cparams("arbitrary", "arbitrary"),
        name="modulation",
    )(cond, w_mod, b_mod.reshape(DEPTH, 1, cols))


def _rope_slab(x, cos, sin, lo_mask):
    swapped = jnp.where(lo_mask, pltpu.roll(x, HEAD_W - 16, axis=1), pltpu.roll(x, 16, axis=1))
    return x * cos + swapped * sin


def _premix_kernel(*refs, rope, cache, layer):
    x_ref, mod_ref, g_ref, w_ref = refs[:4]
    pos = 4
    if rope:
        cos_ref, sin_ref = refs[pos:pos + 2]
        pos += 2
    if cache == "alias":
        pos += 2
    uc_ref, uh_ref, q_ref, k_ref, v_ref = refs[pos:pos + 5]
    pos += 5
    if cache:
        ck_ref, cv_ref = refs[pos:pos + 2]

    x = x_ref[...]
    mod = mod_ref[...]
    h = (_rms(x, g_ref[...]) * (1.0 + mod[1:2]) + mod[0:1]).astype(BF16)
    uc_ref[...] = _dot(h, w_ref[:, 0:COL_HY]).astype(BF16)
    uh_ref[...] = _dot(h, w_ref[:, COL_HY:COL_Q]).astype(BF16)
    uq = _dot(h, w_ref[:, COL_Q:COL_K]) * (DIFF_DK ** -0.5)
    uk = _dot(h, w_ref[:, COL_K:COL_V])
    uv = _dot(h, w_ref[:, COL_V:IN_COLS])
    if rope:
        cos = cos_ref[...]
        sin = sin_ref[...]
        lane = lax.broadcasted_iota(jnp.int32, cos.shape, 1)
        lo_mask = (lane % 32) < 16
        for hh in range(N_HEADS):
            sl = slice(hh * HEAD_W, (hh + 1) * HEAD_W)
            q_ref[:, sl] = _rope_slab(uq[:, sl], cos, sin, lo_mask).astype(BF16)
            k_ref[:, sl] = _rope_slab(uk[:, sl], cos, sin, lo_mask).astype(BF16)
    else:
        q_ref[...] = uq.astype(BF16)
        k_ref[...] = uk.astype(BF16)
    v_ref[...] = uv.astype(BF16)
    if cache == "alias":
        for hh in range(N_HEADS):
            sl = slice(hh * HEAD_W, (hh + 1) * HEAD_W)
            ck_ref[hh] = uk[:, sl]
            cv_ref[hh] = uv[:, sl]
    elif cache == "new":
        for d in range(DEPTH):
            for hh in range(N_HEADS):
                sl = slice(hh * HEAD_W, (hh + 1) * HEAD_W)
                ck_ref[d, hh] = uk[:, sl] if d == layer else jnp.zeros_like(uk[:, sl])
                cv_ref[d, hh] = uv[:, sl] if d == layer else jnp.zeros_like(uv[:, sl])


def _premix(x, mod_l, g, w_in_bf, *, seq_len, cond_row0, per_seq_cond, layer, n_tokens, tok0,
            rope_tables=None, caches=None, want_cache=False):
    T = n_tokens
    tm = TOKEN_BLOCK
    nb = seq_len // tm
    n_seq = T // seq_len
    b0 = tok0 // tm
    if per_seq_cond:
        mod_map = lambda i: (cond_row0 + i // nb, 0, 0)
    else:
        mod_map = lambda i: (cond_row0, 0, 0)
    in_specs = [
        pl.BlockSpec((tm, D_MODEL), lambda i: (i + b0, 0)),
        pl.BlockSpec((None, N_MOD, D_MODEL), mod_map),
        pl.BlockSpec((1, D_MODEL), lambda i: (0, 0)),
        pl.BlockSpec((D_MODEL, IN_COLS), lambda i: (0, 0)),
    ]
    args = [x, mod_l, g.reshape(1, D_MODEL), w_in_bf]
    rope = rope_tables is not None
    if rope:
        in_specs += [pl.BlockSpec((tm, HEAD_W), lambda i: (i % nb, 0))] * 2
        args += list(rope_tables)
    out_shape = [
        jax.ShapeDtypeStruct((T, COL_HY), BF16),
        jax.ShapeDtypeStruct((T, 3 * D_HYENA), BF16),
        jax.ShapeDtypeStruct((T, D_ATTN), BF16),
        jax.ShapeDtypeStruct((T, D_ATTN), BF16),
        jax.ShapeDtypeStruct((T, D_ATTN), BF16),
    ]
    out_specs = [
        pl.BlockSpec((tm, COL_HY), lambda i: (i, 0)),
        pl.BlockSpec((tm, 3 * D_HYENA), lambda i: (i, 0)),
        pl.BlockSpec((tm, D_ATTN), lambda i: (i, 0)),
        pl.BlockSpec((tm, D_ATTN), lambda i: (i, 0)),
        pl.BlockSpec((tm, D_ATTN), lambda i: (i, 0)),
    ]
    cache = None
    aliases = {}
    if want_cache:
        cache = "new"
        cshape = (n_seq, DEPTH, N_HEADS, seq_len, HEAD_W)
        out_shape += [jax.ShapeDtypeStruct(cshape, F32)] * 2
        if caches is None:
            out_specs += [pl.BlockSpec((None, DEPTH, N_HEADS, tm, HEAD_W),
                                       lambda i: (i // nb, 0, 0, i % nb, 0))] * 2
        else:
            cache = "alias"
            out_specs += [pl.BlockSpec((None, None, N_HEADS, tm, HEAD_W),
                                       lambda i: (i // nb, layer, 0, i % nb, 0))] * 2
            aliases = {len(args): 5, len(args) + 1: 6}
            in_specs += [pl.BlockSpec(memory_space=pl.ANY)] * 2
            args += list(caches)
    return pl.pallas_call(
        functools.partial(_premix_kernel, rope=rope, cache=cache, layer=layer),
        out_shape=out_shape,
        grid=(T // tm,),
        in_specs=in_specs,
        out_specs=out_specs,
        input_output_aliases=aliases,
        compiler_params=_cparams("arbitrary"),
        name="premix",
    )(*args)


def _conformer_kernel(u_ref, w_ref, cb_ref, g_ref, b_ref, o_ref, pad_ref, sh_ref, *, seq_len, rows):
    L = seq_len
    u = u_ref[...].astype(F32)
    a = u[:, :D_CONV] * jax.nn.sigmoid(u[:, D_CONV:])
    zeros = jnp.zeros((CONV_PAD, D_CONV), F32)
    pad_ref[0:CONV_PAD, :] = zeros
    pad_ref[CONV_PAD + L:2 * CONV_PAD + L, :] = zeros
    pad_ref[CONV_PAD:CONV_PAD + L, :] = a
    span = L + 2 * CONV_PAD - SUBLANES
    for r in range(SUBLANES):
        sh_ref[r, 0:span, :] = pad_ref[r:r + span, :]
    first = CONV_PAD - (CONV_WIDTH - 1) // 2

    def chunk(c, carry):
        base = pl.multiple_of(c * rows, rows)
        acc = jnp.zeros((rows, D_CONV), F32)
        for t in range(CONV_WIDTH):
            off = first + t
            tap = sh_ref[off % SUBLANES, pl.ds(base + (off // SUBLANES) * SUBLANES, rows), :]
            acc = acc + tap * w_ref[t:t + 1, :]
        acc = acc + cb_ref[...]
        mu = jnp.mean(acc, axis=-1, keepdims=True)
        d = acc - mu
        var = jnp.mean(d * d, axis=-1, keepdims=True)
        y = d * lax.rsqrt(var + EPS) * g_ref[...] + b_ref[...]
        o_ref[pl.ds(base, rows), :] = _silu(y).astype(BF16)
        return carry

    lax.fori_loop(0, L // rows, chunk, 0)


def _conformer(uc, conv_w, conv_b, ln_g, ln_b, *, seq_len):
    T = uc.shape[0]
    n_seq = T // seq_len
    vec = lambda a: a.reshape(1, D_CONV)
    return pl.pallas_call(
        functools.partial(_conformer_kernel, seq_len=seq_len, rows=64),
        out_shape=jax.ShapeDtypeStruct((T, D_CONV), BF16),
        grid=(n_seq,),
        in_specs=[
            pl.BlockSpec((seq_len, 2 * D_CONV), lambda i: (i, 0)),
            pl.BlockSpec((CONV_WIDTH, D_CONV), lambda i: (0, 0)),
            pl.BlockSpec((1, D_CONV), lambda i: (0, 0)),
            pl.BlockSpec((1, D_CONV), lambda i: (0, 0)),
            pl.BlockSpec((1, D_CONV), lambda i: (0, 0)),
        ],
        out_specs=pl.BlockSpec((seq_len, D_CONV), lambda i: (i, 0)),
        scratch_shapes=[pltpu.VMEM((seq_len + 2 * CONV_PAD, D_CONV), F32),
                        pltpu.VMEM((SUBLANES, seq_len + 2 * CONV_PAD, D_CONV), F32)],
        compiler_params=_cparams("arbitrary"),
        name="conformer",
    )(uc, conv_w, vec(conv_b), vec(ln_g), vec(ln_b))


def _dft_matrices(L):
    k = np.arange(L, dtype=np.int64)[:, None]
    m = np.arange(L, dtype=np.int64)[None, :]
    ang = 2.0 * np.pi * (((2 * k + 1) * m) % (4 * L)).astype(np.float64) / (4 * L)
    fwd = np.concatenate([np.cos(ang), np.sin(ang)], axis=0).astype(np.float32)
    return fwd, np.ascontiguousarray(fwd.T)


def _filter_tables(L):
    t = np.linspace(0.0, 1.0, L)
    bands = np.linspace(1e-4, HYENA_BANDS - 1, HYENA_BANDS)
    ang = (2.0 * np.pi / L) * np.arange(L)[:, None] * bands[None, :]
    feats = np.zeros((L, FEAT_PAD), np.float32)
    feats[:, :HYENA_EMB] = np.concatenate([t[:, None], np.cos(ang), -np.sin(ang)], axis=-1)
    deltas = np.abs(np.linspace(HYENA_MIN_DECAY, HYENA_MAX_DECAY, D_HYENA))
    window = np.exp(-t[:, None] * deltas[None, :]).astype(np.float32)
    return feats, window


def _dot3(a, b):
    a_hi, a_lo = _split_bf16(a)
    b_hi, b_lo = _split_bf16(b)
    return _dot(a_hi, b_hi) + (_dot(a_hi, b_lo) + _dot(a_lo, b_hi))


def _spectrum_kernel(feat_ref, win_ref, w1_ref, b1_ref, w2_ref, b2_ref, fr_ref, w3_ref, f_ref,
                     o_ref, *, seq_len):
    L = seq_len
    C = D_HYENA
    fr = fr_ref[...]
    hid = jnp.sin(fr * (_dot3(feat_ref[...], w1_ref[...]) + b1_ref[...]))
    hid = jnp.sin(fr * (_dot3(hid, w2_ref[...]) + b2_ref[...]))
    filt = _dot3(hid, w3_ref[...])
    win = win_ref[...]
    first = lax.broadcasted_iota(jnp.int32, (L, C), 0) == 0
    for o in range(2):
        fwd = filt[:, o * C:(o + 1) * C] * win
        bwd = jnp.where(first, 0.0, filt[:, (2 + o) * C:(3 + o) * C] * win)
        o_ref[o, 0:L, :] = _dot3(f_ref[0:L, :], fwd + bwd) * (1.0 / L)
        o_ref[o, L:2 * L, :] = _dot3(f_ref[L:2 * L, :], bwd - fwd) * (1.0 / L)


def _hyena_spectrum(L, fmat, feats, window, hy_w1, hy_b1, hy_w2, hy_b2, hy_freq, hy_w3):
    w1 = jnp.zeros((FEAT_PAD, hy_w1.shape[1]), F32).at[:HYENA_EMB].set(hy_w1)
    vec = lambda a: a.reshape(1, -1)
    return pl.pallas_call(
        functools.partial(_spectrum_kernel, seq_len=L),
        out_shape=jax.ShapeDtypeStruct((2, 2 * L, D_HYENA), F32),
        compiler_params=pltpu.CompilerParams(vmem_limit_bytes=VMEM_LIMIT),
        name="hyena_spectrum",
    )(feats, window, w1, vec(hy_b1), hy_w2, vec(hy_b2), vec(hy_freq), hy_w3, fmat)


def _hyena_kernel(u_ref, sw_ref, sb_ref, hb_ref, f_ref, ft_ref, ks_ref, o_ref, pad_ref,
                  *, seq_len, n_seq):
    L = seq_len
    C = D_HYENA
    zrow = jnp.zeros((8, 3 * C), F32)
    for b in range(n_seq):
        u = u_ref[b].astype(F32)
        pad_ref[0:8, :] = zrow
        pad_ref[8 + L:16 + L, :] = zrow
        pad_ref[8:8 + L, :] = u
        s = (pad_ref[7:7 + L, :] * sw_ref[0:1, :] + u * sw_ref[1:2, :]
             + pad_ref[9:9 + L, :] * sw_ref[2:3, :] + sb_ref[...])
        z = s[:, 0:C]
        for o in range(2):
            gate = s[:, (o + 1) * C:(o + 2) * C]
            spec = _dot(f_ref[...], z.astype(BF16))
            zr, za = spec[:L], spec[L:]
            kre, kim = ks_ref[o, 0:L, :], ks_ref[o, L:2 * L, :]
            y = jnp.concatenate([zr * kre + za * kim, za * kre - zr * kim], axis=0)
            conv = _dot(ft_ref[...], y.astype(BF16))
            z = gate * (conv + hb_ref[o:o + 1, :] * z)
        o_ref[b] = z.astype(BF16)


def _hyena(uh, short_w, short_b, hy_bias, fmat_bf, fmat_t_bf, kspec, *, seq_len, seqs_per_step):
    T = uh.shape[0]
    n_seq = T // seq_len
    L = seq_len
    nb = seqs_per_step
    u3 = uh.reshape(n_seq, L, 3 * D_HYENA)
    out = pl.pallas_call(
        functools.partial(_hyena_kernel, seq_len=L, n_seq=nb),
        out_shape=jax.ShapeDtypeStruct((n_seq, L, D_HYENA), BF16),
        grid=(n_seq // nb,),
        in_specs=[
            pl.BlockSpec((nb, L, 3 * D_HYENA), lambda i: (i, 0, 0)),
            pl.BlockSpec((3, 3 * D_HYENA), lambda i: (0, 0)),
            pl.BlockSpec((1, 3 * D_HYENA), lambda i: (0, 0)),
            pl.BlockSpec((2, D_HYENA), lambda i: (0, 0)),
            pl.BlockSpec((2 * L, L), lambda i: (0, 0)),
            pl.BlockSpec((L, 2 * L), lambda i: (0, 0)),
            pl.BlockSpec((2, 2 * L, D_HYENA), lambda i: (0, 0, 0)),
        ],
        out_specs=pl.BlockSpec((nb, L, D_HYENA), lambda i: (i, 0, 0)),
        scratch_shapes=[pltpu.VMEM((L + 16, 3 * D_HYENA), F32)],
        compiler_params=_cparams("arbitrary"),
        name="hyena",
    )(u3, short_w, short_b.reshape(1, 3 * D_HYENA), hy_bias, fmat_bf, fmat_t_bf, kspec)
    return out.reshape(T, D_HYENA)


def _attn_kernel(*refs, lam_init, has_ctx):
    q_ref, k_ref, v_ref = refs[:3]
    pos = 3
    if has_ctx:
        ck_ref, cv_ref = refs[pos:pos + 2]
        pos += 2
    lam_ref, g_ref, o_ref = refs[pos:pos + 3]

    lv = lam_ref[...]
    lam = (jnp.exp(jnp.sum(lv[0:1] * lv[1:2], axis=-1, keepdims=True))
           - jnp.exp(jnp.sum(lv[2:3] * lv[3:4], axis=-1, keepdims=True)) + lam_init)

    lane = lax.broadcasted_iota(jnp.int32, (q_ref.shape[0], HEAD_W), 1)
    dn = (((1,), (1,)), ((), ()))
    for h in range(N_HEADS):
        sl = slice(h * HEAD_W, (h + 1) * HEAD_W)
        q = q_ref[:, sl]
        k = k_ref[:, sl]
        v = v_ref[:, sl]
        if has_ctx:
            ck = ck_ref[h].astype(BF16)
            cv = cv_ref[h].astype(BF16)
        zero = jnp.zeros_like(q)
        outs = []
        for c in range(2):
            qc = jnp.where((lane >= c * DIFF_DK) & (lane < (c + 1) * DIFF_DK), q, zero)
            s = lax.dot_general(qc, k, dn, preferred_element_type=F32)
            m = jnp.max(s, axis=-1, keepdims=True)
            if has_ctx:
                sc = lax.dot_general(qc, ck, dn, preferred_element_type=F32)
                m = jnp.maximum(m, jnp.max(sc, axis=-1, keepdims=True))
            p = jnp.exp(s - m)
            den = jnp.sum(p, axis=-1, keepdims=True)
            acc = _dot(p.astype(BF16), v)
            if has_ctx:
                pc = jnp.exp(sc - m)
                den = den + jnp.sum(pc, axis=-1, keepdims=True)
                acc = acc + _dot(pc.astype(BF16), cv)
            outs.append(acc / den)
        o = outs[0] - lam * outs[1]
        o_ref[:, sl] = (_rms(o, g_ref[...]) * (1.0 - lam_init)).astype(BF16)


def _attention(q, k, v, lam_vecs, subln_g, *, seq_len, layer, ctx=None):
    T = q.shape[0]
    L = seq_len
    n_seq = T // L
    tq = TOKEN_BLOCK
    nq = L // tq
    lam_init = 0.8 - 0.6 * math.exp(-0.3 * layer)
    q3, k3, v3 = (a.reshape(n_seq, L, D_ATTN) for a in (q, k, v))
    in_specs = [
        pl.BlockSpec((None, tq, D_ATTN), lambda b, i: (b, i, 0)),
        pl.BlockSpec((None, L, D_ATTN), lambda b, i: (b, 0, 0)),
        pl.BlockSpec((None, L, D_ATTN), lambda b, i: (b, 0, 0)),
    ]
    args = [q3, k3, v3]
    if ctx is not None:
        past = ctx[0].shape[3]
        in_specs += [pl.BlockSpec((None, None, N_HEADS, past, HEAD_W),
                                  lambda b, i: (b, layer, 0, 0, 0))] * 2
        args += list(ctx)
    in_specs += [
        pl.BlockSpec((4, DIFF_DK), lambda b, i: (0, 0)),
        pl.BlockSpec((1, DIFF_DV), lambda b, i: (0, 0)),
    ]
    args += [lam_vecs, subln_g.reshape(1, DIFF_DV)]
    out = pl.pallas_call(
        functools.partial(_attn_kernel, lam_init=lam_init, has_ctx=ctx is not None),
        out_shape=jax.ShapeDtypeStruct((n_seq, L, D_ATTN), BF16),
        grid=(n_seq, nq),
        in_specs=in_specs,
        out_specs=pl.BlockSpec((None, tq, D_ATTN), lambda b, i: (b, i, 0)),
        compiler_params=_cparams("arbitrary", "arbitrary"),
        name="diff_attention",
    )(*args)
    return out.reshape(T, D_ATTN)


def _split_bf16(a):
    hi = a.astype(BF16)
    lo = (a - hi.astype(F32)).astype(BF16)
    return hi, lo


def _pack_pair(a, b):
    ua = lax.bitcast_convert_type(a.astype(BF16).astype(F32), jnp.uint32)
    ub = lax.bitcast_convert_type(b.astype(BF16).astype(F32), jnp.uint32)
    return (ua >> 16) | ub


def _unpack_pair(w):
    a = lax.bitcast_convert_type(w << 16, F32)
    b = lax.bitcast_convert_type(w & jnp.uint32(0xFFFF0000), F32)
    return a, b


def _postmix_kernel(*refs, stream_blocks):
    ns = len(stream_blocks)
    mod_ref, gpost_ref, gpre_ref, w_ref, rw_ref, x1_ref, h2_ref, lg_ref = refs[4 * ns:]
    i = pl.program_id(0)

    def body(ya_ref, yb_ref, o_ref, x_ref):
        y = (_dot(ya_ref[...], w_ref[0:D_CONV, :])
             + _dot(yb_ref[...], w_ref[D_CONV:D_CONV + D_HYENA, :])
             + _dot(o_ref[...], w_ref[D_CONV + D_HYENA:, :]))
        mod = mod_ref[...]
        x1 = x_ref[...] + mod[2:3] * _rms(y, gpost_ref[...])
        x1_ref[...] = x1
        h2 = _rms(x1, gpre_ref[...]) * (1.0 + mod[4:5]) + mod[3:4]
        h2_ref[...] = _pack_pair(h2[:, :HALF], h2[:, HALF:])
        h_hi, h_lo = _split_bf16(h2)
        r_hi, r_lo = _split_bf16(rw_ref[...])
        dn = (((1,), (1,)), ((), ()))
        mm = lambda a, b: lax.dot_general(a, b, dn, preferred_element_type=F32)
        lg_ref[...] = mm(r_hi, h_hi) + (mm(r_lo, h_hi) + mm(r_hi, h_lo))

    lo = 0
    for j, nbk in enumerate(stream_blocks):
        pl.when((i >= lo) & (i < lo + nbk))(functools.partial(body, *refs[4 * j:4 * j + 4]))
        lo += nbk


def _postmix(parts, mod_l, g_post, g_pre, w_out_bf, router_wt, *, mod_map):
    t_all = sum(p[0].shape[0] for p in parts)
    tm = TOKEN_BLOCK
    full = lambda a: pl.BlockSpec(a.shape, lambda i: (0,) * a.ndim)
    gp, ge = g_post.reshape(1, D_MODEL), g_pre.reshape(1, D_MODEL)
    stream_blocks = tuple(p[0].shape[0] // tm for p in parts)
    in_specs, args = [], []
    lo = 0
    for part, nbk in zip(parts, stream_blocks):
        local = lambda i, lo=lo, nbk=nbk: (jnp.clip(i - lo, 0, nbk - 1), 0)
        in_specs += [pl.BlockSpec((tm, D_CONV), local), pl.BlockSpec((tm, D_HYENA), local),
                     pl.BlockSpec((tm, D_ATTN), local), pl.BlockSpec((tm, D_MODEL), local)]
        args += list(part)
        lo += nbk
    in_specs += [pl.BlockSpec((None, N_MOD, D_MODEL), mod_map),
                 full(gp), full(ge), full(w_out_bf), full(router_wt)]
    args += [mod_l, gp, ge, w_out_bf, router_wt]
    return pl.pallas_call(
        functools.partial(_postmix_kernel, stream_blocks=stream_blocks),
        out_shape=[
            jax.ShapeDtypeStruct((t_all, D_MODEL), F32),
            jax.ShapeDtypeStruct((t_all, HALF), jnp.uint32),
            jax.ShapeDtypeStruct((N_EXPERTS, t_all), F32),
        ],
        grid=(t_all // tm,),
        in_specs=in_specs,
        out_specs=[pl.BlockSpec((tm, D_MODEL), lambda i: (i, 0)),
                   pl.BlockSpec((tm, HALF), lambda i: (i, 0)),
                   pl.BlockSpec((N_EXPERTS, tm), lambda i: (0, i))],
        compiler_params=_cparams("arbitrary"),
        name="postmix",
    )(*args)


def _first_max(vals, idx, sentinel):
    m = jnp.max(vals, axis=0, keepdims=True)
    i = jnp.min(jnp.where(vals == m, idx, sentinel), axis=0, keepdims=True)
    return m, i


def _route_topk_kernel(lg_ref, rb_ref, eidx_ref, w_ref, cnt_ref):
    step = pl.program_id(0)
    per = N_EXPERTS // N_GROUPS
    n = lg_ref.shape[1]
    s = jax.nn.sigmoid(lg_ref[...])
    sel = s + rb_ref[...]
    sub = lax.broadcasted_iota(jnp.int32, (per, n), 0)
    neg = jnp.float32(-jnp.inf)

    gs = []
    for g in range(N_GROUPS):
        xg = sel[g * per:(g + 1) * per]
        total = jnp.zeros((1, n), F32)
        for _ in range(GROUP_SCORE_TOP):
            m, i = _first_max(xg, sub, per)
            total = total + m
            xg = jnp.where(sub == i, neg, xg)
        gs.append(total)
    gscore = jnp.concatenate(gs, axis=0)
    gkeep = jnp.zeros((N_GROUPS, n), F32)
    for _ in range(TOPK_GROUPS):
        _, gi = _first_max(gscore, sub, N_GROUPS)
        hit = sub == gi
        gkeep = jnp.where(hit, 1.0, gkeep)
        gscore = jnp.where(hit, neg, gscore)

    eio = [sub + g * per for g in range(N_GROUPS)]
    sg = [s[g * per:(g + 1) * per] for g in range(N_GROUPS)]
    mk = [jnp.where(gkeep[g:g + 1] > 0.5, sel[g * per:(g + 1) * per], neg) for g in range(N_GROUPS)]
    oh = [jnp.zeros((per, n), F32) for _ in range(N_GROUPS)]
    idx_rows, s_rows = [], []
    for _ in range(TOP_K):
        best = functools.reduce(jnp.maximum, mk)
        m = jnp.max(best, axis=0, keepdims=True)
        cand = functools.reduce(jnp.minimum,
                                [jnp.where(mk[g] == m, eio[g], N_EXPERTS) for g in range(N_GROUPS)])
        idx = jnp.min(cand, axis=0, keepdims=True)
        acc = jnp.zeros((per, n), F32)
        for g in range(N_GROUPS):
            hit = eio[g] == idx
            acc = acc + jnp.where(hit, sg[g], 0.0)
            mk[g] = jnp.where(hit, neg, mk[g])
            oh[g] = jnp.where(hit, 1.0, oh[g])
        idx_rows.append(idx)
        s_rows.append(jnp.sum(acc, axis=0, keepdims=True))
    eidx_ref[...] = jnp.concatenate(idx_rows, axis=0)
    sk = jnp.concatenate(s_rows, axis=0)
    w_ref[...] = sk / jnp.sum(sk, axis=0, keepdims=True) * ROUTED_SCALE

    @pl.when(step == 0)
    def _():
        cnt_ref[...] = jnp.zeros_like(cnt_ref)

    tot = jnp.concatenate([jnp.sum(o, axis=1, keepdims=True) for o in oh], axis=0)
    tot8 = _ceil_tile(tot)
    cnt_ref[...] = cnt_ref[...] + jnp.broadcast_to(tot8, cnt_ref.shape).astype(jnp.int32)


def _ceil_tile(count):
    return jnp.floor((count + (SUBLANES - 1.0)) * (1.0 / SUBLANES)) * SUBLANES


def _route_topk(logits_t, router_b):
    t_all = logits_t.shape[1]
    rb = TOKEN_BLOCK
    return pl.pallas_call(
        _route_topk_kernel,
        out_shape=[
            jax.ShapeDtypeStruct((TOP_K, t_all), jnp.int32),
            jax.ShapeDtypeStruct((TOP_K, t_all), F32),
            jax.ShapeDtypeStruct((N_EXPERTS, LANES), jnp.int32),
        ],
        grid=(t_all // rb,),
        in_specs=[pl.BlockSpec((N_EXPERTS, rb), lambda i: (0, i)),
                  pl.BlockSpec((N_EXPERTS, 1), lambda i: (0, 0))],
        out_specs=[pl.BlockSpec((TOP_K, rb), lambda i: (0, i)),
                   pl.BlockSpec((TOP_K, rb), lambda i: (0, i)),
                   pl.BlockSpec((N_EXPERTS, LANES), lambda i: (0, 0))],
        compiler_params=_cparams("arbitrary"),
        name="route_topk",
    )(logits_t, router_b.reshape(N_EXPERTS, 1))


def _route_dest_kernel(eidx_ref, ps_ref, lpos_ref, cs_ref, cn_ref, cd_ref, carry_ref):
    step = pl.program_id(0)
    per = N_EXPERTS // N_GROUPS
    n = eidx_ref.shape[1]

    @pl.when(step == 0)
    def _():
        carry_ref[...] = jnp.zeros_like(carry_ref)
        cs_ref[...] = jnp.zeros_like(cs_ref)
        cn_ref[...] = jnp.zeros_like(cn_ref)
        cd_ref[...] = jnp.zeros_like(cd_ref)

    eidx = eidx_ref[...]
    sub = lax.broadcasted_iota(jnp.int32, (per, n), 0)
    rows = []
    for g in range(N_GROUPS):
        eio = sub + g * per
        o = jnp.zeros((per, n), F32)
        for k in range(TOP_K):
            o = jnp.where(eio == eidx[k:k + 1], 1.0, o)
        rows.append(o)
    onehot = jnp.concatenate(rows, axis=0)
    r = lax.broadcasted_iota(jnp.int32, (n, n), 0)
    c = lax.broadcasted_iota(jnp.int32, (n, n), 1)
    before = jnp.where(r < c, 1.0, 0.0).astype(BF16)
    rank = _dot(onehot.astype(BF16), before)
    size = _ceil_tile(jnp.sum(onehot, axis=1, keepdims=True))
    er = lax.broadcasted_iota(jnp.int32, (N_EXPERTS, N_EXPERTS), 0)
    ec = lax.broadcasted_iota(jnp.int32, (N_EXPERTS, N_EXPERTS), 1)
    lower = jnp.where(ec < er, 1.0, 0.0).astype(BF16)
    tiles = jnp.broadcast_to(size * (1.0 / SUBLANES), (N_EXPERTS, LANES)).astype(BF16)
    start = _dot(lower, tiles)[:, 0:1] * SUBLANES
    base = (rank + start).astype(jnp.int32)
    out = []
    for k in range(TOP_K):
        acc = jnp.zeros((per, n), jnp.int32)
        for g in range(N_GROUPS):
            acc = acc + jnp.where(sub + g * per == eidx[k:k + 1], base[g * per:(g + 1) * per], 0)
        out.append(jnp.sum(acc, axis=0, keepdims=True))
    lpos_ref[...] = jnp.concatenate(out, axis=0)

    col = lax.broadcasted_iota(jnp.int32, cs_ref.shape, 1) == step
    carry = carry_ref[:, 0:1]
    cs_ref[...] = jnp.where(col, start.astype(jnp.int32), cs_ref[...])
    cn_ref[...] = jnp.where(col, size.astype(jnp.int32), cn_ref[...])
    cd_ref[...] = jnp.where(col, carry.astype(jnp.int32) + ps_ref[...], cd_ref[...])
    carry_ref[...] = carry_ref[...] + jnp.broadcast_to(size, carry_ref.shape)


def _route_dest(eidx_t, pstart):
    t_all = eidx_t.shape[1]
    rb = TOKEN_BLOCK
    n_win = t_all // rb
    cols = -(-n_win // LANES) * LANES
    table = jax.ShapeDtypeStruct((N_EXPERTS, cols), jnp.int32)
    table_spec = pl.BlockSpec((N_EXPERTS, cols), lambda i: (0, 0))
    lpos, cs, cn, cd = pl.pallas_call(
        _route_dest_kernel,
        out_shape=[jax.ShapeDtypeStruct((TOP_K, t_all), jnp.int32), table, table, table],
        grid=(n_win,),
        in_specs=[pl.BlockSpec((TOP_K, rb), lambda i: (0, i)),
                  pl.BlockSpec((N_EXPERTS, 1), lambda i: (0, 0))],
        out_specs=[pl.BlockSpec((TOP_K, rb), lambda i: (0, i)), table_spec, table_spec, table_spec],
        scratch_shapes=[pltpu.VMEM((N_EXPERTS, LANES), F32)],
        compiler_params=_cparams("arbitrary"),
        name="route_dest",
    )(eidx_t, pstart.reshape(N_EXPERTS, 1))
    flat = lambda a: a[:, :n_win].T.reshape(-1)
    return lpos, (flat(cs), flat(cn), flat(cd))


def _wait_rows(hbm, sem, n_rows):
    pltpu.make_async_copy(hbm.at[pl.ds(0, n_rows)], hbm.at[pl.ds(0, n_rows)], sem).wait()


def _window_chunks(tables, w, local_ref, hbm, sem, *, to_hbm, spare_row):
    cs_ref, cn_ref, cd_ref = tables

    def copy(local_row, hbm_row, size):
        lo = local_ref.at[pl.ds(pl.multiple_of(local_row, SUBLANES), size)]
        hi = hbm.at[pl.ds(pl.multiple_of(hbm_row, SUBLANES), size)]
        src, dst = (lo, hi) if to_hbm else (hi, lo)
        pltpu.make_async_copy(src, dst, sem).start()

    def pieces(n, local_row, hbm_row, largest):
        size = largest
        while size >= SUBLANES:
            take = (n & size) != 0
            pl.when(take)(functools.partial(copy, local_row, hbm_row, size))
            step = jnp.where(take, size, 0)
            local_row, hbm_row = local_row + step, hbm_row + step
            size //= 2

    def per_expert(e, carry):
        i = w * N_EXPERTS + e
        pieces(cn_ref[i], cs_ref[i], cd_ref[i], TOKEN_BLOCK)
        return carry

    lax.fori_loop(0, N_EXPERTS, per_expert, 0)
    last = w * N_EXPERTS + (N_EXPERTS - 1)
    used = cs_ref[last] + cn_ref[last]
    pieces(LOCAL_ROWS - used, used, spare_row, SPARE_ROWS)


def _zero_padding(cnt_ref, ps_ref, nu_ref, zero_ref, x_hbm, zsem, n_blocks, *, wait):
    def copy(row, size):
        cp = pltpu.make_async_copy(zero_ref.at[pl.ds(0, size)], x_hbm.at[pl.ds(row, size)], zsem)
        if wait:
            cp.wait()
        else:
            cp.start()

    def per_expert(e, carry):
        cnt = cnt_ref[e]
        pad = (EXPERT_BLOCK - cnt % EXPERT_BLOCK) % EXPERT_BLOCK
        row = ps_ref[e] + cnt
        head = pad % SUBLANES
        for j in range(SUBLANES - 1):
            pl.when(j < head)(functools.partial(copy, row + j, 1))
        row = row + head
        size = EXPERT_BLOCK // 2
        while size >= SUBLANES:
            take = (pad & size) != 0
            pl.when(take)(functools.partial(copy, pl.multiple_of(row, SUBLANES), size))
            row = row + jnp.where(take, size, 0)
            size //= 2
        return carry

    lax.fori_loop(0, N_EXPERTS, per_expert, 0)

    def per_block(j, carry):
        copy(pl.multiple_of(j * EXPERT_BLOCK, EXPERT_BLOCK), EXPERT_BLOCK)
        return carry

    lax.fori_loop(nu_ref[0], n_blocks, per_block, 0)


def _dispatch_kernel(cs_ref, cn_ref, cd_ref, cnt_ref, ps_ref, nu_ref, lpos_ref, h_ref, x_hbm,
                     gbuf, zero_ref, sem, zsem, ssem):
    w = pl.program_id(0)
    nsteps = pl.num_programs(0)
    slot = w % 2
    n_blocks = (x_hbm.shape[0] - 2 * SPARE_ROWS) // EXPERT_BLOCK
    spare0 = n_blocks * EXPERT_BLOCK
    zero_args = (cnt_ref, ps_ref, nu_ref, zero_ref, x_hbm, zsem, n_blocks)

    def spare_zero(j):
        return pltpu.make_async_copy(
            zero_ref, x_hbm.at[pl.ds(spare0 + j * EXPERT_BLOCK, EXPERT_BLOCK)], ssem)

    @pl.when(w == 0)
    def _():
        zero_ref[...] = jnp.zeros_like(zero_ref)
        _zero_padding(*zero_args, wait=False)
        for j in range(2 * SPARE_ROWS // EXPERT_BLOCK):
            spare_zero(j).start()

    ha, hb = _unpack_pair(h_ref[...])
    ha, hb = ha.astype(BF16), hb.astype(BF16)
    lp = lpos_ref[...]
    n = h_ref.shape[0]
    for c in range(LOCAL_ROWS // n):
        row = lax.broadcasted_iota(jnp.int32, (n, n), 0) + c * n
        p = jnp.zeros((n, n), F32)
        for k in range(TOP_K):
            p = jnp.where(row == lp[k:k + 1, :], 1.0, p)
        pb = p.astype(BF16)
        gbuf[slot, c * n:(c + 1) * n, :] = _pack_pair(_dot(pb, ha), _dot(pb, hb))

    @pl.when(w == 0)
    def _():
        for j in range(2 * SPARE_ROWS // EXPERT_BLOCK):
            spare_zero(j).wait()

    _window_chunks((cs_ref, cn_ref, cd_ref), w, gbuf.at[slot], x_hbm, sem.at[slot],
                   to_hbm=True, spare_row=spare0 + slot * SPARE_ROWS)

    @pl.when(w > 0)
    def _():
        _wait_rows(x_hbm, sem.at[1 - slot], LOCAL_ROWS)

    @pl.when(w == nsteps - 1)
    def _():
        _wait_rows(x_hbm, sem.at[slot], LOCAL_ROWS)
        _zero_padding(*zero_args, wait=True)


def _dispatch(tables, lpos, counts, pstart, n_used, h2p, n_blocks):
    t_all = h2p.shape[0]
    sb = TOKEN_BLOCK
    smem = lambda: pl.BlockSpec(memory_space=pltpu.SMEM)
    return pl.pallas_call(
        _dispatch_kernel,
        out_shape=jax.ShapeDtypeStruct((n_blocks * EXPERT_BLOCK + 2 * SPARE_ROWS, HALF), jnp.uint32),
        grid=(t_all // sb,),
        in_specs=[smem(), smem(), smem(), smem(), smem(), smem(),
                  pl.BlockSpec((TOP_K, sb), lambda i: (0, i)),
                  pl.BlockSpec((sb, HALF), lambda i: (i, 0))],
        out_specs=pl.BlockSpec(memory_space=pl.ANY),
        scratch_shapes=[pltpu.VMEM((2, LOCAL_ROWS, HALF), jnp.uint32),
                        pltpu.VMEM((EXPERT_BLOCK, HALF), jnp.uint32),
                        pltpu.SemaphoreType.DMA((2,)), pltpu.SemaphoreType.DMA,
                        pltpu.SemaphoreType.DMA],
        compiler_params=_cparams("arbitrary"),
        name="dispatch",
    )(*tables, counts, pstart, n_used, lpos, h2p)


def _expert_kernel(nb_ref, ps_ref, nu_ref, wg_ref, wu_ref, wd_ref, x_hbm, y_hbm,
                   wg_bf, wu_bf, wd_bf, xbuf, ybuf, xsem, ysem):
    e = pl.program_id(0)
    nblk = nb_ref[e]
    first = ps_ref[e] // EXPERT_BLOCK
    n_used = nu_ref[0]
    xs, ys = xbuf.shape[0], ybuf.shape[0]
    wg_bf[...] = wg_ref[...].astype(BF16)
    wu_bf[...] = wu_ref[...].astype(BF16)
    wd_bf[...] = wd_ref[...].astype(BF16)

    def rows(b):
        return pl.ds(pl.multiple_of(b * EXPERT_BLOCK, EXPERT_BLOCK), EXPERT_BLOCK)

    def x_copy(b):
        return pltpu.make_async_copy(x_hbm.at[rows(b)], xbuf.at[b % xs], xsem.at[b % xs])

    def y_copy(b):
        return pltpu.make_async_copy(ybuf.at[b % ys], y_hbm.at[rows(b)], ysem.at[b % ys])

    @pl.when(e == 0)
    def _():
        for b in range(xs - 1):
            pl.when(b < n_used)(lambda b=b: x_copy(b).start())

    def block(j, carry):
        b = first + j
        x_copy(b).wait()
        pl.when(b + xs - 1 < n_used)(lambda: x_copy(b + xs - 1).start())
        pl.when(b >= ys)(lambda: y_copy(b - ys).wait())
        y = _swiglu_packed(xbuf[b % xs], wg_bf, wu_bf, wd_bf)
        ybuf[b % ys] = _pack_pair(y[:, :HALF], y[:, HALF:])
        y_copy(b).start()
        return carry

    lax.fori_loop(0, nblk, block, 0)

    @pl.when(e == pl.num_programs(0) - 1)
    def _():
        for d in range(ys):
            b = n_used - ys + d
            pl.when(b >= 0)(lambda b=b: y_copy(b).wait())
        n_blocks = y_hbm.shape[0] // EXPERT_BLOCK
        ybuf[0] = jnp.zeros_like(ybuf[0])

        def tail(j, carry, wait):
            cp = pltpu.make_async_copy(
                ybuf.at[0], y_hbm.at[pl.ds(pl.multiple_of(j * EXPERT_BLOCK, EXPERT_BLOCK),
                                           EXPERT_BLOCK)], ysem.at[0])
            cp.wait() if wait else cp.start()
            return carry

        lax.fori_loop(nu_ref[0], n_blocks, functools.partial(tail, wait=False), 0)
        lax.fori_loop(nu_ref[0], n_blocks, functools.partial(tail, wait=True), 0)


def _swiglu_packed(xp, wg_ref, wu_ref, wd_ref):
    xa, xb = _unpack_pair(xp)
    xa, xb = xa.astype(BF16), xb.astype(BF16)
    gate = _dot(xa, wg_ref[0:HALF, :]) + _dot(xb, wg_ref[HALF:, :])
    up = _dot(xa, wu_ref[0:HALF, :]) + _dot(xb, wu_ref[HALF:, :])
    return _dot((_silu(gate) * up).astype(BF16), wd_ref[...])


def _experts(x_buf, nblk, pstart, n_used, wg, wu, wd, *, layer):
    grid_spec = pltpu.PrefetchScalarGridSpec(
        num_scalar_prefetch=3,
        grid=(N_EXPERTS,),
        in_specs=[
            pl.BlockSpec((None, None, D_MODEL, D_EXPERT), lambda e, *_: (layer, e, 0, 0)),
            pl.BlockSpec((None, None, D_MODEL, D_EXPERT), lambda e, *_: (layer, e, 0, 0)),
            pl.BlockSpec((None, None, D_EXPERT, D_MODEL), lambda e, *_: (layer, e, 0, 0)),
            pl.BlockSpec(memory_space=pl.ANY),
        ],
        out_specs=pl.BlockSpec(memory_space=pl.ANY),
        scratch_shapes=[
            pltpu.VMEM((D_MODEL, D_EXPERT), BF16),
            pltpu.VMEM((D_MODEL, D_EXPERT), BF16),
            pltpu.VMEM((D_EXPERT, D_MODEL), BF16),
            pltpu.VMEM((EXPERT_IN_SLOTS, EXPERT_BLOCK, HALF), jnp.uint32),
            pltpu.VMEM((EXPERT_OUT_SLOTS, EXPERT_BLOCK, HALF), jnp.uint32),
            pltpu.SemaphoreType.DMA((EXPERT_IN_SLOTS,)),
            pltpu.SemaphoreType.DMA((EXPERT_OUT_SLOTS,)),
        ],
    )
    return pl.pallas_call(
        _expert_kernel,
        out_shape=jax.ShapeDtypeStruct(x_buf.shape, jnp.uint32),
        grid_spec=grid_spec,
        compiler_params=_cparams("arbitrary"),
        name="experts",
    )(nblk, pstart, n_used, wg, wu, wd, x_buf)


def _block_plan(counts, n_blocks):
    nblk = (counts + EXPERT_BLOCK - 1) // EXPERT_BLOCK
    pend = jnp.cumsum(nblk)
    pstart = ((pend - nblk) * EXPERT_BLOCK).astype(jnp.int32)
    return pstart, nblk.astype(jnp.int32), pend[-1:].astype(jnp.int32)


def _ffn_out_kernel(*refs, stream_blocks):
    (cs_ref, cn_ref, cd_ref, lpos_ref, h_ref, w_ref, x_ref, mod_ref, g_ref, wg_ref, wu_ref, wd_ref,
     y_hbm) = refs[:13]
    ns = len(stream_blocks)
    out_refs = refs[13:13 + ns]
    rows_ref, sem = refs[13 + ns:]
    step = pl.program_id(0)
    nsteps = pl.num_programs(0)
    n = h_ref.shape[0]
    slot = step % 2

    def gather(w, to_slot):
        _window_chunks((cs_ref, cn_ref, cd_ref), w, rows_ref.at[to_slot], y_hbm, sem.at[to_slot],
                       to_hbm=False, spare_row=0)

    pl.when(step == 0)(functools.partial(gather, 0, 0))
    for parity in range(2):
        pl.when((step + 1 < nsteps) & (slot == parity))(
            functools.partial(gather, step + 1, 1 - parity))

    ys = _swiglu_packed(h_ref[...], wg_ref, wu_ref, wd_ref)
    _wait_rows(y_hbm, sem.at[slot], LOCAL_ROWS)
    w = w_ref[...]
    lp = lpos_ref[...]
    ya, yb = ys[:, :HALF], ys[:, HALF:]
    for c in range(LOCAL_ROWS // n):
        col = lax.broadcasted_iota(jnp.int32, (n, n), 1) + c * n
        pw = jnp.zeros((n, n), F32)
        for k in range(TOP_K):
            pw = jnp.where(col == lp[:, k:k + 1], w[:, k:k + 1], pw)
        p_hi, p_lo = _split_bf16(pw)
        ra, rb = _unpack_pair(rows_ref[slot, c * n:(c + 1) * n, :])
        ra, rb = ra.astype(BF16), rb.astype(BF16)
        ya = ya + (_dot(p_hi, ra) + _dot(p_lo, ra))
        yb = yb + (_dot(p_hi, rb) + _dot(p_lo, rb))
    ms = (jnp.sum(ya * ya, axis=-1, keepdims=True)
          + jnp.sum(yb * yb, axis=-1, keepdims=True)) * (1.0 / D_MODEL)
    inv = lax.rsqrt(ms + EPS)
    mod = mod_ref[...]
    g = g_ref[...]
    out_a = x_ref[:, 0:HALF] + mod[5:6, 0:HALF] * (ya * inv * g[:, 0:HALF])
    out_b = x_ref[:, HALF:] + mod[5:6, HALF:] * (yb * inv * g[:, HALF:])

    def store(o_ref):
        o_ref[:, 0:HALF] = out_a
        o_ref[:, HALF:] = out_b

    lo = 0
    for o_ref, nbk in zip(out_refs, stream_blocks):
        pl.when((step >= lo) & (step < lo + nbk))(functools.partial(store, o_ref))
        lo += nbk


def _ffn_out(tables, lpos_rows, h2p, w_rows, x1, mod_l, g_post, swg, swu, swd, y_buf, *, mod_map,
             stream_tokens):
    t_all = x1.shape[0]
    tm = TOKEN_BLOCK
    nsteps = t_all // tm
    row = lambda w: pl.BlockSpec((tm, w), lambda i: (i, 0))
    full = lambda a: pl.BlockSpec(a.shape, lambda i: (0,) * a.ndim)
    smem = lambda: pl.BlockSpec(memory_space=pltpu.SMEM)
    g = g_post.reshape(1, D_MODEL)
    stream_blocks = tuple(t // tm for t in stream_tokens)
    out_shape, out_specs = [], []
    lo = 0
    for t, nbk in zip(stream_tokens, stream_blocks):
        out_shape.append(jax.ShapeDtypeStruct((t, D_MODEL), F32))
        out_specs.append(pl.BlockSpec((tm, D_MODEL),
                                      lambda i, lo=lo, nbk=nbk: (jnp.clip(i - lo, 0, nbk - 1), 0)))
        lo += nbk
    return pl.pallas_call(
        functools.partial(_ffn_out_kernel, stream_blocks=stream_blocks),
        out_shape=out_shape,
        grid=(nsteps,),
        in_specs=[smem(), smem(), smem(),
                  row(TOP_K), row(HALF), row(TOP_K), row(D_MODEL),
                  pl.BlockSpec((None, N_MOD, D_MODEL), mod_map),
                  full(g), full(swg), full(swu), full(swd),
                  pl.BlockSpec(memory_space=pl.ANY)],
        out_specs=out_specs,
        scratch_shapes=[pltpu.VMEM((2, LOCAL_ROWS, HALF), jnp.uint32),
                        pltpu.SemaphoreType.DMA((2,))],
        compiler_params=_cparams("arbitrary"),
        name="ffn_out",
    )(*tables, lpos_rows, h2p, w_rows, x1, mod_l, g, swg, swu, swd, y_buf)


def _rope_tables(L):
    rows = L // GRID_W
    row = jnp.repeat(jnp.arange(rows), GRID_W).astype(F32)
    col = jnp.tile(jnp.arange(GRID_W), rows).astype(F32)
    n = DIFF_DK // 4
    inv = ROPE_BASE ** (-jnp.arange(n, dtype=F32) / n)
    ar, ac = row[:, None] * inv[None, :], col[:, None] * inv[None, :]
    cos64 = jnp.concatenate([jnp.cos(ar), jnp.cos(ar), jnp.cos(ac), jnp.cos(ac)], axis=-1)
    sin64 = jnp.concatenate([-jnp.sin(ar), jnp.sin(ar), -jnp.sin(ac), jnp.sin(ac)], axis=-1)
    return jnp.tile(cos64, (1, 2)), jnp.tile(sin64, (1, 2))


def kernel(x_prompt, x_sample, cache_k, cache_v, c, c_ctx, w_mod, b_mod, g_pre_mix, g_post_mix, g_pre_ffn, g_post_ffn, w_in, w_out, conv_w, conv_b, conv_ln_g, conv_ln_b, hy_short_w, hy_short_b, hy_w1, hy_b1, hy_w2, hy_b2, hy_freq, hy_w3, hy_bias, lam_q1, lam_k1, lam_q2, lam_k2, subln_g, router_w, router_b, exp_w_gate, exp_w_up, exp_w_down, sh_w_gate, sh_w_up, sh_w_down):
    n_p, l_p, _ = x_prompt.shape
    n_s, l_s, _ = x_sample.shape
    t_p, t_s = n_p * l_p, n_s * l_s

    cond = jnp.zeros((COND_ROWS, D_MODEL), F32).at[0].set(c_ctx).at[1:1 + n_s].set(c)
    mod = _modulation(cond, w_mod, b_mod).reshape(DEPTH, COND_ROWS, N_MOD, D_MODEL)
    rope = _rope_tables(l_s)

    t_all = t_p + t_s
    streams = {
        "p": dict(x=x_prompt.reshape(t_p, D_MODEL), T=t_p, L=l_p, row0=0, per_seq=False, hy_nb=8),
        "s": dict(x=x_sample.reshape(t_s, D_MODEL), T=t_s, L=l_s, row0=1, per_seq=True, hy_nb=1),
    }
    for st in streams.values():
        fwd, fwd_t = _dft_matrices(st["L"])
        feats, window = _filter_tables(st["L"])
        st["fmat32"] = jnp.asarray(fwd)
        st["fmat"] = jnp.asarray(fwd).astype(BF16)
        st["fmat_t"] = jnp.asarray(fwd_t).astype(BF16)
        st["filter_tables"] = (jnp.asarray(feats), jnp.asarray(window))
    nb_p, nb_s = t_p // TOKEN_BLOCK, l_s // TOKEN_BLOCK
    merged_mod_map = lambda i: (jnp.where(i < nb_p, 0, 1 + (i - nb_p) // nb_s), 0, 0)
    n_blocks = (t_all // TOKEN_BLOCK) * (LOCAL_ROWS // EXPERT_BLOCK) + N_EXPERTS

    caches = None
    for l in range(DEPTH):
        w_in_bf = w_in[l].astype(BF16)
        w_out_bf = w_out[l].astype(BF16)
        swg, swu, swd = (a[l].astype(BF16) for a in (sh_w_gate, sh_w_up, sh_w_down))
        lam_vecs = jnp.stack([lam_q1[l], lam_k1[l], lam_q2[l], lam_k2[l]], axis=0)
        filt_params = (hy_w1[l], hy_b1[l], hy_w2[l], hy_b2[l], hy_freq[l], hy_w3[l])
        parts = []
        for name, st in streams.items():
            L = st["L"]
            kw = dict(seq_len=L, cond_row0=st["row0"], per_seq_cond=st["per_seq"])
            if name == "p":
                uc, uh, q, k, v, ck, cv = _premix(st["x"], mod[l], g_pre_mix[l], w_in_bf, layer=l,
                                                  n_tokens=st["T"], tok0=0,
                                                  caches=caches, want_cache=True, **kw)
                caches = (ck, cv)
                ctx = None
            else:
                uc, uh, q, k, v = _premix(st["x"], mod[l], g_pre_mix[l], w_in_bf, layer=l,
                                          n_tokens=st["T"], tok0=0, rope_tables=rope, **kw)
                ctx = (cache_k, cache_v)
            ya = _conformer(uc, conv_w[l], conv_b[l], conv_ln_g[l], conv_ln_b[l], seq_len=L)
            kspec = _hyena_spectrum(L, st["fmat32"], *st["filter_tables"], *filt_params)
            yb = _hyena(uh, hy_short_w[l], hy_short_b[l], hy_bias[l], st["fmat"], st["fmat_t"],
                        kspec, seq_len=L, seqs_per_step=st["hy_nb"])
            o = _attention(q, k, v, lam_vecs, subln_g[l], seq_len=L, layer=l, ctx=ctx)
            parts.append((ya, yb, o, st["x"]))
        x1, h2p, logits_t = _postmix(parts, mod[l], g_post_mix[l], g_pre_ffn[l], w_out_bf,
                                     router_w[l].T, mod_map=merged_mod_map)

        eidx_t, w_t, cnt = _route_topk(logits_t, router_b[l])
        counts = cnt[:, 0]
        pstart, nblk, n_used = _block_plan(counts, n_blocks)
        lpos, tables = _route_dest(eidx_t, pstart)
        x_buf = _dispatch(tables, lpos, counts, pstart, n_used, h2p, n_blocks)
        y_buf = _experts(x_buf, nblk, pstart, n_used, exp_w_gate, exp_w_up, exp_w_down, layer=l)
        outs = _ffn_out(tables, lpos.T, h2p, w_t.T, x1, mod[l], g_post_ffn[l], swg, swu, swd,
                        y_buf, mod_map=merged_mod_map, stream_tokens=(t_p, t_s))
        streams["p"]["x"], streams["s"]["x"] = outs

    y_prompt = streams["p"]["x"].reshape(n_p, l_p, D_MODEL)
    y_sample = streams["s"]["x"].reshape(n_s, l_s, D_MODEL)
    return (y_prompt, y_sample, caches[0], caches[1])
```

```python
import functools
import math

import numpy as np
import jax
import jax.numpy as jnp
from jax import lax
from jax.experimental import pallas as pl
from jax.experimental.pallas import tpu as pltpu

F32 = jnp.float32
BF16 = jnp.bfloat16

D_MODEL = 1024
DEPTH = 2
GRID_W = 64
D_CONV = 256
D_HYENA = 256
D_ATTN = 512
DIFF_DK = 64
N_HEADS = 4
DIFF_DV = 128
HEAD_W = 2 * DIFF_DK
IN_COLS = 2 * D_CONV + 3 * D_HYENA + 3 * D_ATTN
COL_HY = 2 * D_CONV
COL_Q = COL_HY + 3 * D_HYENA
COL_K = COL_Q + D_ATTN
COL_V = COL_K + D_ATTN
CONV_WIDTH = 31
CONV_PAD = 16
SUBLANES = 8
HYENA_EMB = 17
FEAT_PAD = 32
HYENA_BANDS = 8
HYENA_MIN_DECAY = math.log(1e-2) / 1.5
HYENA_MAX_DECAY = math.log(1e-2) / 0.3
N_EXPERTS = 64
TOP_K = 8
N_GROUPS = 8
TOPK_GROUPS = 4
GROUP_SCORE_TOP = 2
D_EXPERT = 256
ROUTED_SCALE = 2.5
ROPE_BASE = 10000.0
EPS = 1e-6
N_MOD = 6
COND_ROWS = 16

HALF = D_MODEL // 2
LANES = 128
TOKEN_BLOCK = 256
SPARE_ROWS = N_EXPERTS * SUBLANES
LOCAL_ROWS = TOKEN_BLOCK * TOP_K + SPARE_ROWS
EXPERT_BLOCK = 256
EXPERT_IN_SLOTS = 4
EXPERT_OUT_SLOTS = 3
VMEM_LIMIT = 48 * 1024 * 1024


def _cparams(*sem):
    return pltpu.CompilerParams(dimension_semantics=sem, vmem_limit_bytes=VMEM_LIMIT)


def _rms(x, g):
    return x * lax.rsqrt(jnp.mean(x * x, axis=-1, keepdims=True) + EPS) * g


def _silu(x):
    return x * jax.nn.sigmoid(x)


def _dot(a, b):
    return jnp.dot(a, b, preferred_element_type=F32)


def _mod_kernel(c_ref, w_ref, b_ref, o_ref):
    s = _silu(c_ref[...])
    o_ref[...] = _dot(s.astype(BF16), w_ref[...].astype(BF16)) + b_ref[...]


def _modulation(cond, w_mod, b_mod):
    cols = N_MOD * D_MODEL
    cb = 1536
    return pl.pallas_call(
        _mod_kernel,
        out_shape=jax.ShapeDtypeStruct((DEPTH, COND_ROWS, cols), F32),
        grid=(DEPTH, cols // cb),
        in_specs=[
            pl.BlockSpec((COND_ROWS, D_MODEL), lambda l, j: (0, 0)),
            pl.BlockSpec((None, D_MODEL, cb), lambda l, j: (l, 0, j)),
            pl.BlockSpec((None, 1, cb), lambda l, j: (l, 0, j)),
        ],
        out_specs=pl.BlockSpec((None, COND_ROWS, cb), lambda l, j: (l, 0, j)),
        compiler_params=_cparams("arbitrary", "arbitrary"),
        name="modulation",
    )(cond, w_mod, b_mod.reshape(DEPTH, 1, cols))


def _rope_slab(x, cos, sin, lo_mask):
    swapped = jnp.where(lo_mask, pltpu.roll(x, HEAD_W - 16, axis=1), pltpu.roll(x, 16, axis=1))
    return x * cos + swapped * sin


def _premix_kernel(*refs, rope, cache, layer):
    x_ref, mod_ref, g_ref, w_ref = refs[:4]
    pos = 4
    if rope:
        cos_ref, sin_ref = refs[pos:pos + 2]
        pos += 2
    if cache == "alias":
        pos += 2
    uc_ref, uh_ref, q_ref, k_ref, v_ref = refs[pos:pos + 5]
    pos += 5
    if cache:
        ck_ref, cv_ref = refs[pos:pos + 2]

    x = x_ref[...]
    mod = mod_ref[...]
    h = (_rms(x, g_ref[...]) * (1.0 + mod[1:2]) + mod[0:1]).astype(BF16)
    uc_ref[...] = _dot(h, w_ref[:, 0:COL_HY]).astype(BF16)
    uh_ref[...] = _dot(h, w_ref[:, COL_HY:COL_Q]).astype(BF16)
    uq = _dot(h, w_ref[:, COL_Q:COL_K]) * (DIFF_DK ** -0.5)
    uk = _dot(h, w_ref[:, COL_K:COL_V])
    uv = _dot(h, w_ref[:, COL_V:IN_COLS])
    if rope:
        cos = cos_ref[...]
        sin = sin_ref[...]
        lane = lax.broadcasted_iota(jnp.int32, cos.shape, 1)
        lo_mask = (lane % 32) < 16
        for hh in range(N_HEADS):
            sl = slice(hh * HEAD_W, (hh + 1) * HEAD_W)
            q_ref[:, sl] = _rope_slab(uq[:, sl], cos, sin, lo_mask).astype(BF16)
            k_ref[:, sl] = _rope_slab(uk[:, sl], cos, sin, lo_mask).astype(BF16)
    else:
        q_ref[...] = uq.astype(BF16)
        k_ref[...] = uk.astype(BF16)
    v_ref[...] = uv.astype(BF16)
    if cache == "alias":
        for hh in range(N_HEADS):
            sl = slice(hh * HEAD_W, (hh + 1) * HEAD_W)
            ck_ref[hh] = uk[:, sl]
            cv_ref[hh] = uv[:, sl]
    elif cache == "new":
        for d in range(DEPTH):
            for hh in range(N_HEADS):
                sl = slice(hh * HEAD_W, (hh + 1) * HEAD_W)
                ck_ref[d, hh] = uk[:, sl] if d == layer else jnp.zeros_like(uk[:, sl])
                cv_ref[d, hh] = uv[:, sl] if d == layer else jnp.zeros_like(uv[:, sl])


def _premix(x, mod_l, g, w_in_bf, *, seq_len, cond_row0, per_seq_cond, layer, n_tokens, tok0,
            rope_tables=None, caches=None, want_cache=False):
    T = n_tokens
    tm = TOKEN_BLOCK
    nb = seq_len // tm
    n_seq = T // seq_len
    b0 = tok0 // tm
    if per_seq_cond:
        mod_map = lambda i: (cond_row0 + i // nb, 0, 0)
    else:
        mod_map = lambda i: (cond_row0, 0, 0)
    in_specs = [
        pl.BlockSpec((tm, D_MODEL), lambda i: (i + b0, 0)),
        pl.BlockSpec((None, N_MOD, D_MODEL), mod_map),
        pl.BlockSpec((1, D_MODEL), lambda i: (0, 0)),
        pl.BlockSpec((D_MODEL, IN_COLS), lambda i: (0, 0)),
    ]
    args = [x, mod_l, g.reshape(1, D_MODEL), w_in_bf]
    rope = rope_tables is not None
    if rope:
        in_specs += [pl.BlockSpec((tm, HEAD_W), lambda i: (i % nb, 0))] * 2
        args += list(rope_tables)
    out_shape = [
        jax.ShapeDtypeStruct((T, COL_HY), BF16),
        jax.ShapeDtypeStruct((T, 3 * D_HYENA), BF16),
        jax.ShapeDtypeStruct((T, D_ATTN), BF16),
        jax.ShapeDtypeStruct((T, D_ATTN), BF16),
        jax.ShapeDtypeStruct((T, D_ATTN), BF16),
    ]
    out_specs = [
        pl.BlockSpec((tm, COL_HY), lambda i: (i, 0)),
        pl.BlockSpec((tm, 3 * D_HYENA), lambda i: (i, 0)),
        pl.BlockSpec((tm, D_ATTN), lambda i: (i, 0)),
        pl.BlockSpec((tm, D_ATTN), lambda i: (i, 0)),
        pl.BlockSpec((tm, D_ATTN), lambda i: (i, 0)),
    ]
    cache = None
    aliases = {}
    if want_cache:
        cache = "new"
        cshape = (n_seq, DEPTH, N_HEADS, seq_len, HEAD_W)
        out_shape += [jax.ShapeDtypeStruct(cshape, F32)] * 2
        if caches is None:
            out_specs += [pl.BlockSpec((None, DEPTH, N_HEADS, tm, HEAD_W),
                                       lambda i: (i // nb, 0, 0, i % nb, 0))] * 2
        else:
            cache = "alias"
            out_specs += [pl.BlockSpec((None, None, N_HEADS, tm, HEAD_W),
                                       lambda i: (i // nb, layer, 0, i % nb, 0))] * 2
            aliases = {len(args): 5, len(args) + 1: 6}
            in_specs += [pl.BlockSpec(memory_space=pl.ANY)] * 2
            args += list(caches)
    return pl.pallas_call(
        functools.partial(_premix_kernel, rope=rope, cache=cache, layer=layer),
        out_shape=out_shape,
        grid=(T // tm,),
        in_specs=in_specs,
        out_specs=out_specs,
        input_output_aliases=aliases,
        compiler_params=_cparams("arbitrary"),
        name="premix",
    )(*args)


def _conformer_kernel(u_ref, w_ref, cb_ref, g_ref, b_ref, o_ref, pad_ref, sh_ref, *, seq_len, rows):
    L = seq_len
    u = u_ref[...].astype(F32)
    a = u[:, :D_CONV] * jax.nn.sigmoid(u[:, D_CONV:])
    zeros = jnp.zeros((CONV_PAD, D_CONV), F32)
    pad_ref[0:CONV_PAD, :] = zeros
    pad_ref[CONV_PAD + L:2 * CONV_PAD + L, :] = zeros
    pad_ref[CONV_PAD:CONV_PAD + L, :] = a
    span = L + 2 * CONV_PAD - SUBLANES
    for r in range(SUBLANES):
        sh_ref[r, 0:span, :] = pad_ref[r:r + span, :]
    first = CONV_PAD - (CONV_WIDTH - 1) // 2

    def chunk(c, carry):
        base = pl.multiple_of(c * rows, rows)
        acc = jnp.zeros((rows, D_CONV), F32)
        for t in range(CONV_WIDTH):
            off = first + t
            tap = sh_ref[off % SUBLANES, pl.ds(base + (off // SUBLANES) * SUBLANES, rows), :]
            acc = acc + tap * w_ref[t:t + 1, :]
        acc = acc + cb_ref[...]
        mu = jnp.mean(acc, axis=-1, keepdims=True)
        d = acc - mu
        var = jnp.mean(d * d, axis=-1, keepdims=True)
        y = d * lax.rsqrt(var + EPS) * g_ref[...] + b_ref[...]
        o_ref[pl.ds(base, rows), :] = _silu(y).astype(BF16)
        return carry

    lax.fori_loop(0, L // rows, chunk, 0)


def _conformer(uc, conv_w, conv_b, ln_g, ln_b, *, seq_len):
    T = uc.shape[0]
    n_seq = T // seq_len
    vec = lambda a: a.reshape(1, D_CONV)
    return pl.pallas_call(
        functools.partial(_conformer_kernel, seq_len=seq_len, rows=64),
        out_shape=jax.ShapeDtypeStruct((T, D_CONV), BF16),
        grid=(n_seq,),
        in_specs=[
            pl.BlockSpec((seq_len, 2 * D_CONV), lambda i: (i, 0)),
            pl.BlockSpec((CONV_WIDTH, D_CONV), lambda i: (0, 0)),
            pl.BlockSpec((1, D_CONV), lambda i: (0, 0)),
            pl.BlockSpec((1, D_CONV), lambda i: (0, 0)),
            pl.BlockSpec((1, D_CONV), lambda i: (0, 0)),
        ],
        out_specs=pl.BlockSpec((seq_len, D_CONV), lambda i: (i, 0)),
        scratch_shapes=[pltpu.VMEM((seq_len + 2 * CONV_PAD, D_CONV), F32),
                        pltpu.VMEM((SUBLANES, seq_len + 2 * CONV_PAD, D_CONV), F32)],
        compiler_params=_cparams("arbitrary"),
        name="conformer",
    )(uc, conv_w, vec(conv_b), vec(ln_g), vec(ln_b))


def _dft_matrices(L):
    k = np.arange(L, dtype=np.int64)[:, None]
    m = np.arange(L, dtype=np.int64)[None, :]
    ang = 2.0 * np.pi * (((2 * k + 1) * m) % (4 * L)).astype(np.float64) / (4 * L)
    fwd = np.concatenate([np.cos(ang), np.sin(ang)], axis=0).astype(np.float32)
    return fwd, np.ascontiguousarray(fwd.T)


def _filter_tables(L):
    t = np.linspace(0.0, 1.0, L)
    bands = np.linspace(1e-4, HYENA_BANDS - 1, HYENA_BANDS)
    ang = (2.0 * np.pi / L) * np.arange(L)[:, None] * bands[None, :]
    feats = np.zeros((L, FEAT_PAD), np.float32)
    feats[:, :HYENA_EMB] = np.concatenate([t[:, None], np.cos(ang), -np.sin(ang)], axis=-1)
    deltas = np.abs(np.linspace(HYENA_MIN_DECAY, HYENA_MAX_DECAY, D_HYENA))
    window = np.exp(-t[:, None] * deltas[None, :]).astype(np.float32)
    return feats, window


def _dot3(a, b):
    a_hi, a_lo = _split_bf16(a)
    b_hi, b_lo = _split_bf16(b)
    return _dot(a_hi, b_hi) + (_dot(a_hi, b_lo) + _dot(a_lo, b_hi))


def _spectrum_kernel(feat_ref, win_ref, w1_ref, b1_ref, w2_ref, b2_ref, fr_ref, w3_ref, f_ref,
                     o_ref, *, seq_len):
    L = seq_len
    C = D_HYENA
    fr = fr_ref[...]
    hid = jnp.sin(fr * (_dot3(feat_ref[...], w1_ref[...]) + b1_ref[...]))
    hid = jnp.sin(fr * (_dot3(hid, w2_ref[...]) + b2_ref[...]))
    filt = _dot3(hid, w3_ref[...])
    win = win_ref[...]
    first = lax.broadcasted_iota(jnp.int32, (L, C), 0) == 0
    for o in range(2):
        fwd = filt[:, o * C:(o + 1) * C] * win
        bwd = jnp.where(first, 0.0, filt[:, (2 + o) * C:(3 + o) * C] * win)
        o_ref[o, 0:L, :] = _dot3(f_ref[0:L, :], fwd + bwd) * (1.0 / L)
        o_ref[o, L:2 * L, :] = _dot3(f_ref[L:2 * L, :], bwd - fwd) * (1.0 / L)


def _hyena_spectrum(L, fmat, feats, window, hy_w1, hy_b1, hy_w2, hy_b2, hy_freq, hy_w3):
    w1 = jnp.zeros((FEAT_PAD, hy_w1.shape[1]), F32).at[:HYENA_EMB].set(hy_w1)
    vec = lambda a: a.reshape(1, -1)
    return pl.pallas_call(
        functools.partial(_spectrum_kernel, seq_len=L),
        out_shape=jax.ShapeDtypeStruct((2, 2 * L, D_HYENA), F32),
        compiler_params=pltpu.CompilerParams(vmem_limit_bytes=VMEM_LIMIT),
        name="hyena_spectrum",
    )(feats, window, w1, vec(hy_b1), hy_w2, vec(hy_b2), vec(hy_freq), hy_w3, fmat)


def _hyena_kernel(u_ref, sw_ref, sb_ref, hb_ref, f_ref, ft_ref, ks_ref, o_ref, pad_ref,
                  *, seq_len, n_seq):
    L = seq_len
    C = D_HYENA
    zrow = jnp.zeros((8, 3 * C), F32)
    for b in range(n_seq):
        u = u_ref[b].astype(F32)
        pad_ref[0:8, :] = zrow
        pad_ref[8 + L:16 + L, :] = zrow
        pad_ref[8:8 + L, :] = u
        s = (pad_ref[7:7 + L, :] * sw_ref[0:1, :] + u * sw_ref[1:2, :]
             + pad_ref[9:9 + L, :] * sw_ref[2:3, :] + sb_ref[...])
        z = s[:, 0:C]
        for o in range(2):
            gate = s[:, (o + 1) * C:(o + 2) * C]
            spec = _dot(f_ref[...], z.astype(BF16))
            zr, za = spec[:L], spec[L:]
            kre, kim = ks_ref[o, 0:L, :], ks_ref[o, L:2 * L, :]
            y = jnp.concatenate([zr * kre + za * kim, za * kre - zr * kim], axis=0)
            conv = _dot(ft_ref[...], y.astype(BF16))
            z = gate * (conv + hb_ref[o:o + 1, :] * z)
        o_ref[b] = z.astype(BF16)


def _hyena(uh, short_w, short_b, hy_bias, fmat_bf, fmat_t_bf, kspec, *, seq_len, seqs_per_step):
    T = uh.shape[0]
    n_seq = T // seq_len
    L = seq_len
    nb = seqs_per_step
    u3 = uh.reshape(n_seq, L, 3 * D_HYENA)
    out = pl.pallas_call(
        functools.partial(_hyena_kernel, seq_len=L, n_seq=nb),
        out_shape=jax.ShapeDtypeStruct((n_seq, L, D_HYENA), BF16),
        grid=(n_seq // nb,),
        in_specs=[
            pl.BlockSpec((nb, L, 3 * D_HYENA), lambda i: (i, 0, 0)),
            pl.BlockSpec((3, 3 * D_HYENA), lambda i: (0, 0)),
            pl.BlockSpec((1, 3 * D_HYENA), lambda i: (0, 0)),
            pl.BlockSpec((2, D_HYENA), lambda i: (0, 0)),
            pl.BlockSpec((2 * L, L), lambda i: (0, 0)),
            pl.BlockSpec((L, 2 * L), lambda i: (0, 0)),
            pl.BlockSpec((2, 2 * L, D_HYENA), lambda i: (0, 0, 0)),
        ],
        out_specs=pl.BlockSpec((nb, L, D_HYENA), lambda i: (i, 0, 0)),
        scratch_shapes=[pltpu.VMEM((L + 16, 3 * D_HYENA), F32)],
        compiler_params=_cparams("arbitrary"),
        name="hyena",
    )(u3, short_w, short_b.reshape(1, 3 * D_HYENA), hy_bias, fmat_bf, fmat_t_bf, kspec)
    return out.reshape(T, D_HYENA)


def _attn_kernel(*refs, lam_init, has_ctx):
    q_ref, k_ref, v_ref = refs[:3]
    pos = 3
    if has_ctx:
        ck_ref, cv_ref = refs[pos:pos + 2]
        pos += 2
    lam_ref, g_ref, o_ref = refs[pos:pos + 3]

    lv = lam_ref[...]
    lam = (jnp.exp(jnp.sum(lv[0:1] * lv[1:2], axis=-1, keepdims=True))
           - jnp.exp(jnp.sum(lv[2:3] * lv[3:4], axis=-1, keepdims=True)) + lam_init)

    tq = q_ref.shape[0]
    lane = lax.broadcasted_iota(jnp.int32, (tq, HEAD_W), 1)
    dn = (((1,), (1,)), ((), ()))
    for h in range(N_HEADS):
        sl = slice(h * HEAD_W, (h + 1) * HEAD_W)
        q = q_ref[:, sl]
        k = k_ref[:, sl]
        v = v_ref[:, sl]
        if has_ctx:
            ck = ck_ref[h].astype(BF16)
            cv = cv_ref[h].astype(BF16)
        zero = jnp.zeros_like(q)
        q2 = jnp.concatenate([jnp.where(lane < DIFF_DK, q, zero),
                              jnp.where(lane >= DIFF_DK, q, zero)], axis=0)
        s = lax.dot_general(q2, k, dn, preferred_element_type=F32)
        m = jnp.max(s, axis=-1, keepdims=True)
        if has_ctx:
            sc = lax.dot_general(q2, ck, dn, preferred_element_type=F32)
            m = jnp.maximum(m, jnp.max(sc, axis=-1, keepdims=True))
        p = jnp.exp(s - m)
        den = jnp.sum(p, axis=-1, keepdims=True)
        acc = _dot(p.astype(BF16), v)
        if has_ctx:
            pc = jnp.exp(sc - m)
            den = den + jnp.sum(pc, axis=-1, keepdims=True)
            acc = acc + _dot(pc.astype(BF16), cv)
        both = acc / den
        o = both[:tq] - lam * both[tq:]
        o_ref[:, sl] = (_rms(o, g_ref[...]) * (1.0 - lam_init)).astype(BF16)


def _attention(q, k, v, lam_vecs, subln_g, *, seq_len, layer, ctx=None):
    T = q.shape[0]
    L = seq_len
    n_seq = T // L
    tq = TOKEN_BLOCK
    nq = L // tq
    lam_init = 0.8 - 0.6 * math.exp(-0.3 * layer)
    q3, k3, v3 = (a.reshape(n_seq, L, D_ATTN) for a in (q, k, v))
    in_specs = [
        pl.BlockSpec((None, tq, D_ATTN), lambda b, i: (b, i, 0)),
        pl.BlockSpec((None, L, D_ATTN), lambda b, i: (b, 0, 0)),
        pl.BlockSpec((None, L, D_ATTN), lambda b, i: (b, 0, 0)),
    ]
    args = [q3, k3, v3]
    if ctx is not None:
        past = ctx[0].shape[3]
        in_specs += [pl.BlockSpec((None, None, N_HEADS, past, HEAD_W),
                                  lambda b, i: (b, layer, 0, 0, 0))] * 2
        args += list(ctx)
    in_specs += [
        pl.BlockSpec((4, DIFF_DK), lambda b, i: (0, 0)),
        pl.BlockSpec((1, DIFF_DV), lambda b, i: (0, 0)),
    ]
    args += [lam_vecs, subln_g.reshape(1, DIFF_DV)]
    out = pl.pallas_call(
        functools.partial(_attn_kernel, lam_init=lam_init, has_ctx=ctx is not None),
        out_shape=jax.ShapeDtypeStruct((n_seq, L, D_ATTN), BF16),
        grid=(n_seq, nq),
        in_specs=in_specs,
        out_specs=pl.BlockSpec((None, tq, D_ATTN), lambda b, i: (b, i, 0)),
        compiler_params=_cparams("arbitrary", "arbitrary"),
        name="diff_attention",
    )(*args)
    return out.reshape(T, D_ATTN)


def _split_bf16(a):
    hi = a.astype(BF16)
    lo = (a - hi.astype(F32)).astype(BF16)
    return hi, lo


def _pack_pair(a, b):
    ua = lax.bitcast_convert_type(a.astype(BF16).astype(F32), jnp.uint32)
    ub = lax.bitcast_convert_type(b.astype(BF16).astype(F32), jnp.uint32)
    return (ua >> 16) | ub


def _unpack_pair(w):
    a = lax.bitcast_convert_type(w << 16, F32)
    b = lax.bitcast_convert_type(w & jnp.uint32(0xFFFF0000), F32)
    return a, b


def _postmix_kernel(*refs, stream_blocks):
    ns = len(stream_blocks)
    mod_ref, gpost_ref, gpre_ref, w_ref, rw_ref, x1_ref, h2_ref, lg_ref = refs[4 * ns:]
    i = pl.program_id(0)

    def body(ya_ref, yb_ref, o_ref, x_ref):
        y = (_dot(ya_ref[...], w_ref[0:D_CONV, :])
             + _dot(yb_ref[...], w_ref[D_CONV:D_CONV + D_HYENA, :])
             + _dot(o_ref[...], w_ref[D_CONV + D_HYENA:, :]))
        mod = mod_ref[...]
        x1 = x_ref[...] + mod[2:3] * _rms(y, gpost_ref[...])
        x1_ref[...] = x1
        h2 = _rms(x1, gpre_ref[...]) * (1.0 + mod[4:5]) + mod[3:4]
        h2_ref[...] = _pack_pair(h2[:, :HALF], h2[:, HALF:])
        h_hi, h_lo = _split_bf16(h2)
        r_hi, r_lo = _split_bf16(rw_ref[...])
        dn = (((1,), (1,)), ((), ()))
        mm = lambda a, b: lax.dot_general(a, b, dn, preferred_element_type=F32)
        lg_ref[...] = mm(r_hi, h_hi) + (mm(r_lo, h_hi) + mm(r_hi, h_lo))

    lo = 0
    for j, nbk in enumerate(stream_blocks):
        pl.when((i >= lo) & (i < lo + nbk))(functools.partial(body, *refs[4 * j:4 * j + 4]))
        lo += nbk


def _postmix(parts, mod_l, g_post, g_pre, w_out_bf, router_wt, *, mod_map):
    t_all = sum(p[0].shape[0] for p in parts)
    tm = TOKEN_BLOCK
    full = lambda a: pl.BlockSpec(a.shape, lambda i: (0,) * a.ndim)
    gp, ge = g_post.reshape(1, D_MODEL), g_pre.reshape(1, D_MODEL)
    stream_blocks = tuple(p[0].shape[0] // tm for p in parts)
    in_specs, args = [], []
    lo = 0
    for part, nbk in zip(parts, stream_blocks):
        local = lambda i, lo=lo, nbk=nbk: (jnp.clip(i - lo, 0, nbk - 1), 0)
        in_specs += [pl.BlockSpec((tm, D_CONV), local), pl.BlockSpec((tm, D_HYENA), local),
                     pl.BlockSpec((tm, D_ATTN), local), pl.BlockSpec((tm, D_MODEL), local)]
        args += list(part)
        lo += nbk
    in_specs += [pl.BlockSpec((None, N_MOD, D_MODEL), mod_map),
                 full(gp), full(ge), full(w_out_bf), full(router_wt)]
    args += [mod_l, gp, ge, w_out_bf, router_wt]
    return pl.pallas_call(
        functools.partial(_postmix_kernel, stream_blocks=stream_blocks),
        out_shape=[
            jax.ShapeDtypeStruct((t_all, D_MODEL), F32),
            jax.ShapeDtypeStruct((t_all, HALF), jnp.uint32),
            jax.ShapeDtypeStruct((N_EXPERTS, t_all), F32),
        ],
        grid=(t_all // tm,),
        in_specs=in_specs,
        out_specs=[pl.BlockSpec((tm, D_MODEL), lambda i: (i, 0)),
                   pl.BlockSpec((tm, HALF), lambda i: (i, 0)),
                   pl.BlockSpec((N_EXPERTS, tm), lambda i: (0, i))],
        compiler_params=_cparams("arbitrary"),
        name="postmix",
    )(*args)


def _first_max(vals, idx, sentinel):
    m = jnp.max(vals, axis=0, keepdims=True)
    i = jnp.min(jnp.where(vals == m, idx, sentinel), axis=0, keepdims=True)
    return m, i


def _route_topk_kernel(lg_ref, rb_ref, eidx_ref, w_ref, cnt_ref):
    step = pl.program_id(0)
    per = N_EXPERTS // N_GROUPS
    n = lg_ref.shape[1]
    s = jax.nn.sigmoid(lg_ref[...])
    sel = s + rb_ref[...]
    sub = lax.broadcasted_iota(jnp.int32, (per, n), 0)
    neg = jnp.float32(-jnp.inf)

    gs = []
    for g in range(N_GROUPS):
        xg = sel[g * per:(g + 1) * per]
        total = jnp.zeros((1, n), F32)
        for _ in range(GROUP_SCORE_TOP):
            m, i = _first_max(xg, sub, per)
            total = total + m
            xg = jnp.where(sub == i, neg, xg)
        gs.append(total)
    gscore = jnp.concatenate(gs, axis=0)
    gkeep = jnp.zeros((N_GROUPS, n), F32)
    for _ in range(TOPK_GROUPS):
        _, gi = _first_max(gscore, sub, N_GROUPS)
        hit = sub == gi
        gkeep = jnp.where(hit, 1.0, gkeep)
        gscore = jnp.where(hit, neg, gscore)

    eio = [sub + g * per for g in range(N_GROUPS)]
    sg = [s[g * per:(g + 1) * per] for g in range(N_GROUPS)]
    mk = [jnp.where(gkeep[g:g + 1] > 0.5, sel[g * per:(g + 1) * per], neg) for g in range(N_GROUPS)]
    oh = [jnp.zeros((per, n), F32) for _ in range(N_GROUPS)]
    idx_rows, s_rows = [], []
    for _ in range(TOP_K):
        best = functools.reduce(jnp.maximum, mk)
        m = jnp.max(best, axis=0, keepdims=True)
        cand = functools.reduce(jnp.minimum,
                                [jnp.where(mk[g] == m, eio[g], N_EXPERTS) for g in range(N_GROUPS)])
        idx = jnp.min(cand, axis=0, keepdims=True)
        acc = jnp.zeros((per, n), F32)
        for g in range(N_GROUPS):
            hit = eio[g] == idx
            acc = acc + jnp.where(hit, sg[g], 0.0)
            mk[g] = jnp.where(hit, neg, mk[g])
            oh[g] = jnp.where(hit, 1.0, oh[g])
        idx_rows.append(idx)
        s_rows.append(jnp.sum(acc, axis=0, keepdims=True))
    eidx_ref[...] = jnp.concatenate(idx_rows, axis=0)
    sk = jnp.concatenate(s_rows, axis=0)
    w_ref[...] = sk / jnp.sum(sk, axis=0, keepdims=True) * ROUTED_SCALE

    @pl.when(step == 0)
    def _():
        cnt_ref[...] = jnp.zeros_like(cnt_ref)

    tot = jnp.concatenate([jnp.sum(o, axis=1, keepdims=True) for o in oh], axis=0)
    tot8 = _ceil_tile(tot)
    cnt_ref[...] = cnt_ref[...] + jnp.broadcast_to(tot8, cnt_ref.shape).astype(jnp.int32)


def _ceil_tile(count):
    return jnp.floor((count + (SUBLANES - 1.0)) * (1.0 / SUBLANES)) * SUBLANES


def _route_topk(logits_t, router_b):
    t_all = logits_t.shape[1]
    rb = TOKEN_BLOCK
    return pl.pallas_call(
        _route_topk_kernel,
        out_shape=[
            jax.ShapeDtypeStruct((TOP_K, t_all), jnp.int32),
            jax.ShapeDtypeStruct((TOP_K, t_all), F32),
            jax.ShapeDtypeStruct((N_EXPERTS, LANES), jnp.int32),
        ],
        grid=(t_all // rb,),
        in_specs=[pl.BlockSpec((N_EXPERTS, rb), lambda i: (0, i)),
                  pl.BlockSpec((N_EXPERTS, 1), lambda i: (0, 0))],
        out_specs=[pl.BlockSpec((TOP_K, rb), lambda i: (0, i)),
                   pl.BlockSpec((TOP_K, rb), lambda i: (0, i)),
                   pl.BlockSpec((N_EXPERTS, LANES), lambda i: (0, 0))],
        compiler_params=_cparams("arbitrary"),
        name="route_topk",
    )(logits_t, router_b.reshape(N_EXPERTS, 1))


def _route_dest_kernel(eidx_ref, ps_ref, lpos_ref, cs_ref, cn_ref, cd_ref, carry_ref):
    step = pl.program_id(0)
    per = N_EXPERTS // N_GROUPS
    n = eidx_ref.shape[1]

    @pl.when(step == 0)
    def _():
        carry_ref[...] = jnp.zeros_like(carry_ref)
        cs_ref[...] = jnp.zeros_like(cs_ref)
        cn_ref[...] = jnp.zeros_like(cn_ref)
        cd_ref[...] = jnp.zeros_like(cd_ref)

    eidx = eidx_ref[...]
    sub = lax.broadcasted_iota(jnp.int32, (per, n), 0)
    rows = []
    for g in range(N_GROUPS):
        eio = sub + g * per
        o = jnp.zeros((per, n), F32)
        for k in range(TOP_K):
            o = jnp.where(eio == eidx[k:k + 1], 1.0, o)
        rows.append(o)
    onehot = jnp.concatenate(rows, axis=0)
    r = lax.broadcasted_iota(jnp.int32, (n, n), 0)
    c = lax.broadcasted_iota(jnp.int32, (n, n), 1)
    before = jnp.where(r < c, 1.0, 0.0).astype(BF16)
    rank = _dot(onehot.astype(BF16), before)
    size = _ceil_tile(jnp.sum(onehot, axis=1, keepdims=True))
    er = lax.broadcasted_iota(jnp.int32, (N_EXPERTS, N_EXPERTS), 0)
    ec = lax.broadcasted_iota(jnp.int32, (N_EXPERTS, N_EXPERTS), 1)
    lower = jnp.where(ec < er, 1.0, 0.0).astype(BF16)
    tiles = jnp.broadcast_to(size * (1.0 / SUBLANES), (N_EXPERTS, LANES)).astype(BF16)
    start = _dot(lower, tiles)[:, 0:1] * SUBLANES
    base = (rank + start).astype(jnp.int32)
    out = []
    for k in range(TOP_K):
        acc = jnp.zeros((per, n), jnp.int32)
        for g in range(N_GROUPS):
            acc = acc + jnp.where(sub + g * per == eidx[k:k + 1], base[g * per:(g + 1) * per], 0)
        out.append(jnp.sum(acc, axis=0, keepdims=True))
    lpos_ref[...] = jnp.concatenate(out, axis=0)

    col = lax.broadcasted_iota(jnp.int32, cs_ref.shape, 1) == step
    carry = carry_ref[:, 0:1]
    cs_ref[...] = jnp.where(col, start.astype(jnp.int32), cs_ref[...])
    cn_ref[...] = jnp.where(col, size.astype(jnp.int32), cn_ref[...])
    cd_ref[...] = jnp.where(col, carry.astype(jnp.int32) + ps_ref[...], cd_ref[...])
    carry_ref[...] = carry_ref[...] + jnp.broadcast_to(size, carry_ref.shape)


def _route_dest(eidx_t, pstart):
    t_all = eidx_t.shape[1]
    rb = TOKEN_BLOCK
    n_win = t_all // rb
    cols = -(-n_win // LANES) * LANES
    table = jax.ShapeDtypeStruct((N_EXPERTS, cols), jnp.int32)
    table_spec = pl.BlockSpec((N_EXPERTS, cols), lambda i: (0, 0))
    lpos, cs, cn, cd = pl.pallas_call(
        _route_dest_kernel,
        out_shape=[jax.ShapeDtypeStruct((TOP_K, t_all), jnp.int32), table, table, table],
        grid=(n_win,),
        in_specs=[pl.BlockSpec((TOP_K, rb), lambda i: (0, i)),
                  pl.BlockSpec((N_EXPERTS, 1), lambda i: (0, 0))],
        out_specs=[pl.BlockSpec((TOP_K, rb), lambda i: (0, i)), table_spec, table_spec, table_spec],
        scratch_shapes=[pltpu.VMEM((N_EXPERTS, LANES), F32)],
        compiler_params=_cparams("arbitrary"),
        name="route_dest",
    )(eidx_t, pstart.reshape(N_EXPERTS, 1))
    flat = lambda a: a[:, :n_win].T.reshape(-1)
    return lpos, (flat(cs), flat(cn), flat(cd))


def _wait_rows(hbm, sem, n_rows):
    pltpu.make_async_copy(hbm.at[pl.ds(0, n_rows)], hbm.at[pl.ds(0, n_rows)], sem).wait()


def _window_chunks(tables, w, local_ref, hbm, sem, *, to_hbm, spare_row):
    cs_ref, cn_ref, cd_ref = tables

    def copy(local_row, hbm_row, size):
        lo = local_ref.at[pl.ds(pl.multiple_of(local_row, SUBLANES), size)]
        hi = hbm.at[pl.ds(pl.multiple_of(hbm_row, SUBLANES), size)]
        src, dst = (lo, hi) if to_hbm else (hi, lo)
        pltpu.make_async_copy(src, dst, sem).start()

    def pieces(n, local_row, hbm_row, largest):
        size = largest
        while size >= SUBLANES:
            take = (n & size) != 0
            pl.when(take)(functools.partial(copy, local_row, hbm_row, size))
            step = jnp.where(take, size, 0)
            local_row, hbm_row = local_row + step, hbm_row + step
            size //= 2

    def per_expert(e, carry):
        i = w * N_EXPERTS + e
        pieces(cn_ref[i], cs_ref[i], cd_ref[i], TOKEN_BLOCK)
        return carry

    lax.fori_loop(0, N_EXPERTS, per_expert, 0)
    last = w * N_EXPERTS + (N_EXPERTS - 1)
    used = cs_ref[last] + cn_ref[last]
    pieces(LOCAL_ROWS - used, used, spare_row, SPARE_ROWS)


def _zero_padding(cnt_ref, ps_ref, nu_ref, zero_ref, x_hbm, zsem, n_blocks, *, wait):
    def copy(row, size):
        cp = pltpu.make_async_copy(zero_ref.at[pl.ds(0, size)], x_hbm.at[pl.ds(row, size)], zsem)
        if wait:
            cp.wait()
        else:
            cp.start()

    def per_expert(e, carry):
        cnt = cnt_ref[e]
        pad = (EXPERT_BLOCK - cnt % EXPERT_BLOCK) % EXPERT_BLOCK
        row = ps_ref[e] + cnt
        head = pad % SUBLANES
        for j in range(SUBLANES - 1):
            pl.when(j < head)(functools.partial(copy, row + j, 1))
        row = row + head
        size = EXPERT_BLOCK // 2
        while size >= SUBLANES:
            take = (pad & size) != 0
            pl.when(take)(functools.partial(copy, pl.multiple_of(row, SUBLANES), size))
            row = row + jnp.where(take, size, 0)
            size //= 2
        return carry

    lax.fori_loop(0, N_EXPERTS, per_expert, 0)

    def per_block(j, carry):
        copy(pl.multiple_of(j * EXPERT_BLOCK, EXPERT_BLOCK), EXPERT_BLOCK)
        return carry

    lax.fori_loop(nu_ref[0], n_blocks, per_block, 0)


def _dispatch_kernel(cs_ref, cn_ref, cd_ref, cnt_ref, ps_ref, nu_ref, lpos_ref, h_ref, x_hbm,
                     gbuf, zero_ref, sem, zsem, ssem):
    w = pl.program_id(0)
    nsteps = pl.num_programs(0)
    slot = w % 2
    n_blocks = (x_hbm.shape[0] - 2 * SPARE_ROWS) // EXPERT_BLOCK
    spare0 = n_blocks * EXPERT_BLOCK
    zero_args = (cnt_ref, ps_ref, nu_ref, zero_ref, x_hbm, zsem, n_blocks)

    def spare_zero(j):
        return pltpu.make_async_copy(
            zero_ref, x_hbm.at[pl.ds(spare0 + j * EXPERT_BLOCK, EXPERT_BLOCK)], ssem)

    @pl.when(w == 0)
    def _():
        zero_ref[...] = jnp.zeros_like(zero_ref)
        _zero_padding(*zero_args, wait=False)
        for j in range(2 * SPARE_ROWS // EXPERT_BLOCK):
            spare_zero(j).start()

    ha, hb = _unpack_pair(h_ref[...])
    ha, hb = ha.astype(BF16), hb.astype(BF16)
    lp = lpos_ref[...]
    n = h_ref.shape[0]
    last = w * N_EXPERTS + (N_EXPERTS - 1)
    used = cs_ref[last] + cn_ref[last]

    def permute(c):
        row = lax.broadcasted_iota(jnp.int32, (n, n), 0) + c * n
        p = jnp.zeros((n, n), F32)
        for k in range(TOP_K):
            p = jnp.where(row == lp[k:k + 1, :], 1.0, p)
        pb = p.astype(BF16)
        ga = lax.bitcast_convert_type(_dot(pb, ha), jnp.uint32)
        gb = lax.bitcast_convert_type(_dot(pb, hb), jnp.uint32)
        gbuf[slot, c * n:(c + 1) * n, :] = (ga >> 16) | gb

    def blank(c):
        gbuf[slot, c * n:(c + 1) * n, :] = jnp.zeros((n, HALF), jnp.uint32)

    for c in range(LOCAL_ROWS // n):
        if c < TOP_K:
            permute(c)
        else:
            pl.when(c * n < used)(functools.partial(permute, c))
            pl.when(c * n >= used)(functools.partial(blank, c))

    @pl.when(w == 0)
    def _():
        for j in range(2 * SPARE_ROWS // EXPERT_BLOCK):
            spare_zero(j).wait()

    _window_chunks((cs_ref, cn_ref, cd_ref), w, gbuf.at[slot], x_hbm, sem.at[slot],
                   to_hbm=True, spare_row=spare0 + slot * SPARE_ROWS)

    @pl.when(w > 0)
    def _():
        _wait_rows(x_hbm, sem.at[1 - slot], LOCAL_ROWS)

    @pl.when(w == nsteps - 1)
    def _():
        _wait_rows(x_hbm, sem.at[slot], LOCAL_ROWS)
        _zero_padding(*zero_args, wait=True)


def _dispatch(tables, lpos, counts, pstart, n_used, h2p, n_blocks):
    t_all = h2p.shape[0]
    sb = TOKEN_BLOCK
    smem = lambda: pl.BlockSpec(memory_space=pltpu.SMEM)
    return pl.pallas_call(
        _dispatch_kernel,
        out_shape=jax.ShapeDtypeStruct((n_blocks * EXPERT_BLOCK + 2 * SPARE_ROWS, HALF), jnp.uint32),
        grid=(t_all // sb,),
        in_specs=[smem(), smem(), smem(), smem(), smem(), smem(),
                  pl.BlockSpec((TOP_K, sb), lambda i: (0, i)),
                  pl.BlockSpec((sb, HALF), lambda i: (i, 0))],
        out_specs=pl.BlockSpec(memory_space=pl.ANY),
        scratch_shapes=[pltpu.VMEM((2, LOCAL_ROWS, HALF), jnp.uint32),
                        pltpu.VMEM((EXPERT_BLOCK, HALF), jnp.uint32),
                        pltpu.SemaphoreType.DMA((2,)), pltpu.SemaphoreType.DMA,
                        pltpu.SemaphoreType.DMA],
        compiler_params=_cparams("arbitrary"),
        name="dispatch",
    )(*tables, counts, pstart, n_used, lpos, h2p)


def _expert_kernel(nb_ref, ps_ref, nu_ref, wg_ref, wu_ref, wd_ref, x_hbm, y_hbm,
                   wg_bf, wu_bf, wd_bf, xbuf, ybuf, xsem, ysem):
    e = pl.program_id(0)
    nblk = nb_ref[e]
    first = ps_ref[e] // EXPERT_BLOCK
    n_used = nu_ref[0]
    xs, ys = xbuf.shape[0], ybuf.shape[0]
    wg_bf[...] = wg_ref[...].astype(BF16)
    wu_bf[...] = wu_ref[...].astype(BF16)
    wd_bf[...] = wd_ref[...].astype(BF16)

    def rows(b):
        return pl.ds(pl.multiple_of(b * EXPERT_BLOCK, EXPERT_BLOCK), EXPERT_BLOCK)

    def x_copy(b):
        return pltpu.make_async_copy(x_hbm.at[rows(b)], xbuf.at[b % xs], xsem.at[b % xs])

    def y_copy(b):
        return pltpu.make_async_copy(ybuf.at[b % ys], y_hbm.at[rows(b)], ysem.at[b % ys])

    @pl.when(e == 0)
    def _():
        for b in range(xs - 1):
            pl.when(b < n_used)(lambda b=b: x_copy(b).start())

    def block(j, carry):
        b = first + j
        x_copy(b).wait()
        pl.when(b + xs - 1 < n_used)(lambda: x_copy(b + xs - 1).start())
        pl.when(b >= ys)(lambda: y_copy(b - ys).wait())
        y = _swiglu_packed(xbuf[b % xs], wg_bf, wu_bf, wd_bf)
        ybuf[b % ys] = _pack_pair(y[:, :HALF], y[:, HALF:])
        y_copy(b).start()
        return carry

    lax.fori_loop(0, nblk, block, 0)

    @pl.when(e == pl.num_programs(0) - 1)
    def _():
        for d in range(ys):
            b = n_used - ys + d
            pl.when(b >= 0)(lambda b=b: y_copy(b).wait())
        n_blocks = y_hbm.shape[0] // EXPERT_BLOCK
        ybuf[0] = jnp.zeros_like(ybuf[0])

        def tail(j, carry, wait):
            cp = pltpu.make_async_copy(
                ybuf.at[0], y_hbm.at[pl.ds(pl.multiple_of(j * EXPERT_BLOCK, EXPERT_BLOCK),
                                           EXPERT_BLOCK)], ysem.at[0])
            cp.wait() if wait else cp.start()
            return carry

        lax.fori_loop(nu_ref[0], n_blocks, functools.partial(tail, wait=False), 0)
        lax.fori_loop(nu_ref[0], n_blocks, functools.partial(tail, wait=True), 0)


def _swiglu_packed(xp, wg_ref, wu_ref, wd_ref):
    xa, xb = _unpack_pair(xp)
    xa, xb = xa.astype(BF16), xb.astype(BF16)
    gate = _dot(xa, wg_ref[0:HALF, :]) + _dot(xb, wg_ref[HALF:, :])
    up = _dot(xa, wu_ref[0:HALF, :]) + _dot(xb, wu_ref[HALF:, :])
    return _dot((_silu(gate) * up).astype(BF16), wd_ref[...])


def _experts(x_buf, nblk, pstart, n_used, wg, wu, wd, *, layer):
    grid_spec = pltpu.PrefetchScalarGridSpec(
        num_scalar_prefetch=3,
        grid=(N_EXPERTS,),
        in_specs=[
            pl.BlockSpec((None, None, D_MODEL, D_EXPERT), lambda e, *_: (layer, e, 0, 0)),
            pl.BlockSpec((None, None, D_MODEL, D_EXPERT), lambda e, *_: (layer, e, 0, 0)),
            pl.BlockSpec((None, None, D_EXPERT, D_MODEL), lambda e, *_: (layer, e, 0, 0)),
            pl.BlockSpec(memory_space=pl.ANY),
        ],
        out_specs=pl.BlockSpec(memory_space=pl.ANY),
        scratch_shapes=[
            pltpu.VMEM((D_MODEL, D_EXPERT), BF16),
            pltpu.VMEM((D_MODEL, D_EXPERT), BF16),
            pltpu.VMEM((D_EXPERT, D_MODEL), BF16),
            pltpu.VMEM((EXPERT_IN_SLOTS, EXPERT_BLOCK, HALF), jnp.uint32),
            pltpu.VMEM((EXPERT_OUT_SLOTS, EXPERT_BLOCK, HALF), jnp.uint32),
            pltpu.SemaphoreType.DMA((EXPERT_IN_SLOTS,)),
            pltpu.SemaphoreType.DMA((EXPERT_OUT_SLOTS,)),
        ],
    )
    return pl.pallas_call(
        _expert_kernel,
        out_shape=jax.ShapeDtypeStruct(x_buf.shape, jnp.uint32),
        grid_spec=grid_spec,
        compiler_params=_cparams("arbitrary"),
        name="experts",
    )(nblk, pstart, n_used, wg, wu, wd, x_buf)


def _block_plan(counts, n_blocks):
    nblk = (counts + EXPERT_BLOCK - 1) // EXPERT_BLOCK
    pend = jnp.cumsum(nblk)
    pstart = ((pend - nblk) * EXPERT_BLOCK).astype(jnp.int32)
    return pstart, nblk.astype(jnp.int32), pend[-1:].astype(jnp.int32)


def _ffn_out_kernel(*refs, stream_blocks):
    (cs_ref, cn_ref, cd_ref, lpos_ref, h_ref, w_ref, x_ref, mod_ref, g_ref, wg_ref, wu_ref, wd_ref,
     y_hbm) = refs[:13]
    ns = len(stream_blocks)
    out_refs = refs[13:13 + ns]
    rows_ref, sem = refs[13 + ns:]
    step = pl.program_id(0)
    nsteps = pl.num_programs(0)
    n = h_ref.shape[0]
    slot = step % 2

    def gather(w, to_slot):
        _window_chunks((cs_ref, cn_ref, cd_ref), w, rows_ref.at[to_slot], y_hbm, sem.at[to_slot],
                       to_hbm=False, spare_row=0)

    pl.when(step == 0)(functools.partial(gather, 0, 0))
    for parity in range(2):
        pl.when((step + 1 < nsteps) & (slot == parity))(
            functools.partial(gather, step + 1, 1 - parity))

    ys = _swiglu_packed(h_ref[...], wg_ref, wu_ref, wd_ref)
    _wait_rows(y_hbm, sem.at[slot], LOCAL_ROWS)
    w = w_ref[...]
    lp = lpos_ref[...]
    ya, yb = ys[:, :HALF], ys[:, HALF:]
    last = step * N_EXPERTS + (N_EXPERTS - 1)
    used = cs_ref[last] + cn_ref[last]

    def routed(c):
        col = lax.broadcasted_iota(jnp.int32, (n, n), 1) + c * n
        pw = jnp.zeros((n, n), F32)
        for k in range(TOP_K):
            pw = jnp.where(col == lp[:, k:k + 1], w[:, k:k + 1], pw)
        p_hi, p_lo = _split_bf16(pw)
        ra, rb = _unpack_pair(rows_ref[slot, c * n:(c + 1) * n, :])
        ra, rb = ra.astype(BF16), rb.astype(BF16)
        return _dot(p_hi, ra) + _dot(p_lo, ra), _dot(p_hi, rb) + _dot(p_lo, rb)

    def nothing():
        return jnp.zeros((n, HALF), F32), jnp.zeros((n, HALF), F32)

    for c in range(LOCAL_ROWS // n):
        if c < TOP_K:
            da, db = routed(c)
        else:
            da, db = lax.cond(c * n < used, functools.partial(routed, c), nothing)
        ya, yb = ya + da, yb + db
    ms = (jnp.sum(ya * ya, axis=-1, keepdims=True)
          + jnp.sum(yb * yb, axis=-1, keepdims=True)) * (1.0 / D_MODEL)
    inv = lax.rsqrt(ms + EPS)
    mod = mod_ref[...]
    g = g_ref[...]
    out_a = x_ref[:, 0:HALF] + mod[5:6, 0:HALF] * (ya * inv * g[:, 0:HALF])
    out_b = x_ref[:, HALF:] + mod[5:6, HALF:] * (yb * inv * g[:, HALF:])

    def store(o_ref):
        o_ref[:, 0:HALF] = out_a
        o_ref[:, HALF:] = out_b

    lo = 0
    for o_ref, nbk in zip(out_refs, stream_blocks):
        pl.when((step >= lo) & (step < lo + nbk))(functools.partial(store, o_ref))
        lo += nbk


def _ffn_out(tables, lpos_rows, h2p, w_rows, x1, mod_l, g_post, swg, swu, swd, y_buf, *, mod_map,
             stream_tokens):
    t_all = x1.shape[0]
    tm = TOKEN_BLOCK
    nsteps = t_all // tm
    row = lambda w: pl.BlockSpec((tm, w), lambda i: (i, 0))
    full = lambda a: pl.BlockSpec(a.shape, lambda i: (0,) * a.ndim)
    smem = lambda: pl.BlockSpec(memory_space=pltpu.SMEM)
    g = g_post.reshape(1, D_MODEL)
    stream_blocks = tuple(t // tm for t in stream_tokens)
    out_shape, out_specs = [], []
    lo = 0
    for t, nbk in zip(stream_tokens, stream_blocks):
        out_shape.append(jax.ShapeDtypeStruct((t, D_MODEL), F32))
        out_specs.append(pl.BlockSpec((tm, D_MODEL),
                                      lambda i, lo=lo, nbk=nbk: (jnp.clip(i - lo, 0, nbk - 1), 0)))
        lo += nbk
    return pl.pallas_call(
        functools.partial(_ffn_out_kernel, stream_blocks=stream_blocks),
        out_shape=out_shape,
        grid=(nsteps,),
        in_specs=[smem(), smem(), smem(),
                  row(TOP_K), row(HALF), row(TOP_K), row(D_MODEL),
                  pl.BlockSpec((None, N_MOD, D_MODEL), mod_map),
                  full(g), full(swg), full(swu), full(swd),
                  pl.BlockSpec(memory_space=pl.ANY)],
        out_specs=out_specs,
        scratch_shapes=[pltpu.VMEM((2, LOCAL_ROWS, HALF), jnp.uint32),
                        pltpu.SemaphoreType.DMA((2,))],
        compiler_params=_cparams("arbitrary"),
        name="ffn_out",
    )(*tables, lpos_rows, h2p, w_rows, x1, mod_l, g, swg, swu, swd, y_buf)


def _rope_tables(L):
    rows = L // GRID_W
    row = jnp.repeat(jnp.arange(rows), GRID_W).astype(F32)
    col = jnp.tile(jnp.arange(GRID_W), rows).astype(F32)
    n = DIFF_DK // 4
    inv = ROPE_BASE ** (-jnp.arange(n, dtype=F32) / n)
    ar, ac = row[:, None] * inv[None, :], col[:, None] * inv[None, :]
    cos64 = jnp.concatenate([jnp.cos(ar), jnp.cos(ar), jnp.cos(ac), jnp.cos(ac)], axis=-1)
    sin64 = jnp.concatenate([-jnp.sin(ar), jnp.sin(ar), -jnp.sin(ac), jnp.sin(ac)], axis=-1)
    return jnp.tile(cos64, (1, 2)), jnp.tile(sin64, (1, 2))


def kernel(x_prompt, x_sample, cache_k, cache_v, c, c_ctx, w_mod, b_mod, g_pre_mix, g_post_mix, g_pre_ffn, g_post_ffn, w_in, w_out, conv_w, conv_b, conv_ln_g, conv_ln_b, hy_short_w, hy_short_b, hy_w1, hy_b1, hy_w2, hy_b2, hy_freq, hy_w3, hy_bias, lam_q1, lam_k1, lam_q2, lam_k2, subln_g, router_w, router_b, exp_w_gate, exp_w_up, exp_w_down, sh_w_gate, sh_w_up, sh_w_down):
    n_p, l_p, _ = x_prompt.shape
    n_s, l_s, _ = x_sample.shape
    t_p, t_s = n_p * l_p, n_s * l_s

    cond = jnp.zeros((COND_ROWS, D_MODEL), F32).at[0].set(c_ctx).at[1:1 + n_s].set(c)
    mod = _modulation(cond, w_mod, b_mod).reshape(DEPTH, COND_ROWS, N_MOD, D_MODEL)
    rope = _rope_tables(l_s)

    t_all = t_p + t_s
    streams = {
        "p": dict(x=x_prompt.reshape(t_p, D_MODEL), T=t_p, L=l_p, row0=0, per_seq=False, hy_nb=8),
        "s": dict(x=x_sample.reshape(t_s, D_MODEL), T=t_s, L=l_s, row0=1, per_seq=True, hy_nb=1),
    }
    for st in streams.values():
        fwd, fwd_t = _dft_matrices(st["L"])
        feats, window = _filter_tables(st["L"])
        st["fmat32"] = jnp.asarray(fwd)
        st["fmat"] = jnp.asarray(fwd).astype(BF16)
        st["fmat_t"] = jnp.asarray(fwd_t).astype(BF16)
        st["filter_tables"] = (jnp.asarray(feats), jnp.asarray(window))
    nb_p, nb_s = t_p // TOKEN_BLOCK, l_s // TOKEN_BLOCK
    merged_mod_map = lambda i: (jnp.where(i < nb_p, 0, 1 + (i - nb_p) // nb_s), 0, 0)
    n_blocks = (t_all // TOKEN_BLOCK) * (LOCAL_ROWS // EXPERT_BLOCK) + N_EXPERTS

    caches = None
    for l in range(DEPTH):
        w_in_bf = w_in[l].astype(BF16)
        w_out_bf = w_out[l].astype(BF16)
        swg, swu, swd = (a[l].astype(BF16) for a in (sh_w_gate, sh_w_up, sh_w_down))
        lam_vecs = jnp.stack([lam_q1[l], lam_k1[l], lam_q2[l], lam_k2[l]], axis=0)
        filt_params = (hy_w1[l], hy_b1[l], hy_w2[l], hy_b2[l], hy_freq[l], hy_w3[l])
        parts = []
        for name, st in streams.items():
            L = st["L"]
            kw = dict(seq_len=L, cond_row0=st["row0"], per_seq_cond=st["per_seq"])
            if name == "p":
                uc, uh, q, k, v, ck, cv = _premix(st["x"], mod[l], g_pre_mix[l], w_in_bf, layer=l,
                                                  n_tokens=st["T"], tok0=0,
                                                  caches=caches, want_cache=True, **kw)
                caches = (ck, cv)
                ctx = None
            else:
                uc, uh, q, k, v = _premix(st["x"], mod[l], g_pre_mix[l], w_in_bf, layer=l,
                                          n_tokens=st["T"], tok0=0, rope_tables=rope, **kw)
                ctx = (cache_k, cache_v)
            ya = _conformer(uc, conv_w[l], conv_b[l], conv_ln_g[l], conv_ln_b[l], seq_len=L)
            kspec = _hyena_spectrum(L, st["fmat32"], *st["filter_tables"], *filt_params)
            yb = _hyena(uh, hy_short_w[l], hy_short_b[l], hy_bias[l], st["fmat"], st["fmat_t"],
                        kspec, seq_len=L, seqs_per_step=st["hy_nb"])
            o = _attention(q, k, v, lam_vecs, subln_g[l], seq_len=L, layer=l, ctx=ctx)
            parts.append((ya, yb, o, st["x"]))
        x1, h2p, logits_t = _postmix(parts, mod[l], g_post_mix[l], g_pre_ffn[l], w_out_bf,
                                     router_w[l].T, mod_map=merged_mod_map)

        eidx_t, w_t, cnt = _route_topk(logits_t, router_b[l])
        counts = cnt[:, 0]
        pstart, nblk, n_used = _block_plan(counts, n_blocks)
        lpos, tables = _route_dest(eidx_t, pstart)
        x_buf = _dispatch(tables, lpos, counts, pstart, n_used, h2p, n_blocks)
        y_buf = _experts(x_buf, nblk, pstart, n_used, exp_w_gate, exp_w_up, exp_w_down, layer=l)
        outs = _ffn_out(tables, lpos.T, h2p, w_t.T, x1, mod[l], g_post_ffn[l], swg, swu, swd,
                        y_buf, mod_map=merged_mod_map, stream_tokens=(t_p, t_s))
        streams["p"]["x"], streams["s"]["x"] = outs

    y_prompt = streams["p"]["x"].reshape(n_p, l_p, D_MODEL)
    y_sample = streams["s"]["x"].reshape(n_s, l_s, D_MODEL)
    return (y_prompt, y_sample, caches[0], caches[1])
```

```python
import functools
import math

import numpy as np
import jax
import jax.numpy as jnp
from jax import lax
from jax.experimental import pallas as pl
from jax.experimental.pallas import tpu as pltpu

F32 = jnp.float32
BF16 = jnp.bfloat16

D_MODEL = 1024
DEPTH = 2
GRID_W = 64
D_CONV = 256
D_HYENA = 256
D_ATTN = 512
DIFF_DK = 64
N_HEADS = 4
DIFF_DV = 128
HEAD_W = 2 * DIFF_DK
IN_COLS = 2 * D_CONV + 3 * D_HYENA + 3 * D_ATTN
COL_HY = 2 * D_CONV
COL_Q = COL_HY + 3 * D_HYENA
COL_K = COL_Q + D_ATTN
COL_V = COL_K + D_ATTN
CONV_WIDTH = 31
CONV_PAD = 16
SUBLANES = 8
HYENA_EMB = 17
FEAT_PAD = 32
HYENA_BANDS = 8
HYENA_MIN_DECAY = math.log(1e-2) / 1.5
HYENA_MAX_DECAY = math.log(1e-2) / 0.3
N_EXPERTS = 64
TOP_K = 8
N_GROUPS = 8
TOPK_GROUPS = 4
GROUP_SCORE_TOP = 2
D_EXPERT = 256
ROUTED_SCALE = 2.5
ROPE_BASE = 10000.0
EPS = 1e-6
N_MOD = 6
COND_ROWS = 16

HALF = D_MODEL // 2
LANES = 128
TOKEN_BLOCK = 256
SPARE_ROWS = N_EXPERTS * SUBLANES
LOCAL_ROWS = TOKEN_BLOCK * TOP_K + SPARE_ROWS
EXPERT_BLOCK = 256
EXPERT_IN_SLOTS = 4
EXPERT_OUT_SLOTS = 3
VMEM_LIMIT = 48 * 1024 * 1024


def _cparams(*sem):
    return pltpu.CompilerParams(dimension_semantics=sem, vmem_limit_bytes=VMEM_LIMIT)


def _rms(x, g):
    return x * lax.rsqrt(jnp.mean(x * x, axis=-1, keepdims=True) + EPS) * g


def _silu(x):
    return x * jax.nn.sigmoid(x)


def _dot(a, b):
    return jnp.dot(a, b, preferred_element_type=F32)


def _mod_kernel(c_ref, w_ref, b_ref, o_ref):
    s = _silu(c_ref[...])
    o_ref[...] = _dot(s.astype(BF16), w_ref[...].astype(BF16)) + b_ref[...]


def _modulation(cond, w_mod, b_mod):
    cols = N_MOD * D_MODEL
    cb = 1536
    return pl.pallas_call(
        _mod_kernel,
        out_shape=jax.ShapeDtypeStruct((DEPTH, COND_ROWS, cols), F32),
        grid=(DEPTH, cols // cb),
        in_specs=[
            pl.BlockSpec((COND_ROWS, D_MODEL), lambda l, j: (0, 0)),
            pl.BlockSpec((None, D_MODEL, cb), lambda l, j: (l, 0, j)),
            pl.BlockSpec((None, 1, cb), lambda l, j: (l, 0, j)),
        ],
        out_specs=pl.BlockSpec((None, COND_ROWS, cb), lambda l, j: (l, 0, j)),
        compiler_params=_cparams("arbitrary", "arbitrary"),
        name="modulation",
    )(cond, w_mod, b_mod.reshape(DEPTH, 1, cols))


def _rope_slab(x, cos, sin, lo_mask):
    swapped = jnp.where(lo_mask, pltpu.roll(x, HEAD_W - 16, axis=1), pltpu.roll(x, 16, axis=1))
    return x * cos + swapped * sin


def _premix_kernel(*refs, rope, cache, layer):
    x_ref, mod_ref, g_ref, w_ref = refs[:4]
    pos = 4
    if rope:
        cos_ref, sin_ref = refs[pos:pos + 2]
        pos += 2
    if cache == "alias":
        pos += 2
    uc_ref, uh_ref, q_ref, k_ref, v_ref = refs[pos:pos + 5]
    pos += 5
    if cache:
        ck_ref, cv_ref = refs[pos:pos + 2]

    x = x_ref[...]
    mod = mod_ref[...]
    h = (_rms(x, g_ref[...]) * (1.0 + mod[1:2]) + mod[0:1]).astype(BF16)
    uc_ref[...] = _dot(h, w_ref[:, 0:COL_HY]).astype(BF16)
    uh_ref[...] = _dot(h, w_ref[:, COL_HY:COL_Q]).astype(BF16)
    uq = _dot(h, w_ref[:, COL_Q:COL_K]) * (DIFF_DK ** -0.5)
    uk = _dot(h, w_ref[:, COL_K:COL_V])
    uv = _dot(h, w_ref[:, COL_V:IN_COLS])
    if rope:
        cos = cos_ref[...]
        sin = sin_ref[...]
        lane = lax.broadcasted_iota(jnp.int32, cos.shape, 1)
        lo_mask = (lane % 32) < 16
        for hh in range(N_HEADS):
            sl = slice(hh * HEAD_W, (hh + 1) * HEAD_W)
            q_ref[:, sl] = _rope_slab(uq[:, sl], cos, sin, lo_mask).astype(BF16)
            k_ref[:, sl] = _rope_slab(uk[:, sl], cos, sin, lo_mask).astype(BF16)
    else:
        q_ref[...] = uq.astype(BF16)
        k_ref[...] = uk.astype(BF16)
    v_ref[...] = uv.astype(BF16)
    if cache == "alias":
        for hh in range(N_HEADS):
            sl = slice(hh * HEAD_W, (hh + 1) * HEAD_W)
            ck_ref[hh] = uk[:, sl]
            cv_ref[hh] = uv[:, sl]
    elif cache == "new":
        for d in range(DEPTH):
            for hh in range(N_HEADS):
                sl = slice(hh * HEAD_W, (hh + 1) * HEAD_W)
                ck_ref[d, hh] = uk[:, sl] if d == layer else jnp.zeros_like(uk[:, sl])
                cv_ref[d, hh] = uv[:, sl] if d == layer else jnp.zeros_like(uv[:, sl])


def _premix(x, mod_l, g, w_in_bf, *, seq_len, cond_row0, per_seq_cond, layer, n_tokens, tok0,
            rope_tables=None, caches=None, want_cache=False):
    T = n_tokens
    tm = TOKEN_BLOCK
    nb = seq_len // tm
    n_seq = T // seq_len
    b0 = tok0 // tm
    if per_seq_cond:
        mod_map = lambda i: (cond_row0 + i // nb, 0, 0)
    else:
        mod_map = lambda i: (cond_row0, 0, 0)
    in_specs = [
        pl.BlockSpec((tm, D_MODEL), lambda i: (i + b0, 0)),
        pl.BlockSpec((None, N_MOD, D_MODEL), mod_map),
        pl.BlockSpec((1, D_MODEL), lambda i: (0, 0)),
        pl.BlockSpec((D_MODEL, IN_COLS), lambda i: (0, 0)),
    ]
    args = [x, mod_l, g.reshape(1, D_MODEL), w_in_bf]
    rope = rope_tables is not None
    if rope:
        in_specs += [pl.BlockSpec((tm, HEAD_W), lambda i: (i % nb, 0))] * 2
        args += list(rope_tables)
    out_shape = [
        jax.ShapeDtypeStruct((T, COL_HY), BF16),
        jax.ShapeDtypeStruct((T, 3 * D_HYENA), BF16),
        jax.ShapeDtypeStruct((T, D_ATTN), BF16),
        jax.ShapeDtypeStruct((T, D_ATTN), BF16),
        jax.ShapeDtypeStruct((T, D_ATTN), BF16),
    ]
    out_specs = [
        pl.BlockSpec((tm, COL_HY), lambda i: (i, 0)),
        pl.BlockSpec((tm, 3 * D_HYENA), lambda i: (i, 0)),
        pl.BlockSpec((tm, D_ATTN), lambda i: (i, 0)),
        pl.BlockSpec((tm, D_ATTN), lambda i: (i, 0)),
        pl.BlockSpec((tm, D_ATTN), lambda i: (i, 0)),
    ]
    cache = None
    aliases = {}
    if want_cache:
        cache = "new"
        cshape = (n_seq, DEPTH, N_HEADS, seq_len, HEAD_W)
        out_shape += [jax.ShapeDtypeStruct(cshape, F32)] * 2
        if caches is None:
            out_specs += [pl.BlockSpec((None, DEPTH, N_HEADS, tm, HEAD_W),
                                       lambda i: (i // nb, 0, 0, i % nb, 0))] * 2
        else:
            cache = "alias"
            out_specs += [pl.BlockSpec((None, None, N_HEADS, tm, HEAD_W),
                                       lambda i: (i // nb, layer, 0, i % nb, 0))] * 2
            aliases = {len(args): 5, len(args) + 1: 6}
            in_specs += [pl.BlockSpec(memory_space=pl.ANY)] * 2
            args += list(caches)
    return pl.pallas_call(
        functools.partial(_premix_kernel, rope=rope, cache=cache, layer=layer),
        out_shape=out_shape,
        grid=(T // tm,),
        in_specs=in_specs,
        out_specs=out_specs,
        input_output_aliases=aliases,
        compiler_params=_cparams("arbitrary"),
        name="premix",
    )(*args)


def _conformer_kernel(u_ref, w_ref, cb_ref, g_ref, b_ref, o_ref, pad_ref, sh_ref, *, seq_len, rows):
    L = seq_len
    u = u_ref[...].astype(F32)
    a = u[:, :D_CONV] * jax.nn.sigmoid(u[:, D_CONV:])
    zeros = jnp.zeros((CONV_PAD, D_CONV), F32)
    pad_ref[0:CONV_PAD, :] = zeros
    pad_ref[CONV_PAD + L:2 * CONV_PAD + L, :] = zeros
    pad_ref[CONV_PAD:CONV_PAD + L, :] = a
    span = L + 2 * CONV_PAD - SUBLANES
    for r in range(SUBLANES):
        sh_ref[r, 0:span, :] = pad_ref[r:r + span, :]
    first = CONV_PAD - (CONV_WIDTH - 1) // 2

    def chunk(c, carry):
        base = pl.multiple_of(c * rows, rows)
        acc = jnp.zeros((rows, D_CONV), F32)
        for t in range(CONV_WIDTH):
            off = first + t
            tap = sh_ref[off % SUBLANES, pl.ds(base + (off // SUBLANES) * SUBLANES, rows), :]
            acc = acc + tap * w_ref[t:t + 1, :]
        acc = acc + cb_ref[...]
        mu = jnp.mean(acc, axis=-1, keepdims=True)
        d = acc - mu
        var = jnp.mean(d * d, axis=-1, keepdims=True)
        y = d * lax.rsqrt(var + EPS) * g_ref[...] + b_ref[...]
        o_ref[pl.ds(base, rows), :] = _silu(y).astype(BF16)
        return carry

    lax.fori_loop(0, L // rows, chunk, 0)


def _conformer(uc, conv_w, conv_b, ln_g, ln_b, *, seq_len):
    T = uc.shape[0]
    n_seq = T // seq_len
    vec = lambda a: a.reshape(1, D_CONV)
    return pl.pallas_call(
        functools.partial(_conformer_kernel, seq_len=seq_len, rows=64),
        out_shape=jax.ShapeDtypeStruct((T, D_CONV), BF16),
        grid=(n_seq,),
        in_specs=[
            pl.BlockSpec((seq_len, 2 * D_CONV), lambda i: (i, 0)),
            pl.BlockSpec((CONV_WIDTH, D_CONV), lambda i: (0, 0)),
            pl.BlockSpec((1, D_CONV), lambda i: (0, 0)),
            pl.BlockSpec((1, D_CONV), lambda i: (0, 0)),
            pl.BlockSpec((1, D_CONV), lambda i: (0, 0)),
        ],
        out_specs=pl.BlockSpec((seq_len, D_CONV), lambda i: (i, 0)),
        scratch_shapes=[pltpu.VMEM((seq_len + 2 * CONV_PAD, D_CONV), F32),
                        pltpu.VMEM((SUBLANES, seq_len + 2 * CONV_PAD, D_CONV), F32)],
        compiler_params=_cparams("arbitrary"),
        name="conformer",
    )(uc, conv_w, vec(conv_b), vec(ln_g), vec(ln_b))


def _dft_matrices(L):
    k = np.arange(L, dtype=np.int64)[:, None]
    m = np.arange(L, dtype=np.int64)[None, :]
    ang = 2.0 * np.pi * (((2 * k + 1) * m) % (4 * L)).astype(np.float64) / (4 * L)
    fwd = np.concatenate([np.cos(ang), np.sin(ang)], axis=0).astype(np.float32)
    return fwd, np.ascontiguousarray(fwd.T)


def _filter_tables(L):
    t = np.linspace(0.0, 1.0, L)
    bands = np.linspace(1e-4, HYENA_BANDS - 1, HYENA_BANDS)
    ang = (2.0 * np.pi / L) * np.arange(L)[:, None] * bands[None, :]
    feats = np.zeros((L, FEAT_PAD), np.float32)
    feats[:, :HYENA_EMB] = np.concatenate([t[:, None], np.cos(ang), -np.sin(ang)], axis=-1)
    deltas = np.abs(np.linspace(HYENA_MIN_DECAY, HYENA_MAX_DECAY, D_HYENA))
    window = np.exp(-t[:, None] * deltas[None, :]).astype(np.float32)
    return feats, window


def _dot3(a, b):
    a_hi, a_lo = _split_bf16(a)
    b_hi, b_lo = _split_bf16(b)
    return _dot(a_hi, b_hi) + (_dot(a_hi, b_lo) + _dot(a_lo, b_hi))


def _spectrum_kernel(feat_ref, win_ref, w1_ref, b1_ref, w2_ref, b2_ref, fr_ref, w3_ref, f_ref,
                     o_ref, *, seq_len):
    L = seq_len
    C = D_HYENA
    fr = fr_ref[...]
    hid = jnp.sin(fr * (_dot3(feat_ref[...], w1_ref[...]) + b1_ref[...]))
    hid = jnp.sin(fr * (_dot3(hid, w2_ref[...]) + b2_ref[...]))
    filt = _dot3(hid, w3_ref[...])
    win = win_ref[...]
    first = lax.broadcasted_iota(jnp.int32, (L, C), 0) == 0
    for o in range(2):
        fwd = filt[:, o * C:(o + 1) * C] * win
        bwd = jnp.where(first, 0.0, filt[:, (2 + o) * C:(3 + o) * C] * win)
        o_ref[o, 0:L, :] = _dot3(f_ref[0:L, :], fwd + bwd) * (1.0 / L)
        o_ref[o, L:2 * L, :] = _dot3(f_ref[L:2 * L, :], bwd - fwd) * (1.0 / L)


def _hyena_spectrum(L, fmat, feats, window, hy_w1, hy_b1, hy_w2, hy_b2, hy_freq, hy_w3):
    w1 = jnp.zeros((FEAT_PAD, hy_w1.shape[1]), F32).at[:HYENA_EMB].set(hy_w1)
    vec = lambda a: a.reshape(1, -1)
    return pl.pallas_call(
        functools.partial(_spectrum_kernel, seq_len=L),
        out_shape=jax.ShapeDtypeStruct((2, 2 * L, D_HYENA), F32),
        compiler_params=pltpu.CompilerParams(vmem_limit_bytes=VMEM_LIMIT),
        name="hyena_spectrum",
    )(feats, window, w1, vec(hy_b1), hy_w2, vec(hy_b2), vec(hy_freq), hy_w3, fmat)


def _hyena_kernel(u_ref, sw_ref, sb_ref, hb_ref, f_ref, ft_ref, ks_ref, o_ref, pad_ref,
                  *, seq_len, n_seq):
    L = seq_len
    C = D_HYENA
    zrow = jnp.zeros((8, 3 * C), F32)
    for b in range(n_seq):
        u = u_ref[b].astype(F32)
        pad_ref[0:8, :] = zrow
        pad_ref[8 + L:16 + L, :] = zrow
        pad_ref[8:8 + L, :] = u
        s = (pad_ref[7:7 + L, :] * sw_ref[0:1, :] + u * sw_ref[1:2, :]
             + pad_ref[9:9 + L, :] * sw_ref[2:3, :] + sb_ref[...])
        z = s[:, 0:C]
        for o in range(2):
            gate = s[:, (o + 1) * C:(o + 2) * C]
            spec = _dot(f_ref[...], z.astype(BF16))
            zr, za = spec[:L], spec[L:]
            kre, kim = ks_ref[o, 0:L, :], ks_ref[o, L:2 * L, :]
            y = jnp.concatenate([zr * kre + za * kim, za * kre - zr * kim], axis=0)
            conv = _dot(ft_ref[...], y.astype(BF16))
            z = gate * (conv + hb_ref[o:o + 1, :] * z)
        o_ref[b] = z.astype(BF16)


def _hyena(uh, short_w, short_b, hy_bias, fmat_bf, fmat_t_bf, kspec, *, seq_len, seqs_per_step):
    T = uh.shape[0]
    n_seq = T // seq_len
    L = seq_len
    nb = seqs_per_step
    u3 = uh.reshape(n_seq, L, 3 * D_HYENA)
    out = pl.pallas_call(
        functools.partial(_hyena_kernel, seq_len=L, n_seq=nb),
        out_shape=jax.ShapeDtypeStruct((n_seq, L, D_HYENA), BF16),
        grid=(n_seq // nb,),
        in_specs=[
            pl.BlockSpec((nb, L, 3 * D_HYENA), lambda i: (i, 0, 0)),
            pl.BlockSpec((3, 3 * D_HYENA), lambda i: (0, 0)),
            pl.BlockSpec((1, 3 * D_HYENA), lambda i: (0, 0)),
            pl.BlockSpec((2, D_HYENA), lambda i: (0, 0)),
            pl.BlockSpec((2 * L, L), lambda i: (0, 0)),
            pl.BlockSpec((L, 2 * L), lambda i: (0, 0)),
            pl.BlockSpec((2, 2 * L, D_HYENA), lambda i: (0, 0, 0)),
        ],
        out_specs=pl.BlockSpec((nb, L, D_HYENA), lambda i: (i, 0, 0)),
        scratch_shapes=[pltpu.VMEM((L + 16, 3 * D_HYENA), F32)],
        compiler_params=_cparams("arbitrary"),
        name="hyena",
    )(u3, short_w, short_b.reshape(1, 3 * D_HYENA), hy_bias, fmat_bf, fmat_t_bf, kspec)
    return out.reshape(T, D_HYENA)


def _attn_kernel(*refs, lam_init, has_ctx):
    q_ref, k_ref, v_ref = refs[:3]
    pos = 3
    if has_ctx:
        ck_ref, cv_ref = refs[pos:pos + 2]
        pos += 2
    lam_ref, g_ref, o_ref = refs[pos:pos + 3]

    lv = lam_ref[...]
    lam = (jnp.exp(jnp.sum(lv[0:1] * lv[1:2], axis=-1, keepdims=True))
           - jnp.exp(jnp.sum(lv[2:3] * lv[3:4], axis=-1, keepdims=True)) + lam_init)

    tq = q_ref.shape[0]
    lane = lax.broadcasted_iota(jnp.int32, (tq, HEAD_W), 1)
    dn = (((1,), (1,)), ((), ()))
    for h in range(N_HEADS):
        sl = slice(h * HEAD_W, (h + 1) * HEAD_W)
        q = q_ref[:, sl]
        k = k_ref[:, sl]
        v = v_ref[:, sl]
        if has_ctx:
            ck = ck_ref[h].astype(BF16)
            cv = cv_ref[h].astype(BF16)
        zero = jnp.zeros_like(q)
        q2 = jnp.concatenate([jnp.where(lane < DIFF_DK, q, zero),
                              jnp.where(lane >= DIFF_DK, q, zero)], axis=0)
        s = lax.dot_general(q2, k, dn, preferred_element_type=F32)
        m = jnp.max(s, axis=-1, keepdims=True)
        if has_ctx:
            sc = lax.dot_general(q2, ck, dn, preferred_element_type=F32)
            m = jnp.maximum(m, jnp.max(sc, axis=-1, keepdims=True))
        p = jnp.exp(s - m)
        den = jnp.sum(p, axis=-1, keepdims=True)
        acc = _dot(p.astype(BF16), v)
        if has_ctx:
            pc = jnp.exp(sc - m)
            den = den + jnp.sum(pc, axis=-1, keepdims=True)
            acc = acc + _dot(pc.astype(BF16), cv)
        both = acc / den
        o = both[:tq] - lam * both[tq:]
        o_ref[:, sl] = (_rms(o, g_ref[...]) * (1.0 - lam_init)).astype(BF16)


def _attention(q, k, v, lam_vecs, subln_g, *, seq_len, layer, ctx=None):
    T = q.shape[0]
    L = seq_len
    n_seq = T // L
    tq = TOKEN_BLOCK
    nq = L // tq
    lam_init = 0.8 - 0.6 * math.exp(-0.3 * layer)
    q3, k3, v3 = (a.reshape(n_seq, L, D_ATTN) for a in (q, k, v))
    in_specs = [
        pl.BlockSpec((None, tq, D_ATTN), lambda b, i: (b, i, 0)),
        pl.BlockSpec((None, L, D_ATTN), lambda b, i: (b, 0, 0)),
        pl.BlockSpec((None, L, D_ATTN), lambda b, i: (b, 0, 0)),
    ]
    args = [q3, k3, v3]
    if ctx is not None:
        past = ctx[0].shape[3]
        in_specs += [pl.BlockSpec((None, None, N_HEADS, past, HEAD_W),
                                  lambda b, i: (b, layer, 0, 0, 0))] * 2
        args += list(ctx)
    in_specs += [
        pl.BlockSpec((4, DIFF_DK), lambda b, i: (0, 0)),
        pl.BlockSpec((1, DIFF_DV), lambda b, i: (0, 0)),
    ]
    args += [lam_vecs, subln_g.reshape(1, DIFF_DV)]
    out = pl.pallas_call(
        functools.partial(_attn_kernel, lam_init=lam_init, has_ctx=ctx is not None),
        out_shape=jax.ShapeDtypeStruct((n_seq, L, D_ATTN), BF16),
        grid=(n_seq, nq),
        in_specs=in_specs,
        out_specs=pl.BlockSpec((None, tq, D_ATTN), lambda b, i: (b, i, 0)),
        compiler_params=_cparams("arbitrary", "arbitrary"),
        name="diff_attention",
    )(*args)
    return out.reshape(T, D_ATTN)


def _split_bf16(a):
    hi = a.astype(BF16)
    lo = (a - hi.astype(F32)).astype(BF16)
    return hi, lo


def _pack_pair(a, b):
    ua = lax.bitcast_convert_type(a.astype(BF16).astype(F32), jnp.uint32)
    ub = lax.bitcast_convert_type(b.astype(BF16).astype(F32), jnp.uint32)
    return (ua >> 16) | ub


def _unpack_pair(w):
    a = lax.bitcast_convert_type(w << 16, F32)
    b = lax.bitcast_convert_type(w & jnp.uint32(0xFFFF0000), F32)
    return a, b


def _postmix_kernel(*refs, stream_blocks):
    ns = len(stream_blocks)
    mod_ref, gpost_ref, gpre_ref, w_ref, rw_ref, x1_ref, h2_ref, lg_ref = refs[4 * ns:]
    i = pl.program_id(0)

    def body(ya_ref, yb_ref, o_ref, x_ref):
        y = (_dot(ya_ref[...], w_ref[0:D_CONV, :])
             + _dot(yb_ref[...], w_ref[D_CONV:D_CONV + D_HYENA, :])
             + _dot(o_ref[...], w_ref[D_CONV + D_HYENA:, :]))
        mod = mod_ref[...]
        x1 = x_ref[...] + mod[2:3] * _rms(y, gpost_ref[...])
        x1_ref[...] = x1
        h2 = _rms(x1, gpre_ref[...]) * (1.0 + mod[4:5]) + mod[3:4]
        h2_ref[...] = _pack_pair(h2[:, :HALF], h2[:, HALF:])
        h_hi, h_lo = _split_bf16(h2)
        r_hi, r_lo = _split_bf16(rw_ref[...])
        dn = (((1,), (1,)), ((), ()))
        mm = lambda a, b: lax.dot_general(a, b, dn, preferred_element_type=F32)
        lg_ref[...] = mm(r_hi, h_hi) + (mm(r_lo, h_hi) + mm(r_hi, h_lo))

    lo = 0
    for j, nbk in enumerate(stream_blocks):
        pl.when((i >= lo) & (i < lo + nbk))(functools.partial(body, *refs[4 * j:4 * j + 4]))
        lo += nbk


def _postmix(parts, mod_l, g_post, g_pre, w_out_bf, router_wt, *, mod_map):
    t_all = sum(p[0].shape[0] for p in parts)
    tm = TOKEN_BLOCK
    full = lambda a: pl.BlockSpec(a.shape, lambda i: (0,) * a.ndim)
    gp, ge = g_post.reshape(1, D_MODEL), g_pre.reshape(1, D_MODEL)
    stream_blocks = tuple(p[0].shape[0] // tm for p in parts)
    in_specs, args = [], []
    lo = 0
    for part, nbk in zip(parts, stream_blocks):
        local = lambda i, lo=lo, nbk=nbk: (jnp.clip(i - lo, 0, nbk - 1), 0)
        in_specs += [pl.BlockSpec((tm, D_CONV), local), pl.BlockSpec((tm, D_HYENA), local),
                     pl.BlockSpec((tm, D_ATTN), local), pl.BlockSpec((tm, D_MODEL), local)]
        args += list(part)
        lo += nbk
    in_specs += [pl.BlockSpec((None, N_MOD, D_MODEL), mod_map),
                 full(gp), full(ge), full(w_out_bf), full(router_wt)]
    args += [mod_l, gp, ge, w_out_bf, router_wt]
    return pl.pallas_call(
        functools.partial(_postmix_kernel, stream_blocks=stream_blocks),
        out_shape=[
            jax.ShapeDtypeStruct((t_all, D_MODEL), F32),
            jax.ShapeDtypeStruct((t_all, HALF), jnp.uint32),
            jax.ShapeDtypeStruct((N_EXPERTS, t_all), F32),
        ],
        grid=(t_all // tm,),
        in_specs=in_specs,
        out_specs=[pl.BlockSpec((tm, D_MODEL), lambda i: (i, 0)),
                   pl.BlockSpec((tm, HALF), lambda i: (i, 0)),
                   pl.BlockSpec((N_EXPERTS, tm), lambda i: (0, i))],
        compiler_params=_cparams("arbitrary"),
        name="postmix",
    )(*args)


def _first_max(vals, idx, sentinel):
    m = jnp.max(vals, axis=0, keepdims=True)
    i = jnp.min(jnp.where(vals == m, idx, sentinel), axis=0, keepdims=True)
    return m, i


def _route_topk_kernel(lg_ref, rb_ref, eidx_ref, w_ref, cnt_ref):
    step = pl.program_id(0)
    per = N_EXPERTS // N_GROUPS
    n = lg_ref.shape[1]
    s = jax.nn.sigmoid(lg_ref[...])
    sel = s + rb_ref[...]
    sub = lax.broadcasted_iota(jnp.int32, (per, n), 0)
    neg = jnp.float32(-jnp.inf)

    gs = []
    for g in range(N_GROUPS):
        xg = sel[g * per:(g + 1) * per]
        total = jnp.zeros((1, n), F32)
        for _ in range(GROUP_SCORE_TOP):
            m, i = _first_max(xg, sub, per)
            total = total + m
            xg = jnp.where(sub == i, neg, xg)
        gs.append(total)
    gscore = jnp.concatenate(gs, axis=0)
    gkeep = jnp.zeros((N_GROUPS, n), F32)
    for _ in range(TOPK_GROUPS):
        _, gi = _first_max(gscore, sub, N_GROUPS)
        hit = sub == gi
        gkeep = jnp.where(hit, 1.0, gkeep)
        gscore = jnp.where(hit, neg, gscore)

    eio = [sub + g * per for g in range(N_GROUPS)]
    sg = [s[g * per:(g + 1) * per] for g in range(N_GROUPS)]
    mk = [jnp.where(gkeep[g:g + 1] > 0.5, sel[g * per:(g + 1) * per], neg) for g in range(N_GROUPS)]
    oh = [jnp.zeros((per, n), F32) for _ in range(N_GROUPS)]
    idx_rows, s_rows = [], []
    for _ in range(TOP_K):
        best = functools.reduce(jnp.maximum, mk)
        m = jnp.max(best, axis=0, keepdims=True)
        cand = functools.reduce(jnp.minimum,
                                [jnp.where(mk[g] == m, eio[g], N_EXPERTS) for g in range(N_GROUPS)])
        idx = jnp.min(cand, axis=0, keepdims=True)
        acc = jnp.zeros((per, n), F32)
        for g in range(N_GROUPS):
            hit = eio[g] == idx
            acc = acc + jnp.where(hit, sg[g], 0.0)
            mk[g] = jnp.where(hit, neg, mk[g])
            oh[g] = jnp.where(hit, 1.0, oh[g])
        idx_rows.append(idx)
        s_rows.append(jnp.sum(acc, axis=0, keepdims=True))
    eidx_ref[...] = jnp.concatenate(idx_rows, axis=0)
    sk = jnp.concatenate(s_rows, axis=0)
    w_ref[...] = sk / jnp.sum(sk, axis=0, keepdims=True) * ROUTED_SCALE

    @pl.when(step == 0)
    def _():
        cnt_ref[...] = jnp.zeros_like(cnt_ref)

    tot = jnp.concatenate([jnp.sum(o, axis=1, keepdims=True) for o in oh], axis=0)
    tot8 = _ceil_tile(tot)
    cnt_ref[...] = cnt_ref[...] + jnp.broadcast_to(tot8, cnt_ref.shape).astype(jnp.int32)


def _ceil_tile(count):
    return jnp.floor((count + (SUBLANES - 1.0)) * (1.0 / SUBLANES)) * SUBLANES


def _route_topk(logits_t, router_b):
    t_all = logits_t.shape[1]
    rb = TOKEN_BLOCK
    return pl.pallas_call(
        _route_topk_kernel,
        out_shape=[
            jax.ShapeDtypeStruct((TOP_K, t_all), jnp.int32),
            jax.ShapeDtypeStruct((TOP_K, t_all), F32),
            jax.ShapeDtypeStruct((N_EXPERTS, LANES), jnp.int32),
        ],
        grid=(t_all // rb,),
        in_specs=[pl.BlockSpec((N_EXPERTS, rb), lambda i: (0, i)),
                  pl.BlockSpec((N_EXPERTS, 1), lambda i: (0, 0))],
        out_specs=[pl.BlockSpec((TOP_K, rb), lambda i: (0, i)),
                   pl.BlockSpec((TOP_K, rb), lambda i: (0, i)),
                   pl.BlockSpec((N_EXPERTS, LANES), lambda i: (0, 0))],
        compiler_params=_cparams("arbitrary"),
        name="route_topk",
    )(logits_t, router_b.reshape(N_EXPERTS, 1))


def _route_dest_kernel(eidx_ref, ps_ref, lpos_ref, cs_ref, cn_ref, cd_ref, carry_ref):
    step = pl.program_id(0)
    per = N_EXPERTS // N_GROUPS
    n = eidx_ref.shape[1]

    @pl.when(step == 0)
    def _():
        carry_ref[...] = jnp.zeros_like(carry_ref)
        cs_ref[...] = jnp.zeros_like(cs_ref)
        cn_ref[...] = jnp.zeros_like(cn_ref)
        cd_ref[...] = jnp.zeros_like(cd_ref)

    eidx = eidx_ref[...]
    sub = lax.broadcasted_iota(jnp.int32, (per, n), 0)
    rows = []
    for g in range(N_GROUPS):
        eio = sub + g * per
        o = jnp.zeros((per, n), F32)
        for k in range(TOP_K):
            o = jnp.where(eio == eidx[k:k + 1], 1.0, o)
        rows.append(o)
    onehot = jnp.concatenate(rows, axis=0)
    r = lax.broadcasted_iota(jnp.int32, (n, n), 0)
    c = lax.broadcasted_iota(jnp.int32, (n, n), 1)
    before = jnp.where(r < c, 1.0, 0.0).astype(BF16)
    rank = _dot(onehot.astype(BF16), before)
    size = _ceil_tile(jnp.sum(onehot, axis=1, keepdims=True))
    er = lax.broadcasted_iota(jnp.int32, (N_EXPERTS, N_EXPERTS), 0)
    ec = lax.broadcasted_iota(jnp.int32, (N_EXPERTS, N_EXPERTS), 1)
    lower = jnp.where(ec < er, 1.0, 0.0).astype(BF16)
    tiles = jnp.broadcast_to(size * (1.0 / SUBLANES), (N_EXPERTS, LANES)).astype(BF16)
    start = _dot(lower, tiles)[:, 0:1] * SUBLANES
    base = (rank + start).astype(jnp.int32)
    out = []
    for k in range(TOP_K):
        acc = jnp.zeros((per, n), jnp.int32)
        for g in range(N_GROUPS):
            acc = acc + jnp.where(sub + g * per == eidx[k:k + 1], base[g * per:(g + 1) * per], 0)
        out.append(jnp.sum(acc, axis=0, keepdims=True))
    lpos_ref[...] = jnp.concatenate(out, axis=0)

    col = lax.broadcasted_iota(jnp.int32, cs_ref.shape, 1) == step
    carry = carry_ref[:, 0:1]
    cs_ref[...] = jnp.where(col, start.astype(jnp.int32), cs_ref[...])
    cn_ref[...] = jnp.where(col, size.astype(jnp.int32), cn_ref[...])
    cd_ref[...] = jnp.where(col, carry.astype(jnp.int32) + ps_ref[...], cd_ref[...])
    carry_ref[...] = carry_ref[...] + jnp.broadcast_to(size, carry_ref.shape)


def _route_dest(eidx_t, pstart):
    t_all = eidx_t.shape[1]
    rb = TOKEN_BLOCK
    n_win = t_all // rb
    cols = -(-n_win // LANES) * LANES
    table = jax.ShapeDtypeStruct((N_EXPERTS, cols), jnp.int32)
    table_spec = pl.BlockSpec((N_EXPERTS, cols), lambda i: (0, 0))
    lpos, cs, cn, cd = pl.pallas_call(
        _route_dest_kernel,
        out_shape=[jax.ShapeDtypeStruct((TOP_K, t_all), jnp.int32), table, table, table],
        grid=(n_win,),
        in_specs=[pl.BlockSpec((TOP_K, rb), lambda i: (0, i)),
                  pl.BlockSpec((N_EXPERTS, 1), lambda i: (0, 0))],
        out_specs=[pl.BlockSpec((TOP_K, rb), lambda i: (0, i)), table_spec, table_spec, table_spec],
        scratch_shapes=[pltpu.VMEM((N_EXPERTS, LANES), F32)],
        compiler_params=_cparams("arbitrary"),
        name="route_dest",
    )(eidx_t, pstart.reshape(N_EXPERTS, 1))
    flat = lambda a: a[:, :n_win].T.reshape(-1)
    return lpos, (flat(cs), flat(cn), flat(cd))


def _wait_rows(hbm, sem, n_rows):
    pltpu.make_async_copy(hbm.at[pl.ds(0, n_rows)], hbm.at[pl.ds(0, n_rows)], sem).wait()


def _window_chunks(tables, w, local_ref, hbm, sem, *, to_hbm, spare_row):
    cs_ref, cn_ref, cd_ref = tables

    def copy(local_row, hbm_row, size):
        lo = local_ref.at[pl.ds(pl.multiple_of(local_row, SUBLANES), size)]
        hi = hbm.at[pl.ds(pl.multiple_of(hbm_row, SUBLANES), size)]
        src, dst = (lo, hi) if to_hbm else (hi, lo)
        pltpu.make_async_copy(src, dst, sem).start()

    def pieces(n, local_row, hbm_row, largest):
        size = largest
        while size >= SUBLANES:
            take = (n & size) != 0
            pl.when(take)(functools.partial(copy, local_row, hbm_row, size))
            step = jnp.where(take, size, 0)
            local_row, hbm_row = local_row + step, hbm_row + step
            size //= 2

    def per_expert(e, carry):
        i = w * N_EXPERTS + e
        pieces(cn_ref[i], cs_ref[i], cd_ref[i], TOKEN_BLOCK)
        return carry

    lax.fori_loop(0, N_EXPERTS, per_expert, 0)
    last = w * N_EXPERTS + (N_EXPERTS - 1)
    used = cs_ref[last] + cn_ref[last]
    pieces(LOCAL_ROWS - used, used, spare_row, SPARE_ROWS)


def _zero_padding(cnt_ref, ps_ref, nu_ref, zero_ref, x_hbm, zsem, n_blocks, *, wait):
    def copy(row, size):
        cp = pltpu.make_async_copy(zero_ref.at[pl.ds(0, size)], x_hbm.at[pl.ds(row, size)], zsem)
        if wait:
            cp.wait()
        else:
            cp.start()

    def per_expert(e, carry):
        cnt = cnt_ref[e]
        pad = (EXPERT_BLOCK - cnt % EXPERT_BLOCK) % EXPERT_BLOCK
        row = ps_ref[e] + cnt
        head = pad % SUBLANES
        for j in range(SUBLANES - 1):
            pl.when(j < head)(functools.partial(copy, row + j, 1))
        row = row + head
        size = EXPERT_BLOCK // 2
        while size >= SUBLANES:
            take = (pad & size) != 0
            pl.when(take)(functools.partial(copy, pl.multiple_of(row, SUBLANES), size))
            row = row + jnp.where(take, size, 0)
            size //= 2
        return carry

    lax.fori_loop(0, N_EXPERTS, per_expert, 0)

    def per_block(j, carry):
        copy(pl.multiple_of(j * EXPERT_BLOCK, EXPERT_BLOCK), EXPERT_BLOCK)
        return carry

    lax.fori_loop(nu_ref[0], n_blocks, per_block, 0)


def _dispatch_kernel(cs_ref, cn_ref, cd_ref, cnt_ref, ps_ref, nu_ref, lpos_ref, h_ref, x_hbm,
                     gbuf, zero_ref, sem, zsem, ssem):
    w = pl.program_id(0)
    nsteps = pl.num_programs(0)
    slot = w % 2
    n_blocks = (x_hbm.shape[0] - 2 * SPARE_ROWS) // EXPERT_BLOCK
    spare0 = n_blocks * EXPERT_BLOCK
    zero_args = (cnt_ref, ps_ref, nu_ref, zero_ref, x_hbm, zsem, n_blocks)

    def spare_zero(j):
        return pltpu.make_async_copy(
            zero_ref, x_hbm.at[pl.ds(spare0 + j * EXPERT_BLOCK, EXPERT_BLOCK)], ssem)

    @pl.when(w == 0)
    def _():
        zero_ref[...] = jnp.zeros_like(zero_ref)
        _zero_padding(*zero_args, wait=False)
        for j in range(2 * SPARE_ROWS // EXPERT_BLOCK):
            spare_zero(j).start()

    ha, hb = _unpack_pair(h_ref[...])
    ha, hb = ha.astype(BF16), hb.astype(BF16)
    lp = lpos_ref[...]
    n = h_ref.shape[0]
    last = w * N_EXPERTS + (N_EXPERTS - 1)
    used = cs_ref[last] + cn_ref[last]

    lp = lp.astype(jnp.int16)
    one = jnp.ones((n, n), BF16)

    def permute(c):
        row = (lax.broadcasted_iota(jnp.int32, (n, n), 0) + c * n).astype(jnp.int16)
        pb = jnp.zeros((n, n), BF16)
        for k in range(TOP_K):
            pb = jnp.where(row == lp[k:k + 1, :], one, pb)
        ga = lax.bitcast_convert_type(_dot(pb, ha), jnp.uint32)
        gb = lax.bitcast_convert_type(_dot(pb, hb), jnp.uint32)
        gbuf[slot, c * n:(c + 1) * n, :] = (ga >> 16) | gb

    def blank(c):
        gbuf[slot, c * n:(c + 1) * n, :] = jnp.zeros((n, HALF), jnp.uint32)

    for c in range(LOCAL_ROWS // n):
        if c < TOP_K:
            permute(c)
        else:
            pl.when(c * n < used)(functools.partial(permute, c))
            pl.when(c * n >= used)(functools.partial(blank, c))

    @pl.when(w == 0)
    def _():
        for j in range(2 * SPARE_ROWS // EXPERT_BLOCK):
            spare_zero(j).wait()

    _window_chunks((cs_ref, cn_ref, cd_ref), w, gbuf.at[slot], x_hbm, sem.at[slot],
                   to_hbm=True, spare_row=spare0 + slot * SPARE_ROWS)

    @pl.when(w > 0)
    def _():
        _wait_rows(x_hbm, sem.at[1 - slot], LOCAL_ROWS)

    @pl.when(w == nsteps - 1)
    def _():
        _wait_rows(x_hbm, sem.at[slot], LOCAL_ROWS)
        _zero_padding(*zero_args, wait=True)


def _dispatch(tables, lpos, counts, pstart, n_used, h2p, n_blocks):
    t_all = h2p.shape[0]
    sb = TOKEN_BLOCK
    smem = lambda: pl.BlockSpec(memory_space=pltpu.SMEM)
    return pl.pallas_call(
        _dispatch_kernel,
        out_shape=jax.ShapeDtypeStruct((n_blocks * EXPERT_BLOCK + 2 * SPARE_ROWS, HALF), jnp.uint32),
        grid=(t_all // sb,),
        in_specs=[smem(), smem(), smem(), smem(), smem(), smem(),
                  pl.BlockSpec((TOP_K, sb), lambda i: (0, i)),
                  pl.BlockSpec((sb, HALF), lambda i: (i, 0))],
        out_specs=pl.BlockSpec(memory_space=pl.ANY),
        scratch_shapes=[pltpu.VMEM((2, LOCAL_ROWS, HALF), jnp.uint32),
                        pltpu.VMEM((EXPERT_BLOCK, HALF), jnp.uint32),
                        pltpu.SemaphoreType.DMA((2,)), pltpu.SemaphoreType.DMA,
                        pltpu.SemaphoreType.DMA],
        compiler_params=_cparams("arbitrary"),
        name="dispatch",
    )(*tables, counts, pstart, n_used, lpos, h2p)


def _expert_kernel(nb_ref, ps_ref, nu_ref, wg_ref, wu_ref, wd_ref, x_hbm, y_hbm,
                   wg_bf, wu_bf, wd_bf, xbuf, ybuf, xsem, ysem):
    e = pl.program_id(0)
    nblk = nb_ref[e]
    first = ps_ref[e] // EXPERT_BLOCK
    n_used = nu_ref[0]
    xs, ys = xbuf.shape[0], ybuf.shape[0]
    wg_bf[...] = wg_ref[...].astype(BF16)
    wu_bf[...] = wu_ref[...].astype(BF16)
    wd_bf[...] = wd_ref[...].astype(BF16)

    def rows(b):
        return pl.ds(pl.multiple_of(b * EXPERT_BLOCK, EXPERT_BLOCK), EXPERT_BLOCK)

    def x_copy(b):
        return pltpu.make_async_copy(x_hbm.at[rows(b)], xbuf.at[b % xs], xsem.at[b % xs])

    def y_copy(b):
        return pltpu.make_async_copy(ybuf.at[b % ys], y_hbm.at[rows(b)], ysem.at[b % ys])

    @pl.when(e == 0)
    def _():
        for b in range(xs - 1):
            pl.when(b < n_used)(lambda b=b: x_copy(b).start())

    def block(j, carry):
        b = first + j
        x_copy(b).wait()
        pl.when(b + xs - 1 < n_used)(lambda: x_copy(b + xs - 1).start())
        pl.when(b >= ys)(lambda: y_copy(b - ys).wait())
        y = _swiglu_packed(xbuf[b % xs], wg_bf, wu_bf, wd_bf)
        ybuf[b % ys] = _pack_pair(y[:, :HALF], y[:, HALF:])
        y_copy(b).start()
        return carry

    lax.fori_loop(0, nblk, block, 0)

    @pl.when(e == pl.num_programs(0) - 1)
    def _():
        for d in range(ys):
            b = n_used - ys + d
            pl.when(b >= 0)(lambda b=b: y_copy(b).wait())
        n_blocks = y_hbm.shape[0] // EXPERT_BLOCK
        ybuf[0] = jnp.zeros_like(ybuf[0])

        def tail(j, carry, wait):
            cp = pltpu.make_async_copy(
                ybuf.at[0], y_hbm.at[pl.ds(pl.multiple_of(j * EXPERT_BLOCK, EXPERT_BLOCK),
                                           EXPERT_BLOCK)], ysem.at[0])
            cp.wait() if wait else cp.start()
            return carry

        lax.fori_loop(nu_ref[0], n_blocks, functools.partial(tail, wait=False), 0)
        lax.fori_loop(nu_ref[0], n_blocks, functools.partial(tail, wait=True), 0)


def _swiglu_packed(xp, wg_ref, wu_ref, wd_ref):
    xa, xb = _unpack_pair(xp)
    xa, xb = xa.astype(BF16), xb.astype(BF16)
    gate = _dot(xa, wg_ref[0:HALF, :]) + _dot(xb, wg_ref[HALF:, :])
    up = _dot(xa, wu_ref[0:HALF, :]) + _dot(xb, wu_ref[HALF:, :])
    return _dot((_silu(gate) * up).astype(BF16), wd_ref[...])


def _experts(x_buf, nblk, pstart, n_used, wg, wu, wd, *, layer):
    grid_spec = pltpu.PrefetchScalarGridSpec(
        num_scalar_prefetch=3,
        grid=(N_EXPERTS,),
        in_specs=[
            pl.BlockSpec((None, None, D_MODEL, D_EXPERT), lambda e, *_: (layer, e, 0, 0)),
            pl.BlockSpec((None, None, D_MODEL, D_EXPERT), lambda e, *_: (layer, e, 0, 0)),
            pl.BlockSpec((None, None, D_EXPERT, D_MODEL), lambda e, *_: (layer, e, 0, 0)),
            pl.BlockSpec(memory_space=pl.ANY),
        ],
        out_specs=pl.BlockSpec(memory_space=pl.ANY),
        scratch_shapes=[
            pltpu.VMEM((D_MODEL, D_EXPERT), BF16),
            pltpu.VMEM((D_MODEL, D_EXPERT), BF16),
            pltpu.VMEM((D_EXPERT, D_MODEL), BF16),
            pltpu.VMEM((EXPERT_IN_SLOTS, EXPERT_BLOCK, HALF), jnp.uint32),
            pltpu.VMEM((EXPERT_OUT_SLOTS, EXPERT_BLOCK, HALF), jnp.uint32),
            pltpu.SemaphoreType.DMA((EXPERT_IN_SLOTS,)),
            pltpu.SemaphoreType.DMA((EXPERT_OUT_SLOTS,)),
        ],
    )
    return pl.pallas_call(
        _expert_kernel,
        out_shape=jax.ShapeDtypeStruct(x_buf.shape, jnp.uint32),
        grid_spec=grid_spec,
        compiler_params=_cparams("arbitrary"),
        name="experts",
    )(nblk, pstart, n_used, wg, wu, wd, x_buf)


def _block_plan(counts, n_blocks):
    nblk = (counts + EXPERT_BLOCK - 1) // EXPERT_BLOCK
    pend = jnp.cumsum(nblk)
    pstart = ((pend - nblk) * EXPERT_BLOCK).astype(jnp.int32)
    return pstart, nblk.astype(jnp.int32), pend[-1:].astype(jnp.int32)


def _ffn_out_kernel(*refs, stream_blocks):
    (cs_ref, cn_ref, cd_ref, lpos_ref, h_ref, w_ref, x_ref, mod_ref, g_ref, wg_ref, wu_ref, wd_ref,
     y_hbm) = refs[:13]
    ns = len(stream_blocks)
    out_refs = refs[13:13 + ns]
    rows_ref, sem = refs[13 + ns:]
    step = pl.program_id(0)
    nsteps = pl.num_programs(0)
    n = h_ref.shape[0]
    slot = step % 2

    def gather(w, to_slot):
        _window_chunks((cs_ref, cn_ref, cd_ref), w, rows_ref.at[to_slot], y_hbm, sem.at[to_slot],
                       to_hbm=False, spare_row=0)

    pl.when(step == 0)(functools.partial(gather, 0, 0))
    for parity in range(2):
        pl.when((step + 1 < nsteps) & (slot == parity))(
            functools.partial(gather, step + 1, 1 - parity))

    ys = _swiglu_packed(h_ref[...], wg_ref, wu_ref, wd_ref)
    _wait_rows(y_hbm, sem.at[slot], LOCAL_ROWS)
    w = w_ref[...].astype(BF16)
    lp = lpos_ref[...].astype(jnp.int16)
    ya, yb = ys[:, :HALF], ys[:, HALF:]
    for c in range(LOCAL_ROWS // n):
        col = (lax.broadcasted_iota(jnp.int32, (n, n), 1) + c * n).astype(jnp.int16)
        pw = jnp.zeros((n, n), BF16)
        for k in range(TOP_K):
            pw = jnp.where(col == lp[:, k:k + 1], jnp.broadcast_to(w[:, k:k + 1], (n, n)), pw)
        ra, rb = _unpack_pair(rows_ref[slot, c * n:(c + 1) * n, :])
        ya = ya + _dot(pw, ra.astype(BF16))
        yb = yb + _dot(pw, rb.astype(BF16))
    ms = (jnp.sum(ya * ya, axis=-1, keepdims=True)
          + jnp.sum(yb * yb, axis=-1, keepdims=True)) * (1.0 / D_MODEL)
    inv = lax.rsqrt(ms + EPS)
    mod = mod_ref[...]
    g = g_ref[...]
    out_a = x_ref[:, 0:HALF] + mod[5:6, 0:HALF] * (ya * inv * g[:, 0:HALF])
    out_b = x_ref[:, HALF:] + mod[5:6, HALF:] * (yb * inv * g[:, HALF:])

    def store(o_ref):
        o_ref[:, 0:HALF] = out_a
        o_ref[:, HALF:] = out_b

    lo = 0
    for o_ref, nbk in zip(out_refs, stream_blocks):
        pl.when((step >= lo) & (step < lo + nbk))(functools.partial(store, o_ref))
        lo += nbk


def _ffn_out(tables, lpos_rows, h2p, w_rows, x1, mod_l, g_post, swg, swu, swd, y_buf, *, mod_map,
             stream_tokens):
    t_all = x1.shape[0]
    tm = TOKEN_BLOCK
    nsteps = t_all // tm
    row = lambda w: pl.BlockSpec((tm, w), lambda i: (i, 0))
    full = lambda a: pl.BlockSpec(a.shape, lambda i: (0,) * a.ndim)
    smem = lambda: pl.BlockSpec(memory_space=pltpu.SMEM)
    g = g_post.reshape(1, D_MODEL)
    stream_blocks = tuple(t // tm for t in stream_tokens)
    out_shape, out_specs = [], []
    lo = 0
    for t, nbk in zip(stream_tokens, stream_blocks):
        out_shape.append(jax.ShapeDtypeStruct((t, D_MODEL), F32))
        out_specs.append(pl.BlockSpec((tm, D_MODEL),
                                      lambda i, lo=lo, nbk=nbk: (jnp.clip(i - lo, 0, nbk - 1), 0)))
        lo += nbk
    return pl.pallas_call(
        functools.partial(_ffn_out_kernel, stream_blocks=stream_blocks),
        out_shape=out_shape,
        grid=(nsteps,),
        in_specs=[smem(), smem(), smem(),
                  row(TOP_K), row(HALF), row(TOP_K), row(D_MODEL),
                  pl.BlockSpec((None, N_MOD, D_MODEL), mod_map),
                  full(g), full(swg), full(swu), full(swd),
                  pl.BlockSpec(memory_space=pl.ANY)],
        out_specs=out_specs,
        scratch_shapes=[pltpu.VMEM((2, LOCAL_ROWS, HALF), jnp.uint32),
                        pltpu.SemaphoreType.DMA((2,))],
        compiler_params=_cparams("arbitrary"),
        name="ffn_out",
    )(*tables, lpos_rows, h2p, w_rows, x1, mod_l, g, swg, swu, swd, y_buf)


def _rope_tables(L):
    rows = L // GRID_W
    row = jnp.repeat(jnp.arange(rows), GRID_W).astype(F32)
    col = jnp.tile(jnp.arange(GRID_W), rows).astype(F32)
    n = DIFF_DK // 4
    inv = ROPE_BASE ** (-jnp.arange(n, dtype=F32) / n)
    ar, ac = row[:, None] * inv[None, :], col[:, None] * inv[None, :]
    cos64 = jnp.concatenate([jnp.cos(ar), jnp.cos(ar), jnp.cos(ac), jnp.cos(ac)], axis=-1)
    sin64 = jnp.concatenate([-jnp.sin(ar), jnp.sin(ar), -jnp.sin(ac), jnp.sin(ac)], axis=-1)
    return jnp.tile(cos64, (1, 2)), jnp.tile(sin64, (1, 2))


def kernel(x_prompt, x_sample, cache_k, cache_v, c, c_ctx, w_mod, b_mod, g_pre_mix, g_post_mix, g_pre_ffn, g_post_ffn, w_in, w_out, conv_w, conv_b, conv_ln_g, conv_ln_b, hy_short_w, hy_short_b, hy_w1, hy_b1, hy_w2, hy_b2, hy_freq, hy_w3, hy_bias, lam_q1, lam_k1, lam_q2, lam_k2, subln_g, router_w, router_b, exp_w_gate, exp_w_up, exp_w_down, sh_w_gate, sh_w_up, sh_w_down):
    n_p, l_p, _ = x_prompt.shape
    n_s, l_s, _ = x_sample.shape
    t_p, t_s = n_p * l_p, n_s * l_s

    cond = jnp.zeros((COND_ROWS, D_MODEL), F32).at[0].set(c_ctx).at[1:1 + n_s].set(c)
    mod = _modulation(cond, w_mod, b_mod).reshape(DEPTH, COND_ROWS, N_MOD, D_MODEL)
    rope = _rope_tables(l_s)

    t_all = t_p + t_s
    streams = {
        "p": dict(x=x_prompt.reshape(t_p, D_MODEL), T=t_p, L=l_p, row0=0, per_seq=False, hy_nb=8),
        "s": dict(x=x_sample.reshape(t_s, D_MODEL), T=t_s, L=l_s, row0=1, per_seq=True, hy_nb=1),
    }
    for st in streams.values():
        fwd, fwd_t = _dft_matrices(st["L"])
        feats, window = _filter_tables(st["L"])
        st["fmat32"] = jnp.asarray(fwd)
        st["fmat"] = jnp.asarray(fwd).astype(BF16)
        st["fmat_t"] = jnp.asarray(fwd_t).astype(BF16)
        st["filter_tables"] = (jnp.asarray(feats), jnp.asarray(window))
    nb_p, nb_s = t_p // TOKEN_BLOCK, l_s // TOKEN_BLOCK
    merged_mod_map = lambda i: (jnp.where(i < nb_p, 0, 1 + (i - nb_p) // nb_s), 0, 0)
    n_blocks = (t_all // TOKEN_BLOCK) * (LOCAL_ROWS // EXPERT_BLOCK) + N_EXPERTS

    caches = None
    for l in range(DEPTH):
        w_in_bf = w_in[l].astype(BF16)
        w_out_bf = w_out[l].astype(BF16)
        swg, swu, swd = (a[l].astype(BF16) for a in (sh_w_gate, sh_w_up, sh_w_down))
        lam_vecs = jnp.stack([lam_q1[l], lam_k1[l], lam_q2[l], lam_k2[l]], axis=0)
        filt_params = (hy_w1[l], hy_b1[l], hy_w2[l], hy_b2[l], hy_freq[l], hy_w3[l])
        parts = []
        for name, st in streams.items():
            L = st["L"]
            kw = dict(seq_len=L, cond_row0=st["row0"], per_seq_cond=st["per_seq"])
            if name == "p":
                uc, uh, q, k, v, ck, cv = _premix(st["x"], mod[l], g_pre_mix[l], w_in_bf, layer=l,
                                                  n_tokens=st["T"], tok0=0,
                                                  caches=caches, want_cache=True, **kw)
                caches = (ck, cv)
                ctx = None
            else:
                uc, uh, q, k, v = _premix(st["x"], mod[l], g_pre_mix[l], w_in_bf, layer=l,
                                          n_tokens=st["T"], tok0=0, rope_tables=rope, **kw)
                ctx = (cache_k, cache_v)
            ya = _conformer(uc, conv_w[l], conv_b[l], conv_ln_g[l], conv_ln_b[l], seq_len=L)
            kspec = _hyena_spectrum(L, st["fmat32"], *st["filter_tables"], *filt_params)
            yb = _hyena(uh, hy_short_w[l], hy_short_b[l], hy_bias[l], st["fmat"], st["fmat_t"],
                        kspec, seq_len=L, seqs_per_step=st["hy_nb"])
            o = _attention(q, k, v, lam_vecs, subln_g[l], seq_len=L, layer=l, ctx=ctx)
            parts.append((ya, yb, o, st["x"]))
        x1, h2p, logits_t = _postmix(parts, mod[l], g_post_mix[l], g_pre_ffn[l], w_out_bf,
                                     router_w[l].T, mod_map=merged_mod_map)

        eidx_t, w_t, cnt = _route_topk(logits_t, router_b[l])
        counts = cnt[:, 0]
        pstart, nblk, n_used = _block_plan(counts, n_blocks)
        lpos, tables = _route_dest(eidx_t, pstart)
        x_buf = _dispatch(tables, lpos, counts, pstart, n_used, h2p, n_blocks)
        y_buf = _experts(x_buf, nblk, pstart, n_used, exp_w_gate, exp_w_up, exp_w_down, layer=l)
        outs = _ffn_out(tables, lpos.T, h2p, w_t.T, x1, mod[l], g_post_ffn[l], swg, swu, swd,
                        y_buf, mod_map=merged_mod_map, stream_tokens=(t_p, t_s))
        streams["p"]["x"], streams["s"]["x"] = outs

    y_prompt = streams["p"]["x"].reshape(n_p, l_p, D_MODEL)
    y_sample = streams["s"]["x"].reshape(n_s, l_s, D_MODEL)
    return (y_prompt, y_sample, caches[0], caches[1])
```

```python
import functools
import math

import numpy as np
import jax
import jax.numpy as jnp
from jax import lax
from jax.experimental import pallas as pl
from jax.experimental.pallas import tpu as pltpu

F32 = jnp.float32
BF16 = jnp.bfloat16

D_MODEL = 1024
DEPTH = 2
GRID_W = 64
D_CONV = 256
D_HYENA = 256
D_ATTN = 512
DIFF_DK = 64
N_HEADS = 4
DIFF_DV = 128
HEAD_W = 2 * DIFF_DK
IN_COLS = 2 * D_CONV + 3 * D_HYENA + 3 * D_ATTN
COL_HY = 2 * D_CONV
COL_Q = COL_HY + 3 * D_HYENA
COL_K = COL_Q + D_ATTN
COL_V = COL_K + D_ATTN
CONV_WIDTH = 31
CONV_PAD = 16
SUBLANES = 8
HYENA_EMB = 17
FEAT_PAD = 32
HYENA_BANDS = 8
HYENA_MIN_DECAY = math.log(1e-2) / 1.5
HYENA_MAX_DECAY = math.log(1e-2) / 0.3
N_EXPERTS = 64
TOP_K = 8
N_GROUPS = 8
TOPK_GROUPS = 4
GROUP_SCORE_TOP = 2
D_EXPERT = 256
ROUTED_SCALE = 2.5
ROPE_BASE = 10000.0
EPS = 1e-6
N_MOD = 6
COND_ROWS = 16

HALF = D_MODEL // 2
LANES = 128
TOKEN_BLOCK = 256
SPARE_ROWS = N_EXPERTS * SUBLANES
LOCAL_ROWS = TOKEN_BLOCK * TOP_K + SPARE_ROWS
EXPERT_BLOCK = 256
EXPERT_IN_SLOTS = 4
EXPERT_OUT_SLOTS = 3
VMEM_LIMIT = 48 * 1024 * 1024


def _cparams(*sem):
    return pltpu.CompilerParams(dimension_semantics=sem, vmem_limit_bytes=VMEM_LIMIT)


def _rms(x, g):
    return x * lax.rsqrt(jnp.mean(x * x, axis=-1, keepdims=True) + EPS) * g


def _silu(x):
    return x * jax.nn.sigmoid(x)


def _dot(a, b):
    return jnp.dot(a, b, preferred_element_type=F32)


def _mod_kernel(c_ref, w_ref, b_ref, o_ref):
    s = _silu(c_ref[...])
    o_ref[...] = _dot(s.astype(BF16), w_ref[...].astype(BF16)) + b_ref[...]


def _modulation(cond, w_mod, b_mod):
    cols = N_MOD * D_MODEL
    cb = 1536
    return pl.pallas_call(
        _mod_kernel,
        out_shape=jax.ShapeDtypeStruct((DEPTH, COND_ROWS, cols), F32),
        grid=(DEPTH, cols // cb),
        in_specs=[
            pl.BlockSpec((COND_ROWS, D_MODEL), lambda l, j: (0, 0)),
            pl.BlockSpec((None, D_MODEL, cb), lambda l, j: (l, 0, j)),
            pl.BlockSpec((None, 1, cb), lambda l, j: (l, 0, j)),
        ],
        out_specs=pl.BlockSpec((None, COND_ROWS, cb), lambda l, j: (l, 0, j)),
        compiler_params=_cparams("arbitrary", "arbitrary"),
        name="modulation",
    )(cond, w_mod, b_mod.reshape(DEPTH, 1, cols))


def _rope_slab(x, cos, sin, lo_mask):
    swapped = jnp.where(lo_mask, pltpu.roll(x, HEAD_W - 16, axis=1), pltpu.roll(x, 16, axis=1))
    return x * cos + swapped * sin


def _premix_kernel(*refs, rope, cache, layer):
    x_ref, mod_ref, g_ref, w_ref = refs[:4]
    pos = 4
    if rope:
        cos_ref, sin_ref = refs[pos:pos + 2]
        pos += 2
    if cache == "alias":
        pos += 2
    uc_ref, uh_ref, q_ref, k_ref, v_ref = refs[pos:pos + 5]
    pos += 5
    if cache:
        ck_ref, cv_ref = refs[pos:pos + 2]

    x = x_ref[...]
    mod = mod_ref[...]
    h = (_rms(x, g_ref[...]) * (1.0 + mod[1:2]) + mod[0:1]).astype(BF16)
    uc_ref[...] = _dot(h, w_ref[:, 0:COL_HY]).astype(BF16)
    uh_ref[...] = _dot(h, w_ref[:, COL_HY:COL_Q]).astype(BF16)
    uq = _dot(h, w_ref[:, COL_Q:COL_K]) * (DIFF_DK ** -0.5)
    uk = _dot(h, w_ref[:, COL_K:COL_V])
    uv = _dot(h, w_ref[:, COL_V:IN_COLS])
    if rope:
        cos = cos_ref[...]
        sin = sin_ref[...]
        lane = lax.broadcasted_iota(jnp.int32, cos.shape, 1)
        lo_mask = (lane % 32) < 16
        for hh in range(N_HEADS):
            sl = slice(hh * HEAD_W, (hh + 1) * HEAD_W)
            q_ref[:, sl] = _rope_slab(uq[:, sl], cos, sin, lo_mask).astype(BF16)
            k_ref[:, sl] = _rope_slab(uk[:, sl], cos, sin, lo_mask).astype(BF16)
    else:
        q_ref[...] = uq.astype(BF16)
        k_ref[...] = uk.astype(BF16)
    v_ref[...] = uv.astype(BF16)
    if cache == "alias":
        for hh in range(N_HEADS):
            sl = slice(hh * HEAD_W, (hh + 1) * HEAD_W)
            ck_ref[hh] = uk[:, sl]
            cv_ref[hh] = uv[:, sl]
    elif cache == "new":
        for d in range(DEPTH):
            for hh in range(N_HEADS):
                sl = slice(hh * HEAD_W, (hh + 1) * HEAD_W)
                ck_ref[d, hh] = uk[:, sl] if d == layer else jnp.zeros_like(uk[:, sl])
                cv_ref[d, hh] = uv[:, sl] if d == layer else jnp.zeros_like(uv[:, sl])


def _premix(x, mod_l, g, w_in_bf, *, seq_len, cond_row0, per_seq_cond, layer, n_tokens, tok0,
            rope_tables=None, caches=None, want_cache=False):
    T = n_tokens
    tm = TOKEN_BLOCK
    nb = seq_len // tm
    n_seq = T // seq_len
    b0 = tok0 // tm
    if per_seq_cond:
        mod_map = lambda i: (cond_row0 + i // nb, 0, 0)
    else:
        mod_map = lambda i: (cond_row0, 0, 0)
    in_specs = [
        pl.BlockSpec((tm, D_MODEL), lambda i: (i + b0, 0)),
        pl.BlockSpec((None, N_MOD, D_MODEL), mod_map),
        pl.BlockSpec((1, D_MODEL), lambda i: (0, 0)),
        pl.BlockSpec((D_MODEL, IN_COLS), lambda i: (0, 0)),
    ]
    args = [x, mod_l, g.reshape(1, D_MODEL), w_in_bf]
    rope = rope_tables is not None
    if rope:
        in_specs += [pl.BlockSpec((tm, HEAD_W), lambda i: (i % nb, 0))] * 2
        args += list(rope_tables)
    out_shape = [
        jax.ShapeDtypeStruct((T, COL_HY), BF16),
        jax.ShapeDtypeStruct((T, 3 * D_HYENA), BF16),
        jax.ShapeDtypeStruct((T, D_ATTN), BF16),
        jax.ShapeDtypeStruct((T, D_ATTN), BF16),
        jax.ShapeDtypeStruct((T, D_ATTN), BF16),
    ]
    out_specs = [
        pl.BlockSpec((tm, COL_HY), lambda i: (i, 0)),
        pl.BlockSpec((tm, 3 * D_HYENA), lambda i: (i, 0)),
        pl.BlockSpec((tm, D_ATTN), lambda i: (i, 0)),
        pl.BlockSpec((tm, D_ATTN), lambda i: (i, 0)),
        pl.BlockSpec((tm, D_ATTN), lambda i: (i, 0)),
    ]
    cache = None
    aliases = {}
    if want_cache:
        cache = "new"
        cshape = (n_seq, DEPTH, N_HEADS, seq_len, HEAD_W)
        out_shape += [jax.ShapeDtypeStruct(cshape, F32)] * 2
        if caches is None:
            out_specs += [pl.BlockSpec((None, DEPTH, N_HEADS, tm, HEAD_W),
                                       lambda i: (i // nb, 0, 0, i % nb, 0))] * 2
        else:
            cache = "alias"
            out_specs += [pl.BlockSpec((None, None, N_HEADS, tm, HEAD_W),
                                       lambda i: (i // nb, layer, 0, i % nb, 0))] * 2
            aliases = {len(args): 5, len(args) + 1: 6}
            in_specs += [pl.BlockSpec(memory_space=pl.ANY)] * 2
            args += list(caches)
    return pl.pallas_call(
        functools.partial(_premix_kernel, rope=rope, cache=cache, layer=layer),
        out_shape=out_shape,
        grid=(T // tm,),
        in_specs=in_specs,
        out_specs=out_specs,
        input_output_aliases=aliases,
        compiler_params=_cparams("arbitrary"),
        name="premix",
    )(*args)


def _conformer_kernel(u_ref, w_ref, cb_ref, g_ref, b_ref, o_ref, pad_ref, sh_ref, *, seq_len, rows):
    L = seq_len
    u = u_ref[...].astype(F32)
    a = u[:, :D_CONV] * jax.nn.sigmoid(u[:, D_CONV:])
    zeros = jnp.zeros((CONV_PAD, D_CONV), F32)
    pad_ref[0:CONV_PAD, :] = zeros
    pad_ref[CONV_PAD + L:2 * CONV_PAD + L, :] = zeros
    pad_ref[CONV_PAD:CONV_PAD + L, :] = a
    span = L + 2 * CONV_PAD - SUBLANES
    for r in range(SUBLANES):
        sh_ref[r, 0:span, :] = pad_ref[r:r + span, :]
    first = CONV_PAD - (CONV_WIDTH - 1) // 2

    def chunk(c, carry):
        base = pl.multiple_of(c * rows, rows)
        acc = jnp.zeros((rows, D_CONV), F32)
        for t in range(CONV_WIDTH):
            off = first + t
            tap = sh_ref[off % SUBLANES, pl.ds(base + (off // SUBLANES) * SUBLANES, rows), :]
            acc = acc + tap * w_ref[t:t + 1, :]
        acc = acc + cb_ref[...]
        mu = jnp.mean(acc, axis=-1, keepdims=True)
        d = acc - mu
        var = jnp.mean(d * d, axis=-1, keepdims=True)
        y = d * lax.rsqrt(var + EPS) * g_ref[...] + b_ref[...]
        o_ref[pl.ds(base, rows), :] = _silu(y).astype(BF16)
        return carry

    lax.fori_loop(0, L // rows, chunk, 0)


def _conformer(uc, conv_w, conv_b, ln_g, ln_b, *, seq_len):
    T = uc.shape[0]
    n_seq = T // seq_len
    vec = lambda a: a.reshape(1, D_CONV)
    return pl.pallas_call(
        functools.partial(_conformer_kernel, seq_len=seq_len, rows=64),
        out_shape=jax.ShapeDtypeStruct((T, D_CONV), BF16),
        grid=(n_seq,),
        in_specs=[
            pl.BlockSpec((seq_len, 2 * D_CONV), lambda i: (i, 0)),
            pl.BlockSpec((CONV_WIDTH, D_CONV), lambda i: (0, 0)),
            pl.BlockSpec((1, D_CONV), lambda i: (0, 0)),
            pl.BlockSpec((1, D_CONV), lambda i: (0, 0)),
            pl.BlockSpec((1, D_CONV), lambda i: (0, 0)),
        ],
        out_specs=pl.BlockSpec((seq_len, D_CONV), lambda i: (i, 0)),
        scratch_shapes=[pltpu.VMEM((seq_len + 2 * CONV_PAD, D_CONV), F32),
                        pltpu.VMEM((SUBLANES, seq_len + 2 * CONV_PAD, D_CONV), F32)],
        compiler_params=_cparams("arbitrary"),
        name="conformer",
    )(uc, conv_w, vec(conv_b), vec(ln_g), vec(ln_b))


def _dft_matrices(L):
    k = np.arange(L, dtype=np.int64)[:, None]
    m = np.arange(L, dtype=np.int64)[None, :]
    ang = 2.0 * np.pi * (((2 * k + 1) * m) % (4 * L)).astype(np.float64) / (4 * L)
    fwd = np.concatenate([np.cos(ang), np.sin(ang)], axis=0).astype(np.float32)
    return fwd, np.ascontiguousarray(fwd.T)


def _filter_tables(L):
    t = np.linspace(0.0, 1.0, L)
    bands = np.linspace(1e-4, HYENA_BANDS - 1, HYENA_BANDS)
    ang = (2.0 * np.pi / L) * np.arange(L)[:, None] * bands[None, :]
    feats = np.zeros((L, FEAT_PAD), np.float32)
    feats[:, :HYENA_EMB] = np.concatenate([t[:, None], np.cos(ang), -np.sin(ang)], axis=-1)
    deltas = np.abs(np.linspace(HYENA_MIN_DECAY, HYENA_MAX_DECAY, D_HYENA))
    window = np.exp(-t[:, None] * deltas[None, :]).astype(np.float32)
    return feats, window


def _dot3(a, b):
    a_hi, a_lo = _split_bf16(a)
    b_hi, b_lo = _split_bf16(b)
    return _dot(a_hi, b_hi) + (_dot(a_hi, b_lo) + _dot(a_lo, b_hi))


def _spectrum_kernel(feat_ref, win_ref, w1_ref, b1_ref, w2_ref, b2_ref, fr_ref, w3_ref, f_ref,
                     o_ref, *, seq_len):
    L = seq_len
    C = D_HYENA
    fr = fr_ref[...]
    hid = jnp.sin(fr * (_dot3(feat_ref[...], w1_ref[...]) + b1_ref[...]))
    hid = jnp.sin(fr * (_dot3(hid, w2_ref[...]) + b2_ref[...]))
    filt = _dot3(hid, w3_ref[...])
    win = win_ref[...]
    first = lax.broadcasted_iota(jnp.int32, (L, C), 0) == 0
    for o in range(2):
        fwd = filt[:, o * C:(o + 1) * C] * win
        bwd = jnp.where(first, 0.0, filt[:, (2 + o) * C:(3 + o) * C] * win)
        o_ref[o, 0:L, :] = _dot3(f_ref[0:L, :], fwd + bwd) * (1.0 / L)
        o_ref[o, L:2 * L, :] = _dot3(f_ref[L:2 * L, :], bwd - fwd) * (1.0 / L)


def _hyena_spectrum(L, fmat, feats, window, hy_w1, hy_b1, hy_w2, hy_b2, hy_freq, hy_w3):
    w1 = jnp.zeros((FEAT_PAD, hy_w1.shape[1]), F32).at[:HYENA_EMB].set(hy_w1)
    vec = lambda a: a.reshape(1, -1)
    return pl.pallas_call(
        functools.partial(_spectrum_kernel, seq_len=L),
        out_shape=jax.ShapeDtypeStruct((2, 2 * L, D_HYENA), F32),
        compiler_params=pltpu.CompilerParams(vmem_limit_bytes=VMEM_LIMIT),
        name="hyena_spectrum",
    )(feats, window, w1, vec(hy_b1), hy_w2, vec(hy_b2), vec(hy_freq), hy_w3, fmat)


def _hyena_kernel(u_ref, sw_ref, sb_ref, hb_ref, f_ref, ft_ref, ks_ref, o_ref, pad_ref,
                  *, seq_len, n_seq):
    L = seq_len
    C = D_HYENA
    zrow = jnp.zeros((8, 3 * C), F32)
    for b in range(n_seq):
        u = u_ref[b].astype(F32)
        pad_ref[0:8, :] = zrow
        pad_ref[8 + L:16 + L, :] = zrow
        pad_ref[8:8 + L, :] = u
        s = (pad_ref[7:7 + L, :] * sw_ref[0:1, :] + u * sw_ref[1:2, :]
             + pad_ref[9:9 + L, :] * sw_ref[2:3, :] + sb_ref[...])
        z = s[:, 0:C]
        for o in range(2):
            gate = s[:, (o + 1) * C:(o + 2) * C]
            spec = _dot(f_ref[...], z.astype(BF16))
            zr, za = spec[:L], spec[L:]
            kre, kim = ks_ref[o, 0:L, :], ks_ref[o, L:2 * L, :]
            y = jnp.concatenate([zr * kre + za * kim, za * kre - zr * kim], axis=0)
            conv = _dot(ft_ref[...], y.astype(BF16))
            z = gate * (conv + hb_ref[o:o + 1, :] * z)
        o_ref[b] = z.astype(BF16)


def _hyena(uh, short_w, short_b, hy_bias, fmat_bf, fmat_t_bf, kspec, *, seq_len, seqs_per_step):
    T = uh.shape[0]
    n_seq = T // seq_len
    L = seq_len
    nb = seqs_per_step
    u3 = uh.reshape(n_seq, L, 3 * D_HYENA)
    out = pl.pallas_call(
        functools.partial(_hyena_kernel, seq_len=L, n_seq=nb),
        out_shape=jax.ShapeDtypeStruct((n_seq, L, D_HYENA), BF16),
        grid=(n_seq // nb,),
        in_specs=[
            pl.BlockSpec((nb, L, 3 * D_HYENA), lambda i: (i, 0, 0)),
            pl.BlockSpec((3, 3 * D_HYENA), lambda i: (0, 0)),
            pl.BlockSpec((1, 3 * D_HYENA), lambda i: (0, 0)),
            pl.BlockSpec((2, D_HYENA), lambda i: (0, 0)),
            pl.BlockSpec((2 * L, L), lambda i: (0, 0)),
            pl.BlockSpec((L, 2 * L), lambda i: (0, 0)),
            pl.BlockSpec((2, 2 * L, D_HYENA), lambda i: (0, 0, 0)),
        ],
        out_specs=pl.BlockSpec((nb, L, D_HYENA), lambda i: (i, 0, 0)),
        scratch_shapes=[pltpu.VMEM((L + 16, 3 * D_HYENA), F32)],
        compiler_params=_cparams("arbitrary"),
        name="hyena",
    )(u3, short_w, short_b.reshape(1, 3 * D_HYENA), hy_bias, fmat_bf, fmat_t_bf, kspec)
    return out.reshape(T, D_HYENA)


def _attn_kernel(*refs, lam_init, has_ctx):
    q_ref, k_ref, v_ref = refs[:3]
    pos = 3
    if has_ctx:
        ck_ref, cv_ref = refs[pos:pos + 2]
        pos += 2
    lam_ref, g_ref, o_ref = refs[pos:pos + 3]

    lv = lam_ref[...]
    lam = (jnp.exp(jnp.sum(lv[0:1] * lv[1:2], axis=-1, keepdims=True))
           - jnp.exp(jnp.sum(lv[2:3] * lv[3:4], axis=-1, keepdims=True)) + lam_init)

    tq = q_ref.shape[0]
    lane = lax.broadcasted_iota(jnp.int32, (tq, HEAD_W), 1)
    dn = (((1,), (1,)), ((), ()))
    for h in range(N_HEADS):
        sl = slice(h * HEAD_W, (h + 1) * HEAD_W)
        q = q_ref[:, sl]
        k = k_ref[:, sl]
        v = v_ref[:, sl]
        if has_ctx:
            ck = ck_ref[h].astype(BF16)
            cv = cv_ref[h].astype(BF16)
        zero = jnp.zeros_like(q)
        q2 = jnp.concatenate([jnp.where(lane < DIFF_DK, q, zero),
                              jnp.where(lane >= DIFF_DK, q, zero)], axis=0)
        s = lax.dot_general(q2, k, dn, preferred_element_type=F32)
        m = jnp.max(s, axis=-1, keepdims=True)
        if has_ctx:
            sc = lax.dot_general(q2, ck, dn, preferred_element_type=F32)
            m = jnp.maximum(m, jnp.max(sc, axis=-1, keepdims=True))
        p = jnp.exp(s - m)
        den = jnp.sum(p, axis=-1, keepdims=True)
        acc = _dot(p.astype(BF16), v)
        if has_ctx:
            pc = jnp.exp(sc - m)
            den = den + jnp.sum(pc, axis=-1, keepdims=True)
            acc = acc + _dot(pc.astype(BF16), cv)
        both = acc / den
        o = both[:tq] - lam * both[tq:]
        o_ref[:, sl] = (_rms(o, g_ref[...]) * (1.0 - lam_init)).astype(BF16)


def _attention(q, k, v, lam_vecs, subln_g, *, seq_len, layer, ctx=None):
    T = q.shape[0]
    L = seq_len
    n_seq = T // L
    tq = TOKEN_BLOCK
    nq = L // tq
    lam_init = 0.8 - 0.6 * math.exp(-0.3 * layer)
    q3, k3, v3 = (a.reshape(n_seq, L, D_ATTN) for a in (q, k, v))
    in_specs = [
        pl.BlockSpec((None, tq, D_ATTN), lambda b, i: (b, i, 0)),
        pl.BlockSpec((None, L, D_ATTN), lambda b, i: (b, 0, 0)),
        pl.BlockSpec((None, L, D_ATTN), lambda b, i: (b, 0, 0)),
    ]
    args = [q3, k3, v3]
    if ctx is not None:
        past = ctx[0].shape[3]
        in_specs += [pl.BlockSpec((None, None, N_HEADS, past, HEAD_W),
                                  lambda b, i: (b, layer, 0, 0, 0))] * 2
        args += list(ctx)
    in_specs += [
        pl.BlockSpec((4, DIFF_DK), lambda b, i: (0, 0)),
        pl.BlockSpec((1, DIFF_DV), lambda b, i: (0, 0)),
    ]
    args += [lam_vecs, subln_g.reshape(1, DIFF_DV)]
    out = pl.pallas_call(
        functools.partial(_attn_kernel, lam_init=lam_init, has_ctx=ctx is not None),
        out_shape=jax.ShapeDtypeStruct((n_seq, L, D_ATTN), BF16),
        grid=(n_seq, nq),
        in_specs=in_specs,
        out_specs=pl.BlockSpec((None, tq, D_ATTN), lambda b, i: (b, i, 0)),
        compiler_params=_cparams("arbitrary", "arbitrary"),
        name="diff_attention",
    )(*args)
    return out.reshape(T, D_ATTN)


def _split_bf16(a):
    hi = a.astype(BF16)
    lo = (a - hi.astype(F32)).astype(BF16)
    return hi, lo


def _pack_pair(a, b):
    ua = lax.bitcast_convert_type(a.astype(BF16).astype(F32), jnp.uint32)
    ub = lax.bitcast_convert_type(b.astype(BF16).astype(F32), jnp.uint32)
    return (ua >> 16) | ub


def _unpack_pair(w):
    a = lax.bitcast_convert_type(w << 16, F32)
    b = lax.bitcast_convert_type(w & jnp.uint32(0xFFFF0000), F32)
    return a, b


def _postmix_kernel(*refs, stream_blocks):
    ns = len(stream_blocks)
    mod_ref, gpost_ref, gpre_ref, w_ref, rw_ref, x1_ref, h2_ref, lg_ref = refs[4 * ns:]
    i = pl.program_id(0)

    def body(ya_ref, yb_ref, o_ref, x_ref):
        y = (_dot(ya_ref[...], w_ref[0:D_CONV, :])
             + _dot(yb_ref[...], w_ref[D_CONV:D_CONV + D_HYENA, :])
             + _dot(o_ref[...], w_ref[D_CONV + D_HYENA:, :]))
        mod = mod_ref[...]
        x1 = x_ref[...] + mod[2:3] * _rms(y, gpost_ref[...])
        x1_ref[...] = x1
        h2 = _rms(x1, gpre_ref[...]) * (1.0 + mod[4:5]) + mod[3:4]
        h2_ref[...] = _pack_pair(h2[:, :HALF], h2[:, HALF:])
        h_hi, h_lo = _split_bf16(h2)
        r_hi, r_lo = _split_bf16(rw_ref[...])
        dn = (((1,), (1,)), ((), ()))
        mm = lambda a, b: lax.dot_general(a, b, dn, preferred_element_type=F32)
        lg_ref[...] = mm(r_hi, h_hi) + (mm(r_lo, h_hi) + mm(r_hi, h_lo))

    lo = 0
    for j, nbk in enumerate(stream_blocks):
        pl.when((i >= lo) & (i < lo + nbk))(functools.partial(body, *refs[4 * j:4 * j + 4]))
        lo += nbk


def _postmix(parts, mod_l, g_post, g_pre, w_out_bf, router_wt, *, mod_map):
    t_all = sum(p[0].shape[0] for p in parts)
    tm = TOKEN_BLOCK
    full = lambda a: pl.BlockSpec(a.shape, lambda i: (0,) * a.ndim)
    gp, ge = g_post.reshape(1, D_MODEL), g_pre.reshape(1, D_MODEL)
    stream_blocks = tuple(p[0].shape[0] // tm for p in parts)
    in_specs, args = [], []
    lo = 0
    for part, nbk in zip(parts, stream_blocks):
        local = lambda i, lo=lo, nbk=nbk: (jnp.clip(i - lo, 0, nbk - 1), 0)
        in_specs += [pl.BlockSpec((tm, D_CONV), local), pl.BlockSpec((tm, D_HYENA), local),
                     pl.BlockSpec((tm, D_ATTN), local), pl.BlockSpec((tm, D_MODEL), local)]
        args += list(part)
        lo += nbk
    in_specs += [pl.BlockSpec((None, N_MOD, D_MODEL), mod_map),
                 full(gp), full(ge), full(w_out_bf), full(router_wt)]
    args += [mod_l, gp, ge, w_out_bf, router_wt]
    return pl.pallas_call(
        functools.partial(_postmix_kernel, stream_blocks=stream_blocks),
        out_shape=[
            jax.ShapeDtypeStruct((t_all, D_MODEL), F32),
            jax.ShapeDtypeStruct((t_all, HALF), jnp.uint32),
            jax.ShapeDtypeStruct((N_EXPERTS, t_all), F32),
        ],
        grid=(t_all // tm,),
        in_specs=in_specs,
        out_specs=[pl.BlockSpec((tm, D_MODEL), lambda i: (i, 0)),
                   pl.BlockSpec((tm, HALF), lambda i: (i, 0)),
                   pl.BlockSpec((N_EXPERTS, tm), lambda i: (0, i))],
        compiler_params=_cparams("arbitrary"),
        name="postmix",
    )(*args)


def _first_max(vals, idx, sentinel):
    m = jnp.max(vals, axis=0, keepdims=True)
    i = jnp.min(jnp.where(vals == m, idx, sentinel), axis=0, keepdims=True)
    return m, i


def _route_topk_kernel(lg_ref, rb_ref, eidx_ref, w_ref, cnt_ref):
    step = pl.program_id(0)
    per = N_EXPERTS // N_GROUPS
    n = lg_ref.shape[1]
    s = jax.nn.sigmoid(lg_ref[...])
    sel = s + rb_ref[...]
    sub = lax.broadcasted_iota(jnp.int32, (per, n), 0)
    neg = jnp.float32(-jnp.inf)

    gs = []
    for g in range(N_GROUPS):
        xg = sel[g * per:(g + 1) * per]
        total = jnp.zeros((1, n), F32)
        for _ in range(GROUP_SCORE_TOP):
            m, i = _first_max(xg, sub, per)
            total = total + m
            xg = jnp.where(sub == i, neg, xg)
        gs.append(total)
    gscore = jnp.concatenate(gs, axis=0)
    gkeep = jnp.zeros((N_GROUPS, n), F32)
    for _ in range(TOPK_GROUPS):
        _, gi = _first_max(gscore, sub, N_GROUPS)
        hit = sub == gi
        gkeep = jnp.where(hit, 1.0, gkeep)
        gscore = jnp.where(hit, neg, gscore)

    eio = [sub + g * per for g in range(N_GROUPS)]
    sg = [s[g * per:(g + 1) * per] for g in range(N_GROUPS)]
    mk = [jnp.where(gkeep[g:g + 1] > 0.5, sel[g * per:(g + 1) * per], neg) for g in range(N_GROUPS)]
    oh = [jnp.zeros((per, n), F32) for _ in range(N_GROUPS)]
    idx_rows, s_rows = [], []
    for _ in range(TOP_K):
        best = functools.reduce(jnp.maximum, mk)
        m = jnp.max(best, axis=0, keepdims=True)
        cand = functools.reduce(jnp.minimum,
                                [jnp.where(mk[g] == m, eio[g], N_EXPERTS) for g in range(N_GROUPS)])
        idx = jnp.min(cand, axis=0, keepdims=True)
        acc = jnp.zeros((per, n), F32)
        for g in range(N_GROUPS):
            hit = eio[g] == idx
            acc = acc + jnp.where(hit, sg[g], 0.0)
            mk[g] = jnp.where(hit, neg, mk[g])
            oh[g] = jnp.where(hit, 1.0, oh[g])
        idx_rows.append(idx)
        s_rows.append(jnp.sum(acc, axis=0, keepdims=True))
    eidx_ref[...] = jnp.concatenate(idx_rows, axis=0)
    sk = jnp.concatenate(s_rows, axis=0)
    w_ref[...] = sk / jnp.sum(sk, axis=0, keepdims=True) * ROUTED_SCALE

    @pl.when(step == 0)
    def _():
        cnt_ref[...] = jnp.zeros_like(cnt_ref)

    tot = jnp.concatenate([jnp.sum(o, axis=1, keepdims=True) for o in oh], axis=0)
    tot8 = _ceil_tile(tot)
    cnt_ref[...] = cnt_ref[...] + jnp.broadcast_to(tot8, cnt_ref.shape).astype(jnp.int32)


def _ceil_tile(count):
    return jnp.floor((count + (SUBLANES - 1.0)) * (1.0 / SUBLANES)) * SUBLANES


def _route_topk(logits_t, router_b):
    t_all = logits_t.shape[1]
    rb = TOKEN_BLOCK
    return pl.pallas_call(
        _route_topk_kernel,
        out_shape=[
            jax.ShapeDtypeStruct((TOP_K, t_all), jnp.int32),
            jax.ShapeDtypeStruct((TOP_K, t_all), F32),
            jax.ShapeDtypeStruct((N_EXPERTS, LANES), jnp.int32),
        ],
        grid=(t_all // rb,),
        in_specs=[pl.BlockSpec((N_EXPERTS, rb), lambda i: (0, i)),
                  pl.BlockSpec((N_EXPERTS, 1), lambda i: (0, 0))],
        out_specs=[pl.BlockSpec((TOP_K, rb), lambda i: (0, i)),
                   pl.BlockSpec((TOP_K, rb), lambda i: (0, i)),
                   pl.BlockSpec((N_EXPERTS, LANES), lambda i: (0, 0))],
        compiler_params=_cparams("arbitrary"),
        name="route_topk",
    )(logits_t, router_b.reshape(N_EXPERTS, 1))


def _route_dest_kernel(eidx_ref, ps_ref, lpos_ref, cs_ref, cn_ref, cd_ref, carry_ref):
    step = pl.program_id(0)
    per = N_EXPERTS // N_GROUPS
    n = eidx_ref.shape[1]

    @pl.when(step == 0)
    def _():
        carry_ref[...] = jnp.zeros_like(carry_ref)
        cs_ref[...] = jnp.zeros_like(cs_ref)
        cn_ref[...] = jnp.zeros_like(cn_ref)
        cd_ref[...] = jnp.zeros_like(cd_ref)

    eidx = eidx_ref[...]
    sub = lax.broadcasted_iota(jnp.int32, (per, n), 0)
    rows = []
    for g in range(N_GROUPS):
        eio = sub + g * per
        o = jnp.zeros((per, n), F32)
        for k in range(TOP_K):
            o = jnp.where(eio == eidx[k:k + 1], 1.0, o)
        rows.append(o)
    onehot = jnp.concatenate(rows, axis=0)
    r = lax.broadcasted_iota(jnp.int32, (n, n), 0)
    c = lax.broadcasted_iota(jnp.int32, (n, n), 1)
    before = jnp.where(r < c, 1.0, 0.0).astype(BF16)
    rank = _dot(onehot.astype(BF16), before)
    size = _ceil_tile(jnp.sum(onehot, axis=1, keepdims=True))
    er = lax.broadcasted_iota(jnp.int32, (N_EXPERTS, N_EXPERTS), 0)
    ec = lax.broadcasted_iota(jnp.int32, (N_EXPERTS, N_EXPERTS), 1)
    lower = jnp.where(ec < er, 1.0, 0.0).astype(BF16)
    tiles = jnp.broadcast_to(size * (1.0 / SUBLANES), (N_EXPERTS, LANES)).astype(BF16)
    start = _dot(lower, tiles)[:, 0:1] * SUBLANES
    base = (rank + start).astype(jnp.int32)
    out = []
    for k in range(TOP_K):
        acc = jnp.zeros((per, n), jnp.int32)
        for g in range(N_GROUPS):
            acc = acc + jnp.where(sub + g * per == eidx[k:k + 1], base[g * per:(g + 1) * per], 0)
        out.append(jnp.sum(acc, axis=0, keepdims=True))
    lpos_ref[...] = jnp.concatenate(out, axis=0)

    col = lax.broadcasted_iota(jnp.int32, cs_ref.shape, 1) == step
    carry = carry_ref[:, 0:1]
    cs_ref[...] = jnp.where(col, start.astype(jnp.int32), cs_ref[...])
    cn_ref[...] = jnp.where(col, size.astype(jnp.int32), cn_ref[...])
    cd_ref[...] = jnp.where(col, carry.astype(jnp.int32) + ps_ref[...], cd_ref[...])
    carry_ref[...] = carry_ref[...] + jnp.broadcast_to(size, carry_ref.shape)


def _route_dest(eidx_t, pstart):
    t_all = eidx_t.shape[1]
    rb = TOKEN_BLOCK
    n_win = t_all // rb
    cols = -(-n_win // LANES) * LANES
    table = jax.ShapeDtypeStruct((N_EXPERTS, cols), jnp.int32)
    table_spec = pl.BlockSpec((N_EXPERTS, cols), lambda i: (0, 0))
    lpos, cs, cn, cd = pl.pallas_call(
        _route_dest_kernel,
        out_shape=[jax.ShapeDtypeStruct((TOP_K, t_all), jnp.int32), table, table, table],
        grid=(n_win,),
        in_specs=[pl.BlockSpec((TOP_K, rb), lambda i: (0, i)),
                  pl.BlockSpec((N_EXPERTS, 1), lambda i: (0, 0))],
        out_specs=[pl.BlockSpec((TOP_K, rb), lambda i: (0, i)), table_spec, table_spec, table_spec],
        scratch_shapes=[pltpu.VMEM((N_EXPERTS, LANES), F32)],
        compiler_params=_cparams("arbitrary"),
        name="route_dest",
    )(eidx_t, pstart.reshape(N_EXPERTS, 1))
    flat = lambda a: a[:, :n_win].T.reshape(-1)
    return lpos, (flat(cs), flat(cn), flat(cd))


def _wait_rows(hbm, sem, n_rows):
    pltpu.make_async_copy(hbm.at[pl.ds(0, n_rows)], hbm.at[pl.ds(0, n_rows)], sem).wait()


def _window_chunks(tables, w, local_ref, hbm, sem, *, to_hbm, spare_row):
    cs_ref, cn_ref, cd_ref = tables
    tile = lambda v: pl.multiple_of(v, SUBLANES)

    def copy(local_row, hbm_row, size):
        lo = local_ref.at[pl.ds(tile(local_row), tile(size))]
        hi = hbm.at[pl.ds(tile(hbm_row), tile(size))]
        src, dst = (lo, hi) if to_hbm else (hi, lo)
        pltpu.make_async_copy(src, dst, sem).start()

    def per_expert(e, carry):
        i = w * N_EXPERTS + e
        copy(cs_ref[i], cd_ref[i], cn_ref[i])
        return carry

    lax.fori_loop(0, N_EXPERTS, per_expert, 0, unroll=8)
    last = w * N_EXPERTS + (N_EXPERTS - 1)
    used = cs_ref[last] + cn_ref[last]
    copy(used, spare_row, LOCAL_ROWS - used)


def _zero_padding(cnt_ref, ps_ref, nu_ref, zero_ref, x_hbm, zsem, n_blocks, *, wait):
    def copy(row, size):
        cp = pltpu.make_async_copy(zero_ref.at[pl.ds(0, size)], x_hbm.at[pl.ds(row, size)], zsem)
        if wait:
            cp.wait()
        else:
            cp.start()

    def per_expert(e, carry):
        cnt = cnt_ref[e]
        pad = (EXPERT_BLOCK - cnt % EXPERT_BLOCK) % EXPERT_BLOCK
        copy(pl.multiple_of(ps_ref[e] + cnt, SUBLANES), pl.multiple_of(pad, SUBLANES))
        return carry

    lax.fori_loop(0, N_EXPERTS, per_expert, 0)

    def per_block(j, carry):
        copy(pl.multiple_of(j * EXPERT_BLOCK, EXPERT_BLOCK), EXPERT_BLOCK)
        return carry

    lax.fori_loop(nu_ref[0], n_blocks, per_block, 0)


def _dispatch_kernel(cs_ref, cn_ref, cd_ref, cnt_ref, ps_ref, nu_ref, lpos_ref, h_ref, x_hbm,
                     gbuf, zero_ref, sem, zsem, ssem):
    w = pl.program_id(0)
    nsteps = pl.num_programs(0)
    slot = w % 2
    n_blocks = (x_hbm.shape[0] - 2 * SPARE_ROWS) // EXPERT_BLOCK
    spare0 = n_blocks * EXPERT_BLOCK
    zero_args = (cnt_ref, ps_ref, nu_ref, zero_ref, x_hbm, zsem, n_blocks)

    def spare_zero(j):
        return pltpu.make_async_copy(
            zero_ref, x_hbm.at[pl.ds(spare0 + j * EXPERT_BLOCK, EXPERT_BLOCK)], ssem)

    @pl.when(w == 0)
    def _():
        zero_ref[...] = jnp.zeros_like(zero_ref)
        _zero_padding(*zero_args, wait=False)
        for j in range(2 * SPARE_ROWS // EXPERT_BLOCK):
            spare_zero(j).start()

    ha, hb = _unpack_pair(h_ref[...])
    ha, hb = ha.astype(BF16), hb.astype(BF16)
    lp = lpos_ref[...]
    n = h_ref.shape[0]
    last = w * N_EXPERTS + (N_EXPERTS - 1)
    used = cs_ref[last] + cn_ref[last]

    lp = lp.astype(jnp.int16)
    one = jnp.ones((n, n), BF16)

    def permute(c):
        row = (lax.broadcasted_iota(jnp.int32, (n, n), 0) + c * n).astype(jnp.int16)
        pb = jnp.zeros((n, n), BF16)
        for k in range(TOP_K):
            pb = jnp.where(row == lp[k:k + 1, :], one, pb)
        ga = lax.bitcast_convert_type(_dot(pb, ha), jnp.uint32)
        gb = lax.bitcast_convert_type(_dot(pb, hb), jnp.uint32)
        gbuf[slot, c * n:(c + 1) * n, :] = (ga >> 16) | gb

    def blank(c):
        gbuf[slot, c * n:(c + 1) * n, :] = jnp.zeros((n, HALF), jnp.uint32)

    for c in range(LOCAL_ROWS // n):
        if c < TOP_K:
            permute(c)
        else:
            pl.when(c * n < used)(functools.partial(permute, c))
            pl.when(c * n >= used)(functools.partial(blank, c))

    @pl.when(w == 0)
    def _():
        for j in range(2 * SPARE_ROWS // EXPERT_BLOCK):
            spare_zero(j).wait()

    _window_chunks((cs_ref, cn_ref, cd_ref), w, gbuf.at[slot], x_hbm, sem.at[slot],
                   to_hbm=True, spare_row=spare0 + slot * SPARE_ROWS)

    @pl.when(w > 0)
    def _():
        _wait_rows(x_hbm, sem.at[1 - slot], LOCAL_ROWS)

    @pl.when(w == nsteps - 1)
    def _():
        _wait_rows(x_hbm, sem.at[slot], LOCAL_ROWS)
        _zero_padding(*zero_args, wait=True)


def _dispatch(tables, lpos, counts, pstart, n_used, h2p, n_blocks):
    t_all = h2p.shape[0]
    sb = TOKEN_BLOCK
    smem = lambda: pl.BlockSpec(memory_space=pltpu.SMEM)
    return pl.pallas_call(
        _dispatch_kernel,
        out_shape=jax.ShapeDtypeStruct((n_blocks * EXPERT_BLOCK + 2 * SPARE_ROWS, HALF), jnp.uint32),
        grid=(t_all // sb,),
        in_specs=[smem(), smem(), smem(), smem(), smem(), smem(),
                  pl.BlockSpec((TOP_K, sb), lambda i: (0, i)),
                  pl.BlockSpec((sb, HALF), lambda i: (i, 0))],
        out_specs=pl.BlockSpec(memory_space=pl.ANY),
        scratch_shapes=[pltpu.VMEM((2, LOCAL_ROWS, HALF), jnp.uint32),
                        pltpu.VMEM((EXPERT_BLOCK, HALF), jnp.uint32),
                        pltpu.SemaphoreType.DMA((2,)), pltpu.SemaphoreType.DMA,
                        pltpu.SemaphoreType.DMA],
        compiler_params=_cparams("arbitrary"),
        name="dispatch",
    )(*tables, counts, pstart, n_used, lpos, h2p)


def _expert_kernel(nb_ref, ps_ref, nu_ref, wg_ref, wu_ref, wd_ref, x_hbm, y_hbm,
                   wg_bf, wu_bf, wd_bf, xbuf, ybuf, xsem, ysem):
    e = pl.program_id(0)
    nblk = nb_ref[e]
    first = ps_ref[e] // EXPERT_BLOCK
    n_used = nu_ref[0]
    xs, ys = xbuf.shape[0], ybuf.shape[0]
    wg_bf[...] = wg_ref[...].astype(BF16)
    wu_bf[...] = wu_ref[...].astype(BF16)
    wd_bf[...] = wd_ref[...].astype(BF16)

    def rows(b):
        return pl.ds(pl.multiple_of(b * EXPERT_BLOCK, EXPERT_BLOCK), EXPERT_BLOCK)

    def x_copy(b):
        return pltpu.make_async_copy(x_hbm.at[rows(b)], xbuf.at[b % xs], xsem.at[b % xs])

    def y_copy(b):
        return pltpu.make_async_copy(ybuf.at[b % ys], y_hbm.at[rows(b)], ysem.at[b % ys])

    @pl.when(e == 0)
    def _():
        for b in range(xs - 1):
            pl.when(b < n_used)(lambda b=b: x_copy(b).start())

    def block(j, carry):
        b = first + j
        x_copy(b).wait()
        pl.when(b + xs - 1 < n_used)(lambda: x_copy(b + xs - 1).start())
        pl.when(b >= ys)(lambda: y_copy(b - ys).wait())
        y = _swiglu_packed(xbuf[b % xs], wg_bf, wu_bf, wd_bf)
        ybuf[b % ys] = _pack_pair(y[:, :HALF], y[:, HALF:])
        y_copy(b).start()
        return carry

    lax.fori_loop(0, nblk, block, 0)

    @pl.when(e == pl.num_programs(0) - 1)
    def _():
        for d in range(ys):
            b = n_used - ys + d
            pl.when(b >= 0)(lambda b=b: y_copy(b).wait())
        n_blocks = y_hbm.shape[0] // EXPERT_BLOCK
        ybuf[0] = jnp.zeros_like(ybuf[0])

        def tail(j, carry, wait):
            cp = pltpu.make_async_copy(
                ybuf.at[0], y_hbm.at[pl.ds(pl.multiple_of(j * EXPERT_BLOCK, EXPERT_BLOCK),
                                           EXPERT_BLOCK)], ysem.at[0])
            cp.wait() if wait else cp.start()
            return carry

        lax.fori_loop(nu_ref[0], n_blocks, functools.partial(tail, wait=False), 0)
        lax.fori_loop(nu_ref[0], n_blocks, functools.partial(tail, wait=True), 0)


def _swiglu_packed(xp, wg_ref, wu_ref, wd_ref):
    xa, xb = _unpack_pair(xp)
    xa, xb = xa.astype(BF16), xb.astype(BF16)
    gate = _dot(xa, wg_ref[0:HALF, :]) + _dot(xb, wg_ref[HALF:, :])
    up = _dot(xa, wu_ref[0:HALF, :]) + _dot(xb, wu_ref[HALF:, :])
    return _dot((_silu(gate) * up).astype(BF16), wd_ref[...])


def _experts(x_buf, nblk, pstart, n_used, wg, wu, wd, *, layer):
    grid_spec = pltpu.PrefetchScalarGridSpec(
        num_scalar_prefetch=3,
        grid=(N_EXPERTS,),
        in_specs=[
            pl.BlockSpec((None, None, D_MODEL, D_EXPERT), lambda e, *_: (layer, e, 0, 0)),
            pl.BlockSpec((None, None, D_MODEL, D_EXPERT), lambda e, *_: (layer, e, 0, 0)),
            pl.BlockSpec((None, None, D_EXPERT, D_MODEL), lambda e, *_: (layer, e, 0, 0)),
            pl.BlockSpec(memory_space=pl.ANY),
        ],
        out_specs=pl.BlockSpec(memory_space=pl.ANY),
        scratch_shapes=[
            pltpu.VMEM((D_MODEL, D_EXPERT), BF16),
            pltpu.VMEM((D_MODEL, D_EXPERT), BF16),
            pltpu.VMEM((D_EXPERT, D_MODEL), BF16),
            pltpu.VMEM((EXPERT_IN_SLOTS, EXPERT_BLOCK, HALF), jnp.uint32),
            pltpu.VMEM((EXPERT_OUT_SLOTS, EXPERT_BLOCK, HALF), jnp.uint32),
            pltpu.SemaphoreType.DMA((EXPERT_IN_SLOTS,)),
            pltpu.SemaphoreType.DMA((EXPERT_OUT_SLOTS,)),
        ],
    )
    return pl.pallas_call(
        _expert_kernel,
        out_shape=jax.ShapeDtypeStruct(x_buf.shape, jnp.uint32),
        grid_spec=grid_spec,
        compiler_params=_cparams("arbitrary"),
        name="experts",
    )(nblk, pstart, n_used, wg, wu, wd, x_buf)


def _block_plan(counts, n_blocks):
    nblk = (counts + EXPERT_BLOCK - 1) // EXPERT_BLOCK
    pend = jnp.cumsum(nblk)
    pstart = ((pend - nblk) * EXPERT_BLOCK).astype(jnp.int32)
    return pstart, nblk.astype(jnp.int32), pend[-1:].astype(jnp.int32)


def _ffn_out_kernel(*refs, stream_blocks):
    (cs_ref, cn_ref, cd_ref, lpos_ref, h_ref, w_ref, x_ref, mod_ref, g_ref, wg_ref, wu_ref, wd_ref,
     y_hbm) = refs[:13]
    ns = len(stream_blocks)
    out_refs = refs[13:13 + ns]
    rows_ref, sem = refs[13 + ns:]
    step = pl.program_id(0)
    nsteps = pl.num_programs(0)
    n = h_ref.shape[0]
    slot = step % 2

    def gather(w, to_slot):
        _window_chunks((cs_ref, cn_ref, cd_ref), w, rows_ref.at[to_slot], y_hbm, sem.at[to_slot],
                       to_hbm=False, spare_row=0)

    pl.when(step == 0)(functools.partial(gather, 0, 0))
    for parity in range(2):
        pl.when((step + 1 < nsteps) & (slot == parity))(
            functools.partial(gather, step + 1, 1 - parity))

    ys = _swiglu_packed(h_ref[...], wg_ref, wu_ref, wd_ref)
    _wait_rows(y_hbm, sem.at[slot], LOCAL_ROWS)
    w = w_ref[...].astype(BF16)
    lp = lpos_ref[...].astype(jnp.int16)
    ya, yb = ys[:, :HALF], ys[:, HALF:]
    for c in range(LOCAL_ROWS // n):
        col = (lax.broadcasted_iota(jnp.int32, (n, n), 1) + c * n).astype(jnp.int16)
        pw = jnp.zeros((n, n), BF16)
        for k in range(TOP_K):
            pw = jnp.where(col == lp[:, k:k + 1], jnp.broadcast_to(w[:, k:k + 1], (n, n)), pw)
        ra, rb = _unpack_pair(rows_ref[slot, c * n:(c + 1) * n, :])
        ya = ya + _dot(pw, ra.astype(BF16))
        yb = yb + _dot(pw, rb.astype(BF16))
    ms = (jnp.sum(ya * ya, axis=-1, keepdims=True)
          + jnp.sum(yb * yb, axis=-1, keepdims=True)) * (1.0 / D_MODEL)
    inv = lax.rsqrt(ms + EPS)
    mod = mod_ref[...]
    g = g_ref[...]
    out_a = x_ref[:, 0:HALF] + mod[5:6, 0:HALF] * (ya * inv * g[:, 0:HALF])
    out_b = x_ref[:, HALF:] + mod[5:6, HALF:] * (yb * inv * g[:, HALF:])

    def store(o_ref):
        o_ref[:, 0:HALF] = out_a
        o_ref[:, HALF:] = out_b

    lo = 0
    for o_ref, nbk in zip(out_refs, stream_blocks):
        pl.when((step >= lo) & (step < lo + nbk))(functools.partial(store, o_ref))
        lo += nbk


def _ffn_out(tables, lpos_rows, h2p, w_rows, x1, mod_l, g_post, swg, swu, swd, y_buf, *, mod_map,
             stream_tokens):
    t_all = x1.shape[0]
    tm = TOKEN_BLOCK
    nsteps = t_all // tm
    row = lambda w: pl.BlockSpec((tm, w), lambda i: (i, 0))
    full = lambda a: pl.BlockSpec(a.shape, lambda i: (0,) * a.ndim)
    smem = lambda: pl.BlockSpec(memory_space=pltpu.SMEM)
    g = g_post.reshape(1, D_MODEL)
    stream_blocks = tuple(t // tm for t in stream_tokens)
    out_shape, out_specs = [], []
    lo = 0
    for t, nbk in zip(stream_tokens, stream_blocks):
        out_shape.append(jax.ShapeDtypeStruct((t, D_MODEL), F32))
        out_specs.append(pl.BlockSpec((tm, D_MODEL),
                                      lambda i, lo=lo, nbk=nbk: (jnp.clip(i - lo, 0, nbk - 1), 0)))
        lo += nbk
    return pl.pallas_call(
        functools.partial(_ffn_out_kernel, stream_blocks=stream_blocks),
        out_shape=out_shape,
        grid=(nsteps,),
        in_specs=[smem(), smem(), smem(),
                  row(TOP_K), row(HALF), row(TOP_K), row(D_MODEL),
                  pl.BlockSpec((None, N_MOD, D_MODEL), mod_map),
                  full(g), full(swg), full(swu), full(swd),
                  pl.BlockSpec(memory_space=pl.ANY)],
        out_specs=out_specs,
        scratch_shapes=[pltpu.VMEM((2, LOCAL_ROWS, HALF), jnp.uint32),
                        pltpu.SemaphoreType.DMA((2,))],
        compiler_params=_cparams("arbitrary"),
        name="ffn_out",
    )(*tables, lpos_rows, h2p, w_rows, x1, mod_l, g, swg, swu, swd, y_buf)


def _rope_tables(L):
    rows = L // GRID_W
    row = jnp.repeat(jnp.arange(rows), GRID_W).astype(F32)
    col = jnp.tile(jnp.arange(GRID_W), rows).astype(F32)
    n = DIFF_DK // 4
    inv = ROPE_BASE ** (-jnp.arange(n, dtype=F32) / n)
    ar, ac = row[:, None] * inv[None, :], col[:, None] * inv[None, :]
    cos64 = jnp.concatenate([jnp.cos(ar), jnp.cos(ar), jnp.cos(ac), jnp.cos(ac)], axis=-1)
    sin64 = jnp.concatenate([-jnp.sin(ar), jnp.sin(ar), -jnp.sin(ac), jnp.sin(ac)], axis=-1)
    return jnp.tile(cos64, (1, 2)), jnp.tile(sin64, (1, 2))


def kernel(x_prompt, x_sample, cache_k, cache_v, c, c_ctx, w_mod, b_mod, g_pre_mix, g_post_mix, g_pre_ffn, g_post_ffn, w_in, w_out, conv_w, conv_b, conv_ln_g, conv_ln_b, hy_short_w, hy_short_b, hy_w1, hy_b1, hy_w2, hy_b2, hy_freq, hy_w3, hy_bias, lam_q1, lam_k1, lam_q2, lam_k2, subln_g, router_w, router_b, exp_w_gate, exp_w_up, exp_w_down, sh_w_gate, sh_w_up, sh_w_down):
    n_p, l_p, _ = x_prompt.shape
    n_s, l_s, _ = x_sample.shape
    t_p, t_s = n_p * l_p, n_s * l_s

    cond = jnp.zeros((COND_ROWS, D_MODEL), F32).at[0].set(c_ctx).at[1:1 + n_s].set(c)
    mod = _modulation(cond, w_mod, b_mod).reshape(DEPTH, COND_ROWS, N_MOD, D_MODEL)
    rope = _rope_tables(l_s)

    t_all = t_p + t_s
    streams = {
        "p": dict(x=x_prompt.reshape(t_p, D_MODEL), T=t_p, L=l_p, row0=0, per_seq=False, hy_nb=8),
        "s": dict(x=x_sample.reshape(t_s, D_MODEL), T=t_s, L=l_s, row0=1, per_seq=True, hy_nb=1),
    }
    for st in streams.values():
        fwd, fwd_t = _dft_matrices(st["L"])
        feats, window = _filter_tables(st["L"])
        st["fmat32"] = jnp.asarray(fwd)
        st["fmat"] = jnp.asarray(fwd).astype(BF16)
        st["fmat_t"] = jnp.asarray(fwd_t).astype(BF16)
        st["filter_tables"] = (jnp.asarray(feats), jnp.asarray(window))
    nb_p, nb_s = t_p // TOKEN_BLOCK, l_s // TOKEN_BLOCK
    merged_mod_map = lambda i: (jnp.where(i < nb_p, 0, 1 + (i - nb_p) // nb_s), 0, 0)
    n_blocks = (t_all // TOKEN_BLOCK) * (LOCAL_ROWS // EXPERT_BLOCK) + N_EXPERTS

    caches = None
    for l in range(DEPTH):
        w_in_bf = w_in[l].astype(BF16)
        w_out_bf = w_out[l].astype(BF16)
        swg, swu, swd = (a[l].astype(BF16) for a in (sh_w_gate, sh_w_up, sh_w_down))
        lam_vecs = jnp.stack([lam_q1[l], lam_k1[l], lam_q2[l], lam_k2[l]], axis=0)
        filt_params = (hy_w1[l], hy_b1[l], hy_w2[l], hy_b2[l], hy_freq[l], hy_w3[l])
        parts = []
        for name, st in streams.items():
            L = st["L"]
            kw = dict(seq_len=L, cond_row0=st["row0"], per_seq_cond=st["per_seq"])
            if name == "p":
                uc, uh, q, k, v, ck, cv = _premix(st["x"], mod[l], g_pre_mix[l], w_in_bf, layer=l,
                                                  n_tokens=st["T"], tok0=0,
                                                  caches=caches, want_cache=True, **kw)
                caches = (ck, cv)
                ctx = None
            else:
                uc, uh, q, k, v = _premix(st["x"], mod[l], g_pre_mix[l], w_in_bf, layer=l,
                                          n_tokens=st["T"], tok0=0, rope_tables=rope, **kw)
                ctx = (cache_k, cache_v)
            ya = _conformer(uc, conv_w[l], conv_b[l], conv_ln_g[l], conv_ln_b[l], seq_len=L)
            kspec = _hyena_spectrum(L, st["fmat32"], *st["filter_tables"], *filt_params)
            yb = _hyena(uh, hy_short_w[l], hy_short_b[l], hy_bias[l], st["fmat"], st["fmat_t"],
                        kspec, seq_len=L, seqs_per_step=st["hy_nb"])
            o = _attention(q, k, v, lam_vecs, subln_g[l], seq_len=L, layer=l, ctx=ctx)
            parts.append((ya, yb, o, st["x"]))
        x1, h2p, logits_t = _postmix(parts, mod[l], g_post_mix[l], g_pre_ffn[l], w_out_bf,
                                     router_w[l].T, mod_map=merged_mod_map)

        eidx_t, w_t, cnt = _route_topk(logits_t, router_b[l])
        counts = cnt[:, 0]
        pstart, nblk, n_used = _block_plan(counts, n_blocks)
        lpos, tables = _route_dest(eidx_t, pstart)
        x_buf = _dispatch(tables, lpos, counts, pstart, n_used, h2p, n_blocks)
        y_buf = _experts(x_buf, nblk, pstart, n_used, exp_w_gate, exp_w_up, exp_w_down, layer=l)
        outs = _ffn_out(tables, lpos.T, h2p, w_t.T, x1, mod[l], g_post_ffn[l], swg, swu, swd,
                        y_buf, mod_map=merged_mod_map, stream_tokens=(t_p, t_s))
        streams["p"]["x"], streams["s"]["x"] = outs

    y_prompt = streams["p"]["x"].reshape(n_p, l_p, D_MODEL)
    y_sample = streams["s"]["x"].reshape(n_s, l_s, D_MODEL)
    return (y_prompt, y_sample, caches[0], caches[1])
```

```python
import functools
import math

import numpy as np
import jax
import jax.numpy as jnp
from jax import lax
from jax.experimental import pallas as pl
from jax.experimental.pallas import tpu as pltpu

F32 = jnp.float32
BF16 = jnp.bfloat16

D_MODEL = 1024
DEPTH = 2
GRID_W = 64
D_CONV = 256
D_HYENA = 256
D_ATTN = 512
DIFF_DK = 64
N_HEADS = 4
DIFF_DV = 128
HEAD_W = 2 * DIFF_DK
IN_COLS = 2 * D_CONV + 3 * D_HYENA + 3 * D_ATTN
COL_HY = 2 * D_CONV
COL_Q = COL_HY + 3 * D_HYENA
COL_K = COL_Q + D_ATTN
COL_V = COL_K + D_ATTN
CONV_WIDTH = 31
CONV_PAD = 16
SUBLANES = 8
HYENA_EMB = 17
FEAT_PAD = 32
HYENA_BANDS = 8
HYENA_MIN_DECAY = math.log(1e-2) / 1.5
HYENA_MAX_DECAY = math.log(1e-2) / 0.3
N_EXPERTS = 64
TOP_K = 8
N_GROUPS = 8
TOPK_GROUPS = 4
GROUP_SCORE_TOP = 2
D_EXPERT = 256
ROUTED_SCALE = 2.5
ROPE_BASE = 10000.0
EPS = 1e-6
N_MOD = 6
COND_ROWS = 16

HALF = D_MODEL // 2
LANES = 128
TOKEN_BLOCK = 256
SPARE_ROWS = N_EXPERTS * SUBLANES
LOCAL_ROWS = TOKEN_BLOCK * TOP_K + SPARE_ROWS
EXPERT_BLOCK = 256
EXPERT_IN_SLOTS = 4
EXPERT_OUT_SLOTS = 3
VMEM_LIMIT = 48 * 1024 * 1024


def _cparams(*sem):
    return pltpu.CompilerParams(dimension_semantics=sem, vmem_limit_bytes=VMEM_LIMIT)


def _rms(x, g):
    return x * lax.rsqrt(jnp.mean(x * x, axis=-1, keepdims=True) + EPS) * g


def _silu(x):
    return x * jax.nn.sigmoid(x)


def _dot(a, b):
    return jnp.dot(a, b, preferred_element_type=F32)


def _mod_kernel(c_ref, w_ref, b_ref, o_ref):
    s = _silu(c_ref[...])
    o_ref[...] = _dot(s.astype(BF16), w_ref[...].astype(BF16)) + b_ref[...]


def _modulation(cond, w_mod, b_mod):
    cols = N_MOD * D_MODEL
    cb = 1536
    return pl.pallas_call(
        _mod_kernel,
        out_shape=jax.ShapeDtypeStruct((DEPTH, COND_ROWS, cols), F32),
        grid=(DEPTH, cols // cb),
        in_specs=[
            pl.BlockSpec((COND_ROWS, D_MODEL), lambda l, j: (0, 0)),
            pl.BlockSpec((None, D_MODEL, cb), lambda l, j: (l, 0, j)),
            pl.BlockSpec((None, 1, cb), lambda l, j: (l, 0, j)),
        ],
        out_specs=pl.BlockSpec((None, COND_ROWS, cb), lambda l, j: (l, 0, j)),
        compiler_params=_cparams("arbitrary", "arbitrary"),
        name="modulation",
    )(cond, w_mod, b_mod.reshape(DEPTH, 1, cols))


def _rope_slab(x, cos, sin, lo_mask):
    swapped = jnp.where(lo_mask, pltpu.roll(x, HEAD_W - 16, axis=1), pltpu.roll(x, 16, axis=1))
    return x * cos + swapped * sin


def _premix_kernel(*refs, rope, cache, layer):
    x_ref, mod_ref, g_ref, w_ref = refs[:4]
    pos = 4
    if rope:
        cos_ref, sin_ref = refs[pos:pos + 2]
        pos += 2
    if cache == "alias":
        pos += 2
    uc_ref, uh_ref, q_ref, k_ref, v_ref = refs[pos:pos + 5]
    pos += 5
    if cache:
        ck_ref, cv_ref = refs[pos:pos + 2]

    x = x_ref[...]
    mod = mod_ref[...]
    h = (_rms(x, g_ref[...]) * (1.0 + mod[1:2]) + mod[0:1]).astype(BF16)
    uc_ref[...] = _dot(h, w_ref[:, 0:COL_HY]).astype(BF16)
    uh_ref[...] = _dot(h, w_ref[:, COL_HY:COL_Q]).astype(BF16)
    uq = _dot(h, w_ref[:, COL_Q:COL_K]) * (DIFF_DK ** -0.5)
    uk = _dot(h, w_ref[:, COL_K:COL_V])
    uv = _dot(h, w_ref[:, COL_V:IN_COLS])
    if rope:
        cos = cos_ref[...]
        sin = sin_ref[...]
        lane = lax.broadcasted_iota(jnp.int32, cos.shape, 1)
        lo_mask = (lane % 32) < 16
        for hh in range(N_HEADS):
            sl = slice(hh * HEAD_W, (hh + 1) * HEAD_W)
            q_ref[:, sl] = _rope_slab(uq[:, sl], cos, sin, lo_mask).astype(BF16)
            k_ref[:, sl] = _rope_slab(uk[:, sl], cos, sin, lo_mask).astype(BF16)
    else:
        q_ref[...] = uq.astype(BF16)
        k_ref[...] = uk.astype(BF16)
    v_ref[...] = uv.astype(BF16)
    if cache == "alias":
        for hh in range(N_HEADS):
            sl = slice(hh * HEAD_W, (hh + 1) * HEAD_W)
            ck_ref[hh] = uk[:, sl]
            cv_ref[hh] = uv[:, sl]
    elif cache == "new":
        for d in range(DEPTH):
            for hh in range(N_HEADS):
                sl = slice(hh * HEAD_W, (hh + 1) * HEAD_W)
                ck_ref[d, hh] = uk[:, sl] if d == layer else jnp.zeros_like(uk[:, sl])
                cv_ref[d, hh] = uv[:, sl] if d == layer else jnp.zeros_like(uv[:, sl])


def _premix(x, mod_l, g, w_in_bf, *, seq_len, cond_row0, per_seq_cond, layer, n_tokens, tok0,
            rope_tables=None, caches=None, want_cache=False):
    T = n_tokens
    tm = TOKEN_BLOCK
    nb = seq_len // tm
    n_seq = T // seq_len
    b0 = tok0 // tm
    if per_seq_cond:
        mod_map = lambda i: (cond_row0 + i // nb, 0, 0)
    else:
        mod_map = lambda i: (cond_row0, 0, 0)
    in_specs = [
        pl.BlockSpec((tm, D_MODEL), lambda i: (i + b0, 0)),
        pl.BlockSpec((None, N_MOD, D_MODEL), mod_map),
        pl.BlockSpec((1, D_MODEL), lambda i: (0, 0)),
        pl.BlockSpec((D_MODEL, IN_COLS), lambda i: (0, 0)),
    ]
    args = [x, mod_l, g.reshape(1, D_MODEL), w_in_bf]
    rope = rope_tables is not None
    if rope:
        in_specs += [pl.BlockSpec((tm, HEAD_W), lambda i: (i % nb, 0))] * 2
        args += list(rope_tables)
    out_shape = [
        jax.ShapeDtypeStruct((T, COL_HY), BF16),
        jax.ShapeDtypeStruct((T, 3 * D_HYENA), BF16),
        jax.ShapeDtypeStruct((T, D_ATTN), BF16),
        jax.ShapeDtypeStruct((T, D_ATTN), BF16),
        jax.ShapeDtypeStruct((T, D_ATTN), BF16),
    ]
    out_specs = [
        pl.BlockSpec((tm, COL_HY), lambda i: (i, 0)),
        pl.BlockSpec((tm, 3 * D_HYENA), lambda i: (i, 0)),
        pl.BlockSpec((tm, D_ATTN), lambda i: (i, 0)),
        pl.BlockSpec((tm, D_ATTN), lambda i: (i, 0)),
        pl.BlockSpec((tm, D_ATTN), lambda i: (i, 0)),
    ]
    cache = None
    aliases = {}
    if want_cache:
        cache = "new"
        cshape = (n_seq, DEPTH, N_HEADS, seq_len, HEAD_W)
        out_shape += [jax.ShapeDtypeStruct(cshape, F32)] * 2
        if caches is None:
            out_specs += [pl.BlockSpec((None, DEPTH, N_HEADS, tm, HEAD_W),
                                       lambda i: (i // nb, 0, 0, i % nb, 0))] * 2
        else:
            cache = "alias"
            out_specs += [pl.BlockSpec((None, None, N_HEADS, tm, HEAD_W),
                                       lambda i: (i // nb, layer, 0, i % nb, 0))] * 2
            aliases = {len(args): 5, len(args) + 1: 6}
            in_specs += [pl.BlockSpec(memory_space=pl.ANY)] * 2
            args += list(caches)
    return pl.pallas_call(
        functools.partial(_premix_kernel, rope=rope, cache=cache, layer=layer),
        out_shape=out_shape,
        grid=(T // tm,),
        in_specs=in_specs,
        out_specs=out_specs,
        input_output_aliases=aliases,
        compiler_params=_cparams("arbitrary"),
        name="premix",
    )(*args)


def _conformer_kernel(u_ref, w_ref, cb_ref, g_ref, b_ref, o_ref, pad_ref, sh_ref, *, seq_len, rows):
    L = seq_len
    u = u_ref[...].astype(F32)
    a = u[:, :D_CONV] * jax.nn.sigmoid(u[:, D_CONV:])
    zeros = jnp.zeros((CONV_PAD, D_CONV), F32)
    pad_ref[0:CONV_PAD, :] = zeros
    pad_ref[CONV_PAD + L:2 * CONV_PAD + L, :] = zeros
    pad_ref[CONV_PAD:CONV_PAD + L, :] = a
    span = L + 2 * CONV_PAD - SUBLANES
    for r in range(SUBLANES):
        sh_ref[r, 0:span, :] = pad_ref[r:r + span, :]
    first = CONV_PAD - (CONV_WIDTH - 1) // 2

    def chunk(c, carry):
        base = pl.multiple_of(c * rows, rows)
        acc = jnp.zeros((rows, D_CONV), F32)
        for t in range(CONV_WIDTH):
            off = first + t
            tap = sh_ref[off % SUBLANES, pl.ds(base + (off // SUBLANES) * SUBLANES, rows), :]
            acc = acc + tap * w_ref[t:t + 1, :]
        acc = acc + cb_ref[...]
        mu = jnp.mean(acc, axis=-1, keepdims=True)
        d = acc - mu
        var = jnp.mean(d * d, axis=-1, keepdims=True)
        y = d * lax.rsqrt(var + EPS) * g_ref[...] + b_ref[...]
        o_ref[pl.ds(base, rows), :] = _silu(y).astype(BF16)
        return carry

    lax.fori_loop(0, L // rows, chunk, 0)


def _conformer(uc, conv_w, conv_b, ln_g, ln_b, *, seq_len):
    T = uc.shape[0]
    n_seq = T // seq_len
    vec = lambda a: a.reshape(1, D_CONV)
    return pl.pallas_call(
        functools.partial(_conformer_kernel, seq_len=seq_len, rows=64),
        out_shape=jax.ShapeDtypeStruct((T, D_CONV), BF16),
        grid=(n_seq,),
        in_specs=[
            pl.BlockSpec((seq_len, 2 * D_CONV), lambda i: (i, 0)),
            pl.BlockSpec((CONV_WIDTH, D_CONV), lambda i: (0, 0)),
            pl.BlockSpec((1, D_CONV), lambda i: (0, 0)),
            pl.BlockSpec((1, D_CONV), lambda i: (0, 0)),
            pl.BlockSpec((1, D_CONV), lambda i: (0, 0)),
        ],
        out_specs=pl.BlockSpec((seq_len, D_CONV), lambda i: (i, 0)),
        scratch_shapes=[pltpu.VMEM((seq_len + 2 * CONV_PAD, D_CONV), F32),
                        pltpu.VMEM((SUBLANES, seq_len + 2 * CONV_PAD, D_CONV), F32)],
        compiler_params=_cparams("arbitrary"),
        name="conformer",
    )(uc, conv_w, vec(conv_b), vec(ln_g), vec(ln_b))


def _dft_matrices(L):
    k = np.arange(L, dtype=np.int64)[:, None]
    m = np.arange(L, dtype=np.int64)[None, :]
    ang = 2.0 * np.pi * (((2 * k + 1) * m) % (4 * L)).astype(np.float64) / (4 * L)
    fwd = np.concatenate([np.cos(ang), np.sin(ang)], axis=0).astype(np.float32)
    return fwd, np.ascontiguousarray(fwd.T)


def _filter_tables(L):
    t = np.linspace(0.0, 1.0, L)
    bands = np.linspace(1e-4, HYENA_BANDS - 1, HYENA_BANDS)
    ang = (2.0 * np.pi / L) * np.arange(L)[:, None] * bands[None, :]
    feats = np.zeros((L, FEAT_PAD), np.float32)
    feats[:, :HYENA_EMB] = np.concatenate([t[:, None], np.cos(ang), -np.sin(ang)], axis=-1)
    deltas = np.abs(np.linspace(HYENA_MIN_DECAY, HYENA_MAX_DECAY, D_HYENA))
    window = np.exp(-t[:, None] * deltas[None, :]).astype(np.float32)
    return feats, window


def _dot3(a, b):
    a_hi, a_lo = _split_bf16(a)
    b_hi, b_lo = _split_bf16(b)
    return _dot(a_hi, b_hi) + (_dot(a_hi, b_lo) + _dot(a_lo, b_hi))


def _spectrum_kernel(feat_ref, win_ref, w1_ref, b1_ref, w2_ref, b2_ref, fr_ref, w3_ref, f_ref,
                     o_ref, *, seq_len):
    L = seq_len
    C = D_HYENA
    fr = fr_ref[...]
    hid = jnp.sin(fr * (_dot3(feat_ref[...], w1_ref[...]) + b1_ref[...]))
    hid = jnp.sin(fr * (_dot3(hid, w2_ref[...]) + b2_ref[...]))
    filt = _dot3(hid, w3_ref[...])
    win = win_ref[...]
    first = lax.broadcasted_iota(jnp.int32, (L, C), 0) == 0
    for o in range(2):
        fwd = filt[:, o * C:(o + 1) * C] * win
        bwd = jnp.where(first, 0.0, filt[:, (2 + o) * C:(3 + o) * C] * win)
        o_ref[o, 0:L, :] = _dot3(f_ref[0:L, :], fwd + bwd) * (1.0 / L)
        o_ref[o, L:2 * L, :] = _dot3(f_ref[L:2 * L, :], bwd - fwd) * (1.0 / L)


def _hyena_spectrum(L, fmat, feats, window, hy_w1, hy_b1, hy_w2, hy_b2, hy_freq, hy_w3):
    w1 = jnp.zeros((FEAT_PAD, hy_w1.shape[1]), F32).at[:HYENA_EMB].set(hy_w1)
    vec = lambda a: a.reshape(1, -1)
    return pl.pallas_call(
        functools.partial(_spectrum_kernel, seq_len=L),
        out_shape=jax.ShapeDtypeStruct((2, 2 * L, D_HYENA), F32),
        compiler_params=pltpu.CompilerParams(vmem_limit_bytes=VMEM_LIMIT),
        name="hyena_spectrum",
    )(feats, window, w1, vec(hy_b1), hy_w2, vec(hy_b2), vec(hy_freq), hy_w3, fmat)


def _hyena_kernel(u_ref, sw_ref, sb_ref, hb_ref, f_ref, ft_ref, ks_ref, o_ref, pad_ref,
                  *, seq_len, n_seq):
    L = seq_len
    C = D_HYENA
    zrow = jnp.zeros((8, 3 * C), F32)
    seqs = range(n_seq)
    s = []
    for b in seqs:
        u = u_ref[b].astype(F32)
        pad_ref[b, 0:8, :] = zrow
        pad_ref[b, 8 + L:16 + L, :] = zrow
        pad_ref[b, 8:8 + L, :] = u
        s.append(pad_ref[b, 7:7 + L, :] * sw_ref[0:1, :] + u * sw_ref[1:2, :]
                 + pad_ref[b, 9:9 + L, :] * sw_ref[2:3, :] + sb_ref[...])
    z = [s[b][:, 0:C] for b in seqs]
    for o in range(2):
        kre, kim = ks_ref[o, 0:L, :], ks_ref[o, L:2 * L, :]
        spec = [_dot(f_ref[...], z[b].astype(BF16)) for b in seqs]
        y = []
        for b in seqs:
            zr, za = spec[b][:L], spec[b][L:]
            y.append(jnp.concatenate([zr * kre + za * kim, za * kre - zr * kim], axis=0))
        conv = [_dot(ft_ref[...], y[b].astype(BF16)) for b in seqs]
        z = [s[b][:, (o + 1) * C:(o + 2) * C] * (conv[b] + hb_ref[o:o + 1, :] * z[b]) for b in seqs]
    for b in seqs:
        o_ref[b] = z[b].astype(BF16)


def _hyena(uh, short_w, short_b, hy_bias, fmat_bf, fmat_t_bf, kspec, *, seq_len, seqs_per_step):
    T = uh.shape[0]
    n_seq = T // seq_len
    L = seq_len
    nb = seqs_per_step
    u3 = uh.reshape(n_seq, L, 3 * D_HYENA)
    out = pl.pallas_call(
        functools.partial(_hyena_kernel, seq_len=L, n_seq=nb),
        out_shape=jax.ShapeDtypeStruct((n_seq, L, D_HYENA), BF16),
        grid=(n_seq // nb,),
        in_specs=[
            pl.BlockSpec((nb, L, 3 * D_HYENA), lambda i: (i, 0, 0)),
            pl.BlockSpec((3, 3 * D_HYENA), lambda i: (0, 0)),
            pl.BlockSpec((1, 3 * D_HYENA), lambda i: (0, 0)),
            pl.BlockSpec((2, D_HYENA), lambda i: (0, 0)),
            pl.BlockSpec((2 * L, L), lambda i: (0, 0)),
            pl.BlockSpec((L, 2 * L), lambda i: (0, 0)),
            pl.BlockSpec((2, 2 * L, D_HYENA), lambda i: (0, 0, 0)),
        ],
        out_specs=pl.BlockSpec((nb, L, D_HYENA), lambda i: (i, 0, 0)),
        scratch_shapes=[pltpu.VMEM((nb, L + 16, 3 * D_HYENA), F32)],
        compiler_params=_cparams("arbitrary"),
        name="hyena",
    )(u3, short_w, short_b.reshape(1, 3 * D_HYENA), hy_bias, fmat_bf, fmat_t_bf, kspec)
    return out.reshape(T, D_HYENA)


def _attn_kernel(*refs, lam_init, has_ctx):
    q_ref, k_ref, v_ref = refs[:3]
    pos = 3
    if has_ctx:
        ck_ref, cv_ref = refs[pos:pos + 2]
        pos += 2
    lam_ref, g_ref, o_ref = refs[pos:pos + 3]

    lv = lam_ref[...]
    lam = (jnp.exp(jnp.sum(lv[0:1] * lv[1:2], axis=-1, keepdims=True))
           - jnp.exp(jnp.sum(lv[2:3] * lv[3:4], axis=-1, keepdims=True)) + lam_init)

    tq = q_ref.shape[0]
    lane = lax.broadcasted_iota(jnp.int32, (tq, HEAD_W), 1)
    dn = (((1,), (1,)), ((), ()))
    group = 2
    for h0 in range(0, N_HEADS, group):
        heads = range(h0, h0 + group)
        sls = [slice(h * HEAD_W, (h + 1) * HEAD_W) for h in heads]
        q2 = []
        for sl in sls:
            q = q_ref[:, sl]
            zero = jnp.zeros_like(q)
            q2.append(jnp.concatenate([jnp.where(lane < DIFF_DK, q, zero),
                                       jnp.where(lane >= DIFF_DK, q, zero)], axis=0))
        s = [lax.dot_general(q2[j], k_ref[:, sl], dn, preferred_element_type=F32)
             for j, sl in enumerate(sls)]
        m = [jnp.max(x, axis=-1, keepdims=True) for x in s]
        if has_ctx:
            sc = [lax.dot_general(q2[j], ck_ref[h].astype(BF16), dn, preferred_element_type=F32)
                  for j, h in enumerate(heads)]
            m = [jnp.maximum(m[j], jnp.max(sc[j], axis=-1, keepdims=True)) for j in range(group)]
        p = [jnp.exp(s[j] - m[j]) for j in range(group)]
        den = [jnp.sum(x, axis=-1, keepdims=True) for x in p]
        acc = [_dot(p[j].astype(BF16), v_ref[:, sl]) for j, sl in enumerate(sls)]
        if has_ctx:
            pc = [jnp.exp(sc[j] - m[j]) for j in range(group)]
            den = [den[j] + jnp.sum(pc[j], axis=-1, keepdims=True) for j in range(group)]
            acc = [acc[j] + _dot(pc[j].astype(BF16), cv_ref[h].astype(BF16))
                   for j, h in enumerate(heads)]
        for j, sl in enumerate(sls):
            both = acc[j] / den[j]
            o = both[:tq] - lam * both[tq:]
            o_ref[:, sl] = (_rms(o, g_ref[...]) * (1.0 - lam_init)).astype(BF16)


def _attention(q, k, v, lam_vecs, subln_g, *, seq_len, layer, ctx=None):
    T = q.shape[0]
    L = seq_len
    n_seq = T // L
    tq = TOKEN_BLOCK
    nq = L // tq
    lam_init = 0.8 - 0.6 * math.exp(-0.3 * layer)
    q3, k3, v3 = (a.reshape(n_seq, L, D_ATTN) for a in (q, k, v))
    in_specs = [
        pl.BlockSpec((None, tq, D_ATTN), lambda b, i: (b, i, 0)),
        pl.BlockSpec((None, L, D_ATTN), lambda b, i: (b, 0, 0)),
        pl.BlockSpec((None, L, D_ATTN), lambda b, i: (b, 0, 0)),
    ]
    args = [q3, k3, v3]
    if ctx is not None:
        past = ctx[0].shape[3]
        in_specs += [pl.BlockSpec((None, None, N_HEADS, past, HEAD_W),
                                  lambda b, i: (b, layer, 0, 0, 0))] * 2
        args += list(ctx)
    in_specs += [
        pl.BlockSpec((4, DIFF_DK), lambda b, i: (0, 0)),
        pl.BlockSpec((1, DIFF_DV), lambda b, i: (0, 0)),
    ]
    args += [lam_vecs, subln_g.reshape(1, DIFF_DV)]
    out = pl.pallas_call(
        functools.partial(_attn_kernel, lam_init=lam_init, has_ctx=ctx is not None),
        out_shape=jax.ShapeDtypeStruct((n_seq, L, D_ATTN), BF16),
        grid=(n_seq, nq),
        in_specs=in_specs,
        out_specs=pl.BlockSpec((None, tq, D_ATTN), lambda b, i: (b, i, 0)),
        compiler_params=_cparams("arbitrary", "arbitrary"),
        name="diff_attention",
    )(*args)
    return out.reshape(T, D_ATTN)


def _split_bf16(a):
    hi = a.astype(BF16)
    lo = (a - hi.astype(F32)).astype(BF16)
    return hi, lo


def _pack_pair(a, b):
    ua = lax.bitcast_convert_type(a.astype(BF16).astype(F32), jnp.uint32)
    ub = lax.bitcast_convert_type(b.astype(BF16).astype(F32), jnp.uint32)
    return (ua >> 16) | ub


def _unpack_pair(w):
    a = lax.bitcast_convert_type(w << 16, F32)
    b = lax.bitcast_convert_type(w & jnp.uint32(0xFFFF0000), F32)
    return a, b


def _postmix_kernel(*refs, stream_blocks):
    ns = len(stream_blocks)
    mod_ref, gpost_ref, gpre_ref, w_ref, rw_ref, x1_ref, h2_ref, lg_ref = refs[4 * ns:]
    i = pl.program_id(0)

    def body(ya_ref, yb_ref, o_ref, x_ref):
        n = x_ref.shape[0]
        halves = [slice(0, n // 2), slice(n // 2, n)]
        mod = mod_ref[...]
        y = [_dot(ya_ref[r, :], w_ref[0:D_CONV, :])
             + _dot(yb_ref[r, :], w_ref[D_CONV:D_CONV + D_HYENA, :])
             + _dot(o_ref[r, :], w_ref[D_CONV + D_HYENA:, :]) for r in halves]
        x1 = [x_ref[r, :] + mod[2:3] * _rms(y[j], gpost_ref[...]) for j, r in enumerate(halves)]
        for j, r in enumerate(halves):
            x1_ref[r, :] = x1[j]
        h2 = [_rms(x1[j], gpre_ref[...]) * (1.0 + mod[4:5]) + mod[3:4] for j in range(2)]
        for j, r in enumerate(halves):
            h2_ref[r, :] = _pack_pair(h2[j][:, :HALF], h2[j][:, HALF:])
        r_hi, r_lo = _split_bf16(rw_ref[...])
        dn = (((1,), (1,)), ((), ()))
        mm = lambda a, b: lax.dot_general(a, b, dn, preferred_element_type=F32)
        for j, r in enumerate(halves):
            h_hi, h_lo = _split_bf16(h2[j])
            lg_ref[:, r] = mm(r_hi, h_hi) + (mm(r_lo, h_hi) + mm(r_hi, h_lo))

    lo = 0
    for j, nbk in enumerate(stream_blocks):
        pl.when((i >= lo) & (i < lo + nbk))(functools.partial(body, *refs[4 * j:4 * j + 4]))
        lo += nbk


def _postmix(parts, mod_l, g_post, g_pre, w_out_bf, router_wt, *, mod_map):
    t_all = sum(p[0].shape[0] for p in parts)
    tm = TOKEN_BLOCK
    full = lambda a: pl.BlockSpec(a.shape, lambda i: (0,) * a.ndim)
    gp, ge = g_post.reshape(1, D_MODEL), g_pre.reshape(1, D_MODEL)
    stream_blocks = tuple(p[0].shape[0] // tm for p in parts)
    in_specs, args = [], []
    lo = 0
    for part, nbk in zip(parts, stream_blocks):
        local = lambda i, lo=lo, nbk=nbk: (jnp.clip(i - lo, 0, nbk - 1), 0)
        in_specs += [pl.BlockSpec((tm, D_CONV), local), pl.BlockSpec((tm, D_HYENA), local),
                     pl.BlockSpec((tm, D_ATTN), local), pl.BlockSpec((tm, D_MODEL), local)]
        args += list(part)
        lo += nbk
    in_specs += [pl.BlockSpec((None, N_MOD, D_MODEL), mod_map),
                 full(gp), full(ge), full(w_out_bf), full(router_wt)]
    args += [mod_l, gp, ge, w_out_bf, router_wt]
    return pl.pallas_call(
        functools.partial(_postmix_kernel, stream_blocks=stream_blocks),
        out_shape=[
            jax.ShapeDtypeStruct((t_all, D_MODEL), F32),
            jax.ShapeDtypeStruct((t_all, HALF), jnp.uint32),
            jax.ShapeDtypeStruct((N_EXPERTS, t_all), F32),
        ],
        grid=(t_all // tm,),
        in_specs=in_specs,
        out_specs=[pl.BlockSpec((tm, D_MODEL), lambda i: (i, 0)),
                   pl.BlockSpec((tm, HALF), lambda i: (i, 0)),
                   pl.BlockSpec((N_EXPERTS, tm), lambda i: (0, i))],
        compiler_params=_cparams("arbitrary"),
        name="postmix",
    )(*args)


def _first_max(vals, idx, sentinel):
    m = jnp.max(vals, axis=0, keepdims=True)
    i = jnp.min(jnp.where(vals == m, idx, sentinel), axis=0, keepdims=True)
    return m, i


def _route_topk_kernel(lg_ref, rb_ref, eidx_ref, w_ref, cnt_ref):
    step = pl.program_id(0)
    per = N_EXPERTS // N_GROUPS
    n = lg_ref.shape[1]
    s = jax.nn.sigmoid(lg_ref[...])
    sel = s + rb_ref[...]
    sub = lax.broadcasted_iota(jnp.int32, (per, n), 0)
    neg = jnp.float32(-jnp.inf)

    gs = []
    for g in range(N_GROUPS):
        xg = sel[g * per:(g + 1) * per]
        total = jnp.zeros((1, n), F32)
        for _ in range(GROUP_SCORE_TOP):
            m, i = _first_max(xg, sub, per)
            total = total + m
            xg = jnp.where(sub == i, neg, xg)
        gs.append(total)
    gscore = jnp.concatenate(gs, axis=0)
    gkeep = jnp.zeros((N_GROUPS, n), F32)
    for _ in range(TOPK_GROUPS):
        _, gi = _first_max(gscore, sub, N_GROUPS)
        hit = sub == gi
        gkeep = jnp.where(hit, 1.0, gkeep)
        gscore = jnp.where(hit, neg, gscore)

    eio = [sub + g * per for g in range(N_GROUPS)]
    sg = [s[g * per:(g + 1) * per] for g in range(N_GROUPS)]
    mk = [jnp.where(gkeep[g:g + 1] > 0.5, sel[g * per:(g + 1) * per], neg) for g in range(N_GROUPS)]
    oh = [jnp.zeros((per, n), F32) for _ in range(N_GROUPS)]
    idx_rows, s_rows = [], []
    for _ in range(TOP_K):
        best = functools.reduce(jnp.maximum, mk)
        m = jnp.max(best, axis=0, keepdims=True)
        cand = functools.reduce(jnp.minimum,
                                [jnp.where(mk[g] == m, eio[g], N_EXPERTS) for g in range(N_GROUPS)])
        idx = jnp.min(cand, axis=0, keepdims=True)
        acc = jnp.zeros((per, n), F32)
        for g in range(N_GROUPS):
            hit = eio[g] == idx
            acc = acc + jnp.where(hit, sg[g], 0.0)
            mk[g] = jnp.where(hit, neg, mk[g])
            oh[g] = jnp.where(hit, 1.0, oh[g])
        idx_rows.append(idx)
        s_rows.append(jnp.sum(acc, axis=0, keepdims=True))
    eidx_ref[...] = jnp.concatenate(idx_rows, axis=0)
    sk = jnp.concatenate(s_rows, axis=0)
    w_ref[...] = sk / jnp.sum(sk, axis=0, keepdims=True) * ROUTED_SCALE

    @pl.when(step == 0)
    def _():
        cnt_ref[...] = jnp.zeros_like(cnt_ref)

    tot = jnp.concatenate([jnp.sum(o, axis=1, keepdims=True) for o in oh], axis=0)
    tot8 = _ceil_tile(tot)
    cnt_ref[...] = cnt_ref[...] + jnp.broadcast_to(tot8, cnt_ref.shape).astype(jnp.int32)


def _ceil_tile(count):
    return jnp.floor((count + (SUBLANES - 1.0)) * (1.0 / SUBLANES)) * SUBLANES


def _route_topk(logits_t, router_b):
    t_all = logits_t.shape[1]
    rb = TOKEN_BLOCK
    return pl.pallas_call(
        _route_topk_kernel,
        out_shape=[
            jax.ShapeDtypeStruct((TOP_K, t_all), jnp.int32),
            jax.ShapeDtypeStruct((TOP_K, t_all), F32),
            jax.ShapeDtypeStruct((N_EXPERTS, LANES), jnp.int32),
        ],
        grid=(t_all // rb,),
        in_specs=[pl.BlockSpec((N_EXPERTS, rb), lambda i: (0, i)),
                  pl.BlockSpec((N_EXPERTS, 1), lambda i: (0, 0))],
        out_specs=[pl.BlockSpec((TOP_K, rb), lambda i: (0, i)),
                   pl.BlockSpec((TOP_K, rb), lambda i: (0, i)),
                   pl.BlockSpec((N_EXPERTS, LANES), lambda i: (0, 0))],
        compiler_params=_cparams("arbitrary"),
        name="route_topk",
    )(logits_t, router_b.reshape(N_EXPERTS, 1))


def _route_dest_kernel(eidx_ref, ps_ref, lpos_ref, cs_ref, cn_ref, cd_ref, carry_ref):
    step = pl.program_id(0)
    per = N_EXPERTS // N_GROUPS
    n = eidx_ref.shape[1]

    @pl.when(step == 0)
    def _():
        carry_ref[...] = jnp.zeros_like(carry_ref)
        cs_ref[...] = jnp.zeros_like(cs_ref)
        cn_ref[...] = jnp.zeros_like(cn_ref)
        cd_ref[...] = jnp.zeros_like(cd_ref)

    eidx = eidx_ref[...]
    sub = lax.broadcasted_iota(jnp.int32, (per, n), 0)
    rows = []
    for g in range(N_GROUPS):
        eio = sub + g * per
        o = jnp.zeros((per, n), F32)
        for k in range(TOP_K):
            o = jnp.where(eio == eidx[k:k + 1], 1.0, o)
        rows.append(o)
    onehot = jnp.concatenate(rows, axis=0)
    r = lax.broadcasted_iota(jnp.int32, (n, n), 0)
    c = lax.broadcasted_iota(jnp.int32, (n, n), 1)
    before = jnp.where(r < c, 1.0, 0.0).astype(BF16)
    rank = _dot(onehot.astype(BF16), before)
    size = _ceil_tile(jnp.sum(onehot, axis=1, keepdims=True))
    er = lax.broadcasted_iota(jnp.int32, (N_EXPERTS, N_EXPERTS), 0)
    ec = lax.broadcasted_iota(jnp.int32, (N_EXPERTS, N_EXPERTS), 1)
    lower = jnp.where(ec < er, 1.0, 0.0).astype(BF16)
    tiles = jnp.broadcast_to(size * (1.0 / SUBLANES), (N_EXPERTS, LANES)).astype(BF16)
    start = _dot(lower, tiles)[:, 0:1] * SUBLANES
    base = (rank + start).astype(jnp.int32)
    out = []
    for k in range(TOP_K):
        acc = jnp.zeros((per, n), jnp.int32)
        for g in range(N_GROUPS):
            acc = acc + jnp.where(sub + g * per == eidx[k:k + 1], base[g * per:(g + 1) * per], 0)
        out.append(jnp.sum(acc, axis=0, keepdims=True))
    lpos_ref[...] = jnp.concatenate(out, axis=0)

    col = lax.broadcasted_iota(jnp.int32, cs_ref.shape, 1) == step
    carry = carry_ref[:, 0:1]
    cs_ref[...] = jnp.where(col, start.astype(jnp.int32), cs_ref[...])
    cn_ref[...] = jnp.where(col, size.astype(jnp.int32), cn_ref[...])
    cd_ref[...] = jnp.where(col, carry.astype(jnp.int32) + ps_ref[...], cd_ref[...])
    carry_ref[...] = carry_ref[...] + jnp.broadcast_to(size, carry_ref.shape)


def _route_dest(eidx_t, pstart):
    t_all = eidx_t.shape[1]
    rb = TOKEN_BLOCK
    n_win = t_all // rb
    cols = -(-n_win // LANES) * LANES
    table = jax.ShapeDtypeStruct((N_EXPERTS, cols), jnp.int32)
    table_spec = pl.BlockSpec((N_EXPERTS, cols), lambda i: (0, 0))
    lpos, cs, cn, cd = pl.pallas_call(
        _route_dest_kernel,
        out_shape=[jax.ShapeDtypeStruct((TOP_K, t_all), jnp.int32), table, table, table],
        grid=(n_win,),
        in_specs=[pl.BlockSpec((TOP_K, rb), lambda i: (0, i)),
                  pl.BlockSpec((N_EXPERTS, 1), lambda i: (0, 0))],
        out_specs=[pl.BlockSpec((TOP_K, rb), lambda i: (0, i)), table_spec, table_spec, table_spec],
        scratch_shapes=[pltpu.VMEM((N_EXPERTS, LANES), F32)],
        compiler_params=_cparams("arbitrary"),
        name="route_dest",
    )(eidx_t, pstart.reshape(N_EXPERTS, 1))
    flat = lambda a: a[:, :n_win].T.reshape(-1)
    return lpos, (flat(cs), flat(cn), flat(cd))


def _wait_rows(hbm, sem, n_rows):
    pltpu.make_async_copy(hbm.at[pl.ds(0, n_rows)], hbm.at[pl.ds(0, n_rows)], sem).wait()


def _window_chunks(tables, w, local_ref, hbm, sem, *, to_hbm, spare_row):
    cs_ref, cn_ref, cd_ref = tables
    tile = lambda v: pl.multiple_of(v, SUBLANES)

    def copy(local_row, hbm_row, size):
        lo = local_ref.at[pl.ds(tile(local_row), tile(size))]
        hi = hbm.at[pl.ds(tile(hbm_row), tile(size))]
        src, dst = (lo, hi) if to_hbm else (hi, lo)
        pltpu.make_async_copy(src, dst, sem).start()

    def per_expert(e, carry):
        i = w * N_EXPERTS + e
        copy(cs_ref[i], cd_ref[i], cn_ref[i])
        return carry

    lax.fori_loop(0, N_EXPERTS, per_expert, 0, unroll=8)
    last = w * N_EXPERTS + (N_EXPERTS - 1)
    used = cs_ref[last] + cn_ref[last]
    copy(used, spare_row, LOCAL_ROWS - used)


def _zero_padding(cnt_ref, ps_ref, nu_ref, zero_ref, x_hbm, zsem, n_blocks, *, wait):
    def copy(row, size):
        cp = pltpu.make_async_copy(zero_ref.at[pl.ds(0, size)], x_hbm.at[pl.ds(row, size)], zsem)
        if wait:
            cp.wait()
        else:
            cp.start()

    def per_expert(e, carry):
        cnt = cnt_ref[e]
        pad = (EXPERT_BLOCK - cnt % EXPERT_BLOCK) % EXPERT_BLOCK
        copy(pl.multiple_of(ps_ref[e] + cnt, SUBLANES), pl.multiple_of(pad, SUBLANES))
        return carry

    lax.fori_loop(0, N_EXPERTS, per_expert, 0)

    def per_block(j, carry):
        copy(pl.multiple_of(j * EXPERT_BLOCK, EXPERT_BLOCK), EXPERT_BLOCK)
        return carry

    lax.fori_loop(nu_ref[0], n_blocks, per_block, 0)


def _dispatch_kernel(cs_ref, cn_ref, cd_ref, cnt_ref, ps_ref, nu_ref, lpos_ref, h_ref, x_hbm,
                     gbuf, zero_ref, sem, zsem, ssem):
    w = pl.program_id(0)
    nsteps = pl.num_programs(0)
    slot = w % 2
    n_blocks = (x_hbm.shape[0] - 2 * SPARE_ROWS) // EXPERT_BLOCK
    spare0 = n_blocks * EXPERT_BLOCK
    zero_args = (cnt_ref, ps_ref, nu_ref, zero_ref, x_hbm, zsem, n_blocks)

    def spare_zero(j):
        return pltpu.make_async_copy(
            zero_ref, x_hbm.at[pl.ds(spare0 + j * EXPERT_BLOCK, EXPERT_BLOCK)], ssem)

    @pl.when(w == 0)
    def _():
        zero_ref[...] = jnp.zeros_like(zero_ref)
        _zero_padding(*zero_args, wait=False)
        for j in range(2 * SPARE_ROWS // EXPERT_BLOCK):
            spare_zero(j).start()

    ha, hb = _unpack_pair(h_ref[...])
    ha, hb = ha.astype(BF16), hb.astype(BF16)
    lp = lpos_ref[...]
    n = h_ref.shape[0]
    last = w * N_EXPERTS + (N_EXPERTS - 1)
    used = cs_ref[last] + cn_ref[last]

    lp = lp.astype(jnp.int16)
    one = jnp.ones((n, n), BF16)

    def permute(c):
        row = (lax.broadcasted_iota(jnp.int32, (n, n), 0) + c * n).astype(jnp.int16)
        pb = jnp.zeros((n, n), BF16)
        for k in range(TOP_K):
            pb = jnp.where(row == lp[k:k + 1, :], one, pb)
        ga = lax.bitcast_convert_type(_dot(pb, ha), jnp.uint32)
        gb = lax.bitcast_convert_type(_dot(pb, hb), jnp.uint32)
        gbuf[slot, c * n:(c + 1) * n, :] = (ga >> 16) | gb

    def blank(c):
        gbuf[slot, c * n:(c + 1) * n, :] = jnp.zeros((n, HALF), jnp.uint32)

    for c in range(LOCAL_ROWS // n):
        if c < TOP_K:
            permute(c)
        else:
            pl.when(c * n < used)(functools.partial(permute, c))
            pl.when(c * n >= used)(functools.partial(blank, c))

    @pl.when(w == 0)
    def _():
        for j in range(2 * SPARE_ROWS // EXPERT_BLOCK):
            spare_zero(j).wait()

    _window_chunks((cs_ref, cn_ref, cd_ref), w, gbuf.at[slot], x_hbm, sem.at[slot],
                   to_hbm=True, spare_row=spare0 + slot * SPARE_ROWS)

    @pl.when(w > 0)
    def _():
        _wait_rows(x_hbm, sem.at[1 - slot], LOCAL_ROWS)

    @pl.when(w == nsteps - 1)
    def _():
        _wait_rows(x_hbm, sem.at[slot], LOCAL_ROWS)
        _zero_padding(*zero_args, wait=True)


def _dispatch(tables, lpos, counts, pstart, n_used, h2p, n_blocks):
    t_all = h2p.shape[0]
    sb = TOKEN_BLOCK
    smem = lambda: pl.BlockSpec(memory_space=pltpu.SMEM)
    return pl.pallas_call(
        _dispatch_kernel,
        out_shape=jax.ShapeDtypeStruct((n_blocks * EXPERT_BLOCK + 2 * SPARE_ROWS, HALF), jnp.uint32),
        grid=(t_all // sb,),
        in_specs=[smem(), smem(), smem(), smem(), smem(), smem(),
                  pl.BlockSpec((TOP_K, sb), lambda i: (0, i)),
                  pl.BlockSpec((sb, HALF), lambda i: (i, 0))],
        out_specs=pl.BlockSpec(memory_space=pl.ANY),
        scratch_shapes=[pltpu.VMEM((2, LOCAL_ROWS, HALF), jnp.uint32),
                        pltpu.VMEM((EXPERT_BLOCK, HALF), jnp.uint32),
                        pltpu.SemaphoreType.DMA((2,)), pltpu.SemaphoreType.DMA,
                        pltpu.SemaphoreType.DMA],
        compiler_params=_cparams("arbitrary"),
        name="dispatch",
    )(*tables, counts, pstart, n_used, lpos, h2p)


def _expert_kernel(nb_ref, ps_ref, nu_ref, wg_ref, wu_ref, wd_ref, x_hbm, y_hbm,
                   wg_bf, wu_bf, wd_bf, xbuf, ybuf, xsem, ysem):
    e = pl.program_id(0)
    nblk = nb_ref[e]
    first = ps_ref[e] // EXPERT_BLOCK
    n_used = nu_ref[0]
    xs, ys = xbuf.shape[0], ybuf.shape[0]
    wg_bf[...] = wg_ref[...].astype(BF16)
    wu_bf[...] = wu_ref[...].astype(BF16)
    wd_bf[...] = wd_ref[...].astype(BF16)

    def rows(b):
        return pl.ds(pl.multiple_of(b * EXPERT_BLOCK, EXPERT_BLOCK), EXPERT_BLOCK)

    def x_copy(b):
        return pltpu.make_async_copy(x_hbm.at[rows(b)], xbuf.at[b % xs], xsem.at[b % xs])

    def y_copy(b):
        return pltpu.make_async_copy(ybuf.at[b % ys], y_hbm.at[rows(b)], ysem.at[b % ys])

    @pl.when(e == 0)
    def _():
        for b in range(xs - 1):
            pl.when(b < n_used)(lambda b=b: x_copy(b).start())

    def block(j, carry):
        b = first + j
        x_copy(b).wait()
        pl.when(b + xs - 1 < n_used)(lambda: x_copy(b + xs - 1).start())
        pl.when(b >= ys)(lambda: y_copy(b - ys).wait())
        y = _swiglu_packed(xbuf[b % xs], wg_bf, wu_bf, wd_bf)
        ybuf[b % ys] = _pack_pair(y[:, :HALF], y[:, HALF:])
        y_copy(b).start()
        return carry

    lax.fori_loop(0, nblk, block, 0)

    @pl.when(e == pl.num_programs(0) - 1)
    def _():
        for d in range(ys):
            b = n_used - ys + d
            pl.when(b >= 0)(lambda b=b: y_copy(b).wait())
        n_blocks = y_hbm.shape[0] // EXPERT_BLOCK
        ybuf[0] = jnp.zeros_like(ybuf[0])

        def tail(j, carry, wait):
            cp = pltpu.make_async_copy(
                ybuf.at[0], y_hbm.at[pl.ds(pl.multiple_of(j * EXPERT_BLOCK, EXPERT_BLOCK),
                                           EXPERT_BLOCK)], ysem.at[0])
            cp.wait() if wait else cp.start()
            return carry

        lax.fori_loop(nu_ref[0], n_blocks, functools.partial(tail, wait=False), 0)
        lax.fori_loop(nu_ref[0], n_blocks, functools.partial(tail, wait=True), 0)


def _swiglu_packed(xp, wg_ref, wu_ref, wd_ref):
    xa, xb = _unpack_pair(xp)
    xa, xb = xa.astype(BF16), xb.astype(BF16)
    gate = _dot(xa, wg_ref[0:HALF, :]) + _dot(xb, wg_ref[HALF:, :])
    up = _dot(xa, wu_ref[0:HALF, :]) + _dot(xb, wu_ref[HALF:, :])
    return _dot((_silu(gate) * up).astype(BF16), wd_ref[...])


def _experts(x_buf, nblk, pstart, n_used, wg, wu, wd, *, layer):
    grid_spec = pltpu.PrefetchScalarGridSpec(
        num_scalar_prefetch=3,
        grid=(N_EXPERTS,),
        in_specs=[
            pl.BlockSpec((None, None, D_MODEL, D_EXPERT), lambda e, *_: (layer, e, 0, 0)),
            pl.BlockSpec((None, None, D_MODEL, D_EXPERT), lambda e, *_: (layer, e, 0, 0)),
            pl.BlockSpec((None, None, D_EXPERT, D_MODEL), lambda e, *_: (layer, e, 0, 0)),
            pl.BlockSpec(memory_space=pl.ANY),
        ],
        out_specs=pl.BlockSpec(memory_space=pl.ANY),
        scratch_shapes=[
            pltpu.VMEM((D_MODEL, D_EXPERT), BF16),
            pltpu.VMEM((D_MODEL, D_EXPERT), BF16),
            pltpu.VMEM((D_EXPERT, D_MODEL), BF16),
            pltpu.VMEM((EXPERT_IN_SLOTS, EXPERT_BLOCK, HALF), jnp.uint32),
            pltpu.VMEM((EXPERT_OUT_SLOTS, EXPERT_BLOCK, HALF), jnp.uint32),
            pltpu.SemaphoreType.DMA((EXPERT_IN_SLOTS,)),
            pltpu.SemaphoreType.DMA((EXPERT_OUT_SLOTS,)),
        ],
    )
    return pl.pallas_call(
        _expert_kernel,
        out_shape=jax.ShapeDtypeStruct(x_buf.shape, jnp.uint32),
        grid_spec=grid_spec,
        compiler_params=_cparams("arbitrary"),
        name="experts",
    )(nblk, pstart, n_used, wg, wu, wd, x_buf)


def _block_plan(counts, n_blocks):
    nblk = (counts + EXPERT_BLOCK - 1) // EXPERT_BLOCK
    pend = jnp.cumsum(nblk)
    pstart = ((pend - nblk) * EXPERT_BLOCK).astype(jnp.int32)
    return pstart, nblk.astype(jnp.int32), pend[-1:].astype(jnp.int32)


def _ffn_out_kernel(*refs, stream_blocks):
    (cs_ref, cn_ref, cd_ref, lpos_ref, h_ref, w_ref, x_ref, mod_ref, g_ref, wg_ref, wu_ref, wd_ref,
     y_hbm) = refs[:13]
    ns = len(stream_blocks)
    out_refs = refs[13:13 + ns]
    rows_ref, sem = refs[13 + ns:]
    step = pl.program_id(0)
    nsteps = pl.num_programs(0)
    n = h_ref.shape[0]
    slot = step % 2

    def gather(w, to_slot):
        _window_chunks((cs_ref, cn_ref, cd_ref), w, rows_ref.at[to_slot], y_hbm, sem.at[to_slot],
                       to_hbm=False, spare_row=0)

    pl.when(step == 0)(functools.partial(gather, 0, 0))
    for parity in range(2):
        pl.when((step + 1 < nsteps) & (slot == parity))(
            functools.partial(gather, step + 1, 1 - parity))

    ys = _swiglu_packed(h_ref[...], wg_ref, wu_ref, wd_ref)
    _wait_rows(y_hbm, sem.at[slot], LOCAL_ROWS)
    w = w_ref[...].astype(BF16)
    lp = lpos_ref[...].astype(jnp.int16)
    ya, yb = ys[:, :HALF], ys[:, HALF:]
    for c in range(LOCAL_ROWS // n):
        col = (lax.broadcasted_iota(jnp.int32, (n, n), 1) + c * n).astype(jnp.int16)
        pw = jnp.zeros((n, n), BF16)
        for k in range(TOP_K):
            pw = jnp.where(col == lp[:, k:k + 1], jnp.broadcast_to(w[:, k:k + 1], (n, n)), pw)
        ra, rb = _unpack_pair(rows_ref[slot, c * n:(c + 1) * n, :])
        ya = ya + _dot(pw, ra.astype(BF16))
        yb = yb + _dot(pw, rb.astype(BF16))
    ms = (jnp.sum(ya * ya, axis=-1, keepdims=True)
          + jnp.sum(yb * yb, axis=-1, keepdims=True)) * (1.0 / D_MODEL)
    inv = lax.rsqrt(ms + EPS)
    mod = mod_ref[...]
    g = g_ref[...]
    out_a = x_ref[:, 0:HALF] + mod[5:6, 0:HALF] * (ya * inv * g[:, 0:HALF])
    out_b = x_ref[:, HALF:] + mod[5:6, HALF:] * (yb * inv * g[:, HALF:])

    def store(o_ref):
        o_ref[:, 0:HALF] = out_a
        o_ref[:, HALF:] = out_b

    lo = 0
    for o_ref, nbk in zip(out_refs, stream_blocks):
        pl.when((step >= lo) & (step < lo + nbk))(functools.partial(store, o_ref))
        lo += nbk


def _ffn_out(tables, lpos_rows, h2p, w_rows, x1, mod_l, g_post, swg, swu, swd, y_buf, *, mod_map,
             stream_tokens):
    t_all = x1.shape[0]
    tm = TOKEN_BLOCK
    nsteps = t_all // tm
    row = lambda w: pl.BlockSpec((tm, w), lambda i: (i, 0))
    full = lambda a: pl.BlockSpec(a.shape, lambda i: (0,) * a.ndim)
    smem = lambda: pl.BlockSpec(memory_space=pltpu.SMEM)
    g = g_post.reshape(1, D_MODEL)
    stream_blocks = tuple(t // tm for t in stream_tokens)
    out_shape, out_specs = [], []
    lo = 0
    for t, nbk in zip(stream_tokens, stream_blocks):
        out_shape.append(jax.ShapeDtypeStruct((t, D_MODEL), F32))
        out_specs.append(pl.BlockSpec((tm, D_MODEL),
                                      lambda i, lo=lo, nbk=nbk: (jnp.clip(i - lo, 0, nbk - 1), 0)))
        lo += nbk
    return pl.pallas_call(
        functools.partial(_ffn_out_kernel, stream_blocks=stream_blocks),
        out_shape=out_shape,
        grid=(nsteps,),
        in_specs=[smem(), smem(), smem(),
                  row(TOP_K), row(HALF), row(TOP_K), row(D_MODEL),
                  pl.BlockSpec((None, N_MOD, D_MODEL), mod_map),
                  full(g), full(swg), full(swu), full(swd),
                  pl.BlockSpec(memory_space=pl.ANY)],
        out_specs=out_specs,
        scratch_shapes=[pltpu.VMEM((2, LOCAL_ROWS, HALF), jnp.uint32),
                        pltpu.SemaphoreType.DMA((2,))],
        compiler_params=_cparams("arbitrary"),
        name="ffn_out",
    )(*tables, lpos_rows, h2p, w_rows, x1, mod_l, g, swg, swu, swd, y_buf)


def _rope_tables(L):
    rows = L // GRID_W
    row = jnp.repeat(jnp.arange(rows), GRID_W).astype(F32)
    col = jnp.tile(jnp.arange(GRID_W), rows).astype(F32)
    n = DIFF_DK // 4
    inv = ROPE_BASE ** (-jnp.arange(n, dtype=F32) / n)
    ar, ac = row[:, None] * inv[None, :], col[:, None] * inv[None, :]
    cos64 = jnp.concatenate([jnp.cos(ar), jnp.cos(ar), jnp.cos(ac), jnp.cos(ac)], axis=-1)
    sin64 = jnp.concatenate([-jnp.sin(ar), jnp.sin(ar), -jnp.sin(ac), jnp.sin(ac)], axis=-1)
    return jnp.tile(cos64, (1, 2)), jnp.tile(sin64, (1, 2))


def kernel(x_prompt, x_sample, cache_k, cache_v, c, c_ctx, w_mod, b_mod, g_pre_mix, g_post_mix, g_pre_ffn, g_post_ffn, w_in, w_out, conv_w, conv_b, conv_ln_g, conv_ln_b, hy_short_w, hy_short_b, hy_w1, hy_b1, hy_w2, hy_b2, hy_freq, hy_w3, hy_bias, lam_q1, lam_k1, lam_q2, lam_k2, subln_g, router_w, router_b, exp_w_gate, exp_w_up, exp_w_down, sh_w_gate, sh_w_up, sh_w_down):
    n_p, l_p, _ = x_prompt.shape
    n_s, l_s, _ = x_sample.shape
    t_p, t_s = n_p * l_p, n_s * l_s

    cond = jnp.zeros((COND_ROWS, D_MODEL), F32).at[0].set(c_ctx).at[1:1 + n_s].set(c)
    mod = _modulation(cond, w_mod, b_mod).reshape(DEPTH, COND_ROWS, N_MOD, D_MODEL)
    rope = _rope_tables(l_s)

    t_all = t_p + t_s
    streams = {
        "p": dict(x=x_prompt.reshape(t_p, D_MODEL), T=t_p, L=l_p, row0=0, per_seq=False, hy_nb=8),
        "s": dict(x=x_sample.reshape(t_s, D_MODEL), T=t_s, L=l_s, row0=1, per_seq=True,
                  hy_nb=2 if n_s % 2 == 0 else 1),
    }
    for st in streams.values():
        fwd, fwd_t = _dft_matrices(st["L"])
        feats, window = _filter_tables(st["L"])
        st["fmat32"] = jnp.asarray(fwd)
        st["fmat"] = jnp.asarray(fwd).astype(BF16)
        st["fmat_t"] = jnp.asarray(fwd_t).astype(BF16)
        st["filter_tables"] = (jnp.asarray(feats), jnp.asarray(window))
    nb_p, nb_s = t_p // TOKEN_BLOCK, l_s // TOKEN_BLOCK
    merged_mod_map = lambda i: (jnp.where(i < nb_p, 0, 1 + (i - nb_p) // nb_s), 0, 0)
    n_blocks = (t_all // TOKEN_BLOCK) * (LOCAL_ROWS // EXPERT_BLOCK) + N_EXPERTS

    caches = None
    for l in range(DEPTH):
        w_in_bf = w_in[l].astype(BF16)
        w_out_bf = w_out[l].astype(BF16)
        swg, swu, swd = (a[l].astype(BF16) for a in (sh_w_gate, sh_w_up, sh_w_down))
        lam_vecs = jnp.stack([lam_q1[l], lam_k1[l], lam_q2[l], lam_k2[l]], axis=0)
        filt_params = (hy_w1[l], hy_b1[l], hy_w2[l], hy_b2[l], hy_freq[l], hy_w3[l])
        parts = []
        for name, st in streams.items():
            L = st["L"]
            kw = dict(seq_len=L, cond_row0=st["row0"], per_seq_cond=st["per_seq"])
            if name == "p":
                uc, uh, q, k, v, ck, cv = _premix(st["x"], mod[l], g_pre_mix[l], w_in_bf, layer=l,
                                                  n_tokens=st["T"], tok0=0,
                                                  caches=caches, want_cache=True, **kw)
                caches = (ck, cv)
                ctx = None
            else:
                uc, uh, q, k, v = _premix(st["x"], mod[l], g_pre_mix[l], w_in_bf, layer=l,
                                          n_tokens=st["T"], tok0=0, rope_tables=rope, **kw)
                ctx = (cache_k, cache_v)
            ya = _conformer(uc, conv_w[l], conv_b[l], conv_ln_g[l], conv_ln_b[l], seq_len=L)
            kspec = _hyena_spectrum(L, st["fmat32"], *st["filter_tables"], *filt_params)
            yb = _hyena(uh, hy_short_w[l], hy_short_b[l], hy_bias[l], st["fmat"], st["fmat_t"],
                        kspec, seq_len=L, seqs_per_step=st["hy_nb"])
            o = _attention(q, k, v, lam_vecs, subln_g[l], seq_len=L, layer=l, ctx=ctx)
            parts.append((ya, yb, o, st["x"]))
        x1, h2p, logits_t = _postmix(parts, mod[l], g_post_mix[l], g_pre_ffn[l], w_out_bf,
                                     router_w[l].T, mod_map=merged_mod_map)

        eidx_t, w_t, cnt = _route_topk(logits_t, router_b[l])
        counts = cnt[:, 0]
        pstart, nblk, n_used = _block_plan(counts, n_blocks)
        lpos, tables = _route_dest(eidx_t, pstart)
        x_buf = _dispatch(tables, lpos, counts, pstart, n_used, h2p, n_blocks)
        y_buf = _experts(x_buf, nblk, pstart, n_used, exp_w_gate, exp_w_up, exp_w_down, layer=l)
        outs = _ffn_out(tables, lpos.T, h2p, w_t.T, x1, mod[l], g_post_ffn[l], swg, swu, swd,
                        y_buf, mod_map=merged_mod_map, stream_tokens=(t_p, t_s))
        streams["p"]["x"], streams["s"]["x"] = outs

    y_prompt = streams["p"]["x"].reshape(n_p, l_p, D_MODEL)
    y_sample = streams["s"]["x"].reshape(n_s, l_s, D_MODEL)
    return (y_prompt, y_sample, caches[0], caches[1])
```

```python
import functools
import math

import numpy as np
import jax
import jax.numpy as jnp
from jax import lax
from jax.experimental import pallas as pl
from jax.experimental.pallas import tpu as pltpu

F32 = jnp.float32
BF16 = jnp.bfloat16

D_MODEL = 1024
DEPTH = 2
GRID_W = 64
D_CONV = 256
D_HYENA = 256
D_ATTN = 512
DIFF_DK = 64
N_HEADS = 4
DIFF_DV = 128
HEAD_W = 2 * DIFF_DK
IN_COLS = 2 * D_CONV + 3 * D_HYENA + 3 * D_ATTN
COL_HY = 2 * D_CONV
COL_Q = COL_HY + 3 * D_HYENA
COL_K = COL_Q + D_ATTN
COL_V = COL_K + D_ATTN
CONV_WIDTH = 31
CONV_PAD = 16
SUBLANES = 8
HYENA_EMB = 17
FEAT_PAD = 32
HYENA_BANDS = 8
HYENA_MIN_DECAY = math.log(1e-2) / 1.5
HYENA_MAX_DECAY = math.log(1e-2) / 0.3
N_EXPERTS = 64
TOP_K = 8
N_GROUPS = 8
TOPK_GROUPS = 4
GROUP_SCORE_TOP = 2
D_EXPERT = 256
ROUTED_SCALE = 2.5
ROPE_BASE = 10000.0
EPS = 1e-6
N_MOD = 6
COND_ROWS = 16

HALF = D_MODEL // 2
LANES = 128
TOKEN_BLOCK = 256
SPARE_ROWS = N_EXPERTS * SUBLANES
LOCAL_ROWS = TOKEN_BLOCK * TOP_K + SPARE_ROWS
EXPERT_BLOCK = 512
EXPERT_IN_SLOTS = 4
EXPERT_OUT_SLOTS = 3
VMEM_LIMIT = 48 * 1024 * 1024


def _cparams(*sem):
    return pltpu.CompilerParams(dimension_semantics=sem, vmem_limit_bytes=VMEM_LIMIT)


def _rms(x, g):
    return x * lax.rsqrt(jnp.mean(x * x, axis=-1, keepdims=True) + EPS) * g


def _silu(x):
    return x * jax.nn.sigmoid(x)


def _dot(a, b):
    return jnp.dot(a, b, preferred_element_type=F32)


def _mod_kernel(c_ref, w_ref, b_ref, o_ref):
    s = _silu(c_ref[...])
    o_ref[...] = _dot(s.astype(BF16), w_ref[...].astype(BF16)) + b_ref[...]


def _modulation(cond, w_mod, b_mod):
    cols = N_MOD * D_MODEL
    cb = 1536
    return pl.pallas_call(
        _mod_kernel,
        out_shape=jax.ShapeDtypeStruct((DEPTH, COND_ROWS, cols), F32),
        grid=(DEPTH, cols // cb),
        in_specs=[
            pl.BlockSpec((COND_ROWS, D_MODEL), lambda l, j: (0, 0)),
            pl.BlockSpec((None, D_MODEL, cb), lambda l, j: (l, 0, j)),
            pl.BlockSpec((None, 1, cb), lambda l, j: (l, 0, j)),
        ],
        out_specs=pl.BlockSpec((None, COND_ROWS, cb), lambda l, j: (l, 0, j)),
        compiler_params=_cparams("arbitrary", "arbitrary"),
        name="modulation",
    )(cond, w_mod, b_mod.reshape(DEPTH, 1, cols))


def _rope_slab(x, cos, sin, lo_mask):
    swapped = jnp.where(lo_mask, pltpu.roll(x, HEAD_W - 16, axis=1), pltpu.roll(x, 16, axis=1))
    return x * cos + swapped * sin


def _premix_kernel(*refs, rope, cache, layer):
    x_ref, mod_ref, g_ref, w_ref = refs[:4]
    pos = 4
    if rope:
        cos_ref, sin_ref = refs[pos:pos + 2]
        pos += 2
    if cache == "alias":
        pos += 2
    uc_ref, uh_ref, q_ref, k_ref, v_ref = refs[pos:pos + 5]
    pos += 5
    if cache:
        ck_ref, cv_ref = refs[pos:pos + 2]

    x = x_ref[...]
    mod = mod_ref[...]
    h = (_rms(x, g_ref[...]) * (1.0 + mod[1:2]) + mod[0:1]).astype(BF16)
    uc_ref[...] = _dot(h, w_ref[:, 0:COL_HY]).astype(BF16)
    uh_ref[...] = _dot(h, w_ref[:, COL_HY:COL_Q]).astype(BF16)
    uq = _dot(h, w_ref[:, COL_Q:COL_K]) * (DIFF_DK ** -0.5)
    uk = _dot(h, w_ref[:, COL_K:COL_V])
    uv = _dot(h, w_ref[:, COL_V:IN_COLS])
    if rope:
        cos = cos_ref[...]
        sin = sin_ref[...]
        lane = lax.broadcasted_iota(jnp.int32, cos.shape, 1)
        lo_mask = (lane % 32) < 16
        for hh in range(N_HEADS):
            sl = slice(hh * HEAD_W, (hh + 1) * HEAD_W)
            q_ref[:, sl] = _rope_slab(uq[:, sl], cos, sin, lo_mask).astype(BF16)
            k_ref[:, sl] = _rope_slab(uk[:, sl], cos, sin, lo_mask).astype(BF16)
    else:
        q_ref[...] = uq.astype(BF16)
        k_ref[...] = uk.astype(BF16)
    v_ref[...] = uv.astype(BF16)
    if cache == "alias":
        for hh in range(N_HEADS):
            sl = slice(hh * HEAD_W, (hh + 1) * HEAD_W)
            ck_ref[hh] = uk[:, sl]
            cv_ref[hh] = uv[:, sl]
    elif cache == "new":
        for d in range(DEPTH):
            for hh in range(N_HEADS):
                sl = slice(hh * HEAD_W, (hh + 1) * HEAD_W)
                ck_ref[d, hh] = uk[:, sl] if d == layer else jnp.zeros_like(uk[:, sl])
                cv_ref[d, hh] = uv[:, sl] if d == layer else jnp.zeros_like(uv[:, sl])


def _premix(x, mod_l, g, w_in_bf, *, seq_len, cond_row0, per_seq_cond, layer, n_tokens, tok0,
            rope_tables=None, caches=None, want_cache=False):
    T = n_tokens
    tm = TOKEN_BLOCK
    nb = seq_len // tm
    n_seq = T // seq_len
    b0 = tok0 // tm
    if per_seq_cond:
        mod_map = lambda i: (cond_row0 + i // nb, 0, 0)
    else:
        mod_map = lambda i: (cond_row0, 0, 0)
    in_specs = [
        pl.BlockSpec((tm, D_MODEL), lambda i: (i + b0, 0)),
        pl.BlockSpec((None, N_MOD, D_MODEL), mod_map),
        pl.BlockSpec((1, D_MODEL), lambda i: (0, 0)),
        pl.BlockSpec((D_MODEL, IN_COLS), lambda i: (0, 0)),
    ]
    args = [x, mod_l, g.reshape(1, D_MODEL), w_in_bf]
    rope = rope_tables is not None
    if rope:
        in_specs += [pl.BlockSpec((tm, HEAD_W), lambda i: (i % nb, 0))] * 2
        args += list(rope_tables)
    out_shape = [
        jax.ShapeDtypeStruct((T, COL_HY), BF16),
        jax.ShapeDtypeStruct((T, 3 * D_HYENA), BF16),
        jax.ShapeDtypeStruct((T, D_ATTN), BF16),
        jax.ShapeDtypeStruct((T, D_ATTN), BF16),
        jax.ShapeDtypeStruct((T, D_ATTN), BF16),
    ]
    out_specs = [
        pl.BlockSpec((tm, COL_HY), lambda i: (i, 0)),
        pl.BlockSpec((tm, 3 * D_HYENA), lambda i: (i, 0)),
        pl.BlockSpec((tm, D_ATTN), lambda i: (i, 0)),
        pl.BlockSpec((tm, D_ATTN), lambda i: (i, 0)),
        pl.BlockSpec((tm, D_ATTN), lambda i: (i, 0)),
    ]
    cache = None
    aliases = {}
    if want_cache:
        cache = "new"
        cshape = (n_seq, DEPTH, N_HEADS, seq_len, HEAD_W)
        out_shape += [jax.ShapeDtypeStruct(cshape, F32)] * 2
        if caches is None:
            out_specs += [pl.BlockSpec((None, DEPTH, N_HEADS, tm, HEAD_W),
                                       lambda i: (i // nb, 0, 0, i % nb, 0))] * 2
        else:
            cache = "alias"
            out_specs += [pl.BlockSpec((None, None, N_HEADS, tm, HEAD_W),
                                       lambda i: (i // nb, layer, 0, i % nb, 0))] * 2
            aliases = {len(args): 5, len(args) + 1: 6}
            in_specs += [pl.BlockSpec(memory_space=pl.ANY)] * 2
            args += list(caches)
    return pl.pallas_call(
        functools.partial(_premix_kernel, rope=rope, cache=cache, layer=layer),
        out_shape=out_shape,
        grid=(T // tm,),
        in_specs=in_specs,
        out_specs=out_specs,
        input_output_aliases=aliases,
        compiler_params=_cparams("arbitrary"),
        name="premix",
    )(*args)


def _conformer_kernel(u_ref, w_ref, cb_ref, g_ref, b_ref, o_ref, pad_ref, sh_ref, *, seq_len, rows):
    L = seq_len
    u = u_ref[...].astype(F32)
    a = u[:, :D_CONV] * jax.nn.sigmoid(u[:, D_CONV:])
    zeros = jnp.zeros((CONV_PAD, D_CONV), F32)
    pad_ref[0:CONV_PAD, :] = zeros
    pad_ref[CONV_PAD + L:2 * CONV_PAD + L, :] = zeros
    pad_ref[CONV_PAD:CONV_PAD + L, :] = a
    span = L + 2 * CONV_PAD - SUBLANES
    for r in range(SUBLANES):
        sh_ref[r, 0:span, :] = pad_ref[r:r + span, :]
    first = CONV_PAD - (CONV_WIDTH - 1) // 2

    def chunk(c, carry):
        base = pl.multiple_of(c * rows, rows)
        acc = jnp.zeros((rows, D_CONV), F32)
        for t in range(CONV_WIDTH):
            off = first + t
            tap = sh_ref[off % SUBLANES, pl.ds(base + (off // SUBLANES) * SUBLANES, rows), :]
            acc = acc + tap * w_ref[t:t + 1, :]
        acc = acc + cb_ref[...]
        mu = jnp.mean(acc, axis=-1, keepdims=True)
        d = acc - mu
        var = jnp.mean(d * d, axis=-1, keepdims=True)
        y = d * lax.rsqrt(var + EPS) * g_ref[...] + b_ref[...]
        o_ref[pl.ds(base, rows), :] = _silu(y).astype(BF16)
        return carry

    lax.fori_loop(0, L // rows, chunk, 0)


def _conformer(uc, conv_w, conv_b, ln_g, ln_b, *, seq_len):
    T = uc.shape[0]
    n_seq = T // seq_len
    vec = lambda a: a.reshape(1, D_CONV)
    return pl.pallas_call(
        functools.partial(_conformer_kernel, seq_len=seq_len, rows=64),
        out_shape=jax.ShapeDtypeStruct((T, D_CONV), BF16),
        grid=(n_seq,),
        in_specs=[
            pl.BlockSpec((seq_len, 2 * D_CONV), lambda i: (i, 0)),
            pl.BlockSpec((CONV_WIDTH, D_CONV), lambda i: (0, 0)),
            pl.BlockSpec((1, D_CONV), lambda i: (0, 0)),
            pl.BlockSpec((1, D_CONV), lambda i: (0, 0)),
            pl.BlockSpec((1, D_CONV), lambda i: (0, 0)),
        ],
        out_specs=pl.BlockSpec((seq_len, D_CONV), lambda i: (i, 0)),
        scratch_shapes=[pltpu.VMEM((seq_len + 2 * CONV_PAD, D_CONV), F32),
                        pltpu.VMEM((SUBLANES, seq_len + 2 * CONV_PAD, D_CONV), F32)],
        compiler_params=_cparams("arbitrary"),
        name="conformer",
    )(uc, conv_w, vec(conv_b), vec(ln_g), vec(ln_b))


def _dft_matrices(L):
    k = np.arange(L, dtype=np.int64)[:, None]
    m = np.arange(L, dtype=np.int64)[None, :]
    ang = 2.0 * np.pi * (((2 * k + 1) * m) % (4 * L)).astype(np.float64) / (4 * L)
    fwd = np.concatenate([np.cos(ang), np.sin(ang)], axis=0).astype(np.float32)
    return fwd, np.ascontiguousarray(fwd.T)


def _filter_tables(L):
    t = np.linspace(0.0, 1.0, L)
    bands = np.linspace(1e-4, HYENA_BANDS - 1, HYENA_BANDS)
    ang = (2.0 * np.pi / L) * np.arange(L)[:, None] * bands[None, :]
    feats = np.zeros((L, FEAT_PAD), np.float32)
    feats[:, :HYENA_EMB] = np.concatenate([t[:, None], np.cos(ang), -np.sin(ang)], axis=-1)
    deltas = np.abs(np.linspace(HYENA_MIN_DECAY, HYENA_MAX_DECAY, D_HYENA))
    window = np.exp(-t[:, None] * deltas[None, :]).astype(np.float32)
    return feats, window


def _dot3(a, b):
    a_hi, a_lo = _split_bf16(a)
    b_hi, b_lo = _split_bf16(b)
    return _dot(a_hi, b_hi) + (_dot(a_hi, b_lo) + _dot(a_lo, b_hi))


def _spectrum_kernel(feat_ref, win_ref, w1_ref, b1_ref, w2_ref, b2_ref, fr_ref, w3_ref, f_ref,
                     o_ref, *, seq_len):
    L = seq_len
    C = D_HYENA
    fr = fr_ref[...]
    hid = jnp.sin(fr * (_dot3(feat_ref[...], w1_ref[...]) + b1_ref[...]))
    hid = jnp.sin(fr * (_dot3(hid, w2_ref[...]) + b2_ref[...]))
    filt = _dot3(hid, w3_ref[...])
    win = win_ref[...]
    first = lax.broadcasted_iota(jnp.int32, (L, C), 0) == 0
    for o in range(2):
        fwd = filt[:, o * C:(o + 1) * C] * win
        bwd = jnp.where(first, 0.0, filt[:, (2 + o) * C:(3 + o) * C] * win)
        o_ref[o, 0:L, :] = _dot3(f_ref[0:L, :], fwd + bwd) * (1.0 / L)
        o_ref[o, L:2 * L, :] = _dot3(f_ref[L:2 * L, :], bwd - fwd) * (1.0 / L)


def _hyena_spectrum(L, fmat, feats, window, hy_w1, hy_b1, hy_w2, hy_b2, hy_freq, hy_w3):
    w1 = jnp.zeros((FEAT_PAD, hy_w1.shape[1]), F32).at[:HYENA_EMB].set(hy_w1)
    vec = lambda a: a.reshape(1, -1)
    return pl.pallas_call(
        functools.partial(_spectrum_kernel, seq_len=L),
        out_shape=jax.ShapeDtypeStruct((2, 2 * L, D_HYENA), F32),
        compiler_params=pltpu.CompilerParams(vmem_limit_bytes=VMEM_LIMIT),
        name="hyena_spectrum",
    )(feats, window, w1, vec(hy_b1), hy_w2, vec(hy_b2), vec(hy_freq), hy_w3, fmat)


def _hyena_kernel(u_ref, sw_ref, sb_ref, hb_ref, f_ref, ft_ref, ks_ref, o_ref, pad_ref,
                  *, seq_len, n_seq):
    L = seq_len
    C = D_HYENA
    zrow = jnp.zeros((8, 3 * C), F32)
    seqs = range(n_seq)
    s = []
    for b in seqs:
        u = u_ref[b].astype(F32)
        pad_ref[b, 0:8, :] = zrow
        pad_ref[b, 8 + L:16 + L, :] = zrow
        pad_ref[b, 8:8 + L, :] = u
        s.append(pad_ref[b, 7:7 + L, :] * sw_ref[0:1, :] + u * sw_ref[1:2, :]
                 + pad_ref[b, 9:9 + L, :] * sw_ref[2:3, :] + sb_ref[...])
    z = [s[b][:, 0:C] for b in seqs]
    for o in range(2):
        kre, kim = ks_ref[o, 0:L, :], ks_ref[o, L:2 * L, :]
        spec = [_dot(f_ref[...], z[b].astype(BF16)) for b in seqs]
        y = []
        for b in seqs:
            zr, za = spec[b][:L], spec[b][L:]
            y.append(jnp.concatenate([zr * kre + za * kim, za * kre - zr * kim], axis=0))
        conv = [_dot(ft_ref[...], y[b].astype(BF16)) for b in seqs]
        z = [s[b][:, (o + 1) * C:(o + 2) * C] * (conv[b] + hb_ref[o:o + 1, :] * z[b]) for b in seqs]
    for b in seqs:
        o_ref[b] = z[b].astype(BF16)


def _hyena(uh, short_w, short_b, hy_bias, fmat_bf, fmat_t_bf, kspec, *, seq_len, seqs_per_step):
    T = uh.shape[0]
    n_seq = T // seq_len
    L = seq_len
    nb = seqs_per_step
    u3 = uh.reshape(n_seq, L, 3 * D_HYENA)
    out = pl.pallas_call(
        functools.partial(_hyena_kernel, seq_len=L, n_seq=nb),
        out_shape=jax.ShapeDtypeStruct((n_seq, L, D_HYENA), BF16),
        grid=(n_seq // nb,),
        in_specs=[
            pl.BlockSpec((nb, L, 3 * D_HYENA), lambda i: (i, 0, 0)),
            pl.BlockSpec((3, 3 * D_HYENA), lambda i: (0, 0)),
            pl.BlockSpec((1, 3 * D_HYENA), lambda i: (0, 0)),
            pl.BlockSpec((2, D_HYENA), lambda i: (0, 0)),
            pl.BlockSpec((2 * L, L), lambda i: (0, 0)),
            pl.BlockSpec((L, 2 * L), lambda i: (0, 0)),
            pl.BlockSpec((2, 2 * L, D_HYENA), lambda i: (0, 0, 0)),
        ],
        out_specs=pl.BlockSpec((nb, L, D_HYENA), lambda i: (i, 0, 0)),
        scratch_shapes=[pltpu.VMEM((nb, L + 16, 3 * D_HYENA), F32)],
        compiler_params=_cparams("arbitrary"),
        name="hyena",
    )(u3, short_w, short_b.reshape(1, 3 * D_HYENA), hy_bias, fmat_bf, fmat_t_bf, kspec)
    return out.reshape(T, D_HYENA)


def _attn_kernel(*refs, lam_init, has_ctx):
    q_ref, k_ref, v_ref = refs[:3]
    pos = 3
    if has_ctx:
        ck_ref, cv_ref = refs[pos:pos + 2]
        pos += 2
    lam_ref, g_ref, o_ref = refs[pos:pos + 3]

    lv = lam_ref[...]
    lam = (jnp.exp(jnp.sum(lv[0:1] * lv[1:2], axis=-1, keepdims=True))
           - jnp.exp(jnp.sum(lv[2:3] * lv[3:4], axis=-1, keepdims=True)) + lam_init)

    tq = q_ref.shape[0]
    lane = lax.broadcasted_iota(jnp.int32, (tq, HEAD_W), 1)
    dn = (((1,), (1,)), ((), ()))
    group = 2
    for h0 in range(0, N_HEADS, group):
        heads = range(h0, h0 + group)
        sls = [slice(h * HEAD_W, (h + 1) * HEAD_W) for h in heads]
        q2 = []
        for sl in sls:
            q = q_ref[:, sl]
            zero = jnp.zeros_like(q)
            q2.append(jnp.concatenate([jnp.where(lane < DIFF_DK, q, zero),
                                       jnp.where(lane >= DIFF_DK, q, zero)], axis=0))
        s = [lax.dot_general(q2[j], k_ref[:, sl], dn, preferred_element_type=F32)
             for j, sl in enumerate(sls)]
        m = [jnp.max(x, axis=-1, keepdims=True) for x in s]
        if has_ctx:
            sc = [lax.dot_general(q2[j], ck_ref[h].astype(BF16), dn, preferred_element_type=F32)
                  for j, h in enumerate(heads)]
            m = [jnp.maximum(m[j], jnp.max(sc[j], axis=-1, keepdims=True)) for j in range(group)]
        p = [jnp.exp(s[j] - m[j]) for j in range(group)]
        den = [jnp.sum(x, axis=-1, keepdims=True) for x in p]
        acc = [_dot(p[j].astype(BF16), v_ref[:, sl]) for j, sl in enumerate(sls)]
        if has_ctx:
            pc = [jnp.exp(sc[j] - m[j]) for j in range(group)]
            den = [den[j] + jnp.sum(pc[j], axis=-1, keepdims=True) for j in range(group)]
            acc = [acc[j] + _dot(pc[j].astype(BF16), cv_ref[h].astype(BF16))
                   for j, h in enumerate(heads)]
        for j, sl in enumerate(sls):
            both = acc[j] / den[j]
            o = both[:tq] - lam * both[tq:]
            o_ref[:, sl] = (_rms(o, g_ref[...]) * (1.0 - lam_init)).astype(BF16)


def _attention(q, k, v, lam_vecs, subln_g, *, seq_len, layer, ctx=None):
    T = q.shape[0]
    L = seq_len
    n_seq = T // L
    tq = TOKEN_BLOCK
    nq = L // tq
    lam_init = 0.8 - 0.6 * math.exp(-0.3 * layer)
    q3, k3, v3 = (a.reshape(n_seq, L, D_ATTN) for a in (q, k, v))
    in_specs = [
        pl.BlockSpec((None, tq, D_ATTN), lambda b, i: (b, i, 0)),
        pl.BlockSpec((None, L, D_ATTN), lambda b, i: (b, 0, 0)),
        pl.BlockSpec((None, L, D_ATTN), lambda b, i: (b, 0, 0)),
    ]
    args = [q3, k3, v3]
    if ctx is not None:
        past = ctx[0].shape[3]
        in_specs += [pl.BlockSpec((None, None, N_HEADS, past, HEAD_W),
                                  lambda b, i: (b, layer, 0, 0, 0))] * 2
        args += list(ctx)
    in_specs += [
        pl.BlockSpec((4, DIFF_DK), lambda b, i: (0, 0)),
        pl.BlockSpec((1, DIFF_DV), lambda b, i: (0, 0)),
    ]
    args += [lam_vecs, subln_g.reshape(1, DIFF_DV)]
    out = pl.pallas_call(
        functools.partial(_attn_kernel, lam_init=lam_init, has_ctx=ctx is not None),
        out_shape=jax.ShapeDtypeStruct((n_seq, L, D_ATTN), BF16),
        grid=(n_seq, nq),
        in_specs=in_specs,
        out_specs=pl.BlockSpec((None, tq, D_ATTN), lambda b, i: (b, i, 0)),
        compiler_params=_cparams("arbitrary", "arbitrary"),
        name="diff_attention",
    )(*args)
    return out.reshape(T, D_ATTN)


def _split_bf16(a):
    hi = a.astype(BF16)
    lo = (a - hi.astype(F32)).astype(BF16)
    return hi, lo


def _pack_pair(a, b):
    ua = lax.bitcast_convert_type(a.astype(BF16).astype(F32), jnp.uint32)
    ub = lax.bitcast_convert_type(b.astype(BF16).astype(F32), jnp.uint32)
    return (ua >> 16) | ub


def _unpack_pair(w):
    a = lax.bitcast_convert_type(w << 16, F32)
    b = lax.bitcast_convert_type(w & jnp.uint32(0xFFFF0000), F32)
    return a, b


def _postmix_kernel(*refs, stream_blocks):
    ns = len(stream_blocks)
    mod_ref, gpost_ref, gpre_ref, w_ref, rw_ref, x1_ref, h2_ref, lg_ref = refs[4 * ns:]
    i = pl.program_id(0)

    def body(ya_ref, yb_ref, o_ref, x_ref):
        n = x_ref.shape[0]
        halves = [slice(0, n // 2), slice(n // 2, n)]
        mod = mod_ref[...]
        y = [_dot(ya_ref[r, :], w_ref[0:D_CONV, :])
             + _dot(yb_ref[r, :], w_ref[D_CONV:D_CONV + D_HYENA, :])
             + _dot(o_ref[r, :], w_ref[D_CONV + D_HYENA:, :]) for r in halves]
        x1 = [x_ref[r, :] + mod[2:3] * _rms(y[j], gpost_ref[...]) for j, r in enumerate(halves)]
        for j, r in enumerate(halves):
            x1_ref[r, :] = x1[j]
        h2 = [_rms(x1[j], gpre_ref[...]) * (1.0 + mod[4:5]) + mod[3:4] for j in range(2)]
        for j, r in enumerate(halves):
            h2_ref[r, :] = _pack_pair(h2[j][:, :HALF], h2[j][:, HALF:])
        r_hi, r_lo = _split_bf16(rw_ref[...])
        dn = (((1,), (1,)), ((), ()))
        mm = lambda a, b: lax.dot_general(a, b, dn, preferred_element_type=F32)
        for j, r in enumerate(halves):
            h_hi, h_lo = _split_bf16(h2[j])
            lg_ref[:, r] = mm(r_hi, h_hi) + (mm(r_lo, h_hi) + mm(r_hi, h_lo))

    lo = 0
    for j, nbk in enumerate(stream_blocks):
        pl.when((i >= lo) & (i < lo + nbk))(functools.partial(body, *refs[4 * j:4 * j + 4]))
        lo += nbk


def _postmix(parts, mod_l, g_post, g_pre, w_out_bf, router_wt, *, mod_map):
    t_all = sum(p[0].shape[0] for p in parts)
    tm = TOKEN_BLOCK
    full = lambda a: pl.BlockSpec(a.shape, lambda i: (0,) * a.ndim)
    gp, ge = g_post.reshape(1, D_MODEL), g_pre.reshape(1, D_MODEL)
    stream_blocks = tuple(p[0].shape[0] // tm for p in parts)
    in_specs, args = [], []
    lo = 0
    for part, nbk in zip(parts, stream_blocks):
        local = lambda i, lo=lo, nbk=nbk: (jnp.clip(i - lo, 0, nbk - 1), 0)
        in_specs += [pl.BlockSpec((tm, D_CONV), local), pl.BlockSpec((tm, D_HYENA), local),
                     pl.BlockSpec((tm, D_ATTN), local), pl.BlockSpec((tm, D_MODEL), local)]
        args += list(part)
        lo += nbk
    in_specs += [pl.BlockSpec((None, N_MOD, D_MODEL), mod_map),
                 full(gp), full(ge), full(w_out_bf), full(router_wt)]
    args += [mod_l, gp, ge, w_out_bf, router_wt]
    return pl.pallas_call(
        functools.partial(_postmix_kernel, stream_blocks=stream_blocks),
        out_shape=[
            jax.ShapeDtypeStruct((t_all, D_MODEL), F32),
            jax.ShapeDtypeStruct((t_all, HALF), jnp.uint32),
            jax.ShapeDtypeStruct((N_EXPERTS, t_all), F32),
        ],
        grid=(t_all // tm,),
        in_specs=in_specs,
        out_specs=[pl.BlockSpec((tm, D_MODEL), lambda i: (i, 0)),
                   pl.BlockSpec((tm, HALF), lambda i: (i, 0)),
                   pl.BlockSpec((N_EXPERTS, tm), lambda i: (0, i))],
        compiler_params=_cparams("arbitrary"),
        name="postmix",
    )(*args)


def _first_max(vals, idx, sentinel):
    m = jnp.max(vals, axis=0, keepdims=True)
    i = jnp.min(jnp.where(vals == m, idx, sentinel), axis=0, keepdims=True)
    return m, i


def _route_topk_kernel(lg_ref, rb_ref, eidx_ref, w_ref, cnt_ref):
    step = pl.program_id(0)
    per = N_EXPERTS // N_GROUPS
    n = lg_ref.shape[1]
    s = jax.nn.sigmoid(lg_ref[...])
    sel = s + rb_ref[...]
    sub = lax.broadcasted_iota(jnp.int32, (per, n), 0)
    neg = jnp.float32(-jnp.inf)

    gs = []
    for g in range(N_GROUPS):
        xg = sel[g * per:(g + 1) * per]
        total = jnp.zeros((1, n), F32)
        for _ in range(GROUP_SCORE_TOP):
            m, i = _first_max(xg, sub, per)
            total = total + m
            xg = jnp.where(sub == i, neg, xg)
        gs.append(total)
    gscore = jnp.concatenate(gs, axis=0)
    gkeep = jnp.zeros((N_GROUPS, n), F32)
    for _ in range(TOPK_GROUPS):
        _, gi = _first_max(gscore, sub, N_GROUPS)
        hit = sub == gi
        gkeep = jnp.where(hit, 1.0, gkeep)
        gscore = jnp.where(hit, neg, gscore)

    eio = [sub + g * per for g in range(N_GROUPS)]
    sg = [s[g * per:(g + 1) * per] for g in range(N_GROUPS)]
    mk = [jnp.where(gkeep[g:g + 1] > 0.5, sel[g * per:(g + 1) * per], neg) for g in range(N_GROUPS)]
    oh = [jnp.zeros((per, n), F32) for _ in range(N_GROUPS)]
    idx_rows, s_rows = [], []
    for _ in range(TOP_K):
        best = functools.reduce(jnp.maximum, mk)
        m = jnp.max(best, axis=0, keepdims=True)
        cand = functools.reduce(jnp.minimum,
                                [jnp.where(mk[g] == m, eio[g], N_EXPERTS) for g in range(N_GROUPS)])
        idx = jnp.min(cand, axis=0, keepdims=True)
        acc = jnp.zeros((per, n), F32)
        for g in range(N_GROUPS):
            hit = eio[g] == idx
            acc = acc + jnp.where(hit, sg[g], 0.0)
            mk[g] = jnp.where(hit, neg, mk[g])
            oh[g] = jnp.where(hit, 1.0, oh[g])
        idx_rows.append(idx)
        s_rows.append(jnp.sum(acc, axis=0, keepdims=True))
    eidx_ref[...] = jnp.concatenate(idx_rows, axis=0)
    sk = jnp.concatenate(s_rows, axis=0)
    w_ref[...] = sk / jnp.sum(sk, axis=0, keepdims=True) * ROUTED_SCALE

    @pl.when(step == 0)
    def _():
        cnt_ref[...] = jnp.zeros_like(cnt_ref)

    tot = jnp.concatenate([jnp.sum(o, axis=1, keepdims=True) for o in oh], axis=0)
    tot8 = _ceil_tile(tot)
    cnt_ref[...] = cnt_ref[...] + jnp.broadcast_to(tot8, cnt_ref.shape).astype(jnp.int32)


def _ceil_tile(count):
    return jnp.floor((count + (SUBLANES - 1.0)) * (1.0 / SUBLANES)) * SUBLANES


def _route_topk(logits_t, router_b):
    t_all = logits_t.shape[1]
    rb = TOKEN_BLOCK
    return pl.pallas_call(
        _route_topk_kernel,
        out_shape=[
            jax.ShapeDtypeStruct((TOP_K, t_all), jnp.int32),
            jax.ShapeDtypeStruct((TOP_K, t_all), F32),
            jax.ShapeDtypeStruct((N_EXPERTS, LANES), jnp.int32),
        ],
        grid=(t_all // rb,),
        in_specs=[pl.BlockSpec((N_EXPERTS, rb), lambda i: (0, i)),
                  pl.BlockSpec((N_EXPERTS, 1), lambda i: (0, 0))],
        out_specs=[pl.BlockSpec((TOP_K, rb), lambda i: (0, i)),
                   pl.BlockSpec((TOP_K, rb), lambda i: (0, i)),
                   pl.BlockSpec((N_EXPERTS, LANES), lambda i: (0, 0))],
        compiler_params=_cparams("arbitrary"),
        name="route_topk",
    )(logits_t, router_b.reshape(N_EXPERTS, 1))


def _route_dest_kernel(eidx_ref, ps_ref, lpos_ref, cs_ref, cn_ref, cd_ref, carry_ref):
    step = pl.program_id(0)
    per = N_EXPERTS // N_GROUPS
    n = eidx_ref.shape[1]

    @pl.when(step == 0)
    def _():
        carry_ref[...] = jnp.zeros_like(carry_ref)
        cs_ref[...] = jnp.zeros_like(cs_ref)
        cn_ref[...] = jnp.zeros_like(cn_ref)
        cd_ref[...] = jnp.zeros_like(cd_ref)

    eidx = eidx_ref[...]
    sub = lax.broadcasted_iota(jnp.int32, (per, n), 0)
    rows = []
    for g in range(N_GROUPS):
        eio = sub + g * per
        o = jnp.zeros((per, n), F32)
        for k in range(TOP_K):
            o = jnp.where(eio == eidx[k:k + 1], 1.0, o)
        rows.append(o)
    onehot = jnp.concatenate(rows, axis=0)
    r = lax.broadcasted_iota(jnp.int32, (n, n), 0)
    c = lax.broadcasted_iota(jnp.int32, (n, n), 1)
    before = jnp.where(r < c, 1.0, 0.0).astype(BF16)
    rank = _dot(onehot.astype(BF16), before)
    size = _ceil_tile(jnp.sum(onehot, axis=1, keepdims=True))
    er = lax.broadcasted_iota(jnp.int32, (N_EXPERTS, N_EXPERTS), 0)
    ec = lax.broadcasted_iota(jnp.int32, (N_EXPERTS, N_EXPERTS), 1)
    lower = jnp.where(ec < er, 1.0, 0.0).astype(BF16)
    tiles = jnp.broadcast_to(size * (1.0 / SUBLANES), (N_EXPERTS, LANES)).astype(BF16)
    start = _dot(lower, tiles)[:, 0:1] * SUBLANES
    base = (rank + start).astype(jnp.int32)
    out = []
    for k in range(TOP_K):
        acc = jnp.zeros((per, n), jnp.int32)
        for g in range(N_GROUPS):
            acc = acc + jnp.where(sub + g * per == eidx[k:k + 1], base[g * per:(g + 1) * per], 0)
        out.append(jnp.sum(acc, axis=0, keepdims=True))
    lpos_ref[...] = jnp.concatenate(out, axis=0)

    col = lax.broadcasted_iota(jnp.int32, cs_ref.shape, 1) == step
    carry = carry_ref[:, 0:1]
    cs_ref[...] = jnp.where(col, start.astype(jnp.int32), cs_ref[...])
    cn_ref[...] = jnp.where(col, size.astype(jnp.int32), cn_ref[...])
    cd_ref[...] = jnp.where(col, carry.astype(jnp.int32) + ps_ref[...], cd_ref[...])
    carry_ref[...] = carry_ref[...] + jnp.broadcast_to(size, carry_ref.shape)


def _route_dest(eidx_t, pstart):
    t_all = eidx_t.shape[1]
    rb = TOKEN_BLOCK
    n_win = t_all // rb
    cols = -(-n_win // LANES) * LANES
    table = jax.ShapeDtypeStruct((N_EXPERTS, cols), jnp.int32)
    table_spec = pl.BlockSpec((N_EXPERTS, cols), lambda i: (0, 0))
    lpos, cs, cn, cd = pl.pallas_call(
        _route_dest_kernel,
        out_shape=[jax.ShapeDtypeStruct((TOP_K, t_all), jnp.int32), table, table, table],
        grid=(n_win,),
        in_specs=[pl.BlockSpec((TOP_K, rb), lambda i: (0, i)),
                  pl.BlockSpec((N_EXPERTS, 1), lambda i: (0, 0))],
        out_specs=[pl.BlockSpec((TOP_K, rb), lambda i: (0, i)), table_spec, table_spec, table_spec],
        scratch_shapes=[pltpu.VMEM((N_EXPERTS, LANES), F32)],
        compiler_params=_cparams("arbitrary"),
        name="route_dest",
    )(eidx_t, pstart.reshape(N_EXPERTS, 1))
    flat = lambda a: a[:, :n_win].T.reshape(-1)
    return lpos, (flat(cs), flat(cn), flat(cd))


def _wait_rows(hbm, sem, n_rows):
    pltpu.make_async_copy(hbm.at[pl.ds(0, n_rows)], hbm.at[pl.ds(0, n_rows)], sem).wait()


def _window_chunks(tables, w, local_ref, hbm, sem, *, to_hbm, spare_row):
    cs_ref, cn_ref, cd_ref = tables
    tile = lambda v: pl.multiple_of(v, SUBLANES)

    def copy(local_row, hbm_row, size):
        lo = local_ref.at[pl.ds(tile(local_row), tile(size))]
        hi = hbm.at[pl.ds(tile(hbm_row), tile(size))]
        src, dst = (lo, hi) if to_hbm else (hi, lo)
        pltpu.make_async_copy(src, dst, sem).start()

    def per_expert(e, carry):
        i = w * N_EXPERTS + e
        copy(cs_ref[i], cd_ref[i], cn_ref[i])
        return carry

    lax.fori_loop(0, N_EXPERTS, per_expert, 0, unroll=8)
    last = w * N_EXPERTS + (N_EXPERTS - 1)
    used = cs_ref[last] + cn_ref[last]
    copy(used, spare_row, LOCAL_ROWS - used)


def _zero_padding(cnt_ref, ps_ref, nu_ref, zero_ref, x_hbm, zsem, n_blocks, *, wait):
    def copy(row, size):
        cp = pltpu.make_async_copy(zero_ref.at[pl.ds(0, size)], x_hbm.at[pl.ds(row, size)], zsem)
        if wait:
            cp.wait()
        else:
            cp.start()

    def per_expert(e, carry):
        cnt = cnt_ref[e]
        pad = (EXPERT_BLOCK - cnt % EXPERT_BLOCK) % EXPERT_BLOCK
        copy(pl.multiple_of(ps_ref[e] + cnt, SUBLANES), pl.multiple_of(pad, SUBLANES))
        return carry

    lax.fori_loop(0, N_EXPERTS, per_expert, 0)

    def per_block(j, carry):
        copy(pl.multiple_of(j * EXPERT_BLOCK, EXPERT_BLOCK), EXPERT_BLOCK)
        return carry

    lax.fori_loop(nu_ref[0], n_blocks, per_block, 0)


def _dispatch_kernel(cs_ref, cn_ref, cd_ref, cnt_ref, ps_ref, nu_ref, lpos_ref, h_ref, x_hbm,
                     gbuf, zero_ref, sem, zsem, ssem):
    w = pl.program_id(0)
    nsteps = pl.num_programs(0)
    slot = w % 2
    n_blocks = (x_hbm.shape[0] - 2 * SPARE_ROWS) // EXPERT_BLOCK
    spare0 = n_blocks * EXPERT_BLOCK
    zero_args = (cnt_ref, ps_ref, nu_ref, zero_ref, x_hbm, zsem, n_blocks)

    def spare_zero(j):
        return pltpu.make_async_copy(
            zero_ref, x_hbm.at[pl.ds(spare0 + j * EXPERT_BLOCK, EXPERT_BLOCK)], ssem)

    @pl.when(w == 0)
    def _():
        zero_ref[...] = jnp.zeros_like(zero_ref)
        _zero_padding(*zero_args, wait=False)
        for j in range(2 * SPARE_ROWS // EXPERT_BLOCK):
            spare_zero(j).start()

    ha, hb = _unpack_pair(h_ref[...])
    ha, hb = ha.astype(BF16), hb.astype(BF16)
    lp = lpos_ref[...]
    n = h_ref.shape[0]
    last = w * N_EXPERTS + (N_EXPERTS - 1)
    used = cs_ref[last] + cn_ref[last]

    lp = lp.astype(jnp.int16)
    one = jnp.ones((n, n), BF16)

    def permute(c):
        row = (lax.broadcasted_iota(jnp.int32, (n, n), 0) + c * n).astype(jnp.int16)
        pb = jnp.zeros((n, n), BF16)
        for k in range(TOP_K):
            pb = jnp.where(row == lp[k:k + 1, :], one, pb)
        ga = lax.bitcast_convert_type(_dot(pb, ha), jnp.uint32)
        gb = lax.bitcast_convert_type(_dot(pb, hb), jnp.uint32)
        gbuf[slot, c * n:(c + 1) * n, :] = (ga >> 16) | gb

    def blank(c):
        gbuf[slot, c * n:(c + 1) * n, :] = jnp.zeros((n, HALF), jnp.uint32)

    for c in range(LOCAL_ROWS // n):
        if c < TOP_K:
            permute(c)
        else:
            pl.when(c * n < used)(functools.partial(permute, c))
            pl.when(c * n >= used)(functools.partial(blank, c))

    @pl.when(w == 0)
    def _():
        for j in range(2 * SPARE_ROWS // EXPERT_BLOCK):
            spare_zero(j).wait()

    _window_chunks((cs_ref, cn_ref, cd_ref), w, gbuf.at[slot], x_hbm, sem.at[slot],
                   to_hbm=True, spare_row=spare0 + slot * SPARE_ROWS)

    @pl.when(w > 0)
    def _():
        _wait_rows(x_hbm, sem.at[1 - slot], LOCAL_ROWS)

    @pl.when(w == nsteps - 1)
    def _():
        _wait_rows(x_hbm, sem.at[slot], LOCAL_ROWS)
        _zero_padding(*zero_args, wait=True)


def _dispatch(tables, lpos, counts, pstart, n_used, h2p, n_blocks):
    t_all = h2p.shape[0]
    sb = TOKEN_BLOCK
    smem = lambda: pl.BlockSpec(memory_space=pltpu.SMEM)
    return pl.pallas_call(
        _dispatch_kernel,
        out_shape=jax.ShapeDtypeStruct((n_blocks * EXPERT_BLOCK + 2 * SPARE_ROWS, HALF), jnp.uint32),
        grid=(t_all // sb,),
        in_specs=[smem(), smem(), smem(), smem(), smem(), smem(),
                  pl.BlockSpec((TOP_K, sb), lambda i: (0, i)),
                  pl.BlockSpec((sb, HALF), lambda i: (i, 0))],
        out_specs=pl.BlockSpec(memory_space=pl.ANY),
        scratch_shapes=[pltpu.VMEM((2, LOCAL_ROWS, HALF), jnp.uint32),
                        pltpu.VMEM((EXPERT_BLOCK, HALF), jnp.uint32),
                        pltpu.SemaphoreType.DMA((2,)), pltpu.SemaphoreType.DMA,
                        pltpu.SemaphoreType.DMA],
        compiler_params=_cparams("arbitrary"),
        name="dispatch",
    )(*tables, counts, pstart, n_used, lpos, h2p)


def _expert_kernel(nb_ref, ps_ref, nu_ref, wg_ref, wu_ref, wd_ref, x_hbm, y_hbm,
                   wg_bf, wu_bf, wd_bf, xbuf, ybuf, xsem, ysem):
    e = pl.program_id(0)
    nblk = nb_ref[e]
    first = ps_ref[e] // EXPERT_BLOCK
    n_used = nu_ref[0]
    xs, ys = xbuf.shape[0], ybuf.shape[0]
    wg_bf[...] = wg_ref[...].astype(BF16)
    wu_bf[...] = wu_ref[...].astype(BF16)
    wd_bf[...] = wd_ref[...].astype(BF16)

    def rows(b):
        return pl.ds(pl.multiple_of(b * EXPERT_BLOCK, EXPERT_BLOCK), EXPERT_BLOCK)

    def x_copy(b):
        return pltpu.make_async_copy(x_hbm.at[rows(b)], xbuf.at[b % xs], xsem.at[b % xs])

    def y_copy(b):
        return pltpu.make_async_copy(ybuf.at[b % ys], y_hbm.at[rows(b)], ysem.at[b % ys])

    @pl.when(e == 0)
    def _():
        for b in range(xs - 1):
            pl.when(b < n_used)(lambda b=b: x_copy(b).start())

    def block(j, carry):
        b = first + j
        x_copy(b).wait()
        pl.when(b + xs - 1 < n_used)(lambda: x_copy(b + xs - 1).start())
        pl.when(b >= ys)(lambda: y_copy(b - ys).wait())
        y = _swiglu_packed(xbuf[b % xs], wg_bf, wu_bf, wd_bf)
        ybuf[b % ys] = _pack_pair(y[:, :HALF], y[:, HALF:])
        y_copy(b).start()
        return carry

    lax.fori_loop(0, nblk, block, 0)

    @pl.when(e == pl.num_programs(0) - 1)
    def _():
        for d in range(ys):
            b = n_used - ys + d
            pl.when(b >= 0)(lambda b=b: y_copy(b).wait())
        n_blocks = y_hbm.shape[0] // EXPERT_BLOCK
        ybuf[0] = jnp.zeros_like(ybuf[0])

        def tail(j, carry, wait):
            cp = pltpu.make_async_copy(
                ybuf.at[0], y_hbm.at[pl.ds(pl.multiple_of(j * EXPERT_BLOCK, EXPERT_BLOCK),
                                           EXPERT_BLOCK)], ysem.at[0])
            cp.wait() if wait else cp.start()
            return carry

        lax.fori_loop(nu_ref[0], n_blocks, functools.partial(tail, wait=False), 0)
        lax.fori_loop(nu_ref[0], n_blocks, functools.partial(tail, wait=True), 0)


def _swiglu_packed(xp, wg_ref, wu_ref, wd_ref):
    xa, xb = _unpack_pair(xp)
    xa, xb = xa.astype(BF16), xb.astype(BF16)
    gate = _dot(xa, wg_ref[0:HALF, :]) + _dot(xb, wg_ref[HALF:, :])
    up = _dot(xa, wu_ref[0:HALF, :]) + _dot(xb, wu_ref[HALF:, :])
    return _dot((_silu(gate) * up).astype(BF16), wd_ref[...])


def _experts(x_buf, nblk, pstart, n_used, wg, wu, wd, *, layer):
    grid_spec = pltpu.PrefetchScalarGridSpec(
        num_scalar_prefetch=3,
        grid=(N_EXPERTS,),
        in_specs=[
            pl.BlockSpec((None, None, D_MODEL, D_EXPERT), lambda e, *_: (layer, e, 0, 0)),
            pl.BlockSpec((None, None, D_MODEL, D_EXPERT), lambda e, *_: (layer, e, 0, 0)),
            pl.BlockSpec((None, None, D_EXPERT, D_MODEL), lambda e, *_: (layer, e, 0, 0)),
            pl.BlockSpec(memory_space=pl.ANY),
        ],
        out_specs=pl.BlockSpec(memory_space=pl.ANY),
        scratch_shapes=[
            pltpu.VMEM((D_MODEL, D_EXPERT), BF16),
            pltpu.VMEM((D_MODEL, D_EXPERT), BF16),
            pltpu.VMEM((D_EXPERT, D_MODEL), BF16),
            pltpu.VMEM((EXPERT_IN_SLOTS, EXPERT_BLOCK, HALF), jnp.uint32),
            pltpu.VMEM((EXPERT_OUT_SLOTS, EXPERT_BLOCK, HALF), jnp.uint32),
            pltpu.SemaphoreType.DMA((EXPERT_IN_SLOTS,)),
            pltpu.SemaphoreType.DMA((EXPERT_OUT_SLOTS,)),
        ],
    )
    return pl.pallas_call(
        _expert_kernel,
        out_shape=jax.ShapeDtypeStruct(x_buf.shape, jnp.uint32),
        grid_spec=grid_spec,
        compiler_params=_cparams("arbitrary"),
        name="experts",
    )(nblk, pstart, n_used, wg, wu, wd, x_buf)


def _block_plan(counts, n_blocks):
    nblk = (counts + EXPERT_BLOCK - 1) // EXPERT_BLOCK
    pend = jnp.cumsum(nblk)
    pstart = ((pend - nblk) * EXPERT_BLOCK).astype(jnp.int32)
    return pstart, nblk.astype(jnp.int32), pend[-1:].astype(jnp.int32)


def _ffn_out_kernel(*refs, stream_blocks):
    (cs_ref, cn_ref, cd_ref, lpos_ref, h_ref, w_ref, x_ref, mod_ref, g_ref, wg_ref, wu_ref, wd_ref,
     y_hbm) = refs[:13]
    ns = len(stream_blocks)
    out_refs = refs[13:13 + ns]
    rows_ref, sem = refs[13 + ns:]
    step = pl.program_id(0)
    nsteps = pl.num_programs(0)
    n = h_ref.shape[0]
    slot = step % 2

    def gather(w, to_slot):
        _window_chunks((cs_ref, cn_ref, cd_ref), w, rows_ref.at[to_slot], y_hbm, sem.at[to_slot],
                       to_hbm=False, spare_row=0)

    pl.when(step == 0)(functools.partial(gather, 0, 0))
    for parity in range(2):
        pl.when((step + 1 < nsteps) & (slot == parity))(
            functools.partial(gather, step + 1, 1 - parity))

    ys = _swiglu_packed(h_ref[...], wg_ref, wu_ref, wd_ref)
    _wait_rows(y_hbm, sem.at[slot], LOCAL_ROWS)
    w = w_ref[...].astype(BF16)
    lp = lpos_ref[...].astype(jnp.int16)
    ya, yb = ys[:, :HALF], ys[:, HALF:]
    for c in range(LOCAL_ROWS // n):
        col = (lax.broadcasted_iota(jnp.int32, (n, n), 1) + c * n).astype(jnp.int16)
        pw = jnp.zeros((n, n), BF16)
        for k in range(TOP_K):
            pw = jnp.where(col == lp[:, k:k + 1], jnp.broadcast_to(w[:, k:k + 1], (n, n)), pw)
        ra, rb = _unpack_pair(rows_ref[slot, c * n:(c + 1) * n, :])
        ya = ya + _dot(pw, ra.astype(BF16))
        yb = yb + _dot(pw, rb.astype(BF16))
    ms = (jnp.sum(ya * ya, axis=-1, keepdims=True)
          + jnp.sum(yb * yb, axis=-1, keepdims=True)) * (1.0 / D_MODEL)
    inv = lax.rsqrt(ms + EPS)
    mod = mod_ref[...]
    g = g_ref[...]
    out_a = x_ref[:, 0:HALF] + mod[5:6, 0:HALF] * (ya * inv * g[:, 0:HALF])
    out_b = x_ref[:, HALF:] + mod[5:6, HALF:] * (yb * inv * g[:, HALF:])

    def store(o_ref):
        o_ref[:, 0:HALF] = out_a
        o_ref[:, HALF:] = out_b

    lo = 0
    for o_ref, nbk in zip(out_refs, stream_blocks):
        pl.when((step >= lo) & (step < lo + nbk))(functools.partial(store, o_ref))
        lo += nbk


def _ffn_out(tables, lpos_rows, h2p, w_rows, x1, mod_l, g_post, swg, swu, swd, y_buf, *, mod_map,
             stream_tokens):
    t_all = x1.shape[0]
    tm = TOKEN_BLOCK
    nsteps = t_all // tm
    row = lambda w: pl.BlockSpec((tm, w), lambda i: (i, 0))
    full = lambda a: pl.BlockSpec(a.shape, lambda i: (0,) * a.ndim)
    smem = lambda: pl.BlockSpec(memory_space=pltpu.SMEM)
    g = g_post.reshape(1, D_MODEL)
    stream_blocks = tuple(t // tm for t in stream_tokens)
    out_shape, out_specs = [], []
    lo = 0
    for t, nbk in zip(stream_tokens, stream_blocks):
        out_shape.append(jax.ShapeDtypeStruct((t, D_MODEL), F32))
        out_specs.append(pl.BlockSpec((tm, D_MODEL),
                                      lambda i, lo=lo, nbk=nbk: (jnp.clip(i - lo, 0, nbk - 1), 0)))
        lo += nbk
    return pl.pallas_call(
        functools.partial(_ffn_out_kernel, stream_blocks=stream_blocks),
        out_shape=out_shape,
        grid=(nsteps,),
        in_specs=[smem(), smem(), smem(),
                  row(TOP_K), row(HALF), row(TOP_K), row(D_MODEL),
                  pl.BlockSpec((None, N_MOD, D_MODEL), mod_map),
                  full(g), full(swg), full(swu), full(swd),
                  pl.BlockSpec(memory_space=pl.ANY)],
        out_specs=out_specs,
        scratch_shapes=[pltpu.VMEM((2, LOCAL_ROWS, HALF), jnp.uint32),
                        pltpu.SemaphoreType.DMA((2,))],
        compiler_params=_cparams("arbitrary"),
        name="ffn_out",
    )(*tables, lpos_rows, h2p, w_rows, x1, mod_l, g, swg, swu, swd, y_buf)


def _rope_tables(L):
    rows = L // GRID_W
    row = jnp.repeat(jnp.arange(rows), GRID_W).astype(F32)
    col = jnp.tile(jnp.arange(GRID_W), rows).astype(F32)
    n = DIFF_DK // 4
    inv = ROPE_BASE ** (-jnp.arange(n, dtype=F32) / n)
    ar, ac = row[:, None] * inv[None, :], col[:, None] * inv[None, :]
    cos64 = jnp.concatenate([jnp.cos(ar), jnp.cos(ar), jnp.cos(ac), jnp.cos(ac)], axis=-1)
    sin64 = jnp.concatenate([-jnp.sin(ar), jnp.sin(ar), -jnp.sin(ac), jnp.sin(ac)], axis=-1)
    return jnp.tile(cos64, (1, 2)), jnp.tile(sin64, (1, 2))


def kernel(x_prompt, x_sample, cache_k, cache_v, c, c_ctx, w_mod, b_mod, g_pre_mix, g_post_mix, g_pre_ffn, g_post_ffn, w_in, w_out, conv_w, conv_b, conv_ln_g, conv_ln_b, hy_short_w, hy_short_b, hy_w1, hy_b1, hy_w2, hy_b2, hy_freq, hy_w3, hy_bias, lam_q1, lam_k1, lam_q2, lam_k2, subln_g, router_w, router_b, exp_w_gate, exp_w_up, exp_w_down, sh_w_gate, sh_w_up, sh_w_down):
    n_p, l_p, _ = x_prompt.shape
    n_s, l_s, _ = x_sample.shape
    t_p, t_s = n_p * l_p, n_s * l_s

    cond = jnp.zeros((COND_ROWS, D_MODEL), F32).at[0].set(c_ctx).at[1:1 + n_s].set(c)
    mod = _modulation(cond, w_mod, b_mod).reshape(DEPTH, COND_ROWS, N_MOD, D_MODEL)
    rope = _rope_tables(l_s)

    t_all = t_p + t_s
    streams = {
        "p": dict(x=x_prompt.reshape(t_p, D_MODEL), T=t_p, L=l_p, row0=0, per_seq=False, hy_nb=8),
        "s": dict(x=x_sample.reshape(t_s, D_MODEL), T=t_s, L=l_s, row0=1, per_seq=True,
                  hy_nb=2 if n_s % 2 == 0 else 1),
    }
    for st in streams.values():
        fwd, fwd_t = _dft_matrices(st["L"])
        feats, window = _filter_tables(st["L"])
        st["fmat32"] = jnp.asarray(fwd)
        st["fmat"] = jnp.asarray(fwd).astype(BF16)
        st["fmat_t"] = jnp.asarray(fwd_t).astype(BF16)
        st["filter_tables"] = (jnp.asarray(feats), jnp.asarray(window))
    nb_p, nb_s = t_p // TOKEN_BLOCK, l_s // TOKEN_BLOCK
    merged_mod_map = lambda i: (jnp.where(i < nb_p, 0, 1 + (i - nb_p) // nb_s), 0, 0)
    n_blocks = (t_all // TOKEN_BLOCK) * (LOCAL_ROWS // EXPERT_BLOCK) + N_EXPERTS

    caches = None
    for l in range(DEPTH):
        w_in_bf = w_in[l].astype(BF16)
        w_out_bf = w_out[l].astype(BF16)
        swg, swu, swd = (a[l].astype(BF16) for a in (sh_w_gate, sh_w_up, sh_w_down))
        lam_vecs = jnp.stack([lam_q1[l], lam_k1[l], lam_q2[l], lam_k2[l]], axis=0)
        filt_params = (hy_w1[l], hy_b1[l], hy_w2[l], hy_b2[l], hy_freq[l], hy_w3[l])
        parts = []
        for name, st in streams.items():
            L = st["L"]
            kw = dict(seq_len=L, cond_row0=st["row0"], per_seq_cond=st["per_seq"])
            if name == "p":
                uc, uh, q, k, v, ck, cv = _premix(st["x"], mod[l], g_pre_mix[l], w_in_bf, layer=l,
                                                  n_tokens=st["T"], tok0=0,
                                                  caches=caches, want_cache=True, **kw)
                caches = (ck, cv)
                ctx = None
            else:
                uc, uh, q, k, v = _premix(st["x"], mod[l], g_pre_mix[l], w_in_bf, layer=l,
                                          n_tokens=st["T"], tok0=0, rope_tables=rope, **kw)
                ctx = (cache_k, cache_v)
            ya = _conformer(uc, conv_w[l], conv_b[l], conv_ln_g[l], conv_ln_b[l], seq_len=L)
            kspec = _hyena_spectrum(L, st["fmat32"], *st["filter_tables"], *filt_params)
            yb = _hyena(uh, hy_short_w[l], hy_short_b[l], hy_bias[l], st["fmat"], st["fmat_t"],
                        kspec, seq_len=L, seqs_per_step=st["hy_nb"])
            o = _attention(q, k, v, lam_vecs, subln_g[l], seq_len=L, layer=l, ctx=ctx)
            parts.append((ya, yb, o, st["x"]))
        x1, h2p, logits_t = _postmix(parts, mod[l], g_post_mix[l], g_pre_ffn[l], w_out_bf,
                                     router_w[l].T, mod_map=merged_mod_map)

        eidx_t, w_t, cnt = _route_topk(logits_t, router_b[l])
        counts = cnt[:, 0]
        pstart, nblk, n_used = _block_plan(counts, n_blocks)
        lpos, tables = _route_dest(eidx_t, pstart)
        x_buf = _dispatch(tables, lpos, counts, pstart, n_used, h2p, n_blocks)
        y_buf = _experts(x_buf, nblk, pstart, n_used, exp_w_gate, exp_w_up, exp_w_down, layer=l)
        outs = _ffn_out(tables, lpos.T, h2p, w_t.T, x1, mod[l], g_post_ffn[l], swg, swu, swd,
                        y_buf, mod_map=merged_mod_map, stream_tokens=(t_p, t_s))
        streams["p"]["x"], streams["s"]["x"] = outs

    y_prompt = streams["p"]["x"].reshape(n_p, l_p, D_MODEL)
    y_sample = streams["s"]["x"].reshape(n_s, l_s, D_MODEL)
    return (y_prompt, y_sample, caches[0], caches[1])
```

```python
import functools
import math

import numpy as np
import jax
import jax.numpy as jnp
from jax import lax
from jax.experimental import pallas as pl
from jax.experimental.pallas import tpu as pltpu

F32 = jnp.float32
BF16 = jnp.bfloat16

D_MODEL = 1024
DEPTH = 2
GRID_W = 64
D_CONV = 256
D_HYENA = 256
D_ATTN = 512
DIFF_DK = 64
N_HEADS = 4
DIFF_DV = 128
HEAD_W = 2 * DIFF_DK
IN_COLS = 2 * D_CONV + 3 * D_HYENA + 3 * D_ATTN
COL_HY = 2 * D_CONV
COL_Q = COL_HY + 3 * D_HYENA
COL_K = COL_Q + D_ATTN
COL_V = COL_K + D_ATTN
CONV_WIDTH = 31
CONV_PAD = 16
CONV_ROWS = 256
SUBLANES = 8
HYENA_EMB = 17
FEAT_PAD = 32
HYENA_BANDS = 8
HYENA_MIN_DECAY = math.log(1e-2) / 1.5
HYENA_MAX_DECAY = math.log(1e-2) / 0.3
N_EXPERTS = 64
TOP_K = 8
N_GROUPS = 8
TOPK_GROUPS = 4
GROUP_SCORE_TOP = 2
D_EXPERT = 256
ROUTED_SCALE = 2.5
ROPE_BASE = 10000.0
EPS = 1e-6
N_MOD = 6
COND_ROWS = 16

HALF = D_MODEL // 2
LANES = 128
TOKEN_BLOCK = 256
SPARE_ROWS = N_EXPERTS * SUBLANES
LOCAL_ROWS = TOKEN_BLOCK * TOP_K + SPARE_ROWS
EXPERT_BLOCK = 512
EXPERT_IN_SLOTS = 4
EXPERT_OUT_SLOTS = 3
VMEM_LIMIT = 48 * 1024 * 1024


def _cparams(*sem):
    return pltpu.CompilerParams(dimension_semantics=sem, vmem_limit_bytes=VMEM_LIMIT)


def _rms(x, g):
    return x * lax.rsqrt(jnp.mean(x * x, axis=-1, keepdims=True) + EPS) * g


def _silu(x):
    return x * jax.nn.sigmoid(x)


def _dot(a, b):
    return jnp.dot(a, b, preferred_element_type=F32)


def _mod_kernel(c_ref, w_ref, b_ref, o_ref):
    s = _silu(c_ref[...])
    o_ref[...] = _dot(s.astype(BF16), w_ref[...].astype(BF16)) + b_ref[...]


def _modulation(cond, w_mod, b_mod):
    cols = N_MOD * D_MODEL
    cb = 1536
    return pl.pallas_call(
        _mod_kernel,
        out_shape=jax.ShapeDtypeStruct((DEPTH, COND_ROWS, cols), F32),
        grid=(DEPTH, cols // cb),
        in_specs=[
            pl.BlockSpec((COND_ROWS, D_MODEL), lambda l, j: (0, 0)),
            pl.BlockSpec((None, D_MODEL, cb), lambda l, j: (l, 0, j)),
            pl.BlockSpec((None, 1, cb), lambda l, j: (l, 0, j)),
        ],
        out_specs=pl.BlockSpec((None, COND_ROWS, cb), lambda l, j: (l, 0, j)),
        compiler_params=_cparams("arbitrary", "arbitrary"),
        name="modulation",
    )(cond, w_mod, b_mod.reshape(DEPTH, 1, cols))


def _rope_slab(x, cos, sin, lo_mask):
    swapped = jnp.where(lo_mask, pltpu.roll(x, HEAD_W - 16, axis=1), pltpu.roll(x, 16, axis=1))
    return x * cos + swapped * sin


def _premix_kernel(*refs, rope, cache, layer):
    x_ref, mod_ref, g_ref, w_ref = refs[:4]
    pos = 4
    if rope:
        cos_ref, sin_ref = refs[pos:pos + 2]
        pos += 2
    if cache == "alias":
        pos += 2
    uc_ref, uh_ref, q_ref, k_ref, v_ref = refs[pos:pos + 5]
    pos += 5
    if cache:
        ck_ref, cv_ref = refs[pos:pos + 2]

    x = x_ref[...]
    mod = mod_ref[...]
    h = (_rms(x, g_ref[...]) * (1.0 + mod[1:2]) + mod[0:1]).astype(BF16)
    uc_ref[...] = _dot(h, w_ref[:, 0:COL_HY]).astype(BF16)
    uh_ref[...] = _dot(h, w_ref[:, COL_HY:COL_Q]).astype(BF16)
    uq = _dot(h, w_ref[:, COL_Q:COL_K]) * (DIFF_DK ** -0.5)
    uk = _dot(h, w_ref[:, COL_K:COL_V])
    uv = _dot(h, w_ref[:, COL_V:IN_COLS])
    if rope:
        cos = cos_ref[...]
        sin = sin_ref[...]
        lane = lax.broadcasted_iota(jnp.int32, cos.shape, 1)
        lo_mask = (lane % 32) < 16
        for hh in range(N_HEADS):
            sl = slice(hh * HEAD_W, (hh + 1) * HEAD_W)
            q_ref[:, sl] = _rope_slab(uq[:, sl], cos, sin, lo_mask).astype(BF16)
            k_ref[:, sl] = _rope_slab(uk[:, sl], cos, sin, lo_mask).astype(BF16)
    else:
        q_ref[...] = uq.astype(BF16)
        k_ref[...] = uk.astype(BF16)
    v_ref[...] = uv.astype(BF16)
    if cache == "alias":
        for hh in range(N_HEADS):
            sl = slice(hh * HEAD_W, (hh + 1) * HEAD_W)
            ck_ref[hh] = uk[:, sl]
            cv_ref[hh] = uv[:, sl]
    elif cache == "new":
        for d in range(DEPTH):
            for hh in range(N_HEADS):
                sl = slice(hh * HEAD_W, (hh + 1) * HEAD_W)
                ck_ref[d, hh] = uk[:, sl] if d == layer else jnp.zeros_like(uk[:, sl])
                cv_ref[d, hh] = uv[:, sl] if d == layer else jnp.zeros_like(uv[:, sl])


def _premix(x, mod_l, g, w_in_bf, *, seq_len, cond_row0, per_seq_cond, layer, n_tokens, tok0,
            rope_tables=None, caches=None, want_cache=False):
    T = n_tokens
    tm = TOKEN_BLOCK
    nb = seq_len // tm
    n_seq = T // seq_len
    b0 = tok0 // tm
    if per_seq_cond:
        mod_map = lambda i: (cond_row0 + i // nb, 0, 0)
    else:
        mod_map = lambda i: (cond_row0, 0, 0)
    in_specs = [
        pl.BlockSpec((tm, D_MODEL), lambda i: (i + b0, 0)),
        pl.BlockSpec((None, N_MOD, D_MODEL), mod_map),
        pl.BlockSpec((1, D_MODEL), lambda i: (0, 0)),
        pl.BlockSpec((D_MODEL, IN_COLS), lambda i: (0, 0)),
    ]
    args = [x, mod_l, g.reshape(1, D_MODEL), w_in_bf]
    rope = rope_tables is not None
    if rope:
        in_specs += [pl.BlockSpec((tm, HEAD_W), lambda i: (i % nb, 0))] * 2
        args += list(rope_tables)
    out_shape = [
        jax.ShapeDtypeStruct((T, COL_HY), BF16),
        jax.ShapeDtypeStruct((T, 3 * D_HYENA), BF16),
        jax.ShapeDtypeStruct((T, D_ATTN), BF16),
        jax.ShapeDtypeStruct((T, D_ATTN), BF16),
        jax.ShapeDtypeStruct((T, D_ATTN), BF16),
    ]
    out_specs = [
        pl.BlockSpec((tm, COL_HY), lambda i: (i, 0)),
        pl.BlockSpec((tm, 3 * D_HYENA), lambda i: (i, 0)),
        pl.BlockSpec((tm, D_ATTN), lambda i: (i, 0)),
        pl.BlockSpec((tm, D_ATTN), lambda i: (i, 0)),
        pl.BlockSpec((tm, D_ATTN), lambda i: (i, 0)),
    ]
    cache = None
    aliases = {}
    if want_cache:
        cache = "new"
        cshape = (n_seq, DEPTH, N_HEADS, seq_len, HEAD_W)
        out_shape += [jax.ShapeDtypeStruct(cshape, F32)] * 2
        if caches is None:
            out_specs += [pl.BlockSpec((None, DEPTH, N_HEADS, tm, HEAD_W),
                                       lambda i: (i // nb, 0, 0, i % nb, 0))] * 2
        else:
            cache = "alias"
            out_specs += [pl.BlockSpec((None, None, N_HEADS, tm, HEAD_W),
                                       lambda i: (i // nb, layer, 0, i % nb, 0))] * 2
            aliases = {len(args): 5, len(args) + 1: 6}
            in_specs += [pl.BlockSpec(memory_space=pl.ANY)] * 2
            args += list(caches)
    return pl.pallas_call(
        functools.partial(_premix_kernel, rope=rope, cache=cache, layer=layer),
        out_shape=out_shape,
        grid=(T // tm,),
        in_specs=in_specs,
        out_specs=out_specs,
        input_output_aliases=aliases,
        compiler_params=_cparams("arbitrary"),
        name="premix",
    )(*args)


def _conformer_kernel(u_ref, w_ref, cb_ref, g_ref, b_ref, o_ref, pad_ref, sh_ref, *, seq_len, rows):
    L = seq_len
    u = u_ref[...].astype(F32)
    a = u[:, :D_CONV] * jax.nn.sigmoid(u[:, D_CONV:])
    zeros = jnp.zeros((CONV_PAD, D_CONV), F32)
    pad_ref[0:CONV_PAD, :] = zeros
    pad_ref[CONV_PAD + L:2 * CONV_PAD + L, :] = zeros
    pad_ref[CONV_PAD:CONV_PAD + L, :] = a
    span = L + 2 * CONV_PAD - SUBLANES
    for r in range(SUBLANES):
        sh_ref[r, 0:span, :] = pad_ref[r:r + span, :]
    first = CONV_PAD - (CONV_WIDTH - 1) // 2

    def chunk(c, carry):
        base = pl.multiple_of(c * rows, rows)
        acc = jnp.zeros((rows, D_CONV), F32)
        for t in range(CONV_WIDTH):
            off = first + t
            tap = sh_ref[off % SUBLANES, pl.ds(base + (off // SUBLANES) * SUBLANES, rows), :]
            acc = acc + tap * w_ref[t:t + 1, :]
        acc = acc + cb_ref[...]
        mu = jnp.mean(acc, axis=-1, keepdims=True)
        d = acc - mu
        var = jnp.mean(d * d, axis=-1, keepdims=True)
        y = d * lax.rsqrt(var + EPS) * g_ref[...] + b_ref[...]
        o_ref[pl.ds(base, rows), :] = _silu(y).astype(BF16)
        return carry

    lax.fori_loop(0, L // rows, chunk, 0)


def _conformer(uc, conv_w, conv_b, ln_g, ln_b, *, seq_len):
    T = uc.shape[0]
    n_seq = T // seq_len
    vec = lambda a: a.reshape(1, D_CONV)
    return pl.pallas_call(
        functools.partial(_conformer_kernel, seq_len=seq_len, rows=min(seq_len, CONV_ROWS)),
        out_shape=jax.ShapeDtypeStruct((T, D_CONV), BF16),
        grid=(n_seq,),
        in_specs=[
            pl.BlockSpec((seq_len, 2 * D_CONV), lambda i: (i, 0)),
            pl.BlockSpec((CONV_WIDTH, D_CONV), lambda i: (0, 0)),
            pl.BlockSpec((1, D_CONV), lambda i: (0, 0)),
            pl.BlockSpec((1, D_CONV), lambda i: (0, 0)),
            pl.BlockSpec((1, D_CONV), lambda i: (0, 0)),
        ],
        out_specs=pl.BlockSpec((seq_len, D_CONV), lambda i: (i, 0)),
        scratch_shapes=[pltpu.VMEM((seq_len + 2 * CONV_PAD, D_CONV), F32),
                        pltpu.VMEM((SUBLANES, seq_len + 2 * CONV_PAD, D_CONV), F32)],
        compiler_params=_cparams("arbitrary"),
        name="conformer",
    )(uc, conv_w, vec(conv_b), vec(ln_g), vec(ln_b))


def _dft_matrices(L):
    k = np.arange(L, dtype=np.int64)[:, None]
    m = np.arange(L, dtype=np.int64)[None, :]
    ang = 2.0 * np.pi * (((2 * k + 1) * m) % (4 * L)).astype(np.float64) / (4 * L)
    fwd = np.concatenate([np.cos(ang), np.sin(ang)], axis=0).astype(np.float32)
    return fwd, np.ascontiguousarray(fwd.T)


def _filter_tables(L):
    t = np.linspace(0.0, 1.0, L)
    bands = np.linspace(1e-4, HYENA_BANDS - 1, HYENA_BANDS)
    ang = (2.0 * np.pi / L) * np.arange(L)[:, None] * bands[None, :]
    feats = np.zeros((L, FEAT_PAD), np.float32)
    feats[:, :HYENA_EMB] = np.concatenate([t[:, None], np.cos(ang), -np.sin(ang)], axis=-1)
    deltas = np.abs(np.linspace(HYENA_MIN_DECAY, HYENA_MAX_DECAY, D_HYENA))
    window = np.exp(-t[:, None] * deltas[None, :]).astype(np.float32)
    return feats, window


def _dot3(a, b):
    a_hi, a_lo = _split_bf16(a)
    b_hi, b_lo = _split_bf16(b)
    return _dot(a_hi, b_hi) + (_dot(a_hi, b_lo) + _dot(a_lo, b_hi))


def _spectrum_kernel(feat_ref, win_ref, w1_ref, b1_ref, w2_ref, b2_ref, fr_ref, w3_ref, f_ref,
                     o_ref, *, seq_len):
    L = seq_len
    C = D_HYENA
    fr = fr_ref[...]
    hid = jnp.sin(fr * (_dot3(feat_ref[...], w1_ref[...]) + b1_ref[...]))
    hid = jnp.sin(fr * (_dot3(hid, w2_ref[...]) + b2_ref[...]))
    filt = _dot3(hid, w3_ref[...])
    win = win_ref[...]
    first = lax.broadcasted_iota(jnp.int32, (L, C), 0) == 0
    for o in range(2):
        fwd = filt[:, o * C:(o + 1) * C] * win
        bwd = jnp.where(first, 0.0, filt[:, (2 + o) * C:(3 + o) * C] * win)
        o_ref[o, 0:L, :] = _dot3(f_ref[0:L, :], fwd + bwd) * (1.0 / L)
        o_ref[o, L:2 * L, :] = _dot3(f_ref[L:2 * L, :], bwd - fwd) * (1.0 / L)


def _hyena_spectrum(L, fmat, feats, window, hy_w1, hy_b1, hy_w2, hy_b2, hy_freq, hy_w3):
    w1 = jnp.zeros((FEAT_PAD, hy_w1.shape[1]), F32).at[:HYENA_EMB].set(hy_w1)
    vec = lambda a: a.reshape(1, -1)
    return pl.pallas_call(
        functools.partial(_spectrum_kernel, seq_len=L),
        out_shape=jax.ShapeDtypeStruct((2, 2 * L, D_HYENA), F32),
        compiler_params=pltpu.CompilerParams(vmem_limit_bytes=VMEM_LIMIT),
        name="hyena_spectrum",
    )(feats, window, w1, vec(hy_b1), hy_w2, vec(hy_b2), vec(hy_freq), hy_w3, fmat)


def _hyena_kernel(u_ref, sw_ref, sb_ref, hb_ref, f_ref, ft_ref, ks_ref, o_ref, pad_ref,
                  *, seq_len, n_seq):
    L = seq_len
    C = D_HYENA
    zrow = jnp.zeros((8, 3 * C), F32)
    seqs = range(n_seq)
    s = []
    for b in seqs:
        u = u_ref[b].astype(F32)
        pad_ref[b, 0:8, :] = zrow
        pad_ref[b, 8 + L:16 + L, :] = zrow
        pad_ref[b, 8:8 + L, :] = u
        s.append(pad_ref[b, 7:7 + L, :] * sw_ref[0:1, :] + u * sw_ref[1:2, :]
                 + pad_ref[b, 9:9 + L, :] * sw_ref[2:3, :] + sb_ref[...])
    z = [s[b][:, 0:C] for b in seqs]
    for o in range(2):
        kre, kim = ks_ref[o, 0:L, :], ks_ref[o, L:2 * L, :]
        spec = [_dot(f_ref[...], z[b].astype(BF16)) for b in seqs]
        y = []
        for b in seqs:
            zr, za = spec[b][:L], spec[b][L:]
            y.append(jnp.concatenate([zr * kre + za * kim, za * kre - zr * kim], axis=0))
        conv = [_dot(ft_ref[...], y[b].astype(BF16)) for b in seqs]
        z = [s[b][:, (o + 1) * C:(o + 2) * C] * (conv[b] + hb_ref[o:o + 1, :] * z[b]) for b in seqs]
    for b in seqs:
        o_ref[b] = z[b].astype(BF16)


def _hyena(uh, short_w, short_b, hy_bias, fmat_bf, fmat_t_bf, kspec, *, seq_len, seqs_per_step):
    T = uh.shape[0]
    n_seq = T // seq_len
    L = seq_len
    nb = seqs_per_step
    u3 = uh.reshape(n_seq, L, 3 * D_HYENA)
    out = pl.pallas_call(
        functools.partial(_hyena_kernel, seq_len=L, n_seq=nb),
        out_shape=jax.ShapeDtypeStruct((n_seq, L, D_HYENA), BF16),
        grid=(n_seq // nb,),
        in_specs=[
            pl.BlockSpec((nb, L, 3 * D_HYENA), lambda i: (i, 0, 0)),
            pl.BlockSpec((3, 3 * D_HYENA), lambda i: (0, 0)),
            pl.BlockSpec((1, 3 * D_HYENA), lambda i: (0, 0)),
            pl.BlockSpec((2, D_HYENA), lambda i: (0, 0)),
            pl.BlockSpec((2 * L, L), lambda i: (0, 0)),
            pl.BlockSpec((L, 2 * L), lambda i: (0, 0)),
            pl.BlockSpec((2, 2 * L, D_HYENA), lambda i: (0, 0, 0)),
        ],
        out_specs=pl.BlockSpec((nb, L, D_HYENA), lambda i: (i, 0, 0)),
        scratch_shapes=[pltpu.VMEM((nb, L + 16, 3 * D_HYENA), F32)],
        compiler_params=_cparams("arbitrary"),
        name="hyena",
    )(u3, short_w, short_b.reshape(1, 3 * D_HYENA), hy_bias, fmat_bf, fmat_t_bf, kspec)
    return out.reshape(T, D_HYENA)


def _attn_kernel(*refs, lam_init, has_ctx):
    q_ref, k_ref, v_ref = refs[:3]
    pos = 3
    if has_ctx:
        ck_ref, cv_ref = refs[pos:pos + 2]
        pos += 2
    lam_ref, g_ref, o_ref = refs[pos:pos + 3]

    lv = lam_ref[...]
    lam = (jnp.exp(jnp.sum(lv[0:1] * lv[1:2], axis=-1, keepdims=True))
           - jnp.exp(jnp.sum(lv[2:3] * lv[3:4], axis=-1, keepdims=True)) + lam_init)

    tq = q_ref.shape[0]
    lane = lax.broadcasted_iota(jnp.int32, (tq, HEAD_W), 1)
    dn = (((1,), (1,)), ((), ()))
    group = 2
    for h0 in range(0, N_HEADS, group):
        heads = range(h0, h0 + group)
        sls = [slice(h * HEAD_W, (h + 1) * HEAD_W) for h in heads]
        q2 = []
        for sl in sls:
            q = q_ref[:, sl]
            zero = jnp.zeros_like(q)
            q2.append(jnp.concatenate([jnp.where(lane < DIFF_DK, q, zero),
                                       jnp.where(lane >= DIFF_DK, q, zero)], axis=0))
        s = [lax.dot_general(q2[j], k_ref[:, sl], dn, preferred_element_type=F32)
             for j, sl in enumerate(sls)]
        m = [jnp.max(x, axis=-1, keepdims=True) for x in s]
        if has_ctx:
            sc = [lax.dot_general(q2[j], ck_ref[h].astype(BF16), dn, preferred_element_type=F32)
                  for j, h in enumerate(heads)]
            m = [jnp.maximum(m[j], jnp.max(sc[j], axis=-1, keepdims=True)) for j in range(group)]
        p = [jnp.exp(s[j] - m[j]) for j in range(group)]
        den = [jnp.sum(x, axis=-1, keepdims=True) for x in p]
        acc = [_dot(p[j].astype(BF16), v_ref[:, sl]) for j, sl in enumerate(sls)]
        if has_ctx:
            pc = [jnp.exp(sc[j] - m[j]) for j in range(group)]
            den = [den[j] + jnp.sum(pc[j], axis=-1, keepdims=True) for j in range(group)]
            acc = [acc[j] + _dot(pc[j].astype(BF16), cv_ref[h].astype(BF16))
                   for j, h in enumerate(heads)]
        for j, sl in enumerate(sls):
            both = acc[j] / den[j]
            o = both[:tq] - lam * both[tq:]
            o_ref[:, sl] = (_rms(o, g_ref[...]) * (1.0 - lam_init)).astype(BF16)


def _attention(q, k, v, lam_vecs, subln_g, *, seq_len, layer, ctx=None):
    T = q.shape[0]
    L = seq_len
    n_seq = T // L
    tq = TOKEN_BLOCK
    nq = L // tq
    lam_init = 0.8 - 0.6 * math.exp(-0.3 * layer)
    q3, k3, v3 = (a.reshape(n_seq, L, D_ATTN) for a in (q, k, v))
    in_specs = [
        pl.BlockSpec((None, tq, D_ATTN), lambda b, i: (b, i, 0)),
        pl.BlockSpec((None, L, D_ATTN), lambda b, i: (b, 0, 0)),
        pl.BlockSpec((None, L, D_ATTN), lambda b, i: (b, 0, 0)),
    ]
    args = [q3, k3, v3]
    if ctx is not None:
        past = ctx[0].shape[3]
        in_specs += [pl.BlockSpec((None, None, N_HEADS, past, HEAD_W),
                                  lambda b, i: (b, layer, 0, 0, 0))] * 2
        args += list(ctx)
    in_specs += [
        pl.BlockSpec((4, DIFF_DK), lambda b, i: (0, 0)),
        pl.BlockSpec((1, DIFF_DV), lambda b, i: (0, 0)),
    ]
    args += [lam_vecs, subln_g.reshape(1, DIFF_DV)]
    out = pl.pallas_call(
        functools.partial(_attn_kernel, lam_init=lam_init, has_ctx=ctx is not None),
        out_shape=jax.ShapeDtypeStruct((n_seq, L, D_ATTN), BF16),
        grid=(n_seq, nq),
        in_specs=in_specs,
        out_specs=pl.BlockSpec((None, tq, D_ATTN), lambda b, i: (b, i, 0)),
        compiler_params=_cparams("arbitrary", "arbitrary"),
        name="diff_attention",
    )(*args)
    return out.reshape(T, D_ATTN)


def _split_bf16(a):
    hi = a.astype(BF16)
    lo = (a - hi.astype(F32)).astype(BF16)
    return hi, lo


def _pack_pair(a, b):
    ua = lax.bitcast_convert_type(a.astype(BF16).astype(F32), jnp.uint32)
    ub = lax.bitcast_convert_type(b.astype(BF16).astype(F32), jnp.uint32)
    return (ua >> 16) | ub


def _unpack_pair(w):
    a = lax.bitcast_convert_type(w << 16, F32)
    b = lax.bitcast_convert_type(w & jnp.uint32(0xFFFF0000), F32)
    return a, b


def _postmix_kernel(*refs, stream_blocks):
    ns = len(stream_blocks)
    mod_ref, gpost_ref, gpre_ref, w_ref, rw_ref, x1_ref, h2_ref, lg_ref = refs[4 * ns:]
    i = pl.program_id(0)

    def body(ya_ref, yb_ref, o_ref, x_ref):
        n = x_ref.shape[0]
        halves = [slice(0, n // 2), slice(n // 2, n)]
        mod = mod_ref[...]
        y = [_dot(ya_ref[r, :], w_ref[0:D_CONV, :])
             + _dot(yb_ref[r, :], w_ref[D_CONV:D_CONV + D_HYENA, :])
             + _dot(o_ref[r, :], w_ref[D_CONV + D_HYENA:, :]) for r in halves]
        x1 = [x_ref[r, :] + mod[2:3] * _rms(y[j], gpost_ref[...]) for j, r in enumerate(halves)]
        for j, r in enumerate(halves):
            x1_ref[r, :] = x1[j]
        h2 = [_rms(x1[j], gpre_ref[...]) * (1.0 + mod[4:5]) + mod[3:4] for j in range(2)]
        for j, r in enumerate(halves):
            h2_ref[r, :] = _pack_pair(h2[j][:, :HALF], h2[j][:, HALF:])
        r_hi, r_lo = _split_bf16(rw_ref[...])
        dn = (((1,), (1,)), ((), ()))
        mm = lambda a, b: lax.dot_general(a, b, dn, preferred_element_type=F32)
        for j, r in enumerate(halves):
            h_hi, h_lo = _split_bf16(h2[j])
            lg_ref[:, r] = mm(r_hi, h_hi) + (mm(r_lo, h_hi) + mm(r_hi, h_lo))

    lo = 0
    for j, nbk in enumerate(stream_blocks):
        pl.when((i >= lo) & (i < lo + nbk))(functools.partial(body, *refs[4 * j:4 * j + 4]))
        lo += nbk


def _postmix(parts, mod_l, g_post, g_pre, w_out_bf, router_wt, *, mod_map):
    t_all = sum(p[0].shape[0] for p in parts)
    tm = TOKEN_BLOCK
    full = lambda a: pl.BlockSpec(a.shape, lambda i: (0,) * a.ndim)
    gp, ge = g_post.reshape(1, D_MODEL), g_pre.reshape(1, D_MODEL)
    stream_blocks = tuple(p[0].shape[0] // tm for p in parts)
    in_specs, args = [], []
    lo = 0
    for part, nbk in zip(parts, stream_blocks):
        local = lambda i, lo=lo, nbk=nbk: (jnp.clip(i - lo, 0, nbk - 1), 0)
        in_specs += [pl.BlockSpec((tm, D_CONV), local), pl.BlockSpec((tm, D_HYENA), local),
                     pl.BlockSpec((tm, D_ATTN), local), pl.BlockSpec((tm, D_MODEL), local)]
        args += list(part)
        lo += nbk
    in_specs += [pl.BlockSpec((None, N_MOD, D_MODEL), mod_map),
                 full(gp), full(ge), full(w_out_bf), full(router_wt)]
    args += [mod_l, gp, ge, w_out_bf, router_wt]
    return pl.pallas_call(
        functools.partial(_postmix_kernel, stream_blocks=stream_blocks),
        out_shape=[
            jax.ShapeDtypeStruct((t_all, D_MODEL), F32),
            jax.ShapeDtypeStruct((t_all, HALF), jnp.uint32),
            jax.ShapeDtypeStruct((N_EXPERTS, t_all), F32),
        ],
        grid=(t_all // tm,),
        in_specs=in_specs,
        out_specs=[pl.BlockSpec((tm, D_MODEL), lambda i: (i, 0)),
                   pl.BlockSpec((tm, HALF), lambda i: (i, 0)),
                   pl.BlockSpec((N_EXPERTS, tm), lambda i: (0, i))],
        compiler_params=_cparams("arbitrary"),
        name="postmix",
    )(*args)


def _first_max(vals, idx, sentinel):
    m = jnp.max(vals, axis=0, keepdims=True)
    i = jnp.min(jnp.where(vals == m, idx, sentinel), axis=0, keepdims=True)
    return m, i


def _route_topk_kernel(lg_ref, rb_ref, eidx_ref, w_ref, cnt_ref):
    step = pl.program_id(0)
    per = N_EXPERTS // N_GROUPS
    n = lg_ref.shape[1]
    s = jax.nn.sigmoid(lg_ref[...])
    sel = s + rb_ref[...]
    sub = lax.broadcasted_iota(jnp.int32, (per, n), 0)
    neg = jnp.float32(-jnp.inf)

    gs = []
    for g in range(N_GROUPS):
        xg = sel[g * per:(g + 1) * per]
        total = jnp.zeros((1, n), F32)
        for _ in range(GROUP_SCORE_TOP):
            m, i = _first_max(xg, sub, per)
            total = total + m
            xg = jnp.where(sub == i, neg, xg)
        gs.append(total)
    gscore = jnp.concatenate(gs, axis=0)
    gkeep = jnp.zeros((N_GROUPS, n), F32)
    for _ in range(TOPK_GROUPS):
        _, gi = _first_max(gscore, sub, N_GROUPS)
        hit = sub == gi
        gkeep = jnp.where(hit, 1.0, gkeep)
        gscore = jnp.where(hit, neg, gscore)

    eio = [sub + g * per for g in range(N_GROUPS)]
    sg = [s[g * per:(g + 1) * per] for g in range(N_GROUPS)]
    mk = [jnp.where(gkeep[g:g + 1] > 0.5, sel[g * per:(g + 1) * per], neg) for g in range(N_GROUPS)]
    oh = [jnp.zeros((per, n), F32) for _ in range(N_GROUPS)]
    idx_rows, s_rows = [], []
    for _ in range(TOP_K):
        best = functools.reduce(jnp.maximum, mk)
        m = jnp.max(best, axis=0, keepdims=True)
        cand = functools.reduce(jnp.minimum,
                                [jnp.where(mk[g] == m, eio[g], N_EXPERTS) for g in range(N_GROUPS)])
        idx = jnp.min(cand, axis=0, keepdims=True)
        acc = jnp.zeros((per, n), F32)
        for g in range(N_GROUPS):
            hit = eio[g] == idx
            acc = acc + jnp.where(hit, sg[g], 0.0)
            mk[g] = jnp.where(hit, neg, mk[g])
            oh[g] = jnp.where(hit, 1.0, oh[g])
        idx_rows.append(idx)
        s_rows.append(jnp.sum(acc, axis=0, keepdims=True))
    eidx_ref[...] = jnp.concatenate(idx_rows, axis=0)
    sk = jnp.concatenate(s_rows, axis=0)
    w_ref[...] = sk / jnp.sum(sk, axis=0, keepdims=True) * ROUTED_SCALE

    @pl.when(step == 0)
    def _():
        cnt_ref[...] = jnp.zeros_like(cnt_ref)

    tot8 = jnp.zeros((N_EXPERTS, 1), F32)
    for w0 in range(0, n, TOKEN_BLOCK):
        tot = jnp.concatenate([jnp.sum(o[:, w0:w0 + TOKEN_BLOCK], axis=1, keepdims=True)
                               for o in oh], axis=0)
        tot8 = tot8 + _ceil_tile(tot)
    cnt_ref[...] = cnt_ref[...] + jnp.broadcast_to(tot8, cnt_ref.shape).astype(jnp.int32)


def _ceil_tile(count):
    return jnp.floor((count + (SUBLANES - 1.0)) * (1.0 / SUBLANES)) * SUBLANES


def _route_topk(logits_t, router_b):
    t_all = logits_t.shape[1]
    rb = 2 * TOKEN_BLOCK if t_all % (2 * TOKEN_BLOCK) == 0 else TOKEN_BLOCK
    return pl.pallas_call(
        _route_topk_kernel,
        out_shape=[
            jax.ShapeDtypeStruct((TOP_K, t_all), jnp.int32),
            jax.ShapeDtypeStruct((TOP_K, t_all), F32),
            jax.ShapeDtypeStruct((N_EXPERTS, LANES), jnp.int32),
        ],
        grid=(t_all // rb,),
        in_specs=[pl.BlockSpec((N_EXPERTS, rb), lambda i: (0, i)),
                  pl.BlockSpec((N_EXPERTS, 1), lambda i: (0, 0))],
        out_specs=[pl.BlockSpec((TOP_K, rb), lambda i: (0, i)),
                   pl.BlockSpec((TOP_K, rb), lambda i: (0, i)),
                   pl.BlockSpec((N_EXPERTS, LANES), lambda i: (0, 0))],
        compiler_params=_cparams("arbitrary"),
        name="route_topk",
    )(logits_t, router_b.reshape(N_EXPERTS, 1))


def _route_dest_kernel(eidx_ref, ps_ref, lpos_ref, cs_ref, cn_ref, cd_ref, carry_ref):
    step = pl.program_id(0)
    per = N_EXPERTS // N_GROUPS
    n = eidx_ref.shape[1]

    @pl.when(step == 0)
    def _():
        carry_ref[...] = jnp.zeros_like(carry_ref)
        cs_ref[...] = jnp.zeros_like(cs_ref)
        cn_ref[...] = jnp.zeros_like(cn_ref)
        cd_ref[...] = jnp.zeros_like(cd_ref)

    eidx = eidx_ref[...]
    sub = lax.broadcasted_iota(jnp.int32, (per, n), 0)
    rows = []
    for g in range(N_GROUPS):
        eio = sub + g * per
        o = jnp.zeros((per, n), F32)
        for k in range(TOP_K):
            o = jnp.where(eio == eidx[k:k + 1], 1.0, o)
        rows.append(o)
    onehot = jnp.concatenate(rows, axis=0)
    r = lax.broadcasted_iota(jnp.int32, (n, n), 0)
    c = lax.broadcasted_iota(jnp.int32, (n, n), 1)
    before = jnp.where(r < c, 1.0, 0.0).astype(BF16)
    rank = _dot(onehot.astype(BF16), before)
    size = _ceil_tile(jnp.sum(onehot, axis=1, keepdims=True))
    er = lax.broadcasted_iota(jnp.int32, (N_EXPERTS, N_EXPERTS), 0)
    ec = lax.broadcasted_iota(jnp.int32, (N_EXPERTS, N_EXPERTS), 1)
    lower = jnp.where(ec < er, 1.0, 0.0).astype(BF16)
    tiles = jnp.broadcast_to(size * (1.0 / SUBLANES), (N_EXPERTS, LANES)).astype(BF16)
    start = _dot(lower, tiles)[:, 0:1] * SUBLANES
    base = (rank + start).astype(jnp.int32)
    out = []
    for k in range(TOP_K):
        acc = jnp.zeros((per, n), jnp.int32)
        for g in range(N_GROUPS):
            acc = acc + jnp.where(sub + g * per == eidx[k:k + 1], base[g * per:(g + 1) * per], 0)
        out.append(jnp.sum(acc, axis=0, keepdims=True))
    lpos_ref[...] = jnp.concatenate(out, axis=0)

    col = lax.broadcasted_iota(jnp.int32, cs_ref.shape, 1) == step
    carry = carry_ref[:, 0:1]
    cs_ref[...] = jnp.where(col, start.astype(jnp.int32), cs_ref[...])
    cn_ref[...] = jnp.where(col, size.astype(jnp.int32), cn_ref[...])
    cd_ref[...] = jnp.where(col, carry.astype(jnp.int32) + ps_ref[...], cd_ref[...])
    carry_ref[...] = carry_ref[...] + jnp.broadcast_to(size, carry_ref.shape)


def _route_dest(eidx_t, pstart):
    t_all = eidx_t.shape[1]
    rb = TOKEN_BLOCK
    n_win = t_all // rb
    cols = -(-n_win // LANES) * LANES
    table = jax.ShapeDtypeStruct((N_EXPERTS, cols), jnp.int32)
    table_spec = pl.BlockSpec((N_EXPERTS, cols), lambda i: (0, 0))
    lpos, cs, cn, cd = pl.pallas_call(
        _route_dest_kernel,
        out_shape=[jax.ShapeDtypeStruct((TOP_K, t_all), jnp.int32), table, table, table],
        grid=(n_win,),
        in_specs=[pl.BlockSpec((TOP_K, rb), lambda i: (0, i)),
                  pl.BlockSpec((N_EXPERTS, 1), lambda i: (0, 0))],
        out_specs=[pl.BlockSpec((TOP_K, rb), lambda i: (0, i)), table_spec, table_spec, table_spec],
        scratch_shapes=[pltpu.VMEM((N_EXPERTS, LANES), F32)],
        compiler_params=_cparams("arbitrary"),
        name="route_dest",
    )(eidx_t, pstart.reshape(N_EXPERTS, 1))
    flat = lambda a: a[:, :n_win].T.reshape(-1)
    return lpos, (flat(cs), flat(cn), flat(cd))


def _wait_rows(hbm, sem, n_rows):
    pltpu.make_async_copy(hbm.at[pl.ds(0, n_rows)], hbm.at[pl.ds(0, n_rows)], sem).wait()


def _window_chunks(tables, w, local_ref, hbm, sem, *, to_hbm, spare_row):
    cs_ref, cn_ref, cd_ref = tables
    tile = lambda v: pl.multiple_of(v, SUBLANES)

    def copy(local_row, hbm_row, size):
        lo = local_ref.at[pl.ds(tile(local_row), tile(size))]
        hi = hbm.at[pl.ds(tile(hbm_row), tile(size))]
        src, dst = (lo, hi) if to_hbm else (hi, lo)
        pltpu.make_async_copy(src, dst, sem).start()

    def per_expert(e, carry):
        i = w * N_EXPERTS + e
        copy(cs_ref[i], cd_ref[i], cn_ref[i])
        return carry

    lax.fori_loop(0, N_EXPERTS, per_expert, 0, unroll=8)
    last = w * N_EXPERTS + (N_EXPERTS - 1)
    used = cs_ref[last] + cn_ref[last]
    copy(used, spare_row, LOCAL_ROWS - used)


def _zero_padding(cnt_ref, ps_ref, nu_ref, zero_ref, x_hbm, zsem, n_blocks, *, wait):
    def copy(row, size):
        cp = pltpu.make_async_copy(zero_ref.at[pl.ds(0, size)], x_hbm.at[pl.ds(row, size)], zsem)
        if wait:
            cp.wait()
        else:
            cp.start()

    def per_expert(e, carry):
        cnt = cnt_ref[e]
        pad = (EXPERT_BLOCK - cnt % EXPERT_BLOCK) % EXPERT_BLOCK
        copy(pl.multiple_of(ps_ref[e] + cnt, SUBLANES), pl.multiple_of(pad, SUBLANES))
        return carry

    lax.fori_loop(0, N_EXPERTS, per_expert, 0)

    def per_block(j, carry):
        copy(pl.multiple_of(j * EXPERT_BLOCK, EXPERT_BLOCK), EXPERT_BLOCK)
        return carry

    lax.fori_loop(nu_ref[0], n_blocks, per_block, 0)


def _dispatch_kernel(cs_ref, cn_ref, cd_ref, cnt_ref, ps_ref, nu_ref, lpos_ref, h_ref, x_hbm,
                     gbuf, zero_ref, sem, zsem, ssem):
    w = pl.program_id(0)
    nsteps = pl.num_programs(0)
    slot = w % 2
    n_blocks = (x_hbm.shape[0] - 2 * SPARE_ROWS) // EXPERT_BLOCK
    spare0 = n_blocks * EXPERT_BLOCK
    zero_args = (cnt_ref, ps_ref, nu_ref, zero_ref, x_hbm, zsem, n_blocks)

    def spare_zero(j):
        return pltpu.make_async_copy(
            zero_ref, x_hbm.at[pl.ds(spare0 + j * EXPERT_BLOCK, EXPERT_BLOCK)], ssem)

    @pl.when(w == 0)
    def _():
        zero_ref[...] = jnp.zeros_like(zero_ref)
        _zero_padding(*zero_args, wait=False)
        for j in range(2 * SPARE_ROWS // EXPERT_BLOCK):
            spare_zero(j).start()

    ha, hb = _unpack_pair(h_ref[...])
    ha, hb = ha.astype(BF16), hb.astype(BF16)
    lp = lpos_ref[...]
    n = h_ref.shape[0]
    last = w * N_EXPERTS + (N_EXPERTS - 1)
    used = cs_ref[last] + cn_ref[last]

    lp = lp.astype(jnp.int16)
    one = jnp.ones((n, n), BF16)

    def permute(c):
        row = (lax.broadcasted_iota(jnp.int32, (n, n), 0) + c * n).astype(jnp.int16)
        pb = jnp.zeros((n, n), BF16)
        for k in range(TOP_K):
            pb = jnp.where(row == lp[k:k + 1, :], one, pb)
        ga = lax.bitcast_convert_type(_dot(pb, ha), jnp.uint32)
        gb = lax.bitcast_convert_type(_dot(pb, hb), jnp.uint32)
        gbuf[slot, c * n:(c + 1) * n, :] = (ga >> 16) | gb

    def blank(c):
        gbuf[slot, c * n:(c + 1) * n, :] = jnp.zeros((n, HALF), jnp.uint32)

    for c in range(LOCAL_ROWS // n):
        if c < TOP_K:
            permute(c)
        else:
            pl.when(c * n < used)(functools.partial(permute, c))
            pl.when(c * n >= used)(functools.partial(blank, c))

    @pl.when(w == 0)
    def _():
        for j in range(2 * SPARE_ROWS // EXPERT_BLOCK):
            spare_zero(j).wait()

    _window_chunks((cs_ref, cn_ref, cd_ref), w, gbuf.at[slot], x_hbm, sem.at[slot],
                   to_hbm=True, spare_row=spare0 + slot * SPARE_ROWS)

    @pl.when(w > 0)
    def _():
        _wait_rows(x_hbm, sem.at[1 - slot], LOCAL_ROWS)

    @pl.when(w == nsteps - 1)
    def _():
        _wait_rows(x_hbm, sem.at[slot], LOCAL_ROWS)
        _zero_padding(*zero_args, wait=True)


def _dispatch(tables, lpos, counts, pstart, n_used, h2p, n_blocks):
    t_all = h2p.shape[0]
    sb = TOKEN_BLOCK
    smem = lambda: pl.BlockSpec(memory_space=pltpu.SMEM)
    return pl.pallas_call(
        _dispatch_kernel,
        out_shape=jax.ShapeDtypeStruct((n_blocks * EXPERT_BLOCK + 2 * SPARE_ROWS, HALF), jnp.uint32),
        grid=(t_all // sb,),
        in_specs=[smem(), smem(), smem(), smem(), smem(), smem(),
                  pl.BlockSpec((TOP_K, sb), lambda i: (0, i)),
                  pl.BlockSpec((sb, HALF), lambda i: (i, 0))],
        out_specs=pl.BlockSpec(memory_space=pl.ANY),
        scratch_shapes=[pltpu.VMEM((2, LOCAL_ROWS, HALF), jnp.uint32),
                        pltpu.VMEM((EXPERT_BLOCK, HALF), jnp.uint32),
                        pltpu.SemaphoreType.DMA((2,)), pltpu.SemaphoreType.DMA,
                        pltpu.SemaphoreType.DMA],
        compiler_params=_cparams("arbitrary"),
        name="dispatch",
    )(*tables, counts, pstart, n_used, lpos, h2p)


def _expert_kernel(nb_ref, ps_ref, nu_ref, wg_ref, wu_ref, wd_ref, x_hbm, y_hbm,
                   wg_bf, wu_bf, wd_bf, xbuf, ybuf, xsem, ysem):
    e = pl.program_id(0)
    nblk = nb_ref[e]
    first = ps_ref[e] // EXPERT_BLOCK
    n_used = nu_ref[0]
    xs, ys = xbuf.shape[0], ybuf.shape[0]
    wg_bf[...] = wg_ref[...].astype(BF16)
    wu_bf[...] = wu_ref[...].astype(BF16)
    wd_bf[...] = wd_ref[...].astype(BF16)

    def rows(b):
        return pl.ds(pl.multiple_of(b * EXPERT_BLOCK, EXPERT_BLOCK), EXPERT_BLOCK)

    def x_copy(b):
        return pltpu.make_async_copy(x_hbm.at[rows(b)], xbuf.at[b % xs], xsem.at[b % xs])

    def y_copy(b):
        return pltpu.make_async_copy(ybuf.at[b % ys], y_hbm.at[rows(b)], ysem.at[b % ys])

    @pl.when(e == 0)
    def _():
        for b in range(xs - 1):
            pl.when(b < n_used)(lambda b=b: x_copy(b).start())

    def block(j, carry):
        b = first + j
        x_copy(b).wait()
        pl.when(b + xs - 1 < n_used)(lambda: x_copy(b + xs - 1).start())
        pl.when(b >= ys)(lambda: y_copy(b - ys).wait())
        y = _swiglu_packed(xbuf[b % xs], wg_bf, wu_bf, wd_bf)
        ybuf[b % ys] = _pack_pair(y[:, :HALF], y[:, HALF:])
        y_copy(b).start()
        return carry

    lax.fori_loop(0, nblk, block, 0)

    @pl.when(e == pl.num_programs(0) - 1)
    def _():
        for d in range(ys):
            b = n_used - ys + d
            pl.when(b >= 0)(lambda b=b: y_copy(b).wait())
        n_blocks = y_hbm.shape[0] // EXPERT_BLOCK
        ybuf[0] = jnp.zeros_like(ybuf[0])

        def tail(j, carry, wait):
            cp = pltpu.make_async_copy(
                ybuf.at[0], y_hbm.at[pl.ds(pl.multiple_of(j * EXPERT_BLOCK, EXPERT_BLOCK),
                                           EXPERT_BLOCK)], ysem.at[0])
            cp.wait() if wait else cp.start()
            return carry

        lax.fori_loop(nu_ref[0], n_blocks, functools.partial(tail, wait=False), 0)
        lax.fori_loop(nu_ref[0], n_blocks, functools.partial(tail, wait=True), 0)


def _swiglu_packed(xp, wg_ref, wu_ref, wd_ref):
    xa, xb = _unpack_pair(xp)
    xa, xb = xa.astype(BF16), xb.astype(BF16)
    gate = _dot(xa, wg_ref[0:HALF, :]) + _dot(xb, wg_ref[HALF:, :])
    up = _dot(xa, wu_ref[0:HALF, :]) + _dot(xb, wu_ref[HALF:, :])
    return _dot((_silu(gate) * up).astype(BF16), wd_ref[...])


def _experts(x_buf, nblk, pstart, n_used, wg, wu, wd, *, layer):
    grid_spec = pltpu.PrefetchScalarGridSpec(
        num_scalar_prefetch=3,
        grid=(N_EXPERTS,),
        in_specs=[
            pl.BlockSpec((None, None, D_MODEL, D_EXPERT), lambda e, *_: (layer, e, 0, 0)),
            pl.BlockSpec((None, None, D_MODEL, D_EXPERT), lambda e, *_: (layer, e, 0, 0)),
            pl.BlockSpec((None, None, D_EXPERT, D_MODEL), lambda e, *_: (layer, e, 0, 0)),
            pl.BlockSpec(memory_space=pl.ANY),
        ],
        out_specs=pl.BlockSpec(memory_space=pl.ANY),
        scratch_shapes=[
            pltpu.VMEM((D_MODEL, D_EXPERT), BF16),
            pltpu.VMEM((D_MODEL, D_EXPERT), BF16),
            pltpu.VMEM((D_EXPERT, D_MODEL), BF16),
            pltpu.VMEM((EXPERT_IN_SLOTS, EXPERT_BLOCK, HALF), jnp.uint32),
            pltpu.VMEM((EXPERT_OUT_SLOTS, EXPERT_BLOCK, HALF), jnp.uint32),
            pltpu.SemaphoreType.DMA((EXPERT_IN_SLOTS,)),
            pltpu.SemaphoreType.DMA((EXPERT_OUT_SLOTS,)),
        ],
    )
    return pl.pallas_call(
        _expert_kernel,
        out_shape=jax.ShapeDtypeStruct(x_buf.shape, jnp.uint32),
        grid_spec=grid_spec,
        compiler_params=_cparams("arbitrary"),
        name="experts",
    )(nblk, pstart, n_used, wg, wu, wd, x_buf)


def _block_plan(counts, n_blocks):
    nblk = (counts + EXPERT_BLOCK - 1) // EXPERT_BLOCK
    pend = jnp.cumsum(nblk)
    pstart = ((pend - nblk) * EXPERT_BLOCK).astype(jnp.int32)
    return pstart, nblk.astype(jnp.int32), pend[-1:].astype(jnp.int32)


def _ffn_out_kernel(*refs, stream_blocks):
    (cs_ref, cn_ref, cd_ref, lpos_ref, h_ref, w_ref, x_ref, mod_ref, g_ref, wg_ref, wu_ref, wd_ref,
     y_hbm) = refs[:13]
    ns = len(stream_blocks)
    out_refs = refs[13:13 + ns]
    rows_ref, sem = refs[13 + ns:]
    step = pl.program_id(0)
    nsteps = pl.num_programs(0)
    n = h_ref.shape[0]
    slot = step % 2

    def gather(w, to_slot):
        _window_chunks((cs_ref, cn_ref, cd_ref), w, rows_ref.at[to_slot], y_hbm, sem.at[to_slot],
                       to_hbm=False, spare_row=0)

    pl.when(step == 0)(functools.partial(gather, 0, 0))
    for parity in range(2):
        pl.when((step + 1 < nsteps) & (slot == parity))(
            functools.partial(gather, step + 1, 1 - parity))

    ys = _swiglu_packed(h_ref[...], wg_ref, wu_ref, wd_ref)
    _wait_rows(y_hbm, sem.at[slot], LOCAL_ROWS)
    w = w_ref[...].astype(BF16)
    lp = lpos_ref[...].astype(jnp.int16)
    ya, yb = ys[:, :HALF], ys[:, HALF:]
    for c in range(LOCAL_ROWS // n):
        col = (lax.broadcasted_iota(jnp.int32, (n, n), 1) + c * n).astype(jnp.int16)
        pw = jnp.zeros((n, n), BF16)
        for k in range(TOP_K):
            pw = jnp.where(col == lp[:, k:k + 1], jnp.broadcast_to(w[:, k:k + 1], (n, n)), pw)
        ra, rb = _unpack_pair(rows_ref[slot, c * n:(c + 1) * n, :])
        ya = ya + _dot(pw, ra.astype(BF16))
        yb = yb + _dot(pw, rb.astype(BF16))
    ms = (jnp.sum(ya * ya, axis=-1, keepdims=True)
          + jnp.sum(yb * yb, axis=-1, keepdims=True)) * (1.0 / D_MODEL)
    inv = lax.rsqrt(ms + EPS)
    mod = mod_ref[...]
    g = g_ref[...]
    out_a = x_ref[:, 0:HALF] + mod[5:6, 0:HALF] * (ya * inv * g[:, 0:HALF])
    out_b = x_ref[:, HALF:] + mod[5:6, HALF:] * (yb * inv * g[:, HALF:])

    def store(o_ref):
        o_ref[:, 0:HALF] = out_a
        o_ref[:, HALF:] = out_b

    lo = 0
    for o_ref, nbk in zip(out_refs, stream_blocks):
        pl.when((step >= lo) & (step < lo + nbk))(functools.partial(store, o_ref))
        lo += nbk


def _ffn_out(tables, lpos_rows, h2p, w_rows, x1, mod_l, g_post, swg, swu, swd, y_buf, *, mod_map,
             stream_tokens):
    t_all = x1.shape[0]
    tm = TOKEN_BLOCK
    nsteps = t_all // tm
    row = lambda w: pl.BlockSpec((tm, w), lambda i: (i, 0))
    full = lambda a: pl.BlockSpec(a.shape, lambda i: (0,) * a.ndim)
    smem = lambda: pl.BlockSpec(memory_space=pltpu.SMEM)
    g = g_post.reshape(1, D_MODEL)
    stream_blocks = tuple(t // tm for t in stream_tokens)
    out_shape, out_specs = [], []
    lo = 0
    for t, nbk in zip(stream_tokens, stream_blocks):
        out_shape.append(jax.ShapeDtypeStruct((t, D_MODEL), F32))
        out_specs.append(pl.BlockSpec((tm, D_MODEL),
                                      lambda i, lo=lo, nbk=nbk: (jnp.clip(i - lo, 0, nbk - 1), 0)))
        lo += nbk
    return pl.pallas_call(
        functools.partial(_ffn_out_kernel, stream_blocks=stream_blocks),
        out_shape=out_shape,
        grid=(nsteps,),
        in_specs=[smem(), smem(), smem(),
                  row(TOP_K), row(HALF), row(TOP_K), row(D_MODEL),
                  pl.BlockSpec((None, N_MOD, D_MODEL), mod_map),
                  full(g), full(swg), full(swu), full(swd),
                  pl.BlockSpec(memory_space=pl.ANY)],
        out_specs=out_specs,
        scratch_shapes=[pltpu.VMEM((2, LOCAL_ROWS, HALF), jnp.uint32),
                        pltpu.SemaphoreType.DMA((2,))],
        compiler_params=_cparams("arbitrary"),
        name="ffn_out",
    )(*tables, lpos_rows, h2p, w_rows, x1, mod_l, g, swg, swu, swd, y_buf)


def _rope_tables(L):
    rows = L // GRID_W
    row = jnp.repeat(jnp.arange(rows), GRID_W).astype(F32)
    col = jnp.tile(jnp.arange(GRID_W), rows).astype(F32)
    n = DIFF_DK // 4
    inv = ROPE_BASE ** (-jnp.arange(n, dtype=F32) / n)
    ar, ac = row[:, None] * inv[None, :], col[:, None] * inv[None, :]
    cos64 = jnp.concatenate([jnp.cos(ar), jnp.cos(ar), jnp.cos(ac), jnp.cos(ac)], axis=-1)
    sin64 = jnp.concatenate([-jnp.sin(ar), jnp.sin(ar), -jnp.sin(ac), jnp.sin(ac)], axis=-1)
    return jnp.tile(cos64, (1, 2)), jnp.tile(sin64, (1, 2))


def kernel(x_prompt, x_sample, cache_k, cache_v, c, c_ctx, w_mod, b_mod, g_pre_mix, g_post_mix, g_pre_ffn, g_post_ffn, w_in, w_out, conv_w, conv_b, conv_ln_g, conv_ln_b, hy_short_w, hy_short_b, hy_w1, hy_b1, hy_w2, hy_b2, hy_freq, hy_w3, hy_bias, lam_q1, lam_k1, lam_q2, lam_k2, subln_g, router_w, router_b, exp_w_gate, exp_w_up, exp_w_down, sh_w_gate, sh_w_up, sh_w_down):
    n_p, l_p, _ = x_prompt.shape
    n_s, l_s, _ = x_sample.shape
    t_p, t_s = n_p * l_p, n_s * l_s

    cond = jnp.zeros((COND_ROWS, D_MODEL), F32).at[0].set(c_ctx).at[1:1 + n_s].set(c)
    mod = _modulation(cond, w_mod, b_mod).reshape(DEPTH, COND_ROWS, N_MOD, D_MODEL)
    rope = _rope_tables(l_s)

    t_all = t_p + t_s
    streams = {
        "p": dict(x=x_prompt.reshape(t_p, D_MODEL), T=t_p, L=l_p, row0=0, per_seq=False, hy_nb=8),
        "s": dict(x=x_sample.reshape(t_s, D_MODEL), T=t_s, L=l_s, row0=1, per_seq=True,
                  hy_nb=2 if n_s % 2 == 0 else 1),
    }
    for st in streams.values():
        fwd, fwd_t = _dft_matrices(st["L"])
        feats, window = _filter_tables(st["L"])
        st["fmat32"] = jnp.asarray(fwd)
        st["fmat"] = jnp.asarray(fwd).astype(BF16)
        st["fmat_t"] = jnp.asarray(fwd_t).astype(BF16)
        st["filter_tables"] = (jnp.asarray(feats), jnp.asarray(window))
    nb_p, nb_s = t_p // TOKEN_BLOCK, l_s // TOKEN_BLOCK
    merged_mod_map = lambda i: (jnp.where(i < nb_p, 0, 1 + (i - nb_p) // nb_s), 0, 0)
    n_blocks = (t_all // TOKEN_BLOCK) * (LOCAL_ROWS // EXPERT_BLOCK) + N_EXPERTS

    caches = None
    for l in range(DEPTH):
        w_in_bf = w_in[l].astype(BF16)
        w_out_bf = w_out[l].astype(BF16)
        swg, swu, swd = (a[l].astype(BF16) for a in (sh_w_gate, sh_w_up, sh_w_down))
        lam_vecs = jnp.stack([lam_q1[l], lam_k1[l], lam_q2[l], lam_k2[l]], axis=0)
        filt_params = (hy_w1[l], hy_b1[l], hy_w2[l], hy_b2[l], hy_freq[l], hy_w3[l])
        parts = []
        for name, st in streams.items():
            L = st["L"]
            kw = dict(seq_len=L, cond_row0=st["row0"], per_seq_cond=st["per_seq"])
            if name == "p":
                uc, uh, q, k, v, ck, cv = _premix(st["x"], mod[l], g_pre_mix[l], w_in_bf, layer=l,
                                                  n_tokens=st["T"], tok0=0,
                                                  caches=caches, want_cache=True, **kw)
                caches = (ck, cv)
                ctx = None
            else:
                uc, uh, q, k, v = _premix(st["x"], mod[l], g_pre_mix[l], w_in_bf, layer=l,
                                          n_tokens=st["T"], tok0=0, rope_tables=rope, **kw)
                ctx = (cache_k, cache_v)
            ya = _conformer(uc, conv_w[l], conv_b[l], conv_ln_g[l], conv_ln_b[l], seq_len=L)
            kspec = _hyena_spectrum(L, st["fmat32"], *st["filter_tables"], *filt_params)
            yb = _hyena(uh, hy_short_w[l], hy_short_b[l], hy_bias[l], st["fmat"], st["fmat_t"],
                        kspec, seq_len=L, seqs_per_step=st["hy_nb"])
            o = _attention(q, k, v, lam_vecs, subln_g[l], seq_len=L, layer=l, ctx=ctx)
            parts.append((ya, yb, o, st["x"]))
        x1, h2p, logits_t = _postmix(parts, mod[l], g_post_mix[l], g_pre_ffn[l], w_out_bf,
                                     router_w[l].T, mod_map=merged_mod_map)

        eidx_t, w_t, cnt = _route_topk(logits_t, router_b[l])
        counts = cnt[:, 0]
        pstart, nblk, n_used = _block_plan(counts, n_blocks)
        lpos, tables = _route_dest(eidx_t, pstart)
        x_buf = _dispatch(tables, lpos, counts, pstart, n_used, h2p, n_blocks)
        y_buf = _experts(x_buf, nblk, pstart, n_used, exp_w_gate, exp_w_up, exp_w_down, layer=l)
        outs = _ffn_out(tables, lpos.T, h2p, w_t.T, x1, mod[l], g_post_ffn[l], swg, swu, swd,
                        y_buf, mod_map=merged_mod_map, stream_tokens=(t_p, t_s))
        streams["p"]["x"], streams["s"]["x"] = outs

    y_prompt = streams["p"]["x"].reshape(n_p, l_p, D_MODEL)
    y_sample = streams["s"]["x"].reshape(n_s, l_s, D_MODEL)
    return (y_prompt, y_sample, caches[0], caches[1])
```

```python
import functools
import math

import numpy as np
import jax
import jax.numpy as jnp
from jax import lax
from jax.experimental import pallas as pl
from jax.experimental.pallas import tpu as pltpu

F32 = jnp.float32
BF16 = jnp.bfloat16

D_MODEL = 1024
DEPTH = 2
GRID_W = 64
D_CONV = 256
D_HYENA = 256
D_ATTN = 512
DIFF_DK = 64
N_HEADS = 4
DIFF_DV = 128
HEAD_W = 2 * DIFF_DK
ROPE_CHUNK = DIFF_DK // 4
IN_COLS = 2 * D_CONV + 3 * D_HYENA + 3 * D_ATTN
COL_HY = 2 * D_CONV
COL_Q = COL_HY + 3 * D_HYENA
COL_K = COL_Q + D_ATTN
COL_V = COL_K + D_ATTN
CONV_WIDTH = 31
CONV_PAD = 16
CONV_ROWS = 256
SUBLANES = 8
HYENA_EMB = 17
FEAT_PAD = 32
HYENA_BANDS = 8
HYENA_MIN_DECAY = math.log(1e-2) / 1.5
HYENA_MAX_DECAY = math.log(1e-2) / 0.3
N_EXPERTS = 64
TOP_K = 8
N_GROUPS = 8
TOPK_GROUPS = 4
GROUP_SCORE_TOP = 2
D_EXPERT = 256
ROUTED_SCALE = 2.5
ROPE_BASE = 10000.0
EPS = 1e-6
N_MOD = 6
COND_ROWS = 16
MOD_COL_BLOCK = 1536

HALF = D_MODEL // 2
LANES = 128
TOKEN_BLOCK = 256
SPARE_ROWS = N_EXPERTS * SUBLANES
LOCAL_ROWS = TOKEN_BLOCK * TOP_K + SPARE_ROWS
EXPERT_BLOCK = 512
EXPERT_IN_SLOTS = 4
EXPERT_OUT_SLOTS = 3
VMEM_LIMIT = 48 * 1024 * 1024


def _cparams(*sem):
    return pltpu.CompilerParams(dimension_semantics=sem, vmem_limit_bytes=VMEM_LIMIT)


def _rms(x, g):
    return x * lax.rsqrt(jnp.mean(x * x, axis=-1, keepdims=True) + EPS) * g


def _silu(x):
    return x * jax.nn.sigmoid(x)


def _dot(a, b):
    return jnp.dot(a, b, preferred_element_type=F32)


def _mod_kernel(c_ref, w_ref, b_ref, o_ref):
    s = _silu(c_ref[...])
    o_ref[...] = _dot(s.astype(BF16), w_ref[...].astype(BF16)) + b_ref[...]


def _modulation(cond, w_mod, b_mod):
    cols = N_MOD * D_MODEL
    cb = MOD_COL_BLOCK
    return pl.pallas_call(
        _mod_kernel,
        out_shape=jax.ShapeDtypeStruct((DEPTH, COND_ROWS, cols), F32),
        grid=(DEPTH, cols // cb),
        in_specs=[
            pl.BlockSpec((COND_ROWS, D_MODEL), lambda l, j: (0, 0)),
            pl.BlockSpec((None, D_MODEL, cb), lambda l, j: (l, 0, j)),
            pl.BlockSpec((None, 1, cb), lambda l, j: (l, 0, j)),
        ],
        out_specs=pl.BlockSpec((None, COND_ROWS, cb), lambda l, j: (l, 0, j)),
        compiler_params=_cparams("arbitrary", "arbitrary"),
        name="modulation",
    )(cond, w_mod, b_mod.reshape(DEPTH, 1, cols))


def _rope_slab(x, cos, sin, lo_mask):
    swapped = jnp.where(lo_mask, pltpu.roll(x, HEAD_W - ROPE_CHUNK, axis=1),
                        pltpu.roll(x, ROPE_CHUNK, axis=1))
    return x * cos + swapped * sin


def _premix_kernel(*refs, rope, cache, layer):
    x_ref, mod_ref, g_ref, w_ref = refs[:4]
    pos = 4
    if rope:
        cos_ref, sin_ref = refs[pos:pos + 2]
        pos += 2
    if cache == "alias":
        pos += 2
    uc_ref, uh_ref, q_ref, k_ref, v_ref = refs[pos:pos + 5]
    pos += 5
    if cache:
        ck_ref, cv_ref = refs[pos:pos + 2]

    x = x_ref[...]
    mod = mod_ref[...]
    h = (_rms(x, g_ref[...]) * (1.0 + mod[1:2]) + mod[0:1]).astype(BF16)
    uc_ref[...] = _dot(h, w_ref[:, 0:COL_HY]).astype(BF16)
    uh_ref[...] = _dot(h, w_ref[:, COL_HY:COL_Q]).astype(BF16)
    uq = _dot(h, w_ref[:, COL_Q:COL_K]) * (DIFF_DK ** -0.5)
    uk = _dot(h, w_ref[:, COL_K:COL_V])
    uv = _dot(h, w_ref[:, COL_V:IN_COLS])
    if rope:
        cos = cos_ref[...]
        sin = sin_ref[...]
        lane = lax.broadcasted_iota(jnp.int32, cos.shape, 1)
        lo_mask = (lane % (2 * ROPE_CHUNK)) < ROPE_CHUNK
        for hh in range(N_HEADS):
            sl = slice(hh * HEAD_W, (hh + 1) * HEAD_W)
            q_ref[:, sl] = _rope_slab(uq[:, sl], cos, sin, lo_mask).astype(BF16)
            k_ref[:, sl] = _rope_slab(uk[:, sl], cos, sin, lo_mask).astype(BF16)
    else:
        q_ref[...] = uq.astype(BF16)
        k_ref[...] = uk.astype(BF16)
    v_ref[...] = uv.astype(BF16)
    if cache == "alias":
        for hh in range(N_HEADS):
            sl = slice(hh * HEAD_W, (hh + 1) * HEAD_W)
            ck_ref[hh] = uk[:, sl]
            cv_ref[hh] = uv[:, sl]
    elif cache == "new":
        for d in range(DEPTH):
            for hh in range(N_HEADS):
                sl = slice(hh * HEAD_W, (hh + 1) * HEAD_W)
                ck_ref[d, hh] = uk[:, sl] if d == layer else jnp.zeros_like(uk[:, sl])
                cv_ref[d, hh] = uv[:, sl] if d == layer else jnp.zeros_like(uv[:, sl])


def _premix(x, mod_l, g, w_in_bf, *, seq_len, cond_row0, per_seq_cond, layer, n_tokens, tok0,
            rope_tables=None, caches=None, want_cache=False):
    T = n_tokens
    tm = TOKEN_BLOCK
    nb = seq_len // tm
    n_seq = T // seq_len
    b0 = tok0 // tm
    if per_seq_cond:
        mod_map = lambda i: (cond_row0 + i // nb, 0, 0)
    else:
        mod_map = lambda i: (cond_row0, 0, 0)
    in_specs = [
        pl.BlockSpec((tm, D_MODEL), lambda i: (i + b0, 0)),
        pl.BlockSpec((None, N_MOD, D_MODEL), mod_map),
        pl.BlockSpec((1, D_MODEL), lambda i: (0, 0)),
        pl.BlockSpec((D_MODEL, IN_COLS), lambda i: (0, 0)),
    ]
    args = [x, mod_l, g.reshape(1, D_MODEL), w_in_bf]
    rope = rope_tables is not None
    if rope:
        in_specs += [pl.BlockSpec((tm, HEAD_W), lambda i: (i % nb, 0))] * 2
        args += list(rope_tables)
    out_shape = [
        jax.ShapeDtypeStruct((T, COL_HY), BF16),
        jax.ShapeDtypeStruct((T, 3 * D_HYENA), BF16),
        jax.ShapeDtypeStruct((T, D_ATTN), BF16),
        jax.ShapeDtypeStruct((T, D_ATTN), BF16),
        jax.ShapeDtypeStruct((T, D_ATTN), BF16),
    ]
    out_specs = [
        pl.BlockSpec((tm, COL_HY), lambda i: (i, 0)),
        pl.BlockSpec((tm, 3 * D_HYENA), lambda i: (i, 0)),
        pl.BlockSpec((tm, D_ATTN), lambda i: (i, 0)),
        pl.BlockSpec((tm, D_ATTN), lambda i: (i, 0)),
        pl.BlockSpec((tm, D_ATTN), lambda i: (i, 0)),
    ]
    cache = None
    aliases = {}
    if want_cache:
        cache = "new"
        cshape = (n_seq, DEPTH, N_HEADS, seq_len, HEAD_W)
        out_shape += [jax.ShapeDtypeStruct(cshape, F32)] * 2
        if caches is None:
            out_specs += [pl.BlockSpec((None, DEPTH, N_HEADS, tm, HEAD_W),
                                       lambda i: (i // nb, 0, 0, i % nb, 0))] * 2
        else:
            cache = "alias"
            out_specs += [pl.BlockSpec((None, None, N_HEADS, tm, HEAD_W),
                                       lambda i: (i // nb, layer, 0, i % nb, 0))] * 2
            aliases = {len(args): 5, len(args) + 1: 6}
            in_specs += [pl.BlockSpec(memory_space=pl.ANY)] * 2
            args += list(caches)
    return pl.pallas_call(
        functools.partial(_premix_kernel, rope=rope, cache=cache, layer=layer),
        out_shape=out_shape,
        grid=(T // tm,),
        in_specs=in_specs,
        out_specs=out_specs,
        input_output_aliases=aliases,
        compiler_params=_cparams("arbitrary"),
        name="premix",
    )(*args)


def _conformer_kernel(u_ref, w_ref, cb_ref, g_ref, b_ref, o_ref, pad_ref, sh_ref, *, seq_len, rows):
    L = seq_len
    u = u_ref[...].astype(F32)
    a = u[:, :D_CONV] * jax.nn.sigmoid(u[:, D_CONV:])
    zeros = jnp.zeros((CONV_PAD, D_CONV), F32)
    pad_ref[0:CONV_PAD, :] = zeros
    pad_ref[CONV_PAD + L:2 * CONV_PAD + L, :] = zeros
    pad_ref[CONV_PAD:CONV_PAD + L, :] = a
    span = L + 2 * CONV_PAD - SUBLANES
    for r in range(SUBLANES):
        sh_ref[r, 0:span, :] = pad_ref[r:r + span, :]
    first = CONV_PAD - (CONV_WIDTH - 1) // 2

    def chunk(c, carry):
        base = pl.multiple_of(c * rows, rows)
        acc = jnp.zeros((rows, D_CONV), F32)
        for t in range(CONV_WIDTH):
            off = first + t
            tap = sh_ref[off % SUBLANES, pl.ds(base + (off // SUBLANES) * SUBLANES, rows), :]
            acc = acc + tap * w_ref[t:t + 1, :]
        acc = acc + cb_ref[...]
        mu = jnp.mean(acc, axis=-1, keepdims=True)
        d = acc - mu
        var = jnp.mean(d * d, axis=-1, keepdims=True)
        y = d * lax.rsqrt(var + EPS) * g_ref[...] + b_ref[...]
        o_ref[pl.ds(base, rows), :] = _silu(y).astype(BF16)
        return carry

    lax.fori_loop(0, L // rows, chunk, 0)


def _conformer(uc, conv_w, conv_b, ln_g, ln_b, *, seq_len):
    T = uc.shape[0]
    n_seq = T // seq_len
    vec = lambda a: a.reshape(1, D_CONV)
    return pl.pallas_call(
        functools.partial(_conformer_kernel, seq_len=seq_len, rows=min(seq_len, CONV_ROWS)),
        out_shape=jax.ShapeDtypeStruct((T, D_CONV), BF16),
        grid=(n_seq,),
        in_specs=[
            pl.BlockSpec((seq_len, 2 * D_CONV), lambda i: (i, 0)),
            pl.BlockSpec((CONV_WIDTH, D_CONV), lambda i: (0, 0)),
            pl.BlockSpec((1, D_CONV), lambda i: (0, 0)),
            pl.BlockSpec((1, D_CONV), lambda i: (0, 0)),
            pl.BlockSpec((1, D_CONV), lambda i: (0, 0)),
        ],
        out_specs=pl.BlockSpec((seq_len, D_CONV), lambda i: (i, 0)),
        scratch_shapes=[pltpu.VMEM((seq_len + 2 * CONV_PAD, D_CONV), F32),
                        pltpu.VMEM((SUBLANES, seq_len + 2 * CONV_PAD, D_CONV), F32)],
        compiler_params=_cparams("arbitrary"),
        name="conformer",
    )(uc, conv_w, vec(conv_b), vec(ln_g), vec(ln_b))


def _dft_matrices(L):
    k = np.arange(L, dtype=np.int64)[:, None]
    m = np.arange(L, dtype=np.int64)[None, :]
    ang = 2.0 * np.pi * (((2 * k + 1) * m) % (4 * L)).astype(np.float64) / (4 * L)
    fwd = np.concatenate([np.cos(ang), np.sin(ang)], axis=0).astype(np.float32)
    return fwd, np.ascontiguousarray(fwd.T)


def _filter_tables(L):
    t = np.linspace(0.0, 1.0, L)
    bands = np.linspace(1e-4, HYENA_BANDS - 1, HYENA_BANDS)
    ang = (2.0 * np.pi / L) * np.arange(L)[:, None] * bands[None, :]
    feats = np.zeros((L, FEAT_PAD), np.float32)
    feats[:, :HYENA_EMB] = np.concatenate([t[:, None], np.cos(ang), -np.sin(ang)], axis=-1)
    deltas = np.abs(np.linspace(HYENA_MIN_DECAY, HYENA_MAX_DECAY, D_HYENA))
    window = np.exp(-t[:, None] * deltas[None, :]).astype(np.float32)
    return feats, window


def _dot3(a, b):
    a_hi, a_lo = _split_bf16(a)
    b_hi, b_lo = _split_bf16(b)
    return _dot(a_hi, b_hi) + (_dot(a_hi, b_lo) + _dot(a_lo, b_hi))


def _spectrum_kernel(feat_ref, win_ref, w1_ref, b1_ref, w2_ref, b2_ref, fr_ref, w3_ref, f_ref,
                     o_ref, *, seq_len):
    L = seq_len
    C = D_HYENA
    fr = fr_ref[...]
    hid = jnp.sin(fr * (_dot3(feat_ref[...], w1_ref[...]) + b1_ref[...]))
    hid = jnp.sin(fr * (_dot3(hid, w2_ref[...]) + b2_ref[...]))
    filt = _dot3(hid, w3_ref[...])
    win = win_ref[...]
    first = lax.broadcasted_iota(jnp.int32, (L, C), 0) == 0
    for o in range(2):
        fwd = filt[:, o * C:(o + 1) * C] * win
        bwd = jnp.where(first, 0.0, filt[:, (2 + o) * C:(3 + o) * C] * win)
        o_ref[o, 0:L, :] = _dot3(f_ref[0:L, :], fwd + bwd) * (1.0 / L)
        o_ref[o, L:2 * L, :] = _dot3(f_ref[L:2 * L, :], bwd - fwd) * (1.0 / L)


def _hyena_spectrum(L, fmat, feats, window, hy_w1, hy_b1, hy_w2, hy_b2, hy_freq, hy_w3):
    w1 = jnp.zeros((FEAT_PAD, hy_w1.shape[1]), F32).at[:HYENA_EMB].set(hy_w1)
    vec = lambda a: a.reshape(1, -1)
    return pl.pallas_call(
        functools.partial(_spectrum_kernel, seq_len=L),
        out_shape=jax.ShapeDtypeStruct((2, 2 * L, D_HYENA), F32),
        compiler_params=pltpu.CompilerParams(vmem_limit_bytes=VMEM_LIMIT),
        name="hyena_spectrum",
    )(feats, window, w1, vec(hy_b1), hy_w2, vec(hy_b2), vec(hy_freq), hy_w3, fmat)


def _hyena_kernel(u_ref, sw_ref, sb_ref, hb_ref, f_ref, ft_ref, ks_ref, o_ref, pad_ref,
                  *, seq_len, n_seq):
    L = seq_len
    C = D_HYENA
    zrow = jnp.zeros((8, 3 * C), F32)
    seqs = range(n_seq)
    s = []
    for b in seqs:
        u = u_ref[b].astype(F32)
        pad_ref[b, 0:8, :] = zrow
        pad_ref[b, 8 + L:16 + L, :] = zrow
        pad_ref[b, 8:8 + L, :] = u
        s.append(pad_ref[b, 7:7 + L, :] * sw_ref[0:1, :] + u * sw_ref[1:2, :]
                 + pad_ref[b, 9:9 + L, :] * sw_ref[2:3, :] + sb_ref[...])
    z = [s[b][:, 0:C] for b in seqs]
    for o in range(2):
        kre, kim = ks_ref[o, 0:L, :], ks_ref[o, L:2 * L, :]
        spec = [_dot(f_ref[...], z[b].astype(BF16)) for b in seqs]
        y = []
        for b in seqs:
            zr, za = spec[b][:L], spec[b][L:]
            y.append(jnp.concatenate([zr * kre + za * kim, za * kre - zr * kim], axis=0))
        conv = [_dot(ft_ref[...], y[b].astype(BF16)) for b in seqs]
        z = [s[b][:, (o + 1) * C:(o + 2) * C] * (conv[b] + hb_ref[o:o + 1, :] * z[b]) for b in seqs]
    for b in seqs:
        o_ref[b] = z[b].astype(BF16)


def _hyena(uh, short_w, short_b, hy_bias, fmat_bf, fmat_t_bf, kspec, *, seq_len, seqs_per_step):
    T = uh.shape[0]
    n_seq = T // seq_len
    L = seq_len
    nb = seqs_per_step
    u3 = uh.reshape(n_seq, L, 3 * D_HYENA)
    out = pl.pallas_call(
        functools.partial(_hyena_kernel, seq_len=L, n_seq=nb),
        out_shape=jax.ShapeDtypeStruct((n_seq, L, D_HYENA), BF16),
        grid=(n_seq // nb,),
        in_specs=[
            pl.BlockSpec((nb, L, 3 * D_HYENA), lambda i: (i, 0, 0)),
            pl.BlockSpec((3, 3 * D_HYENA), lambda i: (0, 0)),
            pl.BlockSpec((1, 3 * D_HYENA), lambda i: (0, 0)),
            pl.BlockSpec((2, D_HYENA), lambda i: (0, 0)),
            pl.BlockSpec((2 * L, L), lambda i: (0, 0)),
            pl.BlockSpec((L, 2 * L), lambda i: (0, 0)),
            pl.BlockSpec((2, 2 * L, D_HYENA), lambda i: (0, 0, 0)),
        ],
        out_specs=pl.BlockSpec((nb, L, D_HYENA), lambda i: (i, 0, 0)),
        scratch_shapes=[pltpu.VMEM((nb, L + 16, 3 * D_HYENA), F32)],
        compiler_params=_cparams("arbitrary"),
        name="hyena",
    )(u3, short_w, short_b.reshape(1, 3 * D_HYENA), hy_bias, fmat_bf, fmat_t_bf, kspec)
    return out.reshape(T, D_HYENA)


def _attn_kernel(*refs, lam_init, has_ctx):
    q_ref, k_ref, v_ref = refs[:3]
    pos = 3
    if has_ctx:
        ck_ref, cv_ref = refs[pos:pos + 2]
        pos += 2
    lam_ref, g_ref, o_ref = refs[pos:pos + 3]

    lv = lam_ref[...]
    lam = (jnp.exp(jnp.sum(lv[0:1] * lv[1:2], axis=-1, keepdims=True))
           - jnp.exp(jnp.sum(lv[2:3] * lv[3:4], axis=-1, keepdims=True)) + lam_init)

    tq = q_ref.shape[0]
    lane = lax.broadcasted_iota(jnp.int32, (tq, HEAD_W), 1)
    dn = (((1,), (1,)), ((), ()))
    group = 2
    for h0 in range(0, N_HEADS, group):
        heads = range(h0, h0 + group)
        sls = [slice(h * HEAD_W, (h + 1) * HEAD_W) for h in heads]
        q2 = []
        for sl in sls:
            q = q_ref[:, sl]
            zero = jnp.zeros_like(q)
            q2.append(jnp.concatenate([jnp.where(lane < DIFF_DK, q, zero),
                                       jnp.where(lane >= DIFF_DK, q, zero)], axis=0))
        s = [lax.dot_general(q2[j], k_ref[:, sl], dn, preferred_element_type=F32)
             for j, sl in enumerate(sls)]
        m = [jnp.max(x, axis=-1, keepdims=True) for x in s]
        if has_ctx:
            sc = [lax.dot_general(q2[j], ck_ref[h].astype(BF16), dn, preferred_element_type=F32)
                  for j, h in enumerate(heads)]
            m = [jnp.maximum(m[j], jnp.max(sc[j], axis=-1, keepdims=True)) for j in range(group)]
        p = [jnp.exp(s[j] - m[j]) for j in range(group)]
        den = [jnp.sum(x, axis=-1, keepdims=True) for x in p]
        acc = [_dot(p[j].astype(BF16), v_ref[:, sl]) for j, sl in enumerate(sls)]
        if has_ctx:
            pc = [jnp.exp(sc[j] - m[j]) for j in range(group)]
            den = [den[j] + jnp.sum(pc[j], axis=-1, keepdims=True) for j in range(group)]
            acc = [acc[j] + _dot(pc[j].astype(BF16), cv_ref[h].astype(BF16))
                   for j, h in enumerate(heads)]
        for j, sl in enumerate(sls):
            both = acc[j] / den[j]
            o = both[:tq] - lam * both[tq:]
            o_ref[:, sl] = (_rms(o, g_ref[...]) * (1.0 - lam_init)).astype(BF16)


def _attention(q, k, v, lam_vecs, subln_g, *, seq_len, layer, ctx=None):
    T = q.shape[0]
    L = seq_len
    n_seq = T // L
    tq = TOKEN_BLOCK
    nq = L // tq
    lam_init = 0.8 - 0.6 * math.exp(-0.3 * layer)
    q3, k3, v3 = (a.reshape(n_seq, L, D_ATTN) for a in (q, k, v))
    in_specs = [
        pl.BlockSpec((None, tq, D_ATTN), lambda b, i: (b, i, 0)),
        pl.BlockSpec((None, L, D_ATTN), lambda b, i: (b, 0, 0)),
        pl.BlockSpec((None, L, D_ATTN), lambda b, i: (b, 0, 0)),
    ]
    args = [q3, k3, v3]
    if ctx is not None:
        past = ctx[0].shape[3]
        in_specs += [pl.BlockSpec((None, None, N_HEADS, past, HEAD_W),
                                  lambda b, i: (b, layer, 0, 0, 0))] * 2
        args += list(ctx)
    in_specs += [
        pl.BlockSpec((4, DIFF_DK), lambda b, i: (0, 0)),
        pl.BlockSpec((1, DIFF_DV), lambda b, i: (0, 0)),
    ]
    args += [lam_vecs, subln_g.reshape(1, DIFF_DV)]
    out = pl.pallas_call(
        functools.partial(_attn_kernel, lam_init=lam_init, has_ctx=ctx is not None),
        out_shape=jax.ShapeDtypeStruct((n_seq, L, D_ATTN), BF16),
        grid=(n_seq, nq),
        in_specs=in_specs,
        out_specs=pl.BlockSpec((None, tq, D_ATTN), lambda b, i: (b, i, 0)),
        compiler_params=_cparams("arbitrary", "arbitrary"),
        name="diff_attention",
    )(*args)
    return out.reshape(T, D_ATTN)


def _split_bf16(a):
    hi = a.astype(BF16)
    lo = (a - hi.astype(F32)).astype(BF16)
    return hi, lo


def _pack_pair(a, b):
    ua = lax.bitcast_convert_type(a.astype(BF16).astype(F32), jnp.uint32)
    ub = lax.bitcast_convert_type(b.astype(BF16).astype(F32), jnp.uint32)
    return (ua >> 16) | ub


def _unpack_pair(w):
    a = lax.bitcast_convert_type(w << 16, F32)
    b = lax.bitcast_convert_type(w & jnp.uint32(0xFFFF0000), F32)
    return a, b


def _postmix_kernel(*refs, stream_blocks):
    ns = len(stream_blocks)
    mod_ref, gpost_ref, gpre_ref, w_ref, rw_ref, x1_ref, h2_ref, lg_ref = refs[4 * ns:]
    i = pl.program_id(0)

    def body(ya_ref, yb_ref, o_ref, x_ref):
        n = x_ref.shape[0]
        halves = [slice(0, n // 2), slice(n // 2, n)]
        mod = mod_ref[...]
        y = [_dot(ya_ref[r, :], w_ref[0:D_CONV, :])
             + _dot(yb_ref[r, :], w_ref[D_CONV:D_CONV + D_HYENA, :])
             + _dot(o_ref[r, :], w_ref[D_CONV + D_HYENA:, :]) for r in halves]
        x1 = [x_ref[r, :] + mod[2:3] * _rms(y[j], gpost_ref[...]) for j, r in enumerate(halves)]
        for j, r in enumerate(halves):
            x1_ref[r, :] = x1[j]
        h2 = [_rms(x1[j], gpre_ref[...]) * (1.0 + mod[4:5]) + mod[3:4] for j in range(2)]
        for j, r in enumerate(halves):
            h2_ref[r, :] = _pack_pair(h2[j][:, :HALF], h2[j][:, HALF:])
        r_hi, r_lo = _split_bf16(rw_ref[...])
        dn = (((1,), (1,)), ((), ()))
        mm = lambda a, b: lax.dot_general(a, b, dn, preferred_element_type=F32)
        for j, r in enumerate(halves):
            h_hi, h_lo = _split_bf16(h2[j])
            lg_ref[:, r] = mm(r_hi, h_hi) + (mm(r_lo, h_hi) + mm(r_hi, h_lo))

    lo = 0
    for j, nbk in enumerate(stream_blocks):
        pl.when((i >= lo) & (i < lo + nbk))(functools.partial(body, *refs[4 * j:4 * j + 4]))
        lo += nbk


def _postmix(parts, mod_l, g_post, g_pre, w_out_bf, router_wt, *, mod_map):
    t_all = sum(p[0].shape[0] for p in parts)
    tm = TOKEN_BLOCK
    full = lambda a: pl.BlockSpec(a.shape, lambda i: (0,) * a.ndim)
    gp, ge = g_post.reshape(1, D_MODEL), g_pre.reshape(1, D_MODEL)
    stream_blocks = tuple(p[0].shape[0] // tm for p in parts)
    in_specs, args = [], []
    lo = 0
    for part, nbk in zip(parts, stream_blocks):
        local = lambda i, lo=lo, nbk=nbk: (jnp.clip(i - lo, 0, nbk - 1), 0)
        in_specs += [pl.BlockSpec((tm, D_CONV), local), pl.BlockSpec((tm, D_HYENA), local),
                     pl.BlockSpec((tm, D_ATTN), local), pl.BlockSpec((tm, D_MODEL), local)]
        args += list(part)
        lo += nbk
    in_specs += [pl.BlockSpec((None, N_MOD, D_MODEL), mod_map),
                 full(gp), full(ge), full(w_out_bf), full(router_wt)]
    args += [mod_l, gp, ge, w_out_bf, router_wt]
    return pl.pallas_call(
        functools.partial(_postmix_kernel, stream_blocks=stream_blocks),
        out_shape=[
            jax.ShapeDtypeStruct((t_all, D_MODEL), F32),
            jax.ShapeDtypeStruct((t_all, HALF), jnp.uint32),
            jax.ShapeDtypeStruct((N_EXPERTS, t_all), F32),
        ],
        grid=(t_all // tm,),
        in_specs=in_specs,
        out_specs=[pl.BlockSpec((tm, D_MODEL), lambda i: (i, 0)),
                   pl.BlockSpec((tm, HALF), lambda i: (i, 0)),
                   pl.BlockSpec((N_EXPERTS, tm), lambda i: (0, i))],
        compiler_params=_cparams("arbitrary"),
        name="postmix",
    )(*args)


def _first_max(vals, idx, sentinel):
    m = jnp.max(vals, axis=0, keepdims=True)
    i = jnp.min(jnp.where(vals == m, idx, sentinel), axis=0, keepdims=True)
    return m, i


def _route_topk_kernel(lg_ref, rb_ref, eidx_ref, w_ref, cnt_ref):
    step = pl.program_id(0)
    per = N_EXPERTS // N_GROUPS
    n = lg_ref.shape[1]
    s = jax.nn.sigmoid(lg_ref[...])
    sel = s + rb_ref[...]
    sub = lax.broadcasted_iota(jnp.int32, (per, n), 0)
    neg = jnp.float32(-jnp.inf)

    gs = []
    for g in range(N_GROUPS):
        xg = sel[g * per:(g + 1) * per]
        total = jnp.zeros((1, n), F32)
        for _ in range(GROUP_SCORE_TOP):
            m, i = _first_max(xg, sub, per)
            total = total + m
            xg = jnp.where(sub == i, neg, xg)
        gs.append(total)
    gscore = jnp.concatenate(gs, axis=0)
    gkeep = jnp.zeros((N_GROUPS, n), F32)
    for _ in range(TOPK_GROUPS):
        _, gi = _first_max(gscore, sub, N_GROUPS)
        hit = sub == gi
        gkeep = jnp.where(hit, 1.0, gkeep)
        gscore = jnp.where(hit, neg, gscore)

    eio = [sub + g * per for g in range(N_GROUPS)]
    sg = [s[g * per:(g + 1) * per] for g in range(N_GROUPS)]
    mk = [jnp.where(gkeep[g:g + 1] > 0.5, sel[g * per:(g + 1) * per], neg) for g in range(N_GROUPS)]
    oh = [jnp.zeros((per, n), F32) for _ in range(N_GROUPS)]
    idx_rows, s_rows = [], []
    for _ in range(TOP_K):
        best = functools.reduce(jnp.maximum, mk)
        m = jnp.max(best, axis=0, keepdims=True)
        cand = functools.reduce(jnp.minimum,
                                [jnp.where(mk[g] == m, eio[g], N_EXPERTS) for g in range(N_GROUPS)])
        idx = jnp.min(cand, axis=0, keepdims=True)
        acc = jnp.zeros((per, n), F32)
        for g in range(N_GROUPS):
            hit = eio[g] == idx
            acc = acc + jnp.where(hit, sg[g], 0.0)
            mk[g] = jnp.where(hit, neg, mk[g])
            oh[g] = jnp.where(hit, 1.0, oh[g])
        idx_rows.append(idx)
        s_rows.append(jnp.sum(acc, axis=0, keepdims=True))
    eidx_ref[...] = jnp.concatenate(idx_rows, axis=0)
    sk = jnp.concatenate(s_rows, axis=0)
    w_ref[...] = sk / jnp.sum(sk, axis=0, keepdims=True) * ROUTED_SCALE

    @pl.when(step == 0)
    def _():
        cnt_ref[...] = jnp.zeros_like(cnt_ref)

    tot8 = jnp.zeros((N_EXPERTS, 1), F32)
    for w0 in range(0, n, TOKEN_BLOCK):
        tot = jnp.concatenate([jnp.sum(o[:, w0:w0 + TOKEN_BLOCK], axis=1, keepdims=True)
                               for o in oh], axis=0)
        tot8 = tot8 + _ceil_tile(tot)
    cnt_ref[...] = cnt_ref[...] + jnp.broadcast_to(tot8, cnt_ref.shape).astype(jnp.int32)


def _ceil_tile(count):
    return jnp.floor((count + (SUBLANES - 1.0)) * (1.0 / SUBLANES)) * SUBLANES


def _route_topk(logits_t, router_b):
    t_all = logits_t.shape[1]
    rb = 2 * TOKEN_BLOCK if t_all % (2 * TOKEN_BLOCK) == 0 else TOKEN_BLOCK
    return pl.pallas_call(
        _route_topk_kernel,
        out_shape=[
            jax.ShapeDtypeStruct((TOP_K, t_all), jnp.int32),
            jax.ShapeDtypeStruct((TOP_K, t_all), F32),
            jax.ShapeDtypeStruct((N_EXPERTS, LANES), jnp.int32),
        ],
        grid=(t_all // rb,),
        in_specs=[pl.BlockSpec((N_EXPERTS, rb), lambda i: (0, i)),
                  pl.BlockSpec((N_EXPERTS, 1), lambda i: (0, 0))],
        out_specs=[pl.BlockSpec((TOP_K, rb), lambda i: (0, i)),
                   pl.BlockSpec((TOP_K, rb), lambda i: (0, i)),
                   pl.BlockSpec((N_EXPERTS, LANES), lambda i: (0, 0))],
        compiler_params=_cparams("arbitrary"),
        name="route_topk",
    )(logits_t, router_b.reshape(N_EXPERTS, 1))


def _route_dest_kernel(eidx_ref, ps_ref, lpos_ref, cs_ref, cn_ref, cd_ref, carry_ref):
    step = pl.program_id(0)
    per = N_EXPERTS // N_GROUPS
    n = eidx_ref.shape[1]

    @pl.when(step == 0)
    def _():
        carry_ref[...] = jnp.zeros_like(carry_ref)
        cs_ref[...] = jnp.zeros_like(cs_ref)
        cn_ref[...] = jnp.zeros_like(cn_ref)
        cd_ref[...] = jnp.zeros_like(cd_ref)

    eidx = eidx_ref[...]
    sub = lax.broadcasted_iota(jnp.int32, (per, n), 0)
    rows = []
    for g in range(N_GROUPS):
        eio = sub + g * per
        o = jnp.zeros((per, n), F32)
        for k in range(TOP_K):
            o = jnp.where(eio == eidx[k:k + 1], 1.0, o)
        rows.append(o)
    onehot = jnp.concatenate(rows, axis=0)
    r = lax.broadcasted_iota(jnp.int32, (n, n), 0)
    c = lax.broadcasted_iota(jnp.int32, (n, n), 1)
    before = jnp.where(r < c, 1.0, 0.0).astype(BF16)
    rank = _dot(onehot.astype(BF16), before)
    size = _ceil_tile(jnp.sum(onehot, axis=1, keepdims=True))
    er = lax.broadcasted_iota(jnp.int32, (N_EXPERTS, N_EXPERTS), 0)
    ec = lax.broadcasted_iota(jnp.int32, (N_EXPERTS, N_EXPERTS), 1)
    lower = jnp.where(ec < er, 1.0, 0.0).astype(BF16)
    tiles = jnp.broadcast_to(size * (1.0 / SUBLANES), (N_EXPERTS, LANES)).astype(BF16)
    start = _dot(lower, tiles)[:, 0:1] * SUBLANES
    base = (rank + start).astype(jnp.int32)
    out = []
    for k in range(TOP_K):
        acc = jnp.zeros((per, n), jnp.int32)
        for g in range(N_GROUPS):
            acc = acc + jnp.where(sub + g * per == eidx[k:k + 1], base[g * per:(g + 1) * per], 0)
        out.append(jnp.sum(acc, axis=0, keepdims=True))
    lpos_ref[...] = jnp.concatenate(out, axis=0)

    col = lax.broadcasted_iota(jnp.int32, cs_ref.shape, 1) == step
    carry = carry_ref[:, 0:1]
    cs_ref[...] = jnp.where(col, start.astype(jnp.int32), cs_ref[...])
    cn_ref[...] = jnp.where(col, size.astype(jnp.int32), cn_ref[...])
    cd_ref[...] = jnp.where(col, carry.astype(jnp.int32) + ps_ref[...], cd_ref[...])
    carry_ref[...] = carry_ref[...] + jnp.broadcast_to(size, carry_ref.shape)


def _route_dest(eidx_t, pstart):
    t_all = eidx_t.shape[1]
    rb = TOKEN_BLOCK
    n_win = t_all // rb
    cols = -(-n_win // LANES) * LANES
    table = jax.ShapeDtypeStruct((N_EXPERTS, cols), jnp.int32)
    table_spec = pl.BlockSpec((N_EXPERTS, cols), lambda i: (0, 0))
    lpos, cs, cn, cd = pl.pallas_call(
        _route_dest_kernel,
        out_shape=[jax.ShapeDtypeStruct((TOP_K, t_all), jnp.int32), table, table, table],
        grid=(n_win,),
        in_specs=[pl.BlockSpec((TOP_K, rb), lambda i: (0, i)),
                  pl.BlockSpec((N_EXPERTS, 1), lambda i: (0, 0))],
        out_specs=[pl.BlockSpec((TOP_K, rb), lambda i: (0, i)), table_spec, table_spec, table_spec],
        scratch_shapes=[pltpu.VMEM((N_EXPERTS, LANES), F32)],
        compiler_params=_cparams("arbitrary"),
        name="route_dest",
    )(eidx_t, pstart.reshape(N_EXPERTS, 1))
    flat = lambda a: a[:, :n_win].T.reshape(-1)
    return lpos, (flat(cs), flat(cn), flat(cd))


def _wait_rows(hbm, sem, n_rows):
    pltpu.make_async_copy(hbm.at[pl.ds(0, n_rows)], hbm.at[pl.ds(0, n_rows)], sem).wait()


def _window_chunks(tables, w, local_ref, hbm, sem, *, to_hbm, spare_row):
    cs_ref, cn_ref, cd_ref = tables
    tile = lambda v: pl.multiple_of(v, SUBLANES)

    def copy(local_row, hbm_row, size):
        lo = local_ref.at[pl.ds(tile(local_row), tile(size))]
        hi = hbm.at[pl.ds(tile(hbm_row), tile(size))]
        src, dst = (lo, hi) if to_hbm else (hi, lo)
        pltpu.make_async_copy(src, dst, sem).start()

    def per_expert(e, carry):
        i = w * N_EXPERTS + e
        copy(cs_ref[i], cd_ref[i], cn_ref[i])
        return carry

    lax.fori_loop(0, N_EXPERTS, per_expert, 0, unroll=8)
    last = w * N_EXPERTS + (N_EXPERTS - 1)
    used = cs_ref[last] + cn_ref[last]
    copy(used, spare_row, LOCAL_ROWS - used)


def _zero_padding(cnt_ref, ps_ref, nu_ref, zero_ref, x_hbm, zsem, n_blocks, *, wait):
    def copy(row, size):
        cp = pltpu.make_async_copy(zero_ref.at[pl.ds(0, size)], x_hbm.at[pl.ds(row, size)], zsem)
        if wait:
            cp.wait()
        else:
            cp.start()

    def per_expert(e, carry):
        cnt = cnt_ref[e]
        pad = (EXPERT_BLOCK - cnt % EXPERT_BLOCK) % EXPERT_BLOCK
        copy(pl.multiple_of(ps_ref[e] + cnt, SUBLANES), pl.multiple_of(pad, SUBLANES))
        return carry

    lax.fori_loop(0, N_EXPERTS, per_expert, 0)

    def per_block(j, carry):
        copy(pl.multiple_of(j * EXPERT_BLOCK, EXPERT_BLOCK), EXPERT_BLOCK)
        return carry

    lax.fori_loop(nu_ref[0], n_blocks, per_block, 0)


def _dispatch_kernel(cs_ref, cn_ref, cd_ref, cnt_ref, ps_ref, nu_ref, lpos_ref, h_ref, x_hbm,
                     gbuf, zero_ref, sem, zsem, ssem):
    w = pl.program_id(0)
    nsteps = pl.num_programs(0)
    slot = w % 2
    n_blocks = (x_hbm.shape[0] - 2 * SPARE_ROWS) // EXPERT_BLOCK
    spare0 = n_blocks * EXPERT_BLOCK
    zero_args = (cnt_ref, ps_ref, nu_ref, zero_ref, x_hbm, zsem, n_blocks)

    def spare_zero(j):
        return pltpu.make_async_copy(
            zero_ref, x_hbm.at[pl.ds(spare0 + j * EXPERT_BLOCK, EXPERT_BLOCK)], ssem)

    @pl.when(w == 0)
    def _():
        zero_ref[...] = jnp.zeros_like(zero_ref)
        _zero_padding(*zero_args, wait=False)
        for j in range(2 * SPARE_ROWS // EXPERT_BLOCK):
            spare_zero(j).start()

    ha, hb = _unpack_pair(h_ref[...])
    ha, hb = ha.astype(BF16), hb.astype(BF16)
    lp = lpos_ref[...]
    n = h_ref.shape[0]
    last = w * N_EXPERTS + (N_EXPERTS - 1)
    used = cs_ref[last] + cn_ref[last]

    lp = lp.astype(jnp.int16)
    one = jnp.ones((n, n), BF16)

    def permute(c):
        row = (lax.broadcasted_iota(jnp.int32, (n, n), 0) + c * n).astype(jnp.int16)
        pb = jnp.zeros((n, n), BF16)
        for k in range(TOP_K):
            pb = jnp.where(row == lp[k:k + 1, :], one, pb)
        ga = lax.bitcast_convert_type(_dot(pb, ha), jnp.uint32)
        gb = lax.bitcast_convert_type(_dot(pb, hb), jnp.uint32)
        gbuf[slot, c * n:(c + 1) * n, :] = (ga >> 16) | gb

    def blank(c):
        gbuf[slot, c * n:(c + 1) * n, :] = jnp.zeros((n, HALF), jnp.uint32)

    for c in range(LOCAL_ROWS // n):
        if c <= TOP_K:
            permute(c)
        else:
            pl.when(c * n < used)(functools.partial(permute, c))
            pl.when(c * n >= used)(functools.partial(blank, c))

    @pl.when(w == 0)
    def _():
        for j in range(2 * SPARE_ROWS // EXPERT_BLOCK):
            spare_zero(j).wait()

    _window_chunks((cs_ref, cn_ref, cd_ref), w, gbuf.at[slot], x_hbm, sem.at[slot],
                   to_hbm=True, spare_row=spare0 + slot * SPARE_ROWS)

    @pl.when(w > 0)
    def _():
        _wait_rows(x_hbm, sem.at[1 - slot], LOCAL_ROWS)

    @pl.when(w == nsteps - 1)
    def _():
        _wait_rows(x_hbm, sem.at[slot], LOCAL_ROWS)
        _zero_padding(*zero_args, wait=True)


def _dispatch(tables, lpos, counts, pstart, n_used, h2p, n_blocks):
    t_all = h2p.shape[0]
    sb = TOKEN_BLOCK
    smem = lambda: pl.BlockSpec(memory_space=pltpu.SMEM)
    return pl.pallas_call(
        _dispatch_kernel,
        out_shape=jax.ShapeDtypeStruct((n_blocks * EXPERT_BLOCK + 2 * SPARE_ROWS, HALF), jnp.uint32),
        grid=(t_all // sb,),
        in_specs=[smem(), smem(), smem(), smem(), smem(), smem(),
                  pl.BlockSpec((TOP_K, sb), lambda i: (0, i)),
                  pl.BlockSpec((sb, HALF), lambda i: (i, 0))],
        out_specs=pl.BlockSpec(memory_space=pl.ANY),
        scratch_shapes=[pltpu.VMEM((2, LOCAL_ROWS, HALF), jnp.uint32),
                        pltpu.VMEM((EXPERT_BLOCK, HALF), jnp.uint32),
                        pltpu.SemaphoreType.DMA((2,)), pltpu.SemaphoreType.DMA,
                        pltpu.SemaphoreType.DMA],
        compiler_params=_cparams("arbitrary"),
        name="dispatch",
    )(*tables, counts, pstart, n_used, lpos, h2p)


def _expert_kernel(nb_ref, ps_ref, nu_ref, wg_ref, wu_ref, wd_ref, x_hbm, y_hbm,
                   wg_bf, wu_bf, wd_bf, xbuf, ybuf, xsem, ysem):
    e = pl.program_id(0)
    nblk = nb_ref[e]
    first = ps_ref[e] // EXPERT_BLOCK
    n_used = nu_ref[0]
    xs, ys = xbuf.shape[0], ybuf.shape[0]
    wg_bf[...] = wg_ref[...].astype(BF16)
    wu_bf[...] = wu_ref[...].astype(BF16)
    wd_bf[...] = wd_ref[...].astype(BF16)

    def rows(b):
        return pl.ds(pl.multiple_of(b * EXPERT_BLOCK, EXPERT_BLOCK), EXPERT_BLOCK)

    def x_copy(b):
        return pltpu.make_async_copy(x_hbm.at[rows(b)], xbuf.at[b % xs], xsem.at[b % xs])

    def y_copy(b):
        return pltpu.make_async_copy(ybuf.at[b % ys], y_hbm.at[rows(b)], ysem.at[b % ys])

    @pl.when(e == 0)
    def _():
        for b in range(xs - 1):
            pl.when(b < n_used)(lambda b=b: x_copy(b).start())

    def block(j, carry):
        b = first + j
        x_copy(b).wait()
        pl.when(b + xs - 1 < n_used)(lambda: x_copy(b + xs - 1).start())
        pl.when(b >= ys)(lambda: y_copy(b - ys).wait())
        y = _swiglu_packed(xbuf[b % xs], wg_bf, wu_bf, wd_bf)
        ybuf[b % ys] = _pack_pair(y[:, :HALF], y[:, HALF:])
        y_copy(b).start()
        return carry

    lax.fori_loop(0, nblk, block, 0)

    @pl.when(e == pl.num_programs(0) - 1)
    def _():
        for d in range(ys):
            b = n_used - ys + d
            pl.when(b >= 0)(lambda b=b: y_copy(b).wait())
        n_blocks = y_hbm.shape[0] // EXPERT_BLOCK
        ybuf[0] = jnp.zeros_like(ybuf[0])

        def tail(j, carry, wait):
            cp = pltpu.make_async_copy(
                ybuf.at[0], y_hbm.at[pl.ds(pl.multiple_of(j * EXPERT_BLOCK, EXPERT_BLOCK),
                                           EXPERT_BLOCK)], ysem.at[0])
            cp.wait() if wait else cp.start()
            return carry

        lax.fori_loop(nu_ref[0], n_blocks, functools.partial(tail, wait=False), 0)
        lax.fori_loop(nu_ref[0], n_blocks, functools.partial(tail, wait=True), 0)


def _swiglu_packed(xp, wg_ref, wu_ref, wd_ref):
    xa, xb = _unpack_pair(xp)
    xa, xb = xa.astype(BF16), xb.astype(BF16)
    gate = _dot(xa, wg_ref[0:HALF, :]) + _dot(xb, wg_ref[HALF:, :])
    up = _dot(xa, wu_ref[0:HALF, :]) + _dot(xb, wu_ref[HALF:, :])
    return _dot((_silu(gate) * up).astype(BF16), wd_ref[...])


def _experts(x_buf, nblk, pstart, n_used, wg, wu, wd, *, layer):
    grid_spec = pltpu.PrefetchScalarGridSpec(
        num_scalar_prefetch=3,
        grid=(N_EXPERTS,),
        in_specs=[
            pl.BlockSpec((None, None, D_MODEL, D_EXPERT), lambda e, *_: (layer, e, 0, 0)),
            pl.BlockSpec((None, None, D_MODEL, D_EXPERT), lambda e, *_: (layer, e, 0, 0)),
            pl.BlockSpec((None, None, D_EXPERT, D_MODEL), lambda e, *_: (layer, e, 0, 0)),
            pl.BlockSpec(memory_space=pl.ANY),
        ],
        out_specs=pl.BlockSpec(memory_space=pl.ANY),
        scratch_shapes=[
            pltpu.VMEM((D_MODEL, D_EXPERT), BF16),
            pltpu.VMEM((D_MODEL, D_EXPERT), BF16),
            pltpu.VMEM((D_EXPERT, D_MODEL), BF16),
            pltpu.VMEM((EXPERT_IN_SLOTS, EXPERT_BLOCK, HALF), jnp.uint32),
            pltpu.VMEM((EXPERT_OUT_SLOTS, EXPERT_BLOCK, HALF), jnp.uint32),
            pltpu.SemaphoreType.DMA((EXPERT_IN_SLOTS,)),
            pltpu.SemaphoreType.DMA((EXPERT_OUT_SLOTS,)),
        ],
    )
    return pl.pallas_call(
        _expert_kernel,
        out_shape=jax.ShapeDtypeStruct(x_buf.shape, jnp.uint32),
        grid_spec=grid_spec,
        compiler_params=_cparams("arbitrary"),
        name="experts",
    )(nblk, pstart, n_used, wg, wu, wd, x_buf)


def _block_plan(counts, n_blocks):
    nblk = (counts + EXPERT_BLOCK - 1) // EXPERT_BLOCK
    pend = jnp.cumsum(nblk)
    pstart = ((pend - nblk) * EXPERT_BLOCK).astype(jnp.int32)
    return pstart, nblk.astype(jnp.int32), pend[-1:].astype(jnp.int32)


def _ffn_out_kernel(*refs, stream_blocks):
    (cs_ref, cn_ref, cd_ref, lpos_ref, h_ref, w_ref, x_ref, mod_ref, g_ref, wg_ref, wu_ref, wd_ref,
     y_hbm) = refs[:13]
    ns = len(stream_blocks)
    out_refs = refs[13:13 + ns]
    rows_ref, sem = refs[13 + ns:]
    step = pl.program_id(0)
    nsteps = pl.num_programs(0)
    n = h_ref.shape[0]
    slot = step % 2

    def gather(w, to_slot):
        _window_chunks((cs_ref, cn_ref, cd_ref), w, rows_ref.at[to_slot], y_hbm, sem.at[to_slot],
                       to_hbm=False, spare_row=0)

    pl.when(step == 0)(functools.partial(gather, 0, 0))
    for parity in range(2):
        pl.when((step + 1 < nsteps) & (slot == parity))(
            functools.partial(gather, step + 1, 1 - parity))

    ys = _swiglu_packed(h_ref[...], wg_ref, wu_ref, wd_ref)
    _wait_rows(y_hbm, sem.at[slot], LOCAL_ROWS)
    w = w_ref[...].astype(BF16)
    lp = lpos_ref[...].astype(jnp.int16)
    ya, yb = ys[:, :HALF], ys[:, HALF:]
    for c in range(LOCAL_ROWS // n):
        col = (lax.broadcasted_iota(jnp.int32, (n, n), 1) + c * n).astype(jnp.int16)
        pw = jnp.zeros((n, n), BF16)
        for k in range(TOP_K):
            pw = jnp.where(col == lp[:, k:k + 1], jnp.broadcast_to(w[:, k:k + 1], (n, n)), pw)
        ra, rb = _unpack_pair(rows_ref[slot, c * n:(c + 1) * n, :])
        ya = ya + _dot(pw, ra.astype(BF16))
        yb = yb + _dot(pw, rb.astype(BF16))
    ms = (jnp.sum(ya * ya, axis=-1, keepdims=True)
          + jnp.sum(yb * yb, axis=-1, keepdims=True)) * (1.0 / D_MODEL)
    inv = lax.rsqrt(ms + EPS)
    mod = mod_ref[...]
    g = g_ref[...]
    out_a = x_ref[:, 0:HALF] + mod[5:6, 0:HALF] * (ya * inv * g[:, 0:HALF])
    out_b = x_ref[:, HALF:] + mod[5:6, HALF:] * (yb * inv * g[:, HALF:])

    def store(o_ref):
        o_ref[:, 0:HALF] = out_a
        o_ref[:, HALF:] = out_b

    lo = 0
    for o_ref, nbk in zip(out_refs, stream_blocks):
        pl.when((step >= lo) & (step < lo + nbk))(functools.partial(store, o_ref))
        lo += nbk


def _ffn_out(tables, lpos_rows, h2p, w_rows, x1, mod_l, g_post, swg, swu, swd, y_buf, *, mod_map,
             stream_tokens):
    t_all = x1.shape[0]
    tm = TOKEN_BLOCK
    nsteps = t_all // tm
    row = lambda w: pl.BlockSpec((tm, w), lambda i: (i, 0))
    full = lambda a: pl.BlockSpec(a.shape, lambda i: (0,) * a.ndim)
    smem = lambda: pl.BlockSpec(memory_space=pltpu.SMEM)
    g = g_post.reshape(1, D_MODEL)
    stream_blocks = tuple(t // tm for t in stream_tokens)
    out_shape, out_specs = [], []
    lo = 0
    for t, nbk in zip(stream_tokens, stream_blocks):
        out_shape.append(jax.ShapeDtypeStruct((t, D_MODEL), F32))
        out_specs.append(pl.BlockSpec((tm, D_MODEL),
                                      lambda i, lo=lo, nbk=nbk: (jnp.clip(i - lo, 0, nbk - 1), 0)))
        lo += nbk
    return pl.pallas_call(
        functools.partial(_ffn_out_kernel, stream_blocks=stream_blocks),
        out_shape=out_shape,
        grid=(nsteps,),
        in_specs=[smem(), smem(), smem(),
                  row(TOP_K), row(HALF), row(TOP_K), row(D_MODEL),
                  pl.BlockSpec((None, N_MOD, D_MODEL), mod_map),
                  full(g), full(swg), full(swu), full(swd),
                  pl.BlockSpec(memory_space=pl.ANY)],
        out_specs=out_specs,
        scratch_shapes=[pltpu.VMEM((2, LOCAL_ROWS, HALF), jnp.uint32),
                        pltpu.SemaphoreType.DMA((2,))],
        compiler_params=_cparams("arbitrary"),
        name="ffn_out",
    )(*tables, lpos_rows, h2p, w_rows, x1, mod_l, g, swg, swu, swd, y_buf)


def _rope_tables(L):
    rows = L // GRID_W
    row = jnp.repeat(jnp.arange(rows), GRID_W).astype(F32)
    col = jnp.tile(jnp.arange(GRID_W), rows).astype(F32)
    n = DIFF_DK // 4
    inv = ROPE_BASE ** (-jnp.arange(n, dtype=F32) / n)
    ar, ac = row[:, None] * inv[None, :], col[:, None] * inv[None, :]
    cos64 = jnp.concatenate([jnp.cos(ar), jnp.cos(ar), jnp.cos(ac), jnp.cos(ac)], axis=-1)
    sin64 = jnp.concatenate([-jnp.sin(ar), jnp.sin(ar), -jnp.sin(ac), jnp.sin(ac)], axis=-1)
    return jnp.tile(cos64, (1, 2)), jnp.tile(sin64, (1, 2))


def kernel(x_prompt, x_sample, cache_k, cache_v, c, c_ctx, w_mod, b_mod, g_pre_mix, g_post_mix, g_pre_ffn, g_post_ffn, w_in, w_out, conv_w, conv_b, conv_ln_g, conv_ln_b, hy_short_w, hy_short_b, hy_w1, hy_b1, hy_w2, hy_b2, hy_freq, hy_w3, hy_bias, lam_q1, lam_k1, lam_q2, lam_k2, subln_g, router_w, router_b, exp_w_gate, exp_w_up, exp_w_down, sh_w_gate, sh_w_up, sh_w_down):
    n_p, l_p, _ = x_prompt.shape
    n_s, l_s, _ = x_sample.shape
    t_p, t_s = n_p * l_p, n_s * l_s

    cond = jnp.zeros((COND_ROWS, D_MODEL), F32).at[0].set(c_ctx).at[1:1 + n_s].set(c)
    mod = _modulation(cond, w_mod, b_mod).reshape(DEPTH, COND_ROWS, N_MOD, D_MODEL)
    rope = _rope_tables(l_s)

    t_all = t_p + t_s
    streams = {
        "p": dict(x=x_prompt.reshape(t_p, D_MODEL), T=t_p, L=l_p, row0=0, per_seq=False, hy_nb=8),
        "s": dict(x=x_sample.reshape(t_s, D_MODEL), T=t_s, L=l_s, row0=1, per_seq=True,
                  hy_nb=2 if n_s % 2 == 0 else 1),
    }
    for st in streams.values():
        fwd, fwd_t = _dft_matrices(st["L"])
        feats, window = _filter_tables(st["L"])
        st["fmat32"] = jnp.asarray(fwd)
        st["fmat"] = jnp.asarray(fwd).astype(BF16)
        st["fmat_t"] = jnp.asarray(fwd_t).astype(BF16)
        st["filter_tables"] = (jnp.asarray(feats), jnp.asarray(window))
    nb_p, nb_s = t_p // TOKEN_BLOCK, l_s // TOKEN_BLOCK
    merged_mod_map = lambda i: (jnp.where(i < nb_p, 0, 1 + (i - nb_p) // nb_s), 0, 0)
    n_blocks = (t_all // TOKEN_BLOCK) * (LOCAL_ROWS // EXPERT_BLOCK) + N_EXPERTS

    caches = None
    for l in range(DEPTH):
        w_in_bf = w_in[l].astype(BF16)
        w_out_bf = w_out[l].astype(BF16)
        swg, swu, swd = (a[l].astype(BF16) for a in (sh_w_gate, sh_w_up, sh_w_down))
        lam_vecs = jnp.stack([lam_q1[l], lam_k1[l], lam_q2[l], lam_k2[l]], axis=0)
        filt_params = (hy_w1[l], hy_b1[l], hy_w2[l], hy_b2[l], hy_freq[l], hy_w3[l])
        parts = []
        for name, st in streams.items():
            L = st["L"]
            kw = dict(seq_len=L, cond_row0=st["row0"], per_seq_cond=st["per_seq"])
            if name == "p":
                uc, uh, q, k, v, ck, cv = _premix(st["x"], mod[l], g_pre_mix[l], w_in_bf, layer=l,
                                                  n_tokens=st["T"], tok0=0,
                                                  caches=caches, want_cache=True, **kw)
                caches = (ck, cv)
                ctx = None
            else:
                uc, uh, q, k, v = _premix(st["x"], mod[l], g_pre_mix[l], w_in_bf, layer=l,
                                          n_tokens=st["T"], tok0=0, rope_tables=rope, **kw)
                ctx = (cache_k, cache_v)
            ya = _conformer(uc, conv_w[l], conv_b[l], conv_ln_g[l], conv_ln_b[l], seq_len=L)
            kspec = _hyena_spectrum(L, st["fmat32"], *st["filter_tables"], *filt_params)
            yb = _hyena(uh, hy_short_w[l], hy_short_b[l], hy_bias[l], st["fmat"], st["fmat_t"],
                        kspec, seq_len=L, seqs_per_step=st["hy_nb"])
            o = _attention(q, k, v, lam_vecs, subln_g[l], seq_len=L, layer=l, ctx=ctx)
            parts.append((ya, yb, o, st["x"]))
        x1, h2p, logits_t = _postmix(parts, mod[l], g_post_mix[l], g_pre_ffn[l], w_out_bf,
                                     router_w[l].T, mod_map=merged_mod_map)

        eidx_t, w_t, cnt = _route_topk(logits_t, router_b[l])
        counts = cnt[:, 0]
        pstart, nblk, n_used = _block_plan(counts, n_blocks)
        lpos, tables = _route_dest(eidx_t, pstart)
        x_buf = _dispatch(tables, lpos, counts, pstart, n_used, h2p, n_blocks)
        y_buf = _experts(x_buf, nblk, pstart, n_used, exp_w_gate, exp_w_up, exp_w_down, layer=l)
        outs = _ffn_out(tables, lpos.T, h2p, w_t.T, x1, mod[l], g_post_ffn[l], swg, swu, swd,
                        y_buf, mod_map=merged_mod_map, stream_tokens=(t_p, t_s))
        streams["p"]["x"], streams["s"]["x"] = outs

    y_prompt = streams["p"]["x"].reshape(n_p, l_p, D_MODEL)
    y_sample = streams["s"]["x"].reshape(n_s, l_s, D_MODEL)
    return (y_prompt, y_sample, caches[0], caches[1])
```

```python
import functools
import math

import numpy as np
import jax
import jax.numpy as jnp
from jax import lax
from jax.experimental import pallas as pl
from jax.experimental.pallas import tpu as pltpu

F32 = jnp.float32
BF16 = jnp.bfloat16

D_MODEL = 1024
DEPTH = 2
GRID_W = 64
D_CONV = 256
D_HYENA = 256
D_ATTN = 512
DIFF_DK = 64
N_HEADS = 4
DIFF_DV = 128
HEAD_W = 2 * DIFF_DK
ROPE_CHUNK = DIFF_DK // 4
IN_COLS = 2 * D_CONV + 3 * D_HYENA + 3 * D_ATTN
COL_HY = 2 * D_CONV
COL_Q = COL_HY + 3 * D_HYENA
COL_K = COL_Q + D_ATTN
COL_V = COL_K + D_ATTN
CONV_WIDTH = 31
CONV_PAD = 16
CONV_ROWS = 256
SUBLANES = 8
HYENA_EMB = 17
FEAT_PAD = 32
HYENA_BANDS = 8
HYENA_MIN_DECAY = math.log(1e-2) / 1.5
HYENA_MAX_DECAY = math.log(1e-2) / 0.3
N_EXPERTS = 64
TOP_K = 8
N_GROUPS = 8
TOPK_GROUPS = 4
GROUP_SCORE_TOP = 2
D_EXPERT = 256
ROUTED_SCALE = 2.5
ROPE_BASE = 10000.0
EPS = 1e-6
N_MOD = 6
COND_ROWS = 16
MOD_COL_BLOCK = 1536

HALF = D_MODEL // 2
LANES = 128
TOKEN_BLOCK = 256
SPARE_ROWS = N_EXPERTS * SUBLANES
LOCAL_ROWS = TOKEN_BLOCK * TOP_K + SPARE_ROWS
EXPERT_BLOCK = 512
EXPERT_IN_SLOTS = 4
EXPERT_OUT_SLOTS = 3
VMEM_LIMIT = 48 * 1024 * 1024


def _cparams(*sem):
    return pltpu.CompilerParams(dimension_semantics=sem, vmem_limit_bytes=VMEM_LIMIT)


def _rms(x, g):
    return x * lax.rsqrt(jnp.mean(x * x, axis=-1, keepdims=True) + EPS) * g


def _silu(x):
    return x * jax.nn.sigmoid(x)


def _dot(a, b):
    return jnp.dot(a, b, preferred_element_type=F32)


def _mod_kernel(c_ref, w_ref, b_ref, o_ref):
    s = _silu(c_ref[...])
    o_ref[...] = _dot(s.astype(BF16), w_ref[...].astype(BF16)) + b_ref[...]


def _modulation(cond, w_mod, b_mod):
    cols = N_MOD * D_MODEL
    cb = MOD_COL_BLOCK
    return pl.pallas_call(
        _mod_kernel,
        out_shape=jax.ShapeDtypeStruct((DEPTH, COND_ROWS, cols), F32),
        grid=(DEPTH, cols // cb),
        in_specs=[
            pl.BlockSpec((COND_ROWS, D_MODEL), lambda l, j: (0, 0)),
            pl.BlockSpec((None, D_MODEL, cb), lambda l, j: (l, 0, j)),
            pl.BlockSpec((None, 1, cb), lambda l, j: (l, 0, j)),
        ],
        out_specs=pl.BlockSpec((None, COND_ROWS, cb), lambda l, j: (l, 0, j)),
        compiler_params=_cparams("arbitrary", "arbitrary"),
        name="modulation",
    )(cond, w_mod, b_mod.reshape(DEPTH, 1, cols))


def _rope_slab(x, cos, sin, lo_mask):
    swapped = jnp.where(lo_mask, pltpu.roll(x, HEAD_W - ROPE_CHUNK, axis=1),
                        pltpu.roll(x, ROPE_CHUNK, axis=1))
    return x * cos + swapped * sin


def _premix_kernel(*refs, rope, cache, layer):
    x_ref, mod_ref, g_ref, w_ref = refs[:4]
    pos = 4
    if rope:
        cos_ref, sin_ref = refs[pos:pos + 2]
        pos += 2
    if cache == "alias":
        pos += 2
    uc_ref, uh_ref, q_ref, k_ref, v_ref = refs[pos:pos + 5]
    pos += 5
    if cache:
        ck_ref, cv_ref = refs[pos:pos + 2]

    x = x_ref[...]
    mod = mod_ref[...]
    h = (_rms(x, g_ref[...]) * (1.0 + mod[1:2]) + mod[0:1]).astype(BF16)
    uc_ref[...] = _dot(h, w_ref[:, 0:COL_HY]).astype(BF16)
    uh_ref[...] = _dot(h, w_ref[:, COL_HY:COL_Q]).astype(BF16)
    uq = _dot(h, w_ref[:, COL_Q:COL_K]) * (DIFF_DK ** -0.5)
    uk = _dot(h, w_ref[:, COL_K:COL_V])
    uv = _dot(h, w_ref[:, COL_V:IN_COLS])
    if rope:
        cos = cos_ref[...]
        sin = sin_ref[...]
        lane = lax.broadcasted_iota(jnp.int32, cos.shape, 1)
        lo_mask = (lane % (2 * ROPE_CHUNK)) < ROPE_CHUNK
        for hh in range(N_HEADS):
            sl = slice(hh * HEAD_W, (hh + 1) * HEAD_W)
            q_ref[:, sl] = _rope_slab(uq[:, sl], cos, sin, lo_mask).astype(BF16)
            k_ref[:, sl] = _rope_slab(uk[:, sl], cos, sin, lo_mask).astype(BF16)
    else:
        q_ref[...] = uq.astype(BF16)
        k_ref[...] = uk.astype(BF16)
    v_ref[...] = uv.astype(BF16)
    if cache == "alias":
        for hh in range(N_HEADS):
            sl = slice(hh * HEAD_W, (hh + 1) * HEAD_W)
            ck_ref[hh] = uk[:, sl]
            cv_ref[hh] = uv[:, sl]
    elif cache == "new":
        for d in range(DEPTH):
            for hh in range(N_HEADS):
                sl = slice(hh * HEAD_W, (hh + 1) * HEAD_W)
                ck_ref[d, hh] = uk[:, sl] if d == layer else jnp.zeros_like(uk[:, sl])
                cv_ref[d, hh] = uv[:, sl] if d == layer else jnp.zeros_like(uv[:, sl])


def _premix(x, mod_l, g, w_in_bf, *, seq_len, cond_row0, per_seq_cond, layer, n_tokens, tok0,
            rope_tables=None, caches=None, want_cache=False):
    T = n_tokens
    tm = TOKEN_BLOCK
    nb = seq_len // tm
    n_seq = T // seq_len
    b0 = tok0 // tm
    if per_seq_cond:
        mod_map = lambda i: (cond_row0 + i // nb, 0, 0)
    else:
        mod_map = lambda i: (cond_row0, 0, 0)
    in_specs = [
        pl.BlockSpec((tm, D_MODEL), lambda i: (i + b0, 0)),
        pl.BlockSpec((None, N_MOD, D_MODEL), mod_map),
        pl.BlockSpec((1, D_MODEL), lambda i: (0, 0)),
        pl.BlockSpec((D_MODEL, IN_COLS), lambda i: (0, 0)),
    ]
    args = [x, mod_l, g.reshape(1, D_MODEL), w_in_bf]
    rope = rope_tables is not None
    if rope:
        in_specs += [pl.BlockSpec((tm, HEAD_W), lambda i: (i % nb, 0))] * 2
        args += list(rope_tables)
    out_shape = [
        jax.ShapeDtypeStruct((T, COL_HY), BF16),
        jax.ShapeDtypeStruct((T, 3 * D_HYENA), BF16),
        jax.ShapeDtypeStruct((T, D_ATTN), BF16),
        jax.ShapeDtypeStruct((T, D_ATTN), BF16),
        jax.ShapeDtypeStruct((T, D_ATTN), BF16),
    ]
    out_specs = [
        pl.BlockSpec((tm, COL_HY), lambda i: (i, 0)),
        pl.BlockSpec((tm, 3 * D_HYENA), lambda i: (i, 0)),
        pl.BlockSpec((tm, D_ATTN), lambda i: (i, 0)),
        pl.BlockSpec((tm, D_ATTN), lambda i: (i, 0)),
        pl.BlockSpec((tm, D_ATTN), lambda i: (i, 0)),
    ]
    cache = None
    aliases = {}
    if want_cache:
        cache = "new"
        cshape = (n_seq, DEPTH, N_HEADS, seq_len, HEAD_W)
        out_shape += [jax.ShapeDtypeStruct(cshape, F32)] * 2
        if caches is None:
            out_specs += [pl.BlockSpec((None, DEPTH, N_HEADS, tm, HEAD_W),
                                       lambda i: (i // nb, 0, 0, i % nb, 0))] * 2
        else:
            cache = "alias"
            out_specs += [pl.BlockSpec((None, None, N_HEADS, tm, HEAD_W),
                                       lambda i: (i // nb, layer, 0, i % nb, 0))] * 2
            aliases = {len(args): 5, len(args) + 1: 6}
            in_specs += [pl.BlockSpec(memory_space=pl.ANY)] * 2
            args += list(caches)
    return pl.pallas_call(
        functools.partial(_premix_kernel, rope=rope, cache=cache, layer=layer),
        out_shape=out_shape,
        grid=(T // tm,),
        in_specs=in_specs,
        out_specs=out_specs,
        input_output_aliases=aliases,
        compiler_params=_cparams("arbitrary"),
        name="premix",
    )(*args)


def _conformer_kernel(u_ref, w_ref, cb_ref, g_ref, b_ref, o_ref, pad_ref, sh_ref, *, seq_len, rows):
    L = seq_len
    u = u_ref[...].astype(F32)
    a = u[:, :D_CONV] * jax.nn.sigmoid(u[:, D_CONV:])
    zeros = jnp.zeros((CONV_PAD, D_CONV), F32)
    pad_ref[0:CONV_PAD, :] = zeros
    pad_ref[CONV_PAD + L:2 * CONV_PAD + L, :] = zeros
    pad_ref[CONV_PAD:CONV_PAD + L, :] = a
    span = L + 2 * CONV_PAD - SUBLANES
    for r in range(SUBLANES):
        sh_ref[r, 0:span, :] = pad_ref[r:r + span, :]
    first = CONV_PAD - (CONV_WIDTH - 1) // 2

    def chunk(c, carry):
        base = pl.multiple_of(c * rows, rows)
        acc = jnp.zeros((rows, D_CONV), F32)
        for t in range(CONV_WIDTH):
            off = first + t
            tap = sh_ref[off % SUBLANES, pl.ds(base + (off // SUBLANES) * SUBLANES, rows), :]
            acc = acc + tap * w_ref[t:t + 1, :]
        acc = acc + cb_ref[...]
        mu = jnp.mean(acc, axis=-1, keepdims=True)
        d = acc - mu
        var = jnp.mean(d * d, axis=-1, keepdims=True)
        y = d * lax.rsqrt(var + EPS) * g_ref[...] + b_ref[...]
        o_ref[pl.ds(base, rows), :] = _silu(y).astype(BF16)
        return carry

    lax.fori_loop(0, L // rows, chunk, 0)


def _conformer(uc, conv_w, conv_b, ln_g, ln_b, *, seq_len):
    T = uc.shape[0]
    n_seq = T // seq_len
    vec = lambda a: a.reshape(1, D_CONV)
    return pl.pallas_call(
        functools.partial(_conformer_kernel, seq_len=seq_len, rows=min(seq_len, CONV_ROWS)),
        out_shape=jax.ShapeDtypeStruct((T, D_CONV), BF16),
        grid=(n_seq,),
        in_specs=[
            pl.BlockSpec((seq_len, 2 * D_CONV), lambda i: (i, 0)),
            pl.BlockSpec((CONV_WIDTH, D_CONV), lambda i: (0, 0)),
            pl.BlockSpec((1, D_CONV), lambda i: (0, 0)),
            pl.BlockSpec((1, D_CONV), lambda i: (0, 0)),
            pl.BlockSpec((1, D_CONV), lambda i: (0, 0)),
        ],
        out_specs=pl.BlockSpec((seq_len, D_CONV), lambda i: (i, 0)),
        scratch_shapes=[pltpu.VMEM((seq_len + 2 * CONV_PAD, D_CONV), F32),
                        pltpu.VMEM((SUBLANES, seq_len + 2 * CONV_PAD, D_CONV), F32)],
        compiler_params=_cparams("arbitrary"),
        name="conformer",
    )(uc, conv_w, vec(conv_b), vec(ln_g), vec(ln_b))


def _dft_matrices(L):
    k = np.arange(L, dtype=np.int64)[:, None]
    m = np.arange(L, dtype=np.int64)[None, :]
    ang = 2.0 * np.pi * (((2 * k + 1) * m) % (4 * L)).astype(np.float64) / (4 * L)
    fwd = np.concatenate([np.cos(ang), np.sin(ang)], axis=0).astype(np.float32)
    return fwd, np.ascontiguousarray(fwd.T)


def _filter_tables(L):
    t = np.linspace(0.0, 1.0, L)
    bands = np.linspace(1e-4, HYENA_BANDS - 1, HYENA_BANDS)
    ang = (2.0 * np.pi / L) * np.arange(L)[:, None] * bands[None, :]
    feats = np.zeros((L, FEAT_PAD), np.float32)
    feats[:, :HYENA_EMB] = np.concatenate([t[:, None], np.cos(ang), -np.sin(ang)], axis=-1)
    deltas = np.abs(np.linspace(HYENA_MIN_DECAY, HYENA_MAX_DECAY, D_HYENA))
    window = np.exp(-t[:, None] * deltas[None, :]).astype(np.float32)
    return feats, window


def _dot3(a, b):
    a_hi, a_lo = _split_bf16(a)
    b_hi, b_lo = _split_bf16(b)
    return _dot(a_hi, b_hi) + (_dot(a_hi, b_lo) + _dot(a_lo, b_hi))


def _spectrum_kernel(feat_ref, win_ref, w1_ref, b1_ref, w2_ref, b2_ref, fr_ref, w3_ref, f_ref,
                     o_ref, *, seq_len):
    L = seq_len
    C = D_HYENA
    fr = fr_ref[...]
    hid = jnp.sin(fr * (_dot3(feat_ref[...], w1_ref[...]) + b1_ref[...]))
    hid = jnp.sin(fr * (_dot3(hid, w2_ref[...]) + b2_ref[...]))
    filt = _dot3(hid, w3_ref[...])
    win = win_ref[...]
    first = lax.broadcasted_iota(jnp.int32, (L, C), 0) == 0
    for o in range(2):
        fwd = filt[:, o * C:(o + 1) * C] * win
        bwd = jnp.where(first, 0.0, filt[:, (2 + o) * C:(3 + o) * C] * win)
        o_ref[o, 0:L, :] = _dot3(f_ref[0:L, :], fwd + bwd) * (1.0 / L)
        o_ref[o, L:2 * L, :] = _dot3(f_ref[L:2 * L, :], bwd - fwd) * (1.0 / L)


def _hyena_spectrum(L, fmat, feats, window, hy_w1, hy_b1, hy_w2, hy_b2, hy_freq, hy_w3):
    w1 = jnp.zeros((FEAT_PAD, hy_w1.shape[1]), F32).at[:HYENA_EMB].set(hy_w1)
    vec = lambda a: a.reshape(1, -1)
    return pl.pallas_call(
        functools.partial(_spectrum_kernel, seq_len=L),
        out_shape=jax.ShapeDtypeStruct((2, 2 * L, D_HYENA), F32),
        compiler_params=pltpu.CompilerParams(vmem_limit_bytes=VMEM_LIMIT),
        name="hyena_spectrum",
    )(feats, window, w1, vec(hy_b1), hy_w2, vec(hy_b2), vec(hy_freq), hy_w3, fmat)


def _hyena_kernel(u_ref, sw_ref, sb_ref, hb_ref, f_ref, ft_ref, ks_ref, o_ref, pad_ref,
                  *, seq_len, n_seq):
    L = seq_len
    C = D_HYENA
    zrow = jnp.zeros((8, 3 * C), F32)
    seqs = range(n_seq)
    s = []
    for b in seqs:
        u = u_ref[b].astype(F32)
        pad_ref[b, 0:8, :] = zrow
        pad_ref[b, 8 + L:16 + L, :] = zrow
        pad_ref[b, 8:8 + L, :] = u
        s.append(pad_ref[b, 7:7 + L, :] * sw_ref[0:1, :] + u * sw_ref[1:2, :]
                 + pad_ref[b, 9:9 + L, :] * sw_ref[2:3, :] + sb_ref[...])
    z = [s[b][:, 0:C] for b in seqs]
    for o in range(2):
        kre, kim = ks_ref[o, 0:L, :], ks_ref[o, L:2 * L, :]
        spec = [_dot(f_ref[...], z[b].astype(BF16)) for b in seqs]
        y = []
        for b in seqs:
            zr, za = spec[b][:L], spec[b][L:]
            y.append(jnp.concatenate([zr * kre + za * kim, za * kre - zr * kim], axis=0))
        conv = [_dot(ft_ref[...], y[b].astype(BF16)) for b in seqs]
        z = [s[b][:, (o + 1) * C:(o + 2) * C] * (conv[b] + hb_ref[o:o + 1, :] * z[b]) for b in seqs]
    for b in seqs:
        o_ref[b] = z[b].astype(BF16)


def _hyena(uh, short_w, short_b, hy_bias, fmat_bf, fmat_t_bf, kspec, *, seq_len, seqs_per_step):
    T = uh.shape[0]
    n_seq = T // seq_len
    L = seq_len
    nb = seqs_per_step
    u3 = uh.reshape(n_seq, L, 3 * D_HYENA)
    out = pl.pallas_call(
        functools.partial(_hyena_kernel, seq_len=L, n_seq=nb),
        out_shape=jax.ShapeDtypeStruct((n_seq, L, D_HYENA), BF16),
        grid=(n_seq // nb,),
        in_specs=[
            pl.BlockSpec((nb, L, 3 * D_HYENA), lambda i: (i, 0, 0)),
            pl.BlockSpec((3, 3 * D_HYENA), lambda i: (0, 0)),
            pl.BlockSpec((1, 3 * D_HYENA), lambda i: (0, 0)),
            pl.BlockSpec((2, D_HYENA), lambda i: (0, 0)),
            pl.BlockSpec((2 * L, L), lambda i: (0, 0)),
            pl.BlockSpec((L, 2 * L), lambda i: (0, 0)),
            pl.BlockSpec((2, 2 * L, D_HYENA), lambda i: (0, 0, 0)),
        ],
        out_specs=pl.BlockSpec((nb, L, D_HYENA), lambda i: (i, 0, 0)),
        scratch_shapes=[pltpu.VMEM((nb, L + 16, 3 * D_HYENA), F32)],
        compiler_params=_cparams("arbitrary"),
        name="hyena",
    )(u3, short_w, short_b.reshape(1, 3 * D_HYENA), hy_bias, fmat_bf, fmat_t_bf, kspec)
    return out.reshape(T, D_HYENA)


def _attn_kernel(*refs, lam_init, has_ctx):
    q_ref, k_ref, v_ref = refs[:3]
    pos = 3
    if has_ctx:
        ck_ref, cv_ref = refs[pos:pos + 2]
        pos += 2
    lam_ref, g_ref, o_ref = refs[pos:pos + 3]

    lv = lam_ref[...]
    lam = (jnp.exp(jnp.sum(lv[0:1] * lv[1:2], axis=-1, keepdims=True))
           - jnp.exp(jnp.sum(lv[2:3] * lv[3:4], axis=-1, keepdims=True)) + lam_init)

    tq = q_ref.shape[0]
    lane = lax.broadcasted_iota(jnp.int32, (tq, HEAD_W), 1)
    dn = (((1,), (1,)), ((), ()))
    group = 2
    for h0 in range(0, N_HEADS, group):
        heads = range(h0, h0 + group)
        sls = [slice(h * HEAD_W, (h + 1) * HEAD_W) for h in heads]
        q2 = []
        for sl in sls:
            q = q_ref[:, sl]
            zero = jnp.zeros_like(q)
            q2.append(jnp.concatenate([jnp.where(lane < DIFF_DK, q, zero),
                                       jnp.where(lane >= DIFF_DK, q, zero)], axis=0))
        s = [lax.dot_general(q2[j], k_ref[:, sl], dn, preferred_element_type=F32)
             for j, sl in enumerate(sls)]
        m = [jnp.max(x, axis=-1, keepdims=True) for x in s]
        if has_ctx:
            sc = [lax.dot_general(q2[j], ck_ref[h].astype(BF16), dn, preferred_element_type=F32)
                  for j, h in enumerate(heads)]
            m = [jnp.maximum(m[j], jnp.max(sc[j], axis=-1, keepdims=True)) for j in range(group)]
        p = [jnp.exp(s[j] - m[j]) for j in range(group)]
        den = [jnp.sum(x, axis=-1, keepdims=True) for x in p]
        acc = [_dot(p[j].astype(BF16), v_ref[:, sl]) for j, sl in enumerate(sls)]
        if has_ctx:
            pc = [jnp.exp(sc[j] - m[j]) for j in range(group)]
            den = [den[j] + jnp.sum(pc[j], axis=-1, keepdims=True) for j in range(group)]
            acc = [acc[j] + _dot(pc[j].astype(BF16), cv_ref[h].astype(BF16))
                   for j, h in enumerate(heads)]
        for j, sl in enumerate(sls):
            both = acc[j] / den[j]
            o = both[:tq] - lam * both[tq:]
            o_ref[:, sl] = (_rms(o, g_ref[...]) * (1.0 - lam_init)).astype(BF16)


def _attention(q, k, v, lam_vecs, subln_g, *, seq_len, layer, ctx=None):
    T = q.shape[0]
    L = seq_len
    n_seq = T // L
    tq = TOKEN_BLOCK
    nq = L // tq
    lam_init = 0.8 - 0.6 * math.exp(-0.3 * layer)
    q3, k3, v3 = (a.reshape(n_seq, L, D_ATTN) for a in (q, k, v))
    in_specs = [
        pl.BlockSpec((None, tq, D_ATTN), lambda b, i: (b, i, 0)),
        pl.BlockSpec((None, L, D_ATTN), lambda b, i: (b, 0, 0)),
        pl.BlockSpec((None, L, D_ATTN), lambda b, i: (b, 0, 0)),
    ]
    args = [q3, k3, v3]
    if ctx is not None:
        past = ctx[0].shape[3]
        in_specs += [pl.BlockSpec((None, None, N_HEADS, past, HEAD_W),
                                  lambda b, i: (b, layer, 0, 0, 0))] * 2
        args += list(ctx)
    in_specs += [
        pl.BlockSpec((4, DIFF_DK), lambda b, i: (0, 0)),
        pl.BlockSpec((1, DIFF_DV), lambda b, i: (0, 0)),
    ]
    args += [lam_vecs, subln_g.reshape(1, DIFF_DV)]
    out = pl.pallas_call(
        functools.partial(_attn_kernel, lam_init=lam_init, has_ctx=ctx is not None),
        out_shape=jax.ShapeDtypeStruct((n_seq, L, D_ATTN), BF16),
        grid=(n_seq, nq),
        in_specs=in_specs,
        out_specs=pl.BlockSpec((None, tq, D_ATTN), lambda b, i: (b, i, 0)),
        compiler_params=_cparams("arbitrary", "arbitrary"),
        name="diff_attention",
    )(*args)
    return out.reshape(T, D_ATTN)


def _split_bf16(a):
    hi = a.astype(BF16)
    lo = (a - hi.astype(F32)).astype(BF16)
    return hi, lo


def _pack_pair(a, b):
    ua = lax.bitcast_convert_type(a.astype(BF16).astype(F32), jnp.uint32)
    ub = lax.bitcast_convert_type(b.astype(BF16).astype(F32), jnp.uint32)
    return (ua >> 16) | ub


def _unpack_pair(w):
    a = lax.bitcast_convert_type(w << 16, F32)
    b = lax.bitcast_convert_type(w & jnp.uint32(0xFFFF0000), F32)
    return a, b


def _postmix_kernel(*refs, stream_blocks):
    ns = len(stream_blocks)
    mod_ref, gpost_ref, gpre_ref, w_ref, rw_ref, x1_ref, h2_ref, lg_ref = refs[4 * ns:]
    i = pl.program_id(0)

    def body(ya_ref, yb_ref, o_ref, x_ref):
        n = x_ref.shape[0]
        halves = [slice(0, n // 2), slice(n // 2, n)]
        mod = mod_ref[...]
        y = [_dot(ya_ref[r, :], w_ref[0:D_CONV, :])
             + _dot(yb_ref[r, :], w_ref[D_CONV:D_CONV + D_HYENA, :])
             + _dot(o_ref[r, :], w_ref[D_CONV + D_HYENA:, :]) for r in halves]
        x1 = [x_ref[r, :] + mod[2:3] * _rms(y[j], gpost_ref[...]) for j, r in enumerate(halves)]
        for j, r in enumerate(halves):
            x1_ref[r, :] = x1[j]
        h2 = [_rms(x1[j], gpre_ref[...]) * (1.0 + mod[4:5]) + mod[3:4] for j in range(2)]
        for j, r in enumerate(halves):
            h2_ref[r, :] = _pack_pair(h2[j][:, :HALF], h2[j][:, HALF:])
        r_hi, r_lo = _split_bf16(rw_ref[...])
        dn = (((1,), (1,)), ((), ()))
        mm = lambda a, b: lax.dot_general(a, b, dn, preferred_element_type=F32)
        for j, r in enumerate(halves):
            h_hi, h_lo = _split_bf16(h2[j])
            lg_ref[:, r] = mm(r_hi, h_hi) + (mm(r_lo, h_hi) + mm(r_hi, h_lo))

    lo = 0
    for j, nbk in enumerate(stream_blocks):
        pl.when((i >= lo) & (i < lo + nbk))(functools.partial(body, *refs[4 * j:4 * j + 4]))
        lo += nbk


def _postmix(parts, mod_l, g_post, g_pre, w_out_bf, router_wt, *, mod_map):
    t_all = sum(p[0].shape[0] for p in parts)
    tm = TOKEN_BLOCK
    full = lambda a: pl.BlockSpec(a.shape, lambda i: (0,) * a.ndim)
    gp, ge = g_post.reshape(1, D_MODEL), g_pre.reshape(1, D_MODEL)
    stream_blocks = tuple(p[0].shape[0] // tm for p in parts)
    in_specs, args = [], []
    lo = 0
    for part, nbk in zip(parts, stream_blocks):
        local = lambda i, lo=lo, nbk=nbk: (jnp.clip(i - lo, 0, nbk - 1), 0)
        in_specs += [pl.BlockSpec((tm, D_CONV), local), pl.BlockSpec((tm, D_HYENA), local),
                     pl.BlockSpec((tm, D_ATTN), local), pl.BlockSpec((tm, D_MODEL), local)]
        args += list(part)
        lo += nbk
    in_specs += [pl.BlockSpec((None, N_MOD, D_MODEL), mod_map),
                 full(gp), full(ge), full(w_out_bf), full(router_wt)]
    args += [mod_l, gp, ge, w_out_bf, router_wt]
    return pl.pallas_call(
        functools.partial(_postmix_kernel, stream_blocks=stream_blocks),
        out_shape=[
            jax.ShapeDtypeStruct((t_all, D_MODEL), F32),
            jax.ShapeDtypeStruct((t_all, HALF), jnp.uint32),
            jax.ShapeDtypeStruct((N_EXPERTS, t_all), F32),
        ],
        grid=(t_all // tm,),
        in_specs=in_specs,
        out_specs=[pl.BlockSpec((tm, D_MODEL), lambda i: (i, 0)),
                   pl.BlockSpec((tm, HALF), lambda i: (i, 0)),
                   pl.BlockSpec((N_EXPERTS, tm), lambda i: (0, i))],
        compiler_params=_cparams("arbitrary"),
        name="postmix",
    )(*args)


def _first_max(vals, idx, sentinel):
    m = jnp.max(vals, axis=0, keepdims=True)
    i = jnp.min(jnp.where(vals == m, idx, sentinel), axis=0, keepdims=True)
    return m, i


def _route_topk_kernel(lg_ref, rb_ref, eidx_ref, w_ref, cnt_ref):
    step = pl.program_id(0)
    per = N_EXPERTS // N_GROUPS
    n = lg_ref.shape[1]
    s = jax.nn.sigmoid(lg_ref[...])
    sel = s + rb_ref[...]
    sub = lax.broadcasted_iota(jnp.int32, (per, n), 0)
    neg = jnp.float32(-jnp.inf)

    gs = []
    for g in range(N_GROUPS):
        xg = sel[g * per:(g + 1) * per]
        total = jnp.zeros((1, n), F32)
        for _ in range(GROUP_SCORE_TOP):
            m, i = _first_max(xg, sub, per)
            total = total + m
            xg = jnp.where(sub == i, neg, xg)
        gs.append(total)
    gscore = jnp.concatenate(gs, axis=0)
    gkeep = jnp.zeros((N_GROUPS, n), F32)
    for _ in range(TOPK_GROUPS):
        _, gi = _first_max(gscore, sub, N_GROUPS)
        hit = sub == gi
        gkeep = jnp.where(hit, 1.0, gkeep)
        gscore = jnp.where(hit, neg, gscore)

    eio = [sub + g * per for g in range(N_GROUPS)]
    sg = [s[g * per:(g + 1) * per] for g in range(N_GROUPS)]
    mk = [jnp.where(gkeep[g:g + 1] > 0.5, sel[g * per:(g + 1) * per], neg) for g in range(N_GROUPS)]
    oh = [jnp.zeros((per, n), F32) for _ in range(N_GROUPS)]
    idx_rows, s_rows = [], []
    for _ in range(TOP_K):
        best = functools.reduce(jnp.maximum, mk)
        m = jnp.max(best, axis=0, keepdims=True)
        cand = functools.reduce(jnp.minimum,
                                [jnp.where(mk[g] == m, eio[g], N_EXPERTS) for g in range(N_GROUPS)])
        idx = jnp.min(cand, axis=0, keepdims=True)
        acc = jnp.zeros((per, n), F32)
        for g in range(N_GROUPS):
            hit = eio[g] == idx
            acc = acc + jnp.where(hit, sg[g], 0.0)
            mk[g] = jnp.where(hit, neg, mk[g])
            oh[g] = jnp.where(hit, 1.0, oh[g])
        idx_rows.append(idx)
        s_rows.append(jnp.sum(acc, axis=0, keepdims=True))
    eidx_ref[...] = jnp.concatenate(idx_rows, axis=0)
    sk = jnp.concatenate(s_rows, axis=0)
    w_ref[...] = sk / jnp.sum(sk, axis=0, keepdims=True) * ROUTED_SCALE

    @pl.when(step == 0)
    def _():
        cnt_ref[...] = jnp.zeros_like(cnt_ref)

    tot8 = jnp.zeros((N_EXPERTS, 1), F32)
    for w0 in range(0, n, TOKEN_BLOCK):
        tot = jnp.concatenate([jnp.sum(o[:, w0:w0 + TOKEN_BLOCK], axis=1, keepdims=True)
                               for o in oh], axis=0)
        tot8 = tot8 + _ceil_tile(tot)
    cnt_ref[...] = cnt_ref[...] + jnp.broadcast_to(tot8, cnt_ref.shape).astype(jnp.int32)


def _ceil_tile(count):
    return jnp.floor((count + (SUBLANES - 1.0)) * (1.0 / SUBLANES)) * SUBLANES


def _route_topk(logits_t, router_b):
    t_all = logits_t.shape[1]
    rb = 2 * TOKEN_BLOCK if t_all % (2 * TOKEN_BLOCK) == 0 else TOKEN_BLOCK
    return pl.pallas_call(
        _route_topk_kernel,
        out_shape=[
            jax.ShapeDtypeStruct((TOP_K, t_all), jnp.int32),
            jax.ShapeDtypeStruct((TOP_K, t_all), F32),
            jax.ShapeDtypeStruct((N_EXPERTS, LANES), jnp.int32),
        ],
        grid=(t_all // rb,),
        in_specs=[pl.BlockSpec((N_EXPERTS, rb), lambda i: (0, i)),
                  pl.BlockSpec((N_EXPERTS, 1), lambda i: (0, 0))],
        out_specs=[pl.BlockSpec((TOP_K, rb), lambda i: (0, i)),
                   pl.BlockSpec((TOP_K, rb), lambda i: (0, i)),
                   pl.BlockSpec((N_EXPERTS, LANES), lambda i: (0, 0))],
        compiler_params=_cparams("arbitrary"),
        name="route_topk",
    )(logits_t, router_b.reshape(N_EXPERTS, 1))


def _route_dest_kernel(eidx_ref, ps_ref, lpos_ref, cs_ref, cn_ref, cd_ref, carry_ref):
    step = pl.program_id(0)
    per = N_EXPERTS // N_GROUPS
    n = eidx_ref.shape[1]

    @pl.when(step == 0)
    def _():
        carry_ref[...] = jnp.zeros_like(carry_ref)
        cs_ref[...] = jnp.zeros_like(cs_ref)
        cn_ref[...] = jnp.zeros_like(cn_ref)
        cd_ref[...] = jnp.zeros_like(cd_ref)

    eidx = eidx_ref[...]
    sub = lax.broadcasted_iota(jnp.int32, (per, n), 0)
    rows = []
    for g in range(N_GROUPS):
        eio = sub + g * per
        o = jnp.zeros((per, n), F32)
        for k in range(TOP_K):
            o = jnp.where(eio == eidx[k:k + 1], 1.0, o)
        rows.append(o)
    onehot = jnp.concatenate(rows, axis=0)
    r = lax.broadcasted_iota(jnp.int32, (n, n), 0)
    c = lax.broadcasted_iota(jnp.int32, (n, n), 1)
    before = jnp.where(r < c, 1.0, 0.0).astype(BF16)
    rank = _dot(onehot.astype(BF16), before)
    size = _ceil_tile(jnp.sum(onehot, axis=1, keepdims=True))
    er = lax.broadcasted_iota(jnp.int32, (N_EXPERTS, N_EXPERTS), 0)
    ec = lax.broadcasted_iota(jnp.int32, (N_EXPERTS, N_EXPERTS), 1)
    lower = jnp.where(ec < er, 1.0, 0.0).astype(BF16)
    tiles = jnp.broadcast_to(size * (1.0 / SUBLANES), (N_EXPERTS, LANES)).astype(BF16)
    start = _dot(lower, tiles)[:, 0:1] * SUBLANES
    base = (rank + start).astype(jnp.int32)
    out = []
    for k in range(TOP_K):
        acc = jnp.zeros((per, n), jnp.int32)
        for g in range(N_GROUPS):
            acc = acc + jnp.where(sub + g * per == eidx[k:k + 1], base[g * per:(g + 1) * per], 0)
        out.append(jnp.sum(acc, axis=0, keepdims=True))
    lpos_ref[...] = jnp.concatenate(out, axis=0)

    col = lax.broadcasted_iota(jnp.int32, cs_ref.shape, 1) == step
    carry = carry_ref[:, 0:1]
    cs_ref[...] = jnp.where(col, start.astype(jnp.int32), cs_ref[...])
    cn_ref[...] = jnp.where(col, size.astype(jnp.int32), cn_ref[...])
    cd_ref[...] = jnp.where(col, carry.astype(jnp.int32) + ps_ref[...], cd_ref[...])
    carry_ref[...] = carry_ref[...] + jnp.broadcast_to(size, carry_ref.shape)


def _route_dest(eidx_t, pstart):
    t_all = eidx_t.shape[1]
    rb = TOKEN_BLOCK
    n_win = t_all // rb
    cols = -(-n_win // LANES) * LANES
    table = jax.ShapeDtypeStruct((N_EXPERTS, cols), jnp.int32)
    table_spec = pl.BlockSpec((N_EXPERTS, cols), lambda i: (0, 0))
    lpos, cs, cn, cd = pl.pallas_call(
        _route_dest_kernel,
        out_shape=[jax.ShapeDtypeStruct((TOP_K, t_all), jnp.int32), table, table, table],
        grid=(n_win,),
        in_specs=[pl.BlockSpec((TOP_K, rb), lambda i: (0, i)),
                  pl.BlockSpec((N_EXPERTS, 1), lambda i: (0, 0))],
        out_specs=[pl.BlockSpec((TOP_K, rb), lambda i: (0, i)), table_spec, table_spec, table_spec],
        scratch_shapes=[pltpu.VMEM((N_EXPERTS, LANES), F32)],
        compiler_params=_cparams("arbitrary"),
        name="route_dest",
    )(eidx_t, pstart.reshape(N_EXPERTS, 1))
    flat = lambda a: a[:, :n_win].T.reshape(-1)
    return lpos, (flat(cs), flat(cn), flat(cd))


def _wait_rows(hbm, sem, n_rows):
    pltpu.make_async_copy(hbm.at[pl.ds(0, n_rows)], hbm.at[pl.ds(0, n_rows)], sem).wait()


def _window_chunks(tables, w, local_ref, hbm, sem, *, to_hbm, spare_row):
    cs_ref, cn_ref, cd_ref = tables
    tile = lambda v: pl.multiple_of(v, SUBLANES)

    def copy(local_row, hbm_row, size):
        lo = local_ref.at[pl.ds(tile(local_row), tile(size))]
        hi = hbm.at[pl.ds(tile(hbm_row), tile(size))]
        src, dst = (lo, hi) if to_hbm else (hi, lo)
        pltpu.make_async_copy(src, dst, sem).start()

    def per_expert(e, carry):
        i = w * N_EXPERTS + e
        copy(cs_ref[i], cd_ref[i], cn_ref[i])
        return carry

    lax.fori_loop(0, N_EXPERTS, per_expert, 0, unroll=8)
    last = w * N_EXPERTS + (N_EXPERTS - 1)
    used = cs_ref[last] + cn_ref[last]
    copy(used, spare_row, LOCAL_ROWS - used)


def _zero_padding(cnt_ref, ps_ref, nu_ref, zero_ref, x_hbm, zsem, n_blocks, *, wait):
    def copy(row, size):
        cp = pltpu.make_async_copy(zero_ref.at[pl.ds(0, size)], x_hbm.at[pl.ds(row, size)], zsem)
        if wait:
            cp.wait()
        else:
            cp.start()

    def per_expert(e, carry):
        cnt = cnt_ref[e]
        pad = (EXPERT_BLOCK - cnt % EXPERT_BLOCK) % EXPERT_BLOCK
        copy(pl.multiple_of(ps_ref[e] + cnt, SUBLANES), pl.multiple_of(pad, SUBLANES))
        return carry

    lax.fori_loop(0, N_EXPERTS, per_expert, 0)

    def per_block(j, carry):
        copy(pl.multiple_of(j * EXPERT_BLOCK, EXPERT_BLOCK), EXPERT_BLOCK)
        return carry

    lax.fori_loop(nu_ref[0], n_blocks, per_block, 0)


def _dispatch_kernel(cs_ref, cn_ref, cd_ref, cnt_ref, ps_ref, nu_ref, lpos_ref, h_ref, x_hbm,
                     gbuf, zero_ref, sem, zsem, ssem):
    w = pl.program_id(0)
    nsteps = pl.num_programs(0)
    slot = w % 2
    n_blocks = (x_hbm.shape[0] - 2 * SPARE_ROWS) // EXPERT_BLOCK
    spare0 = n_blocks * EXPERT_BLOCK
    zero_args = (cnt_ref, ps_ref, nu_ref, zero_ref, x_hbm, zsem, n_blocks)

    def spare_zero(j):
        return pltpu.make_async_copy(
            zero_ref, x_hbm.at[pl.ds(spare0 + j * EXPERT_BLOCK, EXPERT_BLOCK)], ssem)

    @pl.when(w == 0)
    def _():
        zero_ref[...] = jnp.zeros_like(zero_ref)
        _zero_padding(*zero_args, wait=False)
        for j in range(2 * SPARE_ROWS // EXPERT_BLOCK):
            spare_zero(j).start()

    ha, hb = _unpack_pair(h_ref[...])
    ha, hb = ha.astype(BF16), hb.astype(BF16)
    lp = lpos_ref[...]
    n = h_ref.shape[0]
    last = w * N_EXPERTS + (N_EXPERTS - 1)
    used = cs_ref[last] + cn_ref[last]

    lp = lp.astype(jnp.int16)
    one = jnp.ones((n, n), BF16)

    def permute(c):
        row = (lax.broadcasted_iota(jnp.int32, (n, n), 0) + c * n).astype(jnp.int16)
        pb = jnp.zeros((n, n), BF16)
        for k in range(TOP_K):
            pb = jnp.where(row == lp[k:k + 1, :], one, pb)
        ga = lax.bitcast_convert_type(_dot(pb, ha), jnp.uint32)
        gb = lax.bitcast_convert_type(_dot(pb, hb), jnp.uint32)
        gbuf[slot, c * n:(c + 1) * n, :] = (ga >> 16) | gb

    def blank(c):
        gbuf[slot, c * n:(c + 1) * n, :] = jnp.zeros((n, HALF), jnp.uint32)

    for c in range(LOCAL_ROWS // n):
        if c <= TOP_K:
            permute(c)
        else:
            pl.when(c * n < used)(functools.partial(permute, c))
            pl.when(c * n >= used)(functools.partial(blank, c))

    @pl.when(w == 0)
    def _():
        for j in range(2 * SPARE_ROWS // EXPERT_BLOCK):
            spare_zero(j).wait()

    _window_chunks((cs_ref, cn_ref, cd_ref), w, gbuf.at[slot], x_hbm, sem.at[slot],
                   to_hbm=True, spare_row=spare0 + slot * SPARE_ROWS)

    @pl.when(w > 0)
    def _():
        _wait_rows(x_hbm, sem.at[1 - slot], LOCAL_ROWS)

    @pl.when(w == nsteps - 1)
    def _():
        _wait_rows(x_hbm, sem.at[slot], LOCAL_ROWS)
        _zero_padding(*zero_args, wait=True)


def _dispatch(tables, lpos, counts, pstart, n_used, h2p, n_blocks):
    t_all = h2p.shape[0]
    sb = TOKEN_BLOCK
    smem = lambda: pl.BlockSpec(memory_space=pltpu.SMEM)
    return pl.pallas_call(
        _dispatch_kernel,
        out_shape=jax.ShapeDtypeStruct((n_blocks * EXPERT_BLOCK + 2 * SPARE_ROWS, HALF), jnp.uint32),
        grid=(t_all // sb,),
        in_specs=[smem(), smem(), smem(), smem(), smem(), smem(),
                  pl.BlockSpec((TOP_K, sb), lambda i: (0, i)),
                  pl.BlockSpec((sb, HALF), lambda i: (i, 0))],
        out_specs=pl.BlockSpec(memory_space=pl.ANY),
        scratch_shapes=[pltpu.VMEM((2, LOCAL_ROWS, HALF), jnp.uint32),
                        pltpu.VMEM((EXPERT_BLOCK, HALF), jnp.uint32),
                        pltpu.SemaphoreType.DMA((2,)), pltpu.SemaphoreType.DMA,
                        pltpu.SemaphoreType.DMA],
        compiler_params=_cparams("arbitrary"),
        name="dispatch",
    )(*tables, counts, pstart, n_used, lpos, h2p)


def _expert_kernel(nb_ref, ps_ref, nu_ref, wg_ref, wu_ref, wd_ref, x_hbm, y_hbm,
                   wg_bf, wu_bf, wd_bf, xbuf, ybuf, zero_ref, xsem, ysem, zsem):
    e = pl.program_id(0)
    nblk = nb_ref[e]
    first = ps_ref[e] // EXPERT_BLOCK
    n_used = nu_ref[0]
    xs, ys = xbuf.shape[0], ybuf.shape[0]
    wg_bf[...] = wg_ref[...].astype(BF16)
    wu_bf[...] = wu_ref[...].astype(BF16)
    wd_bf[...] = wd_ref[...].astype(BF16)

    def rows(b):
        return pl.ds(pl.multiple_of(b * EXPERT_BLOCK, EXPERT_BLOCK), EXPERT_BLOCK)

    def x_copy(b):
        return pltpu.make_async_copy(x_hbm.at[rows(b)], xbuf.at[b % xs], xsem.at[b % xs])

    def y_copy(b):
        return pltpu.make_async_copy(ybuf.at[b % ys], y_hbm.at[rows(b)], ysem.at[b % ys])

    n_blocks = y_hbm.shape[0] // EXPERT_BLOCK

    def tail(j, carry, wait):
        cp = pltpu.make_async_copy(zero_ref, y_hbm.at[rows(j)], zsem)
        cp.wait() if wait else cp.start()
        return carry

    @pl.when(e == 0)
    def _():
        for b in range(xs - 1):
            pl.when(b < n_used)(lambda b=b: x_copy(b).start())
        zero_ref[...] = jnp.zeros_like(zero_ref)
        lax.fori_loop(n_used, n_blocks, functools.partial(tail, wait=False), 0)

    def block(j, carry):
        b = first + j
        x_copy(b).wait()
        pl.when(b + xs - 1 < n_used)(lambda: x_copy(b + xs - 1).start())
        pl.when(b >= ys)(lambda: y_copy(b - ys).wait())
        y = _swiglu_packed(xbuf[b % xs], wg_bf, wu_bf, wd_bf)
        ybuf[b % ys] = _pack_pair(y[:, :HALF], y[:, HALF:])
        y_copy(b).start()
        return carry

    lax.fori_loop(0, nblk, block, 0)

    @pl.when(e == pl.num_programs(0) - 1)
    def _():
        for d in range(ys):
            b = n_used - ys + d
            pl.when(b >= 0)(lambda b=b: y_copy(b).wait())
        lax.fori_loop(n_used, n_blocks, functools.partial(tail, wait=True), 0)


def _swiglu_packed(xp, wg_ref, wu_ref, wd_ref):
    xa, xb = _unpack_pair(xp)
    xa, xb = xa.astype(BF16), xb.astype(BF16)
    gate = _dot(xa, wg_ref[0:HALF, :]) + _dot(xb, wg_ref[HALF:, :])
    up = _dot(xa, wu_ref[0:HALF, :]) + _dot(xb, wu_ref[HALF:, :])
    return _dot((_silu(gate) * up).astype(BF16), wd_ref[...])


def _experts(x_buf, nblk, pstart, n_used, wg, wu, wd, *, layer):
    grid_spec = pltpu.PrefetchScalarGridSpec(
        num_scalar_prefetch=3,
        grid=(N_EXPERTS,),
        in_specs=[
            pl.BlockSpec((None, None, D_MODEL, D_EXPERT), lambda e, *_: (layer, e, 0, 0)),
            pl.BlockSpec((None, None, D_MODEL, D_EXPERT), lambda e, *_: (layer, e, 0, 0)),
            pl.BlockSpec((None, None, D_EXPERT, D_MODEL), lambda e, *_: (layer, e, 0, 0)),
            pl.BlockSpec(memory_space=pl.ANY),
        ],
        out_specs=pl.BlockSpec(memory_space=pl.ANY),
        scratch_shapes=[
            pltpu.VMEM((D_MODEL, D_EXPERT), BF16),
            pltpu.VMEM((D_MODEL, D_EXPERT), BF16),
            pltpu.VMEM((D_EXPERT, D_MODEL), BF16),
            pltpu.VMEM((EXPERT_IN_SLOTS, EXPERT_BLOCK, HALF), jnp.uint32),
            pltpu.VMEM((EXPERT_OUT_SLOTS, EXPERT_BLOCK, HALF), jnp.uint32),
            pltpu.VMEM((EXPERT_BLOCK, HALF), jnp.uint32),
            pltpu.SemaphoreType.DMA((EXPERT_IN_SLOTS,)),
            pltpu.SemaphoreType.DMA((EXPERT_OUT_SLOTS,)),
            pltpu.SemaphoreType.DMA,
        ],
    )
    return pl.pallas_call(
        _expert_kernel,
        out_shape=jax.ShapeDtypeStruct(x_buf.shape, jnp.uint32),
        grid_spec=grid_spec,
        compiler_params=_cparams("arbitrary"),
        name="experts",
    )(nblk, pstart, n_used, wg, wu, wd, x_buf)


def _block_plan(counts, n_blocks):
    nblk = (counts + EXPERT_BLOCK - 1) // EXPERT_BLOCK
    pend = jnp.cumsum(nblk)
    pstart = ((pend - nblk) * EXPERT_BLOCK).astype(jnp.int32)
    return pstart, nblk.astype(jnp.int32), pend[-1:].astype(jnp.int32)


def _ffn_out_kernel(*refs, stream_blocks):
    (cs_ref, cn_ref, cd_ref, lpos_ref, h_ref, w_ref, x_ref, mod_ref, g_ref, wg_ref, wu_ref, wd_ref,
     y_hbm) = refs[:13]
    ns = len(stream_blocks)
    out_refs = refs[13:13 + ns]
    rows_ref, sem = refs[13 + ns:]
    step = pl.program_id(0)
    nsteps = pl.num_programs(0)
    n = h_ref.shape[0]
    slot = step % 2

    def gather(w, to_slot):
        _window_chunks((cs_ref, cn_ref, cd_ref), w, rows_ref.at[to_slot], y_hbm, sem.at[to_slot],
                       to_hbm=False, spare_row=0)

    pl.when(step == 0)(functools.partial(gather, 0, 0))
    for parity in range(2):
        pl.when((step + 1 < nsteps) & (slot == parity))(
            functools.partial(gather, step + 1, 1 - parity))

    ys = _swiglu_packed(h_ref[...], wg_ref, wu_ref, wd_ref)
    _wait_rows(y_hbm, sem.at[slot], LOCAL_ROWS)
    w = w_ref[...].astype(BF16)
    lp = lpos_ref[...].astype(jnp.int16)
    ya, yb = ys[:, :HALF], ys[:, HALF:]
    for c in range(LOCAL_ROWS // n):
        col = (lax.broadcasted_iota(jnp.int32, (n, n), 1) + c * n).astype(jnp.int16)
        pw = jnp.zeros((n, n), BF16)
        for k in range(TOP_K):
            pw = jnp.where(col == lp[:, k:k + 1], jnp.broadcast_to(w[:, k:k + 1], (n, n)), pw)
        ra, rb = _unpack_pair(rows_ref[slot, c * n:(c + 1) * n, :])
        ya = ya + _dot(pw, ra.astype(BF16))
        yb = yb + _dot(pw, rb.astype(BF16))
    ms = (jnp.sum(ya * ya, axis=-1, keepdims=True)
          + jnp.sum(yb * yb, axis=-1, keepdims=True)) * (1.0 / D_MODEL)
    inv = lax.rsqrt(ms + EPS)
    mod = mod_ref[...]
    g = g_ref[...]
    out_a = x_ref[:, 0:HALF] + mod[5:6, 0:HALF] * (ya * inv * g[:, 0:HALF])
    out_b = x_ref[:, HALF:] + mod[5:6, HALF:] * (yb * inv * g[:, HALF:])

    def store(o_ref):
        o_ref[:, 0:HALF] = out_a
        o_ref[:, HALF:] = out_b

    lo = 0
    for o_ref, nbk in zip(out_refs, stream_blocks):
        pl.when((step >= lo) & (step < lo + nbk))(functools.partial(store, o_ref))
        lo += nbk


def _ffn_out(tables, lpos_rows, h2p, w_rows, x1, mod_l, g_post, swg, swu, swd, y_buf, *, mod_map,
             stream_tokens):
    t_all = x1.shape[0]
    tm = TOKEN_BLOCK
    nsteps = t_all // tm
    row = lambda w: pl.BlockSpec((tm, w), lambda i: (i, 0))
    full = lambda a: pl.BlockSpec(a.shape, lambda i: (0,) * a.ndim)
    smem = lambda: pl.BlockSpec(memory_space=pltpu.SMEM)
    g = g_post.reshape(1, D_MODEL)
    stream_blocks = tuple(t // tm for t in stream_tokens)
    out_shape, out_specs = [], []
    lo = 0
    for t, nbk in zip(stream_tokens, stream_blocks):
        out_shape.append(jax.ShapeDtypeStruct((t, D_MODEL), F32))
        out_specs.append(pl.BlockSpec((tm, D_MODEL),
                                      lambda i, lo=lo, nbk=nbk: (jnp.clip(i - lo, 0, nbk - 1), 0)))
        lo += nbk
    return pl.pallas_call(
        functools.partial(_ffn_out_kernel, stream_blocks=stream_blocks),
        out_shape=out_shape,
        grid=(nsteps,),
        in_specs=[smem(), smem(), smem(),
                  row(TOP_K), row(HALF), row(TOP_K), row(D_MODEL),
                  pl.BlockSpec((None, N_MOD, D_MODEL), mod_map),
                  full(g), full(swg), full(swu), full(swd),
                  pl.BlockSpec(memory_space=pl.ANY)],
        out_specs=out_specs,
        scratch_shapes=[pltpu.VMEM((2, LOCAL_ROWS, HALF), jnp.uint32),
                        pltpu.SemaphoreType.DMA((2,))],
        compiler_params=_cparams("arbitrary"),
        name="ffn_out",
    )(*tables, lpos_rows, h2p, w_rows, x1, mod_l, g, swg, swu, swd, y_buf)


def _rope_tables(L):
    rows = L // GRID_W
    row = jnp.repeat(jnp.arange(rows), GRID_W).astype(F32)
    col = jnp.tile(jnp.arange(GRID_W), rows).astype(F32)
    n = DIFF_DK // 4
    inv = ROPE_BASE ** (-jnp.arange(n, dtype=F32) / n)
    ar, ac = row[:, None] * inv[None, :], col[:, None] * inv[None, :]
    cos64 = jnp.concatenate([jnp.cos(ar), jnp.cos(ar), jnp.cos(ac), jnp.cos(ac)], axis=-1)
    sin64 = jnp.concatenate([-jnp.sin(ar), jnp.sin(ar), -jnp.sin(ac), jnp.sin(ac)], axis=-1)
    return jnp.tile(cos64, (1, 2)), jnp.tile(sin64, (1, 2))


def kernel(x_prompt, x_sample, cache_k, cache_v, c, c_ctx, w_mod, b_mod, g_pre_mix, g_post_mix, g_pre_ffn, g_post_ffn, w_in, w_out, conv_w, conv_b, conv_ln_g, conv_ln_b, hy_short_w, hy_short_b, hy_w1, hy_b1, hy_w2, hy_b2, hy_freq, hy_w3, hy_bias, lam_q1, lam_k1, lam_q2, lam_k2, subln_g, router_w, router_b, exp_w_gate, exp_w_up, exp_w_down, sh_w_gate, sh_w_up, sh_w_down):
    n_p, l_p, _ = x_prompt.shape
    n_s, l_s, _ = x_sample.shape
    t_p, t_s = n_p * l_p, n_s * l_s

    cond = jnp.zeros((COND_ROWS, D_MODEL), F32).at[0].set(c_ctx).at[1:1 + n_s].set(c)
    mod = _modulation(cond, w_mod, b_mod).reshape(DEPTH, COND_ROWS, N_MOD, D_MODEL)
    rope = _rope_tables(l_s)

    t_all = t_p + t_s
    streams = {
        "p": dict(x=x_prompt.reshape(t_p, D_MODEL), T=t_p, L=l_p, row0=0, per_seq=False, hy_nb=8),
        "s": dict(x=x_sample.reshape(t_s, D_MODEL), T=t_s, L=l_s, row0=1, per_seq=True,
                  hy_nb=2 if n_s % 2 == 0 else 1),
    }
    for st in streams.values():
        fwd, fwd_t = _dft_matrices(st["L"])
        feats, window = _filter_tables(st["L"])
        st["fmat32"] = jnp.asarray(fwd)
        st["fmat"] = jnp.asarray(fwd).astype(BF16)
        st["fmat_t"] = jnp.asarray(fwd_t).astype(BF16)
        st["filter_tables"] = (jnp.asarray(feats), jnp.asarray(window))
    nb_p, nb_s = t_p // TOKEN_BLOCK, l_s // TOKEN_BLOCK
    merged_mod_map = lambda i: (jnp.where(i < nb_p, 0, 1 + (i - nb_p) // nb_s), 0, 0)
    n_blocks = (t_all // TOKEN_BLOCK) * (LOCAL_ROWS // EXPERT_BLOCK) + N_EXPERTS

    caches = None
    for l in range(DEPTH):
        w_in_bf = w_in[l].astype(BF16)
        w_out_bf = w_out[l].astype(BF16)
        swg, swu, swd = (a[l].astype(BF16) for a in (sh_w_gate, sh_w_up, sh_w_down))
        lam_vecs = jnp.stack([lam_q1[l], lam_k1[l], lam_q2[l], lam_k2[l]], axis=0)
        filt_params = (hy_w1[l], hy_b1[l], hy_w2[l], hy_b2[l], hy_freq[l], hy_w3[l])
        parts = []
        for name, st in streams.items():
            L = st["L"]
            kw = dict(seq_len=L, cond_row0=st["row0"], per_seq_cond=st["per_seq"])
            if name == "p":
                uc, uh, q, k, v, ck, cv = _premix(st["x"], mod[l], g_pre_mix[l], w_in_bf, layer=l,
                                                  n_tokens=st["T"], tok0=0,
                                                  caches=caches, want_cache=True, **kw)
                caches = (ck, cv)
                ctx = None
            else:
                uc, uh, q, k, v = _premix(st["x"], mod[l], g_pre_mix[l], w_in_bf, layer=l,
                                          n_tokens=st["T"], tok0=0, rope_tables=rope, **kw)
                ctx = (cache_k, cache_v)
            ya = _conformer(uc, conv_w[l], conv_b[l], conv_ln_g[l], conv_ln_b[l], seq_len=L)
            kspec = _hyena_spectrum(L, st["fmat32"], *st["filter_tables"], *filt_params)
            yb = _hyena(uh, hy_short_w[l], hy_short_b[l], hy_bias[l], st["fmat"], st["fmat_t"],
                        kspec, seq_len=L, seqs_per_step=st["hy_nb"])
            o = _attention(q, k, v, lam_vecs, subln_g[l], seq_len=L, layer=l, ctx=ctx)
            parts.append((ya, yb, o, st["x"]))
        x1, h2p, logits_t = _postmix(parts, mod[l], g_post_mix[l], g_pre_ffn[l], w_out_bf,
                                     router_w[l].T, mod_map=merged_mod_map)

        eidx_t, w_t, cnt = _route_topk(logits_t, router_b[l])
        counts = cnt[:, 0]
        pstart, nblk, n_used = _block_plan(counts, n_blocks)
        lpos, tables = _route_dest(eidx_t, pstart)
        x_buf = _dispatch(tables, lpos, counts, pstart, n_used, h2p, n_blocks)
        y_buf = _experts(x_buf, nblk, pstart, n_used, exp_w_gate, exp_w_up, exp_w_down, layer=l)
        outs = _ffn_out(tables, lpos.T, h2p, w_t.T, x1, mod[l], g_post_ffn[l], swg, swu, swd,
                        y_buf, mod_map=merged_mod_map, stream_tokens=(t_p, t_s))
        streams["p"]["x"], streams["s"]["x"] = outs

    y_prompt = streams["p"]["x"].reshape(n_p, l_p, D_MODEL)
    y_sample = streams["s"]["x"].reshape(n_s, l_s, D_MODEL)
    return (y_prompt, y_sample, caches[0], caches[1])
```
